```python
import jax, jax.numpy as jnp
from jax import lax
import numpy as np

D_MODEL = 2048
BATCH = 1
SEQ = 16384
DEPTH = 2

N_EVEN = (DEPTH + 1) // 2
N_ODD = DEPTH // 2
PLE_DIM = 256
D_FF = 4 * D_MODEL
ROPE_THETA = 10000.0
NORM_EPS = 1e-6
ATTN_Q_BLOCK = 128

MLA_HEADS = 8
MLA_Q_RANK = 512
MLA_KV_RANK = 512
MLA_NOPE = 128
MLA_ROPE = 64
MLA_V = 128
MLA_QK = MLA_NOPE + MLA_ROPE

SWA_HEADS = 16
SWA_KV_HEADS = 2
SWA_HD = 64
SWA_WINDOW = 128
SWA_BLOCK = 128

EVEN_IN_SPLITS = (MLA_Q_RANK, MLA_KV_RANK, MLA_ROPE, SWA_HEADS * SWA_HD, SWA_KV_HEADS * SWA_HD, SWA_KV_HEADS * SWA_HD)
EVEN_IN_WIDTH = sum(EVEN_IN_SPLITS)
EVEN_MIX_WIDTH = MLA_HEADS * MLA_V + SWA_HEADS * SWA_HD

MOBA_HEADS = 16
MOBA_HD = D_MODEL // MOBA_HEADS
MOBA_BLOCK = 256
MOBA_TOPK = 3
MOBA_Q_CHUNK = 32

kernel_name = 'hybrid_mla_swa_moba_sandwich_block'


def rms_norm(t, g):
    tf = t.astype(jnp.float32)
    y = tf * lax.rsqrt(jnp.mean(tf * tf, axis=-1, keepdims=True) + NORM_EPS)
    return (y * g.astype(jnp.float32)).astype(t.dtype)


def apply_rope(t, positions):
    d = t.shape[-1]
    half = d // 2
    inv_freq = jnp.power(ROPE_THETA, -jnp.arange(half, dtype=jnp.float32) * (2.0 / d))
    ang = positions.astype(jnp.float32)[..., None] * inv_freq
    cos = jnp.cos(ang)[:, :, None, :]
    sin = jnp.sin(ang)[:, :, None, :]
    tf = t.astype(jnp.float32)
    t1, t2 = tf[..., :half], tf[..., half:]
    return jnp.concatenate([t1 * cos - t2 * sin, t2 * cos + t1 * sin], axis=-1).astype(t.dtype)


def mla_causal_attention(q_nope, q_rope, k_nope, k_rope, v):
    B, S, H, _ = q_nope.shape
    nb = S // ATTN_Q_BLOCK
    scale = MLA_QK ** -0.5
    qn = q_nope.reshape(B, nb, ATTN_Q_BLOCK, H, MLA_NOPE).transpose(1, 0, 2, 3, 4)
    qr = q_rope.reshape(B, nb, ATTN_Q_BLOCK, H, MLA_ROPE).transpose(1, 0, 2, 3, 4)
    k_idx = jnp.arange(S)

    def one_block(args):
        i, qn_b, qr_b = args
        s = (jnp.einsum('bqhd,bkhd->bhqk', qn_b, k_nope, preferred_element_type=jnp.float32)
             + jnp.einsum('bqhd,bkd->bhqk', qr_b, k_rope, preferred_element_type=jnp.float32)) * scale
        q_idx = i * ATTN_Q_BLOCK + jnp.arange(ATTN_Q_BLOCK)
        s = jnp.where(k_idx[None, :] <= q_idx[:, None], s, -jnp.inf)
        w = jax.nn.softmax(s, axis=-1).astype(v.dtype)
        return jnp.einsum('bhqk,bkhd->bqhd', w, v)

    out = lax.map(one_block, (jnp.arange(nb), qn, qr))
    return out.transpose(1, 0, 2, 3, 4).reshape(B, S, H * MLA_V)


def swa_sink_attention(q, k, v, sinks):
    B, S, HQ, d = q.shape
    G = HQ // SWA_KV_HEADS
    nb = S // SWA_BLOCK
    qb = q.reshape(B, nb, SWA_BLOCK, SWA_KV_HEADS, G, d)
    kb = k.reshape(B, nb, SWA_BLOCK, SWA_KV_HEADS, d)
    vb = v.reshape(B, nb, SWA_BLOCK, SWA_KV_HEADS, d)

    def with_prev(t):
        prev = jnp.pad(t, ((0, 0), (1, 0), (0, 0), (0, 0), (0, 0)))[:, :-1]
        return jnp.concatenate([prev, t], axis=2)

    kw, vw = with_prev(kb), with_prev(vb)
    s = jnp.einsum('bnqkgd,bnjkd->bnkgqj', qb, kw, preferred_element_type=jnp.float32) * (d ** -0.5)
    q_pos = jnp.arange(SWA_BLOCK)[:, None] + SWA_BLOCK
    k_pos = jnp.arange(2 * SWA_BLOCK)[None, :]
    valid = (k_pos <= q_pos) & (q_pos - k_pos < SWA_WINDOW)
    valid_b = valid[None] & ((jnp.arange(nb)[:, None, None] > 0) | (k_pos[None] >= SWA_BLOCK))
    s = jnp.where(valid_b[None, :, None, None], s, -jnp.inf)
    sink = jnp.broadcast_to(sinks.astype(jnp.float32).reshape(SWA_KV_HEADS, G)[None, None, :, :, None, None],
                            s.shape[:-1] + (1,))
    w = jax.nn.softmax(jnp.concatenate([s, sink], axis=-1), axis=-1)[..., :-1].astype(v.dtype)
    o = jnp.einsum('bnkgqj,bnjkd->bnqkgd', w, vw)
    return o.reshape(B, S, HQ * d)


def moba_attention(q, k, v):
    B, S, H, d = q.shape
    pad = (-S) % MOBA_BLOCK
    padw = ((0, 0), (0, pad), (0, 0), (0, 0))
    q, k, v = jnp.pad(q, padw), jnp.pad(k, padw), jnp.pad(v, padw)
    Sp = S + pad
    nkb = Sp // MOBA_BLOCK
    k_sel_n = min(MOBA_TOPK, nkb)
    scale = d ** -0.5
    kb = k.reshape(B, nkb, MOBA_BLOCK, H, d).transpose(0, 3, 1, 2, 4)
    vb = v.reshape(B, nkb, MOBA_BLOCK, H, d).transpose(0, 3, 1, 2, 4)
    k_mean = jnp.mean(kb.astype(jnp.float32), axis=3)
    gate = jnp.einsum('bshd,bhnd->bhsn', q.astype(jnp.float32), k_mean)
    q_blk = jnp.arange(Sp) // MOBA_BLOCK
    fully_past = jnp.arange(nkb)[None, :] < q_blk[:, None]
    gate = jnp.where(fully_past[None, None], gate, -jnp.inf)
    top_val, top_idx = lax.top_k(gate, k_sel_n)
    sel_ok = jnp.isfinite(top_val)
    C = MOBA_Q_CHUNK
    nq = Sp // C
    qc = q.reshape(B, nq, C, H, d).transpose(1, 0, 2, 3, 4)
    idx_c = top_idx.reshape(B, H, nq, C, k_sel_n).transpose(2, 0, 1, 3, 4)
    ok_c = sel_ok.reshape(B, H, nq, C, k_sel_n).transpose(2, 0, 1, 3, 4)
    b_ix = jnp.arange(B)[:, None, None, None]
    h_ix = jnp.arange(H)[None, :, None, None]

    def one_chunk(args):
        ci, q_c, idx, ok = args
        start = ci * C
        blk = start // MOBA_BLOCK
        k_g = kb[b_ix, h_ix, idx]
        v_g = vb[b_ix, h_ix, idx]
        s_sel = jnp.einsum('bqhd,bhqkjd->bhqkj', q_c, k_g, preferred_element_type=jnp.float32) * scale
        s_sel = jnp.where(ok[..., None], s_sel, -jnp.inf).reshape(B, H, C, k_sel_n * MOBA_BLOCK)
        own_k = lax.dynamic_index_in_dim(kb, blk, axis=2, keepdims=False)
        own_v = lax.dynamic_index_in_dim(vb, blk, axis=2, keepdims=False)
        s_own = jnp.einsum('bqhd,bhjd->bhqj', q_c, own_k, preferred_element_type=jnp.float32) * scale
        q_pos = start + jnp.arange(C)
        k_pos = blk * MOBA_BLOCK + jnp.arange(MOBA_BLOCK)
        s_own = jnp.where(k_pos[None, :] <= q_pos[:, None], s_own, -jnp.inf)
        w = jax.nn.softmax(jnp.concatenate([s_sel, s_own], axis=-1), axis=-1).astype(v.dtype)
        w_sel = w[..., :k_sel_n * MOBA_BLOCK].reshape(B, H, C, k_sel_n, MOBA_BLOCK)
        w_own = w[..., k_sel_n * MOBA_BLOCK:]
        return (jnp.einsum('bhqkj,bhqkjd->bqhd', w_sel, v_g)
                + jnp.einsum('bhqj,bhjd->bqhd', w_own, own_v))

    o = lax.map(one_chunk, (jnp.arange(nq), qc, idx_c, ok_c))
    return o.transpose(1, 0, 2, 3, 4).reshape(B, Sp, H * d)[:, :S]


def even_mixer(h, positions, w_in, q_norm_g, w_q_up, kv_norm_g, w_kv_up, sinks, w_out):
    B, S, _ = h.shape
    proj = h @ w_in
    offs = np.cumsum(EVEN_IN_SPLITS)[:-1].tolist()
    cq, ckv, k_rope, q_s, k_s, v_s = jnp.split(proj, offs, axis=-1)
    q = (rms_norm(cq, q_norm_g) @ w_q_up).reshape(B, S, MLA_HEADS, MLA_QK)
    q_nope = q[..., :MLA_NOPE]
    q_rope = apply_rope(q[..., MLA_NOPE:], positions)
    kv = (rms_norm(ckv, kv_norm_g) @ w_kv_up).reshape(B, S, MLA_HEADS, MLA_NOPE + MLA_V)
    k_nope, v_mla = kv[..., :MLA_NOPE], kv[..., MLA_NOPE:]
    k_rope = apply_rope(k_rope[:, :, None, :], positions)[:, :, 0, :]
    o_mla = mla_causal_attention(q_nope, q_rope, k_nope, k_rope, v_mla)
    q_s = apply_rope(q_s.reshape(B, S, SWA_HEADS, SWA_HD), positions)
    k_s = apply_rope(k_s.reshape(B, S, SWA_KV_HEADS, SWA_HD), positions)
    v_s = v_s.reshape(B, S, SWA_KV_HEADS, SWA_HD)
    o_swa = swa_sink_attention(q_s, k_s, v_s, sinks)
    return jnp.concatenate([o_mla, o_swa], axis=-1) @ w_out


def odd_mixer(h, positions, w_qkv, w_out):
    B, S, _ = h.shape
    q, k, v = jnp.split(h @ w_qkv, 3, axis=-1)
    q = apply_rope(q.reshape(B, S, MOBA_HEADS, MOBA_HD), positions)
    k = apply_rope(k.reshape(B, S, MOBA_HEADS, MOBA_HD), positions)
    v = v.reshape(B, S, MOBA_HEADS, MOBA_HD)
    return moba_attention(q, k, v) @ w_out


def sq_relu_mlp(h, w_up, w_down):
    a = jax.nn.relu(h @ w_up)
    return (a * a) @ w_down


def setup_inputs(seed: int = 0) -> dict:
    key = jax.random.key(seed)
    ks = iter(jax.random.split(key, 32))

    def w(shape, fan_in):
        return jax.random.normal(next(ks), shape, jnp.float32) * (fan_in ** -0.5)

    def gain(shape):
        return 1.0 + 0.05 * jax.random.normal(next(ks), shape, jnp.float32)

    x = jax.random.normal(next(ks), (BATCH, SEQ, D_MODEL), jnp.float32)
    p = jax.random.normal(next(ks), (DEPTH, BATCH, SEQ, PLE_DIM), jnp.float32)
    positions = jnp.broadcast_to(jnp.arange(SEQ, dtype=jnp.int32)[None, :], (BATCH, SEQ))
    return {
        'x': x,
        'p': p,
        'positions': positions,
        'even_pre_g': gain((N_EVEN, D_MODEL)),
        'even_w_in': w((N_EVEN, D_MODEL, EVEN_IN_WIDTH), D_MODEL),
        'mla_q_norm_g': gain((N_EVEN, MLA_Q_RANK)),
        'mla_w_q_up': w((N_EVEN, MLA_Q_RANK, MLA_HEADS * MLA_QK), MLA_Q_RANK),
        'mla_kv_norm_g': gain((N_EVEN, MLA_KV_RANK)),
        'mla_w_kv_up': w((N_EVEN, MLA_KV_RANK, MLA_HEADS * (MLA_NOPE + MLA_V)), MLA_KV_RANK),
        'swa_sinks': jax.random.normal(next(ks), (N_EVEN, SWA_HEADS), jnp.float32),
        'even_w_out': w((N_EVEN, EVEN_MIX_WIDTH, D_MODEL), EVEN_MIX_WIDTH),
        'even_post_g': gain((N_EVEN, D_MODEL)),
        'odd_pre_g': gain((N_ODD, D_MODEL)),
        'moba_w_qkv': w((N_ODD, D_MODEL, 3 * MOBA_HEADS * MOBA_HD), D_MODEL),
        'odd_w_out': w((N_ODD, MOBA_HEADS * MOBA_HD, D_MODEL), MOBA_HEADS * MOBA_HD),
        'odd_post_g': gain((N_ODD, D_MODEL)),
        'mlp_pre_g': gain((DEPTH, D_MODEL)),
        'mlp_w_up': w((DEPTH, D_MODEL, D_FF), D_MODEL),
        'mlp_w_down': w((DEPTH, D_FF, D_MODEL), D_FF),
        'mlp_post_g': gain((DEPTH, D_MODEL)),
        'ple_w_gate': w((DEPTH, D_MODEL, D_MODEL), D_MODEL),
        'ple_b_gate': 0.02 * jax.random.normal(next(ks), (DEPTH, D_MODEL), jnp.float32),
        'ple_w_proj': w((DEPTH, PLE_DIM, D_MODEL), PLE_DIM),
    }


def reference(x, p, positions, even_pre_g, even_w_in, mla_q_norm_g, mla_w_q_up, mla_kv_norm_g,
              mla_w_kv_up, swa_sinks, even_w_out, even_post_g, odd_pre_g, moba_w_qkv, odd_w_out,
              odd_post_g, mlp_pre_g, mlp_w_up, mlp_w_down, mlp_post_g, ple_w_gate, ple_b_gate,
              ple_w_proj):
    h = x
    for i in range(DEPTH):
        j = i // 2
        if i % 2 == 0:
            m = even_mixer(rms_norm(h, even_pre_g[j]), positions, even_w_in[j], mla_q_norm_g[j],
                           mla_w_q_up[j], mla_kv_norm_g[j], mla_w_kv_up[j], swa_sinks[j], even_w_out[j])
            h = h + rms_norm(m, even_post_g[j])
        else:
            m = odd_mixer(rms_norm(h, odd_pre_g[j]), positions, moba_w_qkv[j], odd_w_out[j])
            h = h + rms_norm(m, odd_post_g[j])
        f = sq_relu_mlp(rms_norm(h, mlp_pre_g[i]), mlp_w_up[i], mlp_w_down[i])
        h = h + rms_norm(f, mlp_post_g[i])
        gate = jax.nn.sigmoid(h @ ple_w_gate[i] + ple_b_gate[i])
        h = h + gate * (p[i] @ ple_w_proj[i])
    return h
```

```python
import functools

import jax
import jax.numpy as jnp
import numpy as np
from jax import lax
from jax.experimental import pallas as pl
from jax.experimental.pallas import tpu as pltpu

F32 = jnp.float32
BF16 = jnp.bfloat16

NORM_EPS = 1e-6
ROPE_THETA = 10000.0

MLA_HEADS = 8
MLA_Q_RANK = 512
MLA_KV_RANK = 512
MLA_NOPE = 128
MLA_ROPE = 64
MLA_V = 128
SWA_HEADS = 16
SWA_KV_HEADS = 2
SWA_HD = 64
SWA_BLOCK = 128
MOBA_HEADS = 16
MOBA_HD = 128
MOBA_BLOCK = 256
MOBA_TOPK = 3

LANES = 128
V7X_VMEM_BYTES = 64 * 1024 * 1024
VMEM_LIMIT = 56 * 1024 * 1024
MASK_BIAS = -1e9


def _params(sem):
    return pltpu.CompilerParams(dimension_semantics=sem, vmem_limit_bytes=VMEM_LIMIT)


def _rms(t, g):
    return t * lax.rsqrt(jnp.mean(t * t, axis=-1, keepdims=True) + NORM_EPS) * g


def _rope(t, cos, sin_signed, half):
    width = t.shape[1]
    reps = width // LANES
    if reps > 1:
        cos = jnp.concatenate([cos] * reps, axis=1)
        sin_signed = jnp.concatenate([sin_signed] * reps, axis=1)
    lane = lax.broadcasted_iota(jnp.int32, t.shape, 1)
    first = (lane % (2 * half)) < half
    partner = jnp.where(first, pltpu.roll(t, width - half, 1), pltpu.roll(t, half, 1))
    return t * cos + partner * sin_signed


def _tables_kernel(pos_ref, invf_ref, sign_ref, cos_ref, sin_ref):
    ang = pos_ref[...].astype(F32) * invf_ref[...]
    cos_ref[...] = jnp.cos(ang)
    sin_ref[...] = jnp.sin(ang) * sign_ref[...]


def _rope_tables(positions, seq):
    def inv_freq(d):
        half = d // 2
        return jnp.power(ROPE_THETA, -jnp.arange(half, dtype=F32) * (2.0 / d))

    f64, f128 = inv_freq(64), inv_freq(128)
    invf = jnp.concatenate([f64, f64, f64, f64, f128, f128])[None, :]
    sign = np.concatenate([-np.ones(32), np.ones(32), -np.ones(32), np.ones(32),
                           -np.ones(64), np.ones(64)]).astype(np.float32)[None, :]
    tm = min(seq, 1024)
    return pl.pallas_call(
        _tables_kernel,
        grid=(seq // tm,),
        in_specs=[pl.BlockSpec((tm, 1), lambda i: (i, 0)),
                  pl.BlockSpec((1, 256), lambda i: (0, 0)),
                  pl.BlockSpec((1, 256), lambda i: (0, 0))],
        out_specs=[pl.BlockSpec((tm, 256), lambda i: (i, 0)),
                   pl.BlockSpec((tm, 256), lambda i: (i, 0))],
        out_shape=[jax.ShapeDtypeStruct((seq, 256), F32)] * 2,
        compiler_params=_params(("parallel",)),
        name="rope_tables",
    )(positions.reshape(seq, 1), invf, jnp.asarray(sign))


def _prenorm_kernel(x_ref, g_ref, o_ref):
    o_ref[...] = _rms(x_ref[...], g_ref[...]).astype(BF16)


def _prenorm(x, g):
    seq, d = x.shape
    tm = min(seq, 512)
    return pl.pallas_call(
        _prenorm_kernel,
        grid=(seq // tm,),
        in_specs=[pl.BlockSpec((tm, d), lambda i: (i, 0)),
                  pl.BlockSpec((1, d), lambda i: (0, 0))],
        out_specs=pl.BlockSpec((tm, d), lambda i: (i, 0)),
        out_shape=jax.ShapeDtypeStruct((seq, d), BF16),
        compiler_params=_params(("parallel",)),
        name="prenorm",
    )(x, g[None, :])


_EVEN_COLS = (0, 512, 1024, 1152, 2176, 2304, 2432)


def _even_proj_kernel(x_ref, w_ref, gq_ref, gkv_ref, cos_ref, sin_ref,
                      cq_ref, ckv_ref, kr_ref, qs_ref, ks_ref, vs_ref):
    x = x_ref[...]
    cos, sin = cos_ref[...], sin_ref[...]
    c = _EVEN_COLS

    def mm(k):
        return jnp.dot(x, w_ref[:, c[k]:c[k + 1]], preferred_element_type=F32)

    cq_ref[...] = _rms(mm(0), gq_ref[...]).astype(BF16)
    ckv_ref[...] = _rms(mm(1), gkv_ref[...]).astype(BF16)
    kr_ref[...] = _rope(mm(2), cos, sin, 32).astype(BF16)
    qs_ref[...] = (_rope(mm(3), cos, sin, 32) * (SWA_HD ** -0.5)).astype(BF16)
    ks_ref[...] = _rope(mm(4), cos, sin, 32).astype(BF16)
    vs_ref[...] = mm(5).astype(BF16)


def _even_proj(xn, w, gq, gkv, tabs_cos, tabs_sin):
    seq, d = xn.shape
    tm = min(seq, 512)
    widths = [_EVEN_COLS[k + 1] - _EVEN_COLS[k] for k in range(6)]
    row = lambda i: (i, 0)
    fixed = lambda i: (0, 0)
    return pl.pallas_call(
        _even_proj_kernel,
        grid=(seq // tm,),
        in_specs=[pl.BlockSpec((tm, d), row),
                  pl.BlockSpec(w.shape, fixed),
                  pl.BlockSpec((1, 512), fixed),
                  pl.BlockSpec((1, 512), fixed),
                  pl.BlockSpec((tm, LANES), row),
                  pl.BlockSpec((tm, LANES), row)],
        out_specs=[pl.BlockSpec((tm, n), row) for n in widths],
        out_shape=[jax.ShapeDtypeStruct((seq, n), BF16) for n in widths],
        compiler_params=_params(("parallel",)),
        name="even_proj",
    )(xn, w, gq[None, :], gkv[None, :], tabs_cos, tabs_sin)


def _mla_up_kernel(cq_ref, ckv_ref, wq_ref, wkv_ref, cos_ref, sin_ref,
                   qn_ref, qr_ref, kn_ref, v_ref, *, scale):
    cq, ckv = cq_ref[...], ckv_ref[...]
    n = MLA_HEADS * LANES
    qn = jnp.dot(cq, wq_ref[:, :n], preferred_element_type=F32)
    qr = jnp.dot(cq, wq_ref[:, n:], preferred_element_type=F32)
    qn_ref[...] = (qn * scale).astype(BF16)
    qr_ref[...] = (_rope(qr, cos_ref[...], sin_ref[...], 32) * scale).astype(BF16)
    kn_ref[...] = jnp.dot(ckv, wkv_ref[:, :n], preferred_element_type=F32).astype(BF16)
    v_ref[...] = jnp.dot(ckv, wkv_ref[:, n:], preferred_element_type=F32).astype(BF16)


def _mla_up(cq, ckv, wq, wkv, tabs_cos, tabs_sin):
    seq = cq.shape[0]
    tm = min(seq, 512)
    n = MLA_HEADS * LANES
    row = lambda i: (i, 0)
    fixed = lambda i: (0, 0)
    scale = (MLA_NOPE + MLA_ROPE) ** -0.5
    return pl.pallas_call(
        functools.partial(_mla_up_kernel, scale=scale),
        grid=(seq // tm,),
        in_specs=[pl.BlockSpec((tm, MLA_Q_RANK), row),
                  pl.BlockSpec((tm, MLA_KV_RANK), row),
                  pl.BlockSpec(wq.shape, fixed),
                  pl.BlockSpec(wkv.shape, fixed),
                  pl.BlockSpec((tm, LANES), row),
                  pl.BlockSpec((tm, LANES), row)],
        out_specs=[pl.BlockSpec((tm, n), row)] * 4,
        out_shape=[jax.ShapeDtypeStruct((seq, n), BF16)] * 4,
        compiler_params=_params(("parallel",)),
        name="mla_up",
    )(cq, ckv, wq, wkv, tabs_cos, tabs_sin)


def _flash_step(q, k, v, m_ref, l_ref, acc_ref, mask=None):
    s = lax.dot_general(q, k, (((1,), (1,)), ((), ())), preferred_element_type=F32)
    if mask is not None:
        s = jnp.where(mask, s, -jnp.inf)
    m_old = m_ref[...]
    m_new = jnp.maximum(m_old, jnp.max(s, axis=1, keepdims=True))
    alpha = jnp.exp(m_old - m_new)
    p = jnp.exp(s - m_new)
    l_ref[...] = alpha * l_ref[...] + jnp.sum(p, axis=1, keepdims=True)
    acc_ref[...] = alpha * acc_ref[...] + jnp.dot(p.astype(BF16), v, preferred_element_type=F32)
    m_ref[...] = m_new


def _flash_init(m_ref, l_ref, acc_ref):
    m_ref[...] = jnp.full(m_ref.shape, -jnp.inf, F32)
    l_ref[...] = jnp.zeros(l_ref.shape, F32)
    acc_ref[...] = jnp.zeros(acc_ref.shape, F32)


def _mla_attn_kernel(qn_ref, qr_ref, kn_ref, kr_ref, v_ref, o_ref,
                     m_ref, l_ref, acc_ref, *, tq, tk):
    i = pl.program_id(1)
    q = jnp.concatenate([qn_ref[...], qr_ref[...]], axis=1)
    _flash_init(m_ref, l_ref, acc_ref)

    def kv(j):
        rows = pl.ds(pl.multiple_of(j * tk, tk), tk)
        return jnp.concatenate([kn_ref[rows, :], kr_ref[rows, :]], axis=1), v_ref[rows, :]

    n_past = (i * tq) // tk

    def past(j, carry):
        k, v = kv(j)
        _flash_step(q, k, v, m_ref, l_ref, acc_ref)
        return carry

    lax.fori_loop(0, n_past, past, 0)

    row = i * tq + lax.broadcasted_iota(jnp.int32, (tq, tk), 0)
    for d in range(tq // tk):
        j = n_past + d
        col = j * tk + lax.broadcasted_iota(jnp.int32, (tq, tk), 1)
        k, v = kv(j)
        _flash_step(q, k, v, m_ref, l_ref, acc_ref, mask=col <= row)

    o_ref[...] = (acc_ref[...] / l_ref[...]).astype(BF16)


def _mla_attn(qn, qr, kn, kr, v):
    seq = qn.shape[0]
    tq = min(seq, 512)
    tk = tq
    return pl.pallas_call(
        functools.partial(_mla_attn_kernel, tq=tq, tk=tk),
        grid=(MLA_HEADS, seq // tq),
        in_specs=[pl.BlockSpec((tq, LANES), lambda h, i: (i, h)),
                  pl.BlockSpec((tq, LANES), lambda h, i: (i, h)),
                  pl.BlockSpec((seq, LANES), lambda h, i: (0, h)),
                  pl.BlockSpec((seq, LANES), lambda h, i: (0, 0)),
                  pl.BlockSpec((seq, LANES), lambda h, i: (0, h))],
        out_specs=pl.BlockSpec((tq, LANES), lambda h, i: (i, h)),
        out_shape=jax.ShapeDtypeStruct((seq, MLA_HEADS * MLA_V), BF16),
        scratch_shapes=[pltpu.VMEM((tq, 1), F32), pltpu.VMEM((tq, 1), F32),
                        pltpu.VMEM((tq, LANES), F32)],
        compiler_params=_params(("parallel", "arbitrary")),
        name="mla_attn",
    )(qn, qr, kn, kr, v)


def _swa_kernel(sink_ref, q_ref, kc_ref, kp_ref, vc_ref, vp_ref, o_ref, *, tq):
    i = pl.program_id(0)
    nb = tq // SWA_BLOCK
    group = SWA_HEADS // SWA_KV_HEADS
    lane = lax.broadcasted_iota(jnp.int32, (2 * SWA_BLOCK, LANES), 1)
    qp = lax.broadcasted_iota(jnp.int32, (SWA_BLOCK, 2 * SWA_BLOCK), 0) + SWA_BLOCK
    kp = lax.broadcasted_iota(jnp.int32, (SWA_BLOCK, 2 * SWA_BLOCK), 1)
    band = (kp <= qp) & (qp - kp < SWA_BLOCK)

    def split(t, c):
        mine = jnp.where((lane >= c * SWA_HD) & (lane < (c + 1) * SWA_HD), t, 0.0)
        other = pltpu.roll(mine, SWA_HD, 1)
        lo, hi = (mine, other) if c == 0 else (other, mine)
        return jnp.concatenate([lo, hi], axis=0).astype(BF16)

    for b in range(nb):
        rows = slice(b * SWA_BLOCK, (b + 1) * SWA_BLOCK)
        if b == 0:
            k_prev, v_prev = kp_ref[...], vp_ref[...]
        else:
            prev = slice((b - 1) * SWA_BLOCK, b * SWA_BLOCK)
            k_prev, v_prev = kc_ref[prev, :], vc_ref[prev, :]
        kw = jnp.concatenate([k_prev, kc_ref[rows, :]], axis=0).astype(F32)
        vw = jnp.concatenate([v_prev, vc_ref[rows, :]], axis=0).astype(F32)
        first_key = jnp.where(i * nb + b == 0, SWA_BLOCK, 0)
        valid = band & (kp >= first_key)
        for c in range(SWA_KV_HEADS):
            kcat = split(kw, c)
            vcat = split(vw, c)
            for a in range(group // 2):
                cols = slice((c * (group // 2) + a) * LANES, (c * (group // 2) + a + 1) * LANES)
                s = lax.dot_general(q_ref[rows, cols], kcat, (((1,), (1,)), ((), ())),
                                    preferred_element_type=F32)
                ws = []
                for e in range(2):
                    sink = sink_ref[c * group + 2 * a + e]
                    se = jnp.where(valid, s[:, e * 2 * SWA_BLOCK:(e + 1) * 2 * SWA_BLOCK], -jnp.inf)
                    m = jnp.maximum(jnp.max(se, axis=1, keepdims=True), sink)
                    p = jnp.exp(se - m)
                    den = jnp.sum(p, axis=1, keepdims=True) + jnp.exp(sink - m)
                    ws.append((p / den).astype(BF16))
                w = jnp.concatenate(ws, axis=1)
                o_ref[rows, cols] = jnp.dot(w, vcat, preferred_element_type=F32).astype(BF16)


def _swa_attn(qs, ks, vs, sinks):
    seq = qs.shape[0]
    tq = min(seq, 512)
    nb = tq // SWA_BLOCK
    cur = lambda i: (i, 0)
    prev = lambda i: (jnp.maximum(i * nb - 1, 0), 0)
    return pl.pallas_call(
        functools.partial(_swa_kernel, tq=tq),
        grid=(seq // tq,),
        in_specs=[pl.BlockSpec(memory_space=pltpu.SMEM),
                  pl.BlockSpec((tq, SWA_HEADS * SWA_HD), cur),
                  pl.BlockSpec((tq, LANES), cur),
                  pl.BlockSpec((SWA_BLOCK, LANES), prev),
                  pl.BlockSpec((tq, LANES), cur),
                  pl.BlockSpec((SWA_BLOCK, LANES), prev)],
        out_specs=pl.BlockSpec((tq, SWA_HEADS * SWA_HD), cur),
        out_shape=jax.ShapeDtypeStruct((seq, SWA_HEADS * SWA_HD), BF16),
        compiler_params=_params(("parallel",)),
        name="swa_attn",
    )(sinks, qs, ks, ks, vs, vs)


def _out_proj_kernel(*refs, n_a):
    a_refs = refs[:n_a]
    w_ref, h_ref, gpost_ref, gnext_ref, hout_ref, hn_ref = refs[n_a:]
    m = None
    k0 = 0
    for a_ref in a_refs:
        kw = a_ref.shape[1]
        part = jnp.dot(a_ref[...], w_ref[k0:k0 + kw, :], preferred_element_type=F32)
        m = part if m is None else m + part
        k0 += kw
    h = h_ref[...] + _rms(m, gpost_ref[...])
    hout_ref[...] = h
    hn_ref[...] = _rms(h, gnext_ref[...]).astype(BF16)


def _out_proj(a_list, w, h, g_post, g_next):
    seq, d = h.shape
    tm = min(seq, 512)
    row = lambda i: (i, 0)
    fixed = lambda i: (0, 0)
    return pl.pallas_call(
        functools.partial(_out_proj_kernel, n_a=len(a_list)),
        grid=(seq // tm,),
        in_specs=[pl.BlockSpec((tm, a.shape[1]), row) for a in a_list]
        + [pl.BlockSpec(w.shape, fixed), pl.BlockSpec((tm, d), row),
           pl.BlockSpec((1, d), fixed), pl.BlockSpec((1, d), fixed)],
        out_specs=[pl.BlockSpec((tm, d), row), pl.BlockSpec((tm, d), row)],
        out_shape=[jax.ShapeDtypeStruct((seq, d), F32), jax.ShapeDtypeStruct((seq, d), BF16)],
        compiler_params=_params(("parallel",)),
        name="out_proj",
    )(*a_list, w, h, g_post[None, :], g_next[None, :])


def _mlp_kernel(x_ref, wu_ref, wd_ref, h_ref, g_ref, o_ref, acc_ref):
    f = pl.program_id(1)
    a = jnp.maximum(jnp.dot(x_ref[...], wu_ref[...], preferred_element_type=F32), 0.0)
    part = jnp.dot((a * a).astype(BF16), wd_ref[...], preferred_element_type=F32)

    @pl.when(f == 0)
    def _():
        acc_ref[...] = part

    @pl.when(f > 0)
    def _():
        acc_ref[...] += part

    @pl.when(f == pl.num_programs(1) - 1)
    def _():
        o_ref[...] = h_ref[...] + _rms(acc_ref[...], g_ref[...])


def _mlp(xn, w_up, w_down, h, g_post):
    seq, d = h.shape
    d_ff = w_up.shape[1]
    tm = min(seq, 512)
    tf = min(d_ff, 1024)
    return pl.pallas_call(
        _mlp_kernel,
        grid=(seq // tm, d_ff // tf),
        in_specs=[pl.BlockSpec((tm, d), lambda i, f: (i, 0)),
                  pl.BlockSpec((d, tf), lambda i, f: (0, f)),
                  pl.BlockSpec((tf, d), lambda i, f: (f, 0)),
                  pl.BlockSpec((tm, d), lambda i, f: (i, 0)),
                  pl.BlockSpec((1, d), lambda i, f: (0, 0))],
        out_specs=pl.BlockSpec((tm, d), lambda i, f: (i, 0)),
        out_shape=jax.ShapeDtypeStruct((seq, d), F32),
        scratch_shapes=[pltpu.VMEM((tm, d), F32)],
        compiler_params=_params(("parallel", "arbitrary")),
        name="mlp",
    )(xn, w_up, w_down, h, g_post[None, :])


def _ple_kernel(h_ref, p_ref, wg_ref, b_ref, wp_ref, *rest, with_next):
    h = h_ref[...]
    z = jnp.dot(h.astype(BF16), wg_ref[...], preferred_element_type=F32) + b_ref[...]
    gate = 1.0 / (1.0 + jnp.exp(-z))
    e = jnp.dot(p_ref[...].astype(BF16), wp_ref[...], preferred_element_type=F32)
    out = h + gate * e
    if with_next:
        gnext_ref, o_ref, hn_ref = rest
        hn_ref[...] = _rms(out, gnext_ref[...]).astype(BF16)
    else:
        (o_ref,) = rest
    o_ref[...] = out


def _ple(h, p, w_gate, b_gate, w_proj, g_next=None):
    seq, d = h.shape
    tm = min(seq, 512)
    row = lambda i: (i, 0)
    fixed = lambda i: (0, 0)
    with_next = g_next is not None
    in_specs = [pl.BlockSpec((tm, d), row), pl.BlockSpec((tm, p.shape[1]), row),
                pl.BlockSpec(w_gate.shape, fixed), pl.BlockSpec((1, d), fixed),
                pl.BlockSpec(w_proj.shape, fixed)]
    args = [h, p, w_gate, b_gate[None, :], w_proj]
    out_specs = [pl.BlockSpec((tm, d), row)]
    out_shape = [jax.ShapeDtypeStruct((seq, d), F32)]
    if with_next:
        in_specs.append(pl.BlockSpec((1, d), fixed))
        args.append(g_next[None, :])
        out_specs.append(pl.BlockSpec((tm, d), row))
        out_shape.append(jax.ShapeDtypeStruct((seq, d), BF16))
    res = pl.pallas_call(
        functools.partial(_ple_kernel, with_next=with_next),
        grid=(seq // tm,),
        in_specs=in_specs, out_specs=out_specs, out_shape=out_shape,
        compiler_params=_params(("parallel",)),
        name="ple",
    )(*args)
    return res if with_next else (res[0], None)


def _odd_proj_kernel(x_ref, w_ref, cos_ref, sin_ref, o_ref, *rest, rope, scale, kmean, tm):
    t = jnp.dot(x_ref[...], w_ref[...], preferred_element_type=F32)
    if rope:
        t = _rope(t, cos_ref[...], sin_ref[...], MOBA_HD // 2)
    if kmean:
        (km_ref,) = rest
        for b in range(tm // MOBA_BLOCK):
            blk = t[b * MOBA_BLOCK:(b + 1) * MOBA_BLOCK, :]
            km_ref[b] = jnp.sum(blk, axis=0, keepdims=True) * (1.0 / MOBA_BLOCK)
    if scale != 1.0:
        t = t * scale
    o_ref[...] = t.astype(BF16)


def _odd_proj(xn, w, tabs_cos, tabs_sin, *, rope, scale=1.0, kmean=False):
    seq, d = xn.shape
    n = w.shape[1]
    tm = min(seq, 512)
    row = lambda i: (i, 0)
    fixed = lambda i: (0, 0)
    out_specs = [pl.BlockSpec((tm, n), row)]
    out_shape = [jax.ShapeDtypeStruct((seq, n), BF16)]
    if kmean:
        nb = tm // MOBA_BLOCK
        out_specs.append(pl.BlockSpec((nb, 1, n), lambda i: (i, 0, 0)))
        out_shape.append(jax.ShapeDtypeStruct((seq // MOBA_BLOCK, 1, n), F32))
    res = pl.pallas_call(
        functools.partial(_odd_proj_kernel, rope=rope, scale=scale, kmean=kmean, tm=tm),
        grid=(seq // tm,),
        in_specs=[pl.BlockSpec((tm, d), row), pl.BlockSpec(w.shape, fixed),
                  pl.BlockSpec((tm, LANES), lambda i: (i, 1)),
                  pl.BlockSpec((tm, LANES), lambda i: (i, 1))],
        out_specs=out_specs, out_shape=out_shape,
        compiler_params=_params(("parallel",)),
        name="odd_proj",
    )(xn, w, tabs_cos, tabs_sin)
    return res


def _moba_kernel(q_ref, k_ref, v_ref, km_ref, o_ref, m_ref, l_ref, acc_ref, *, tq, tk, nkb):
    i = pl.program_id(1)
    q = q_ref[...]
    _flash_init(m_ref, l_ref, acc_ref)

    km = km_ref[...]
    if nkb < 64:
        km = jnp.concatenate([km, jnp.zeros((64 - nkb, LANES), F32)], axis=0)
    km_hi = km.astype(BF16).astype(F32)
    r1 = km - km_hi
    km_mid = r1.astype(BF16).astype(F32)
    km_lo = (r1 - km_mid).astype(BF16).astype(F32)
    kcat = jnp.concatenate([km_hi, km_mid, km_lo, jnp.zeros((64, LANES), F32)], axis=0).astype(BF16)
    g3 = lax.dot_general(q, kcat, (((1,), (1,)), ((), ())), preferred_element_type=F32)
    g01 = g3[:, :LANES]
    gate = g01 + pltpu.roll(g01, 64, 1) + g3[:, LANES:]

    lane = lax.broadcasted_iota(jnp.int32, (tq, LANES), 1)
    qblk = (i * tq + lax.broadcasted_iota(jnp.int32, (tq, LANES), 0)) // MOBA_BLOCK
    gm = jnp.where(lane < qblk, gate, -jnp.inf)
    sel = lane == qblk
    for _ in range(MOBA_TOPK):
        mx = jnp.max(gm, axis=1, keepdims=True)
        is_max = (gm == mx) & (mx > -jnp.inf)
        idx = jnp.min(jnp.where(is_max, lane, LANES), axis=1, keepdims=True)
        pick = lane == idx
        sel = sel | pick
        gm = jnp.where(pick, -jnp.inf, gm)
    bias = jnp.where(sel, 0.0, MASK_BIAS).astype(BF16)
    qa = jnp.concatenate([q, bias], axis=1)

    kb_per_tile = tk // MOBA_BLOCK
    krow = lax.broadcasted_iota(jnp.int32, (tk, LANES), 0) // MOBA_BLOCK
    klane = lax.broadcasted_iota(jnp.int32, (tk, LANES), 1)

    def kv(j):
        rows = pl.ds(pl.multiple_of(j * tk, tk), tk)
        onehot = jnp.where(klane == krow + j * kb_per_tile, 1.0, 0.0).astype(BF16)
        return jnp.concatenate([k_ref[rows, :], onehot], axis=1), v_ref[rows, :]

    n_past = (i * tq) // tk

    def past(j, carry):
        k, v = kv(j)
        _flash_step(qa, k, v, m_ref, l_ref, acc_ref)
        return carry

    lax.fori_loop(0, n_past, past, 0)

    row = i * tq + lax.broadcasted_iota(jnp.int32, (tq, tk), 0)
    for d in range(tq // tk):
        j = n_past + d
        col = j * tk + lax.broadcasted_iota(jnp.int32, (tq, tk), 1)
        k, v = kv(j)
        _flash_step(qa, k, v, m_ref, l_ref, acc_ref, mask=col <= row)

    o_ref[...] = (acc_ref[...] / l_ref[...]).astype(BF16)


def _moba_attn(q, k, v, kmean):
    seq = q.shape[0]
    nkb = seq // MOBA_BLOCK
    assert nkb <= 64 and nkb % 8 == 0, "the gate matmul packs three 64-lane groups"
    tq = min(seq, 512)
    tk = tq
    return pl.pallas_call(
        functools.partial(_moba_kernel, tq=tq, tk=tk, nkb=nkb),
        grid=(MOBA_HEADS, seq // tq),
        in_specs=[pl.BlockSpec((tq, LANES), lambda h, i: (i, h)),
                  pl.BlockSpec((seq, LANES), lambda h, i: (0, h)),
                  pl.BlockSpec((seq, LANES), lambda h, i: (0, h)),
                  pl.BlockSpec((nkb, LANES), lambda h, i: (0, h))],
        out_specs=pl.BlockSpec((tq, LANES), lambda h, i: (i, h)),
        out_shape=jax.ShapeDtypeStruct((seq, MOBA_HEADS * MOBA_HD), BF16),
        scratch_shapes=[pltpu.VMEM((tq, 1), F32), pltpu.VMEM((tq, 1), F32),
                        pltpu.VMEM((tq, LANES), F32)],
        compiler_params=_params(("parallel", "arbitrary")),
        name="moba_attn",
    )(q, k, v, kmean)


def _even_in_weight(w_in):
    d = w_in.shape[0]
    cq, ckv, kr, qs, ks, vs = jnp.split(w_in, [512, 1024, 1088, 2112, 2240], axis=1)
    return jnp.concatenate([cq, ckv, kr, jnp.zeros((d, 64), w_in.dtype), qs, ks, vs], axis=1).astype(BF16)


def _mla_q_weight(w_q_up):
    r = w_q_up.shape[0]
    w = w_q_up.reshape(r, MLA_HEADS, MLA_NOPE + MLA_ROPE)
    nope = w[:, :, :MLA_NOPE].reshape(r, MLA_HEADS * MLA_NOPE)
    rope = jnp.pad(w[:, :, MLA_NOPE:], ((0, 0), (0, 0), (0, LANES - MLA_ROPE))).reshape(r, MLA_HEADS * LANES)
    return jnp.concatenate([nope, rope], axis=1).astype(BF16)


def _mla_kv_weight(w_kv_up):
    r = w_kv_up.shape[0]
    w = w_kv_up.reshape(r, MLA_HEADS, MLA_NOPE + MLA_V)
    nope = w[:, :, :MLA_NOPE].reshape(r, MLA_HEADS * MLA_NOPE)
    val = w[:, :, MLA_NOPE:].reshape(r, MLA_HEADS * MLA_V)
    return jnp.concatenate([nope, val], axis=1).astype(BF16)


def kernel(x, p, positions, even_pre_g, even_w_in, mla_q_norm_g, mla_w_q_up, mla_kv_norm_g, mla_w_kv_up, swa_sinks, even_w_out, even_post_g, odd_pre_g, moba_w_qkv, odd_w_out, odd_post_g, mlp_pre_g, mlp_w_up, mlp_w_down, mlp_post_g, ple_w_gate, ple_b_gate, ple_w_proj):
    batch, seq, d = x.shape
    assert batch == 1
    h = x.reshape(seq, d)
    tabs_cos, tabs_sin = _rope_tables(positions, seq)

    xn = _prenorm(h, even_pre_g[0])
    cq, ckv, kr, qs, ks, vs = _even_proj(xn, _even_in_weight(even_w_in[0]), mla_q_norm_g[0],
                                         mla_kv_norm_g[0], tabs_cos, tabs_sin)
    qn, qr, kn, v = _mla_up(cq, ckv, _mla_q_weight(mla_w_q_up[0]), _mla_kv_weight(mla_w_kv_up[0]),
                            tabs_cos, tabs_sin)
    o_mla = _mla_attn(qn, qr, kn, kr, v)
    o_swa = _swa_attn(qs, ks, vs, swa_sinks[0])
    h, hn = _out_proj([o_mla, o_swa], even_w_out[0].astype(BF16), h, even_post_g[0], mlp_pre_g[0])
    h = _mlp(hn, mlp_w_up[0].astype(BF16), mlp_w_down[0].astype(BF16), h, mlp_post_g[0])
    h, hn = _ple(h, p[0, 0], ple_w_gate[0].astype(BF16), ple_b_gate[0], ple_w_proj[0].astype(BF16),
                 g_next=odd_pre_g[0])

    wq, wk, wv = jnp.split(moba_w_qkv[0].astype(BF16), 3, axis=1)
    (q,) = _odd_proj(hn, wq, tabs_cos, tabs_sin, rope=True, scale=MOBA_HD ** -0.5)
    k, kmean = _odd_proj(hn, wk, tabs_cos, tabs_sin, rope=True, kmean=True)
    (v,) = _odd_proj(hn, wv, tabs_cos, tabs_sin, rope=False)
    o = _moba_attn(q, k, v, kmean.reshape(seq // MOBA_BLOCK, MOBA_HEADS * MOBA_HD))
    h, hn = _out_proj([o], odd_w_out[0].astype(BF16), h, odd_post_g[0], mlp_pre_g[1])
    h = _mlp(hn, mlp_w_up[1].astype(BF16), mlp_w_down[1].astype(BF16), h, mlp_post_g[1])
    h, _ = _ple(h, p[1, 0], ple_w_gate[1].astype(BF16), ple_b_gate[1], ple_w_proj[1].astype(BF16))
    return h.reshape(batch, seq, d)
```

```python
import functools

import jax
import jax.numpy as jnp
import numpy as np
from jax import lax
from jax.experimental import pallas as pl
from jax.experimental.pallas import tpu as pltpu

F32 = jnp.float32
BF16 = jnp.bfloat16

NORM_EPS = 1e-6
ROPE_THETA = 10000.0

MLA_HEADS = 8
MLA_Q_RANK = 512
MLA_KV_RANK = 512
MLA_NOPE = 128
MLA_ROPE = 64
MLA_V = 128
SWA_HEADS = 16
SWA_KV_HEADS = 2
SWA_HD = 64
SWA_BLOCK = 128
MOBA_HEADS = 16
MOBA_HD = 128
MOBA_BLOCK = 256
MOBA_TOPK = 3

LANES = 128
V7X_VMEM_BYTES = 64 * 1024 * 1024
VMEM_LIMIT = 56 * 1024 * 1024
MASK_BIAS = -1e9
LOG2E = 1.4426950408889634


def _params(sem):
    return pltpu.CompilerParams(dimension_semantics=sem, vmem_limit_bytes=VMEM_LIMIT)


def _rms(t, g):
    return t * lax.rsqrt(jnp.mean(t * t, axis=-1, keepdims=True) + NORM_EPS) * g


def _rope(t, cos, sin_signed, half):
    width = t.shape[1]
    reps = width // LANES
    if reps > 1:
        cos = jnp.concatenate([cos] * reps, axis=1)
        sin_signed = jnp.concatenate([sin_signed] * reps, axis=1)
    lane = lax.broadcasted_iota(jnp.int32, t.shape, 1)
    first = (lane % (2 * half)) < half
    partner = jnp.where(first, pltpu.roll(t, width - half, 1), pltpu.roll(t, half, 1))
    return t * cos + partner * sin_signed


def _tables_kernel(pos_ref, invf_ref, sign_ref, cos_ref, sin_ref):
    ang = pos_ref[...].astype(F32) * invf_ref[...]
    cos_ref[...] = jnp.cos(ang)
    sin_ref[...] = jnp.sin(ang) * sign_ref[...]


def _rope_tables(positions, seq):
    def inv_freq(d):
        half = d // 2
        return jnp.power(ROPE_THETA, -jnp.arange(half, dtype=F32) * (2.0 / d))

    f64, f128 = inv_freq(64), inv_freq(128)
    invf = jnp.concatenate([f64, f64, f64, f64, f128, f128])[None, :]
    sign = np.concatenate([-np.ones(32), np.ones(32), -np.ones(32), np.ones(32),
                           -np.ones(64), np.ones(64)]).astype(np.float32)[None, :]
    tm = min(seq, 1024)
    return pl.pallas_call(
        _tables_kernel,
        grid=(seq // tm,),
        in_specs=[pl.BlockSpec((tm, 1), lambda i: (i, 0)),
                  pl.BlockSpec((1, 256), lambda i: (0, 0)),
                  pl.BlockSpec((1, 256), lambda i: (0, 0))],
        out_specs=[pl.BlockSpec((tm, 256), lambda i: (i, 0)),
                   pl.BlockSpec((tm, 256), lambda i: (i, 0))],
        out_shape=[jax.ShapeDtypeStruct((seq, 256), F32)] * 2,
        compiler_params=_params(("parallel",)),
        name="rope_tables",
    )(positions.reshape(seq, 1), invf, jnp.asarray(sign))


def _prenorm_kernel(x_ref, g_ref, o_ref):
    o_ref[...] = _rms(x_ref[...], g_ref[...]).astype(BF16)


def _prenorm(x, g):
    seq, d = x.shape
    tm = min(seq, 512)
    return pl.pallas_call(
        _prenorm_kernel,
        grid=(seq // tm,),
        in_specs=[pl.BlockSpec((tm, d), lambda i: (i, 0)),
                  pl.BlockSpec((1, d), lambda i: (0, 0))],
        out_specs=pl.BlockSpec((tm, d), lambda i: (i, 0)),
        out_shape=jax.ShapeDtypeStruct((seq, d), BF16),
        compiler_params=_params(("parallel",)),
        name="prenorm",
    )(x, g[None, :])


_EVEN_COLS = (0, 512, 1024, 1152, 2176, 2304, 2432)


def _even_proj_kernel(x_ref, w_ref, gq_ref, gkv_ref, cos_ref, sin_ref,
                      cq_ref, ckv_ref, kr_ref, qs_ref, ks_ref, vs_ref):
    x = x_ref[...]
    cos, sin = cos_ref[...], sin_ref[...]
    c = _EVEN_COLS

    def mm(k):
        return jnp.dot(x, w_ref[:, c[k]:c[k + 1]], preferred_element_type=F32)

    cq_ref[...] = _rms(mm(0), gq_ref[...]).astype(BF16)
    ckv_ref[...] = _rms(mm(1), gkv_ref[...]).astype(BF16)
    kr_ref[...] = _rope(mm(2), cos, sin, 32).astype(BF16)
    qs_ref[...] = (_rope(mm(3), cos, sin, 32) * (SWA_HD ** -0.5)).astype(BF16)
    ks_ref[...] = _rope(mm(4), cos, sin, 32).astype(BF16)
    vs_ref[...] = mm(5).astype(BF16)


def _even_proj(xn, w, gq, gkv, tabs_cos, tabs_sin):
    seq, d = xn.shape
    tm = min(seq, 512)
    widths = [_EVEN_COLS[k + 1] - _EVEN_COLS[k] for k in range(6)]
    row = lambda i: (i, 0)
    fixed = lambda i: (0, 0)
    return pl.pallas_call(
        _even_proj_kernel,
        grid=(seq // tm,),
        in_specs=[pl.BlockSpec((tm, d), row),
                  pl.BlockSpec(w.shape, fixed),
                  pl.BlockSpec((1, 512), fixed),
                  pl.BlockSpec((1, 512), fixed),
                  pl.BlockSpec((tm, LANES), row),
                  pl.BlockSpec((tm, LANES), row)],
        out_specs=[pl.BlockSpec((tm, n), row) for n in widths],
        out_shape=[jax.ShapeDtypeStruct((seq, n), BF16) for n in widths],
        compiler_params=_params(("parallel",)),
        name="even_proj",
    )(xn, w, gq[None, :], gkv[None, :], tabs_cos, tabs_sin)


def _mla_up_kernel(cq_ref, ckv_ref, wq_ref, wkv_ref, cos_ref, sin_ref,
                   qn_ref, qr_ref, kn_ref, v_ref, *, scale):
    cq, ckv = cq_ref[...], ckv_ref[...]
    n = MLA_HEADS * LANES
    qn = jnp.dot(cq, wq_ref[:, :n], preferred_element_type=F32)
    qr = jnp.dot(cq, wq_ref[:, n:], preferred_element_type=F32)
    qn_ref[...] = (qn * scale).astype(BF16)
    qr_ref[...] = (_rope(qr, cos_ref[...], sin_ref[...], 32) * scale).astype(BF16)
    kn_ref[...] = jnp.dot(ckv, wkv_ref[:, :n], preferred_element_type=F32).astype(BF16)
    v_ref[...] = jnp.dot(ckv, wkv_ref[:, n:], preferred_element_type=F32).astype(BF16)


def _mla_up(cq, ckv, wq, wkv, tabs_cos, tabs_sin):
    seq = cq.shape[0]
    tm = min(seq, 512)
    n = MLA_HEADS * LANES
    row = lambda i: (i, 0)
    fixed = lambda i: (0, 0)
    scale = (MLA_NOPE + MLA_ROPE) ** -0.5 * LOG2E
    return pl.pallas_call(
        functools.partial(_mla_up_kernel, scale=scale),
        grid=(seq // tm,),
        in_specs=[pl.BlockSpec((tm, MLA_Q_RANK), row),
                  pl.BlockSpec((tm, MLA_KV_RANK), row),
                  pl.BlockSpec(wq.shape, fixed),
                  pl.BlockSpec(wkv.shape, fixed),
                  pl.BlockSpec((tm, LANES), row),
                  pl.BlockSpec((tm, LANES), row)],
        out_specs=[pl.BlockSpec((tm, n), row)] * 4,
        out_shape=[jax.ShapeDtypeStruct((seq, n), BF16)] * 4,
        compiler_params=_params(("parallel",)),
        name="mla_up",
    )(cq, ckv, wq, wkv, tabs_cos, tabs_sin)


def _flash_step(q, k, v, m_ref, l_ref, acc_ref, mask=None):
    s = lax.dot_general(q, k, (((1,), (1,)), ((), ())), preferred_element_type=F32)
    if mask is not None:
        s = jnp.where(mask, s, -jnp.inf)
    m_old = m_ref[...]
    m_new = jnp.maximum(m_old, jnp.max(s, axis=1, keepdims=True))
    alpha = jnp.exp2(m_old - m_new)
    p = jnp.exp2(s - jnp.tile(m_new, (1, s.shape[1] // LANES)))
    l_ref[...] = alpha * l_ref[...] + jnp.sum(p, axis=1, keepdims=True)
    acc_ref[...] = alpha * acc_ref[...] + jnp.dot(p.astype(BF16), v, preferred_element_type=F32)
    m_ref[...] = m_new


def _flash_init(m_ref, l_ref, acc_ref):
    m_ref[...] = jnp.full(m_ref.shape, -jnp.inf, F32)
    l_ref[...] = jnp.zeros(l_ref.shape, F32)
    acc_ref[...] = jnp.zeros(acc_ref.shape, F32)


def _mla_attn_kernel(qn_ref, qr_ref, kn_ref, kr_ref, v_ref, o_ref,
                     m_ref, l_ref, acc_ref, *, tq, tk):
    i = pl.program_id(1)
    q = jnp.concatenate([qn_ref[...], qr_ref[...]], axis=1)
    _flash_init(m_ref, l_ref, acc_ref)

    def kv(j):
        rows = pl.ds(pl.multiple_of(j * tk, tk), tk)
        return jnp.concatenate([kn_ref[rows, :], kr_ref[rows, :]], axis=1), v_ref[rows, :]

    sub = tq // tk

    def past(jj, carry):
        for d in range(sub):
            k, v = kv(jj * sub + d)
            _flash_step(q, k, v, m_ref, l_ref, acc_ref)
        return carry

    lax.fori_loop(0, i, past, 0)

    row = i * tq + lax.broadcasted_iota(jnp.int32, (tq, tk), 0)
    for d in range(sub):
        j = i * sub + d
        col = j * tk + lax.broadcasted_iota(jnp.int32, (tq, tk), 1)
        k, v = kv(j)
        _flash_step(q, k, v, m_ref, l_ref, acc_ref, mask=col <= row)

    o_ref[...] = (acc_ref[...] / l_ref[...]).astype(BF16)


def _mla_attn(qn, qr, kn, kr, v):
    seq = qn.shape[0]
    tq = min(seq, 1024)
    tk = min(seq, 512)
    return pl.pallas_call(
        functools.partial(_mla_attn_kernel, tq=tq, tk=tk),
        grid=(MLA_HEADS, seq // tq),
        in_specs=[pl.BlockSpec((tq, LANES), lambda h, i: (i, h)),
                  pl.BlockSpec((tq, LANES), lambda h, i: (i, h)),
                  pl.BlockSpec((seq, LANES), lambda h, i: (0, h)),
                  pl.BlockSpec((seq, LANES), lambda h, i: (0, 0)),
                  pl.BlockSpec((seq, LANES), lambda h, i: (0, h))],
        out_specs=pl.BlockSpec((tq, LANES), lambda h, i: (i, h)),
        out_shape=jax.ShapeDtypeStruct((seq, MLA_HEADS * MLA_V), BF16),
        scratch_shapes=[pltpu.VMEM((tq, LANES), F32), pltpu.VMEM((tq, LANES), F32),
                        pltpu.VMEM((tq, LANES), F32)],
        compiler_params=_params(("parallel", "arbitrary")),
        name="mla_attn",
    )(qn, qr, kn, kr, v)


def _swa_kernel(sink_ref, q_ref, kc_ref, kp_ref, vc_ref, vp_ref, o_ref, *, tq):
    i = pl.program_id(0)
    nb = tq // SWA_BLOCK
    group = SWA_HEADS // SWA_KV_HEADS
    lane = lax.broadcasted_iota(jnp.int32, (2 * SWA_BLOCK, LANES), 1)
    qp = lax.broadcasted_iota(jnp.int32, (SWA_BLOCK, 2 * SWA_BLOCK), 0) + SWA_BLOCK
    kp = lax.broadcasted_iota(jnp.int32, (SWA_BLOCK, 2 * SWA_BLOCK), 1)
    band = (kp <= qp) & (qp - kp < SWA_BLOCK)

    def split(t, c):
        mine = jnp.where((lane >= c * SWA_HD) & (lane < (c + 1) * SWA_HD), t, 0.0)
        other = pltpu.roll(mine, SWA_HD, 1)
        lo, hi = (mine, other) if c == 0 else (other, mine)
        return jnp.concatenate([lo, hi], axis=0).astype(BF16)

    for b in range(nb):
        rows = slice(b * SWA_BLOCK, (b + 1) * SWA_BLOCK)
        if b == 0:
            k_prev, v_prev = kp_ref[...], vp_ref[...]
        else:
            prev = slice((b - 1) * SWA_BLOCK, b * SWA_BLOCK)
            k_prev, v_prev = kc_ref[prev, :], vc_ref[prev, :]
        kw = jnp.concatenate([k_prev, kc_ref[rows, :]], axis=0).astype(F32)
        vw = jnp.concatenate([v_prev, vc_ref[rows, :]], axis=0).astype(F32)
        first_key = jnp.where(i * nb + b == 0, SWA_BLOCK, 0)
        valid = band & (kp >= first_key)
        for c in range(SWA_KV_HEADS):
            kcat = split(kw, c)
            vcat = split(vw, c)
            for a in range(group // 2):
                cols = slice((c * (group // 2) + a) * LANES, (c * (group // 2) + a + 1) * LANES)
                s = lax.dot_general(q_ref[rows, cols], kcat, (((1,), (1,)), ((), ())),
                                    preferred_element_type=F32)
                ws = []
                for e in range(2):
                    sink = sink_ref[c * group + 2 * a + e]
                    se = jnp.where(valid, s[:, e * 2 * SWA_BLOCK:(e + 1) * 2 * SWA_BLOCK], -jnp.inf)
                    m = jnp.maximum(jnp.max(se, axis=1, keepdims=True), sink)
                    p = jnp.exp(se - m)
                    den = jnp.sum(p, axis=1, keepdims=True) + jnp.exp(sink - m)
                    ws.append((p / den).astype(BF16))
                w = jnp.concatenate(ws, axis=1)
                o_ref[rows, cols] = jnp.dot(w, vcat, preferred_element_type=F32).astype(BF16)


def _swa_attn(qs, ks, vs, sinks):
    seq = qs.shape[0]
    tq = min(seq, 512)
    nb = tq // SWA_BLOCK
    cur = lambda i: (i, 0)
    prev = lambda i: (jnp.maximum(i * nb - 1, 0), 0)
    return pl.pallas_call(
        functools.partial(_swa_kernel, tq=tq),
        grid=(seq // tq,),
        in_specs=[pl.BlockSpec(memory_space=pltpu.SMEM),
                  pl.BlockSpec((tq, SWA_HEADS * SWA_HD), cur),
                  pl.BlockSpec((tq, LANES), cur),
                  pl.BlockSpec((SWA_BLOCK, LANES), prev),
                  pl.BlockSpec((tq, LANES), cur),
                  pl.BlockSpec((SWA_BLOCK, LANES), prev)],
        out_specs=pl.BlockSpec((tq, SWA_HEADS * SWA_HD), cur),
        out_shape=jax.ShapeDtypeStruct((seq, SWA_HEADS * SWA_HD), BF16),
        compiler_params=_params(("parallel",)),
        name="swa_attn",
    )(sinks, qs, ks, ks, vs, vs)


def _out_proj_kernel(*refs, n_a):
    a_refs = refs[:n_a]
    w_ref, h_ref, gpost_ref, gnext_ref, hout_ref, hn_ref = refs[n_a:]
    m = None
    k0 = 0
    for a_ref in a_refs:
        kw = a_ref.shape[1]
        part = jnp.dot(a_ref[...], w_ref[k0:k0 + kw, :], preferred_element_type=F32)
        m = part if m is None else m + part
        k0 += kw
    h = h_ref[...] + _rms(m, gpost_ref[...])
    hout_ref[...] = h
    hn_ref[...] = _rms(h, gnext_ref[...]).astype(BF16)


def _out_proj(a_list, w, h, g_post, g_next):
    seq, d = h.shape
    tm = min(seq, 512)
    row = lambda i: (i, 0)
    fixed = lambda i: (0, 0)
    return pl.pallas_call(
        functools.partial(_out_proj_kernel, n_a=len(a_list)),
        grid=(seq // tm,),
        in_specs=[pl.BlockSpec((tm, a.shape[1]), row) for a in a_list]
        + [pl.BlockSpec(w.shape, fixed), pl.BlockSpec((tm, d), row),
           pl.BlockSpec((1, d), fixed), pl.BlockSpec((1, d), fixed)],
        out_specs=[pl.BlockSpec((tm, d), row), pl.BlockSpec((tm, d), row)],
        out_shape=[jax.ShapeDtypeStruct((seq, d), F32), jax.ShapeDtypeStruct((seq, d), BF16)],
        compiler_params=_params(("parallel",)),
        name="out_proj",
    )(*a_list, w, h, g_post[None, :], g_next[None, :])


def _mlp_kernel(x_ref, wu_ref, wd_ref, h_ref, g_ref, o_ref, acc_ref):
    f = pl.program_id(1)
    a = jnp.maximum(jnp.dot(x_ref[...], wu_ref[...], preferred_element_type=F32), 0.0)
    part = jnp.dot((a * a).astype(BF16), wd_ref[...], preferred_element_type=F32)

    @pl.when(f == 0)
    def _():
        acc_ref[...] = part

    @pl.when(f > 0)
    def _():
        acc_ref[...] += part

    @pl.when(f == pl.num_programs(1) - 1)
    def _():
        o_ref[...] = h_ref[...] + _rms(acc_ref[...], g_ref[...])


def _mlp(xn, w_up, w_down, h, g_post):
    seq, d = h.shape
    d_ff = w_up.shape[1]
    tm = min(seq, 512)
    tf = min(d_ff, 1024)
    return pl.pallas_call(
        _mlp_kernel,
        grid=(seq // tm, d_ff // tf),
        in_specs=[pl.BlockSpec((tm, d), lambda i, f: (i, 0)),
                  pl.BlockSpec((d, tf), lambda i, f: (0, f)),
                  pl.BlockSpec((tf, d), lambda i, f: (f, 0)),
                  pl.BlockSpec((tm, d), lambda i, f: (i, 0)),
                  pl.BlockSpec((1, d), lambda i, f: (0, 0))],
        out_specs=pl.BlockSpec((tm, d), lambda i, f: (i, 0)),
        out_shape=jax.ShapeDtypeStruct((seq, d), F32),
        scratch_shapes=[pltpu.VMEM((tm, d), F32)],
        compiler_params=_params(("parallel", "arbitrary")),
        name="mlp",
    )(xn, w_up, w_down, h, g_post[None, :])


def _ple_kernel(h_ref, p_ref, wg_ref, b_ref, wp_ref, *rest, with_next):
    h = h_ref[...]
    z = jnp.dot(h.astype(BF16), wg_ref[...], preferred_element_type=F32) + b_ref[...]
    gate = 1.0 / (1.0 + jnp.exp(-z))
    e = jnp.dot(p_ref[...].astype(BF16), wp_ref[...], preferred_element_type=F32)
    out = h + gate * e
    if with_next:
        gnext_ref, o_ref, hn_ref = rest
        hn_ref[...] = _rms(out, gnext_ref[...]).astype(BF16)
    else:
        (o_ref,) = rest
    o_ref[...] = out


def _ple(h, p, w_gate, b_gate, w_proj, g_next=None):
    seq, d = h.shape
    tm = min(seq, 512)
    row = lambda i: (i, 0)
    fixed = lambda i: (0, 0)
    with_next = g_next is not None
    in_specs = [pl.BlockSpec((tm, d), row), pl.BlockSpec((tm, p.shape[1]), row),
                pl.BlockSpec(w_gate.shape, fixed), pl.BlockSpec((1, d), fixed),
                pl.BlockSpec(w_proj.shape, fixed)]
    args = [h, p, w_gate, b_gate[None, :], w_proj]
    out_specs = [pl.BlockSpec((tm, d), row)]
    out_shape = [jax.ShapeDtypeStruct((seq, d), F32)]
    if with_next:
        in_specs.append(pl.BlockSpec((1, d), fixed))
        args.append(g_next[None, :])
        out_specs.append(pl.BlockSpec((tm, d), row))
        out_shape.append(jax.ShapeDtypeStruct((seq, d), BF16))
    res = pl.pallas_call(
        functools.partial(_ple_kernel, with_next=with_next),
        grid=(seq // tm,),
        in_specs=in_specs, out_specs=out_specs, out_shape=out_shape,
        compiler_params=_params(("parallel",)),
        name="ple",
    )(*args)
    return res if with_next else (res[0], None)


def _odd_proj_kernel(x_ref, w_ref, cos_ref, sin_ref, o_ref, *rest, rope, scale, kmean, tm):
    t = jnp.dot(x_ref[...], w_ref[...], preferred_element_type=F32)
    if rope:
        t = _rope(t, cos_ref[...], sin_ref[...], MOBA_HD // 2)
    if kmean:
        (km_ref,) = rest
        for b in range(tm // MOBA_BLOCK):
            blk = t[b * MOBA_BLOCK:(b + 1) * MOBA_BLOCK, :]
            km_ref[b] = jnp.sum(blk, axis=0, keepdims=True) * (1.0 / MOBA_BLOCK)
    if scale != 1.0:
        t = t * scale
    o_ref[...] = t.astype(BF16)


def _odd_proj(xn, w, tabs_cos, tabs_sin, *, rope, scale=1.0, kmean=False):
    seq, d = xn.shape
    n = w.shape[1]
    tm = min(seq, 512)
    row = lambda i: (i, 0)
    fixed = lambda i: (0, 0)
    out_specs = [pl.BlockSpec((tm, n), row)]
    out_shape = [jax.ShapeDtypeStruct((seq, n), BF16)]
    if kmean:
        nb = tm // MOBA_BLOCK
        out_specs.append(pl.BlockSpec((nb, 1, n), lambda i: (i, 0, 0)))
        out_shape.append(jax.ShapeDtypeStruct((seq // MOBA_BLOCK, 1, n), F32))
    res = pl.pallas_call(
        functools.partial(_odd_proj_kernel, rope=rope, scale=scale, kmean=kmean, tm=tm),
        grid=(seq // tm,),
        in_specs=[pl.BlockSpec((tm, d), row), pl.BlockSpec(w.shape, fixed),
                  pl.BlockSpec((tm, LANES), lambda i: (i, 1)),
                  pl.BlockSpec((tm, LANES), lambda i: (i, 1))],
        out_specs=out_specs, out_shape=out_shape,
        compiler_params=_params(("parallel",)),
        name="odd_proj",
    )(xn, w, tabs_cos, tabs_sin)
    return res


def _moba_kernel(q_ref, k_ref, v_ref, km_ref, o_ref, m_ref, l_ref, acc_ref, *, tq, tk, nkb):
    i = pl.program_id(1)
    q = q_ref[...]
    _flash_init(m_ref, l_ref, acc_ref)

    km = km_ref[...]
    if nkb < 64:
        km = jnp.concatenate([km, jnp.zeros((64 - nkb, LANES), F32)], axis=0)
    km_hi = km.astype(BF16).astype(F32)
    r1 = km - km_hi
    km_mid = r1.astype(BF16).astype(F32)
    km_lo = (r1 - km_mid).astype(BF16).astype(F32)
    kcat = jnp.concatenate([km_hi, km_mid, km_lo, jnp.zeros((64, LANES), F32)], axis=0).astype(BF16)
    g3 = lax.dot_general(q, kcat, (((1,), (1,)), ((), ())), preferred_element_type=F32)
    g01 = g3[:, :LANES]
    gate = g01 + pltpu.roll(g01, 64, 1) + g3[:, LANES:]

    lane = lax.broadcasted_iota(jnp.int32, (tq, LANES), 1)
    qblk = (i * tq + lax.broadcasted_iota(jnp.int32, (tq, LANES), 0)) // MOBA_BLOCK
    gm = jnp.where(lane < qblk, gate, -jnp.inf)
    sel = lane == qblk
    for _ in range(MOBA_TOPK):
        mx = jnp.max(gm, axis=1, keepdims=True)
        is_max = (gm == mx) & (mx > -jnp.inf)
        idx = jnp.min(jnp.where(is_max, lane, LANES), axis=1, keepdims=True)
        pick = lane == idx
        sel = sel | pick
        gm = jnp.where(pick, -jnp.inf, gm)
    bias = jnp.where(sel, 0.0, MASK_BIAS).astype(BF16)
    qa = jnp.concatenate([q, bias], axis=1)

    kb_per_tile = tk // MOBA_BLOCK
    krow = lax.broadcasted_iota(jnp.int32, (tk, LANES), 0) // MOBA_BLOCK
    klane = lax.broadcasted_iota(jnp.int32, (tk, LANES), 1)

    def kv(j):
        rows = pl.ds(pl.multiple_of(j * tk, tk), tk)
        onehot = jnp.where(klane == krow + j * kb_per_tile, 1.0, 0.0).astype(BF16)
        return jnp.concatenate([k_ref[rows, :], onehot], axis=1), v_ref[rows, :]

    sub = tq // tk

    def past(jj, carry):
        for d in range(sub):
            k, v = kv(jj * sub + d)
            _flash_step(qa, k, v, m_ref, l_ref, acc_ref)
        return carry

    lax.fori_loop(0, i, past, 0)

    row = i * tq + lax.broadcasted_iota(jnp.int32, (tq, tk), 0)
    for d in range(sub):
        j = i * sub + d
        col = j * tk + lax.broadcasted_iota(jnp.int32, (tq, tk), 1)
        k, v = kv(j)
        _flash_step(qa, k, v, m_ref, l_ref, acc_ref, mask=col <= row)

    o_ref[...] = (acc_ref[...] / l_ref[...]).astype(BF16)


def _moba_attn(q, k, v, kmean):
    seq = q.shape[0]
    nkb = seq // MOBA_BLOCK
    assert nkb <= 64 and nkb % 8 == 0, "the gate matmul packs three 64-lane groups"
    tq = min(seq, 1024)
    tk = min(seq, 512)
    return pl.pallas_call(
        functools.partial(_moba_kernel, tq=tq, tk=tk, nkb=nkb),
        grid=(MOBA_HEADS, seq // tq),
        in_specs=[pl.BlockSpec((tq, LANES), lambda h, i: (i, h)),
                  pl.BlockSpec((seq, LANES), lambda h, i: (0, h)),
                  pl.BlockSpec((seq, LANES), lambda h, i: (0, h)),
                  pl.BlockSpec((nkb, LANES), lambda h, i: (0, h))],
        out_specs=pl.BlockSpec((tq, LANES), lambda h, i: (i, h)),
        out_shape=jax.ShapeDtypeStruct((seq, MOBA_HEADS * MOBA_HD), BF16),
        scratch_shapes=[pltpu.VMEM((tq, LANES), F32), pltpu.VMEM((tq, LANES), F32),
                        pltpu.VMEM((tq, LANES), F32)],
        compiler_params=_params(("parallel", "arbitrary")),
        name="moba_attn",
    )(q, k, v, kmean)


def _even_in_weight(w_in):
    d = w_in.shape[0]
    cq, ckv, kr, qs, ks, vs = jnp.split(w_in, [512, 1024, 1088, 2112, 2240], axis=1)
    return jnp.concatenate([cq, ckv, kr, jnp.zeros((d, 64), w_in.dtype), qs, ks, vs], axis=1).astype(BF16)


def _mla_q_weight(w_q_up):
    r = w_q_up.shape[0]
    w = w_q_up.reshape(r, MLA_HEADS, MLA_NOPE + MLA_ROPE)
    nope = w[:, :, :MLA_NOPE].reshape(r, MLA_HEADS * MLA_NOPE)
    rope = jnp.pad(w[:, :, MLA_NOPE:], ((0, 0), (0, 0), (0, LANES - MLA_ROPE))).reshape(r, MLA_HEADS * LANES)
    return jnp.concatenate([nope, rope], axis=1).astype(BF16)


def _mla_kv_weight(w_kv_up):
    r = w_kv_up.shape[0]
    w = w_kv_up.reshape(r, MLA_HEADS, MLA_NOPE + MLA_V)
    nope = w[:, :, :MLA_NOPE].reshape(r, MLA_HEADS * MLA_NOPE)
    val = w[:, :, MLA_NOPE:].reshape(r, MLA_HEADS * MLA_V)
    return jnp.concatenate([nope, val], axis=1).astype(BF16)


def kernel(x, p, positions, even_pre_g, even_w_in, mla_q_norm_g, mla_w_q_up, mla_kv_norm_g, mla_w_kv_up, swa_sinks, even_w_out, even_post_g, odd_pre_g, moba_w_qkv, odd_w_out, odd_post_g, mlp_pre_g, mlp_w_up, mlp_w_down, mlp_post_g, ple_w_gate, ple_b_gate, ple_w_proj):
    batch, seq, d = x.shape
    assert batch == 1
    h = x.reshape(seq, d)
    tabs_cos, tabs_sin = _rope_tables(positions, seq)

    xn = _prenorm(h, even_pre_g[0])
    cq, ckv, kr, qs, ks, vs = _even_proj(xn, _even_in_weight(even_w_in[0]), mla_q_norm_g[0],
                                         mla_kv_norm_g[0], tabs_cos, tabs_sin)
    qn, qr, kn, v = _mla_up(cq, ckv, _mla_q_weight(mla_w_q_up[0]), _mla_kv_weight(mla_w_kv_up[0]),
                            tabs_cos, tabs_sin)
    o_mla = _mla_attn(qn, qr, kn, kr, v)
    o_swa = _swa_attn(qs, ks, vs, swa_sinks[0])
    h, hn = _out_proj([o_mla, o_swa], even_w_out[0].astype(BF16), h, even_post_g[0], mlp_pre_g[0])
    h = _mlp(hn, mlp_w_up[0].astype(BF16), mlp_w_down[0].astype(BF16), h, mlp_post_g[0])
    h, hn = _ple(h, p[0, 0], ple_w_gate[0].astype(BF16), ple_b_gate[0], ple_w_proj[0].astype(BF16),
                 g_next=odd_pre_g[0])

    wq, wk, wv = jnp.split(moba_w_qkv[0].astype(BF16), 3, axis=1)
    (q,) = _odd_proj(hn, wq, tabs_cos, tabs_sin, rope=True, scale=MOBA_HD ** -0.5 * LOG2E)
    k, kmean = _odd_proj(hn, wk, tabs_cos, tabs_sin, rope=True, kmean=True)
    (v,) = _odd_proj(hn, wv, tabs_cos, tabs_sin, rope=False)
    o = _moba_attn(q, k, v, kmean.reshape(seq // MOBA_BLOCK, MOBA_HEADS * MOBA_HD))
    h, hn = _out_proj([o], odd_w_out[0].astype(BF16), h, odd_post_g[0], mlp_pre_g[1])
    h = _mlp(hn, mlp_w_up[1].astype(BF16), mlp_w_down[1].astype(BF16), h, mlp_post_g[1])
    h, _ = _ple(h, p[1, 0], ple_w_gate[1].astype(BF16), ple_b_gate[1], ple_w_proj[1].astype(BF16))
    return h.reshape(batch, seq, d)
```

```python
import functools

import jax
import jax.numpy as jnp
import numpy as np
from jax import lax
from jax.experimental import pallas as pl
from jax.experimental.pallas import tpu as pltpu

F32 = jnp.float32
BF16 = jnp.bfloat16

NORM_EPS = 1e-6
ROPE_THETA = 10000.0

MLA_HEADS = 8
MLA_Q_RANK = 512
MLA_KV_RANK = 512
MLA_NOPE = 128
MLA_ROPE = 64
MLA_V = 128
SWA_HEADS = 16
SWA_KV_HEADS = 2
SWA_HD = 64
SWA_BLOCK = 128
MOBA_HEADS = 16
MOBA_HD = 128
MOBA_BLOCK = 256
MOBA_TOPK = 3

LANES = 128
V7X_VMEM_BYTES = 64 * 1024 * 1024
VMEM_LIMIT = 56 * 1024 * 1024
MASK_BIAS = -1e9
LOG2E = 1.4426950408889634
FLASH_TQ = 2048
FLASH_TK = 512


def _params(sem):
    return pltpu.CompilerParams(dimension_semantics=sem, vmem_limit_bytes=VMEM_LIMIT)


def _rms(t, g):
    return t * lax.rsqrt(jnp.mean(t * t, axis=-1, keepdims=True) + NORM_EPS) * g


def _rope(t, cos, sin_signed, half):
    width = t.shape[1]
    reps = width // LANES
    if reps > 1:
        cos = jnp.concatenate([cos] * reps, axis=1)
        sin_signed = jnp.concatenate([sin_signed] * reps, axis=1)
    lane = lax.broadcasted_iota(jnp.int32, t.shape, 1)
    first = (lane % (2 * half)) < half
    partner = jnp.where(first, pltpu.roll(t, width - half, 1), pltpu.roll(t, half, 1))
    return t * cos + partner * sin_signed


def _tables_kernel(pos_ref, invf_ref, sign_ref, cos_ref, sin_ref):
    ang = pos_ref[...].astype(F32) * invf_ref[...]
    cos_ref[...] = jnp.cos(ang)
    sin_ref[...] = jnp.sin(ang) * sign_ref[...]


def _rope_tables(positions, seq):
    def inv_freq(d):
        half = d // 2
        return jnp.power(ROPE_THETA, -jnp.arange(half, dtype=F32) * (2.0 / d))

    f64, f128 = inv_freq(64), inv_freq(128)
    invf = jnp.concatenate([f64, f64, f64, f64, f128, f128])[None, :]
    sign = np.concatenate([-np.ones(32), np.ones(32), -np.ones(32), np.ones(32),
                           -np.ones(64), np.ones(64)]).astype(np.float32)[None, :]
    tm = min(seq, 1024)
    return pl.pallas_call(
        _tables_kernel,
        grid=(seq // tm,),
        in_specs=[pl.BlockSpec((tm, 1), lambda i: (i, 0)),
                  pl.BlockSpec((1, 256), lambda i: (0, 0)),
                  pl.BlockSpec((1, 256), lambda i: (0, 0))],
        out_specs=[pl.BlockSpec((tm, 256), lambda i: (i, 0)),
                   pl.BlockSpec((tm, 256), lambda i: (i, 0))],
        out_shape=[jax.ShapeDtypeStruct((seq, 256), F32)] * 2,
        compiler_params=_params(("parallel",)),
        name="rope_tables",
    )(positions.reshape(seq, 1), invf, jnp.asarray(sign))


_EVEN_COLS = (0, 512, 1024, 1152, 2176, 2304, 2432)


def _even_proj_kernel(x_ref, gpre_ref, w_ref, gq_ref, gkv_ref, cos_ref, sin_ref,
                      cq_ref, ckv_ref, kr_ref, qs_ref, ks_ref, vs_ref):
    x = _rms(x_ref[...], gpre_ref[...]).astype(BF16)
    cos, sin = cos_ref[...], sin_ref[...]
    c = _EVEN_COLS

    def mm(k):
        return jnp.dot(x, w_ref[:, c[k]:c[k + 1]], preferred_element_type=F32)

    cq_ref[...] = _rms(mm(0), gq_ref[...]).astype(BF16)
    ckv_ref[...] = _rms(mm(1), gkv_ref[...]).astype(BF16)
    kr_ref[...] = _rope(mm(2), cos, sin, 32).astype(BF16)
    qs_ref[...] = (_rope(mm(3), cos, sin, 32) * (SWA_HD ** -0.5)).astype(BF16)
    ks_ref[...] = _rope(mm(4), cos, sin, 32).astype(BF16)
    vs_ref[...] = mm(5).astype(BF16)


def _even_proj(x, g_pre, w, gq, gkv, tabs_cos, tabs_sin):
    seq, d = x.shape
    tm = min(seq, 512)
    widths = [_EVEN_COLS[k + 1] - _EVEN_COLS[k] for k in range(6)]
    row = lambda i: (i, 0)
    fixed = lambda i: (0, 0)
    return pl.pallas_call(
        _even_proj_kernel,
        grid=(seq // tm,),
        in_specs=[pl.BlockSpec((tm, d), row),
                  pl.BlockSpec((1, d), fixed),
                  pl.BlockSpec(w.shape, fixed),
                  pl.BlockSpec((1, 512), fixed),
                  pl.BlockSpec((1, 512), fixed),
                  pl.BlockSpec((tm, LANES), row),
                  pl.BlockSpec((tm, LANES), row)],
        out_specs=[pl.BlockSpec((tm, n), row) for n in widths],
        out_shape=[jax.ShapeDtypeStruct((seq, n), BF16) for n in widths],
        compiler_params=_params(("parallel",)),
        name="even_proj",
    )(x, g_pre[None, :], w, gq[None, :], gkv[None, :], tabs_cos, tabs_sin)


def _mla_up_kernel(cq_ref, ckv_ref, wq_ref, wkv_ref, cos_ref, sin_ref,
                   qn_ref, qr_ref, kn_ref, v_ref, *, scale):
    cq, ckv = cq_ref[...], ckv_ref[...]
    n = MLA_HEADS * LANES
    qn = jnp.dot(cq, wq_ref[:, :n], preferred_element_type=F32)
    qr = jnp.dot(cq, wq_ref[:, n:], preferred_element_type=F32)
    qn_ref[...] = (qn * scale).astype(BF16)
    qr_ref[...] = (_rope(qr, cos_ref[...], sin_ref[...], 32) * scale).astype(BF16)
    kn_ref[...] = jnp.dot(ckv, wkv_ref[:, :n], preferred_element_type=F32).astype(BF16)
    v_ref[...] = jnp.dot(ckv, wkv_ref[:, n:], preferred_element_type=F32).astype(BF16)


def _mla_up(cq, ckv, wq, wkv, tabs_cos, tabs_sin):
    seq = cq.shape[0]
    tm = min(seq, 512)
    n = MLA_HEADS * LANES
    row = lambda i: (i, 0)
    fixed = lambda i: (0, 0)
    scale = (MLA_NOPE + MLA_ROPE) ** -0.5 * LOG2E
    return pl.pallas_call(
        functools.partial(_mla_up_kernel, scale=scale),
        grid=(seq // tm,),
        in_specs=[pl.BlockSpec((tm, MLA_Q_RANK), row),
                  pl.BlockSpec((tm, MLA_KV_RANK), row),
                  pl.BlockSpec(wq.shape, fixed),
                  pl.BlockSpec(wkv.shape, fixed),
                  pl.BlockSpec((tm, LANES), row),
                  pl.BlockSpec((tm, LANES), row)],
        out_specs=[pl.BlockSpec((tm, n), row)] * 4,
        out_shape=[jax.ShapeDtypeStruct((seq, n), BF16)] * 4,
        compiler_params=_params(("parallel",)),
        name="mla_up",
    )(cq, ckv, wq, wkv, tabs_cos, tabs_sin)


def _flash_step(q, k, v1, m_ref, acc_ref, rows, mask=None):
    s = lax.dot_general(q, k, (((1,), (1,)), ((), ())), preferred_element_type=F32)
    if mask is not None:
        s = jnp.where(mask, s, -jnp.inf)
    m_old = m_ref[rows, :]
    m_new = jnp.maximum(m_old, jnp.max(s, axis=1, keepdims=True))
    alpha = jnp.exp2(m_old - m_new)
    p = jnp.exp2(s - jnp.tile(m_new, (1, s.shape[1] // LANES)))
    pv = jnp.dot(p.astype(BF16), v1[:, :LANES], preferred_element_type=F32)
    acc_ref[rows, :LANES] = alpha * acc_ref[rows, :LANES] + pv
    acc_ref[rows, LANES:] = alpha * acc_ref[rows, LANES:] + jnp.sum(p, axis=1, keepdims=True)
    m_ref[rows, :] = m_new


def _flash_attend(q, kv, i, tq, tk, m_ref, acc_ref, o_ref):
    m_ref[...] = jnp.full(m_ref.shape, -jnp.inf, F32)
    acc_ref[...] = jnp.zeros(acc_ref.shape, F32)
    sub = tq // tk

    def past(jj, carry):
        for d in range(sub):
            k, v1 = kv(jj * sub + d)
            _flash_step(q, k, v1, m_ref, acc_ref, slice(None))
        return carry

    lax.fori_loop(0, i, past, 0)

    for d in range(sub):
        n = tq - d * tk
        row = lax.broadcasted_iota(jnp.int32, (n, tk), 0)
        col = lax.broadcasted_iota(jnp.int32, (n, tk), 1)
        k, v1 = kv(i * sub + d)
        _flash_step(q[d * tk:], k, v1, m_ref, acc_ref, slice(d * tk, tq), mask=col <= row)

    acc = acc_ref[...]
    o_ref[...] = (acc[:, :LANES] / acc[:, LANES:]).astype(BF16)


def _mla_attn_kernel(qn_ref, qr_ref, kn_ref, kr_ref, v_ref, o_ref, m_ref, acc_ref, *, tq, tk):
    i = pl.program_id(1)
    q = jnp.concatenate([qn_ref[...], qr_ref[...]], axis=1)
    ones = jnp.ones((tk, LANES), BF16)

    def kv(j):
        rows = pl.ds(pl.multiple_of(j * tk, tk), tk)
        return (jnp.concatenate([kn_ref[rows, :], kr_ref[rows, :]], axis=1),
                jnp.concatenate([v_ref[rows, :], ones], axis=1))

    _flash_attend(q, kv, i, tq, tk, m_ref, acc_ref, o_ref)


def _mla_attn(qn, qr, kn, kr, v):
    seq = qn.shape[0]
    tq = min(seq, FLASH_TQ)
    tk = min(seq, FLASH_TK)
    return pl.pallas_call(
        functools.partial(_mla_attn_kernel, tq=tq, tk=tk),
        grid=(MLA_HEADS, seq // tq),
        in_specs=[pl.BlockSpec((tq, LANES), lambda h, i: (i, h)),
                  pl.BlockSpec((tq, LANES), lambda h, i: (i, h)),
                  pl.BlockSpec((seq, LANES), lambda h, i: (0, h)),
                  pl.BlockSpec((seq, LANES), lambda h, i: (0, 0)),
                  pl.BlockSpec((seq, LANES), lambda h, i: (0, h))],
        out_specs=pl.BlockSpec((tq, LANES), lambda h, i: (i, h)),
        out_shape=jax.ShapeDtypeStruct((seq, MLA_HEADS * MLA_V), BF16),
        scratch_shapes=[pltpu.VMEM((tq, LANES), F32), pltpu.VMEM((tq, 2 * LANES), F32)],
        compiler_params=_params(("parallel", "arbitrary")),
        name="mla_attn",
    )(qn, qr, kn, kr, v)


def _swa_kernel(sink_ref, q_ref, kc_ref, kp_ref, vc_ref, vp_ref, o_ref, *, tq):
    i = pl.program_id(0)
    nb = tq // SWA_BLOCK
    group = SWA_HEADS // SWA_KV_HEADS
    lane = lax.broadcasted_iota(jnp.int32, (2 * SWA_BLOCK, LANES), 1)
    qp = lax.broadcasted_iota(jnp.int32, (SWA_BLOCK, 2 * SWA_BLOCK), 0) + SWA_BLOCK
    kp = lax.broadcasted_iota(jnp.int32, (SWA_BLOCK, 2 * SWA_BLOCK), 1)
    band = (kp <= qp) & (qp - kp < SWA_BLOCK)

    def split(t, c):
        mine = jnp.where((lane >= c * SWA_HD) & (lane < (c + 1) * SWA_HD), t, 0.0)
        other = pltpu.roll(mine, SWA_HD, 1)
        lo, hi = (mine, other) if c == 0 else (other, mine)
        return jnp.concatenate([lo, hi], axis=0).astype(BF16)

    for b in range(nb):
        rows = slice(b * SWA_BLOCK, (b + 1) * SWA_BLOCK)
        if b == 0:
            k_prev, v_prev = kp_ref[...], vp_ref[...]
        else:
            prev = slice((b - 1) * SWA_BLOCK, b * SWA_BLOCK)
            k_prev, v_prev = kc_ref[prev, :], vc_ref[prev, :]
        kw = jnp.concatenate([k_prev, kc_ref[rows, :]], axis=0).astype(F32)
        vw = jnp.concatenate([v_prev, vc_ref[rows, :]], axis=0).astype(F32)
        first_key = jnp.where(i * nb + b == 0, SWA_BLOCK, 0)
        valid = band & (kp >= first_key)
        for c in range(SWA_KV_HEADS):
            kcat = split(kw, c)
            vcat = split(vw, c)
            for a in range(group // 2):
                cols = slice((c * (group // 2) + a) * LANES, (c * (group // 2) + a + 1) * LANES)
                s = lax.dot_general(q_ref[rows, cols], kcat, (((1,), (1,)), ((), ())),
                                    preferred_element_type=F32)
                ws = []
                for e in range(2):
                    sink = sink_ref[c * group + 2 * a + e]
                    se = jnp.where(valid, s[:, e * 2 * SWA_BLOCK:(e + 1) * 2 * SWA_BLOCK], -jnp.inf)
                    m = jnp.maximum(jnp.max(se, axis=1, keepdims=True), sink)
                    p = jnp.exp(se - m)
                    den = jnp.sum(p, axis=1, keepdims=True) + jnp.exp(sink - m)
                    ws.append((p / den).astype(BF16))
                w = jnp.concatenate(ws, axis=1)
                o_ref[rows, cols] = jnp.dot(w, vcat, preferred_element_type=F32).astype(BF16)


def _swa_attn(qs, ks, vs, sinks):
    seq = qs.shape[0]
    tq = min(seq, 512)
    nb = tq // SWA_BLOCK
    cur = lambda i: (i, 0)
    prev = lambda i: (jnp.maximum(i * nb - 1, 0), 0)
    return pl.pallas_call(
        functools.partial(_swa_kernel, tq=tq),
        grid=(seq // tq,),
        in_specs=[pl.BlockSpec(memory_space=pltpu.SMEM),
                  pl.BlockSpec((tq, SWA_HEADS * SWA_HD), cur),
                  pl.BlockSpec((tq, LANES), cur),
                  pl.BlockSpec((SWA_BLOCK, LANES), prev),
                  pl.BlockSpec((tq, LANES), cur),
                  pl.BlockSpec((SWA_BLOCK, LANES), prev)],
        out_specs=pl.BlockSpec((tq, SWA_HEADS * SWA_HD), cur),
        out_shape=jax.ShapeDtypeStruct((seq, SWA_HEADS * SWA_HD), BF16),
        compiler_params=_params(("parallel",)),
        name="swa_attn",
    )(sinks, qs, ks, ks, vs, vs)


def _out_proj_kernel(*refs, n_a):
    a_refs = refs[:n_a]
    w_ref, h_ref, gpost_ref, gnext_ref, hout_ref, hn_ref = refs[n_a:]
    m = None
    k0 = 0
    for a_ref in a_refs:
        kw = a_ref.shape[1]
        part = jnp.dot(a_ref[...], w_ref[k0:k0 + kw, :], preferred_element_type=F32)
        m = part if m is None else m + part
        k0 += kw
    h = h_ref[...] + _rms(m, gpost_ref[...])
    hout_ref[...] = h
    hn_ref[...] = _rms(h, gnext_ref[...]).astype(BF16)


def _out_proj(a_list, w, h, g_post, g_next):
    seq, d = h.shape
    tm = min(seq, 512)
    row = lambda i: (i, 0)
    fixed = lambda i: (0, 0)
    return pl.pallas_call(
        functools.partial(_out_proj_kernel, n_a=len(a_list)),
        grid=(seq // tm,),
        in_specs=[pl.BlockSpec((tm, a.shape[1]), row) for a in a_list]
        + [pl.BlockSpec(w.shape, fixed), pl.BlockSpec((tm, d), row),
           pl.BlockSpec((1, d), fixed), pl.BlockSpec((1, d), fixed)],
        out_specs=[pl.BlockSpec((tm, d), row), pl.BlockSpec((tm, d), row)],
        out_shape=[jax.ShapeDtypeStruct((seq, d), F32), jax.ShapeDtypeStruct((seq, d), BF16)],
        compiler_params=_params(("parallel",)),
        name="out_proj",
    )(*a_list, w, h, g_post[None, :], g_next[None, :])


def _mlp_kernel(x_ref, wu_ref, wd_ref, h_ref, g_ref, o_ref, acc_ref):
    f = pl.program_id(1)

    @pl.when(f == 0)
    def _():
        acc_ref[...] = jnp.zeros(acc_ref.shape, F32)

    a = jnp.maximum(jnp.dot(x_ref[...], wu_ref[...], preferred_element_type=F32), 0.0)
    acc_ref[...] += jnp.dot((a * a).astype(BF16), wd_ref[...], preferred_element_type=F32)

    @pl.when(f == pl.num_programs(1) - 1)
    def _():
        o_ref[...] = h_ref[...] + _rms(acc_ref[...], g_ref[...])


def _mlp(xn, w_up, w_down, h, g_post):
    seq, d = h.shape
    d_ff = w_up.shape[1]
    tm = min(seq, 512)
    tf = min(d_ff, 1024)
    return pl.pallas_call(
        _mlp_kernel,
        grid=(seq // tm, d_ff // tf),
        in_specs=[pl.BlockSpec((tm, d), lambda i, f: (i, 0)),
                  pl.BlockSpec((d, tf), lambda i, f: (0, f)),
                  pl.BlockSpec((tf, d), lambda i, f: (f, 0)),
                  pl.BlockSpec((tm, d), lambda i, f: (i, 0)),
                  pl.BlockSpec((1, d), lambda i, f: (0, 0))],
        out_specs=pl.BlockSpec((tm, d), lambda i, f: (i, 0)),
        out_shape=jax.ShapeDtypeStruct((seq, d), F32),
        scratch_shapes=[pltpu.VMEM((tm, d), F32)],
        compiler_params=_params(("parallel", "arbitrary")),
        name="mlp",
    )(xn, w_up, w_down, h, g_post[None, :])


def _ple_kernel(h_ref, p_ref, wg_ref, b_ref, wp_ref, *rest, with_next):
    h = h_ref[...]
    z = jnp.dot(h.astype(BF16), wg_ref[...], preferred_element_type=F32) + b_ref[...]
    gate = 1.0 / (1.0 + jnp.exp(-z))
    e = jnp.dot(p_ref[...].astype(BF16), wp_ref[...], preferred_element_type=F32)
    out = h + gate * e
    if with_next:
        gnext_ref, o_ref, hn_ref = rest
        hn_ref[...] = _rms(out, gnext_ref[...]).astype(BF16)
    else:
        (o_ref,) = rest
    o_ref[...] = out


def _ple(h, p, w_gate, b_gate, w_proj, g_next=None):
    seq, d = h.shape
    tm = min(seq, 512)
    row = lambda i: (i, 0)
    fixed = lambda i: (0, 0)
    with_next = g_next is not None
    in_specs = [pl.BlockSpec((tm, d), row), pl.BlockSpec((tm, p.shape[1]), row),
                pl.BlockSpec(w_gate.shape, fixed), pl.BlockSpec((1, d), fixed),
                pl.BlockSpec(w_proj.shape, fixed)]
    args = [h, p, w_gate, b_gate[None, :], w_proj]
    out_specs = [pl.BlockSpec((tm, d), row)]
    out_shape = [jax.ShapeDtypeStruct((seq, d), F32)]
    if with_next:
        in_specs.append(pl.BlockSpec((1, d), fixed))
        args.append(g_next[None, :])
        out_specs.append(pl.BlockSpec((tm, d), row))
        out_shape.append(jax.ShapeDtypeStruct((seq, d), BF16))
    res = pl.pallas_call(
        functools.partial(_ple_kernel, with_next=with_next),
        grid=(seq // tm,),
        in_specs=in_specs, out_specs=out_specs, out_shape=out_shape,
        compiler_params=_params(("parallel",)),
        name="ple",
    )(*args)
    return res if with_next else (res[0], None)


def _odd_proj_kernel(x_ref, w_ref, cos_ref, sin_ref, o_ref, *rest, rope, scale, kmean, tm):
    t = jnp.dot(x_ref[...], w_ref[...], preferred_element_type=F32)
    if rope:
        t = _rope(t, cos_ref[...], sin_ref[...], MOBA_HD // 2)
    if kmean:
        (km_ref,) = rest
        for b in range(tm // MOBA_BLOCK):
            blk = t[b * MOBA_BLOCK:(b + 1) * MOBA_BLOCK, :]
            km_ref[b] = jnp.sum(blk, axis=0, keepdims=True) * (1.0 / MOBA_BLOCK)
    if scale != 1.0:
        t = t * scale
    o_ref[...] = t.astype(BF16)


def _odd_proj(xn, w, col, tabs_cos, tabs_sin, *, rope, scale=1.0, kmean=False):
    seq, d = xn.shape
    n = d
    tm = min(seq, 512)
    row = lambda i: (i, 0)
    fixed = lambda i: (0, 0)
    out_specs = [pl.BlockSpec((tm, n), row)]
    out_shape = [jax.ShapeDtypeStruct((seq, n), BF16)]
    if kmean:
        nb = tm // MOBA_BLOCK
        out_specs.append(pl.BlockSpec((nb, 1, n), lambda i: (i, 0, 0)))
        out_shape.append(jax.ShapeDtypeStruct((seq // MOBA_BLOCK, 1, n), F32))
    res = pl.pallas_call(
        functools.partial(_odd_proj_kernel, rope=rope, scale=scale, kmean=kmean, tm=tm),
        grid=(seq // tm,),
        in_specs=[pl.BlockSpec((tm, d), row), pl.BlockSpec((d, n), lambda i: (0, col)),
                  pl.BlockSpec((tm, LANES), lambda i: (i, 1)),
                  pl.BlockSpec((tm, LANES), lambda i: (i, 1))],
        out_specs=out_specs, out_shape=out_shape,
        compiler_params=_params(("parallel",)),
        name="odd_proj",
    )(xn, w, tabs_cos, tabs_sin)
    return res


def _moba_kernel(q_ref, k_ref, v_ref, km_ref, o_ref, m_ref, acc_ref, *, tq, tk, nkb):
    i = pl.program_id(1)
    q = q_ref[...]

    km = km_ref[...]
    if nkb < 64:
        km = jnp.concatenate([km, jnp.zeros((64 - nkb, LANES), F32)], axis=0)
    km_hi = km.astype(BF16).astype(F32)
    r1 = km - km_hi
    km_mid = r1.astype(BF16).astype(F32)
    km_lo = (r1 - km_mid).astype(BF16).astype(F32)
    kcat = jnp.concatenate([km_hi, km_mid, km_lo, jnp.zeros((64, LANES), F32)], axis=0).astype(BF16)
    g3 = lax.dot_general(q, kcat, (((1,), (1,)), ((), ())), preferred_element_type=F32)
    g01 = g3[:, :LANES]
    gate = g01 + pltpu.roll(g01, 64, 1) + g3[:, LANES:]

    lane = lax.broadcasted_iota(jnp.int32, (tq, LANES), 1)
    qblk = (i * tq + lax.broadcasted_iota(jnp.int32, (tq, LANES), 0)) // MOBA_BLOCK
    gm = jnp.where(lane < qblk, gate, -jnp.inf)
    sel = lane == qblk
    for _ in range(MOBA_TOPK):
        mx = jnp.max(gm, axis=1, keepdims=True)
        is_max = (gm == mx) & (mx > -jnp.inf)
        idx = jnp.min(jnp.where(is_max, lane, LANES), axis=1, keepdims=True)
        pick = lane == idx
        sel = sel | pick
        gm = jnp.where(pick, -jnp.inf, gm)
    bias = jnp.where(sel, 0.0, MASK_BIAS).astype(BF16)
    qa = jnp.concatenate([q, bias], axis=1)

    kb_per_tile = tk // MOBA_BLOCK
    krow = lax.broadcasted_iota(jnp.int32, (tk, LANES), 0) // MOBA_BLOCK
    klane = lax.broadcasted_iota(jnp.int32, (tk, LANES), 1)

    ones = jnp.ones((tk, LANES), BF16)

    def kv(j):
        rows = pl.ds(pl.multiple_of(j * tk, tk), tk)
        onehot = jnp.where(klane == krow + j * kb_per_tile, 1.0, 0.0).astype(BF16)
        return (jnp.concatenate([k_ref[rows, :], onehot], axis=1),
                jnp.concatenate([v_ref[rows, :], ones], axis=1))

    _flash_attend(qa, kv, i, tq, tk, m_ref, acc_ref, o_ref)


def _moba_attn(q, k, v, kmean):
    seq = q.shape[0]
    nkb = seq // MOBA_BLOCK
    assert nkb <= 64 and nkb % 8 == 0, "the gate matmul packs three 64-lane groups"
    tq = min(seq, FLASH_TQ)
    tk = min(seq, FLASH_TK)
    return pl.pallas_call(
        functools.partial(_moba_kernel, tq=tq, tk=tk, nkb=nkb),
        grid=(MOBA_HEADS, seq // tq),
        in_specs=[pl.BlockSpec((tq, LANES), lambda h, i: (i, h)),
                  pl.BlockSpec((seq, LANES), lambda h, i: (0, h)),
                  pl.BlockSpec((seq, LANES), lambda h, i: (0, h)),
                  pl.BlockSpec((nkb, LANES), lambda h, i: (0, h))],
        out_specs=pl.BlockSpec((tq, LANES), lambda h, i: (i, h)),
        out_shape=jax.ShapeDtypeStruct((seq, MOBA_HEADS * MOBA_HD), BF16),
        scratch_shapes=[pltpu.VMEM((tq, LANES), F32), pltpu.VMEM((tq, 2 * LANES), F32)],
        compiler_params=_params(("parallel", "arbitrary")),
        name="moba_attn",
    )(q, k, v, kmean)


def _even_in_weight(w_in):
    d = w_in.shape[0]
    cq, ckv, kr, qs, ks, vs = jnp.split(w_in, [512, 1024, 1088, 2112, 2240], axis=1)
    return jnp.concatenate([cq, ckv, kr, jnp.zeros((d, 64), w_in.dtype), qs, ks, vs], axis=1).astype(BF16)


def _mla_q_weight(w_q_up):
    r = w_q_up.shape[0]
    w = w_q_up.reshape(r, MLA_HEADS, MLA_NOPE + MLA_ROPE)
    nope = w[:, :, :MLA_NOPE].reshape(r, MLA_HEADS * MLA_NOPE)
    rope = jnp.pad(w[:, :, MLA_NOPE:], ((0, 0), (0, 0), (0, LANES - MLA_ROPE))).reshape(r, MLA_HEADS * LANES)
    return jnp.concatenate([nope, rope], axis=1).astype(BF16)


def _mla_kv_weight(w_kv_up):
    r = w_kv_up.shape[0]
    w = w_kv_up.reshape(r, MLA_HEADS, MLA_NOPE + MLA_V)
    nope = w[:, :, :MLA_NOPE].reshape(r, MLA_HEADS * MLA_NOPE)
    val = w[:, :, MLA_NOPE:].reshape(r, MLA_HEADS * MLA_V)
    return jnp.concatenate([nope, val], axis=1).astype(BF16)


def kernel(x, p, positions, even_pre_g, even_w_in, mla_q_norm_g, mla_w_q_up, mla_kv_norm_g, mla_w_kv_up, swa_sinks, even_w_out, even_post_g, odd_pre_g, moba_w_qkv, odd_w_out, odd_post_g, mlp_pre_g, mlp_w_up, mlp_w_down, mlp_post_g, ple_w_gate, ple_b_gate, ple_w_proj):
    batch, seq, d = x.shape
    assert batch == 1
    h = x.reshape(seq, d)
    tabs_cos, tabs_sin = _rope_tables(positions, seq)

    cq, ckv, kr, qs, ks, vs = _even_proj(h, even_pre_g[0], _even_in_weight(even_w_in[0]),
                                         mla_q_norm_g[0], mla_kv_norm_g[0], tabs_cos, tabs_sin)
    qn, qr, kn, v = _mla_up(cq, ckv, _mla_q_weight(mla_w_q_up[0]), _mla_kv_weight(mla_w_kv_up[0]),
                            tabs_cos, tabs_sin)
    o_mla = _mla_attn(qn, qr, kn, kr, v)
    o_swa = _swa_attn(qs, ks, vs, swa_sinks[0])
    h, hn = _out_proj([o_mla, o_swa], even_w_out[0].astype(BF16), h, even_post_g[0], mlp_pre_g[0])
    h = _mlp(hn, mlp_w_up[0].astype(BF16), mlp_w_down[0].astype(BF16), h, mlp_post_g[0])
    h, hn = _ple(h, p[0, 0], ple_w_gate[0].astype(BF16), ple_b_gate[0], ple_w_proj[0].astype(BF16),
                 g_next=odd_pre_g[0])

    w_qkv = moba_w_qkv[0].astype(BF16)
    (q,) = _odd_proj(hn, w_qkv, 0, tabs_cos, tabs_sin, rope=True, scale=MOBA_HD ** -0.5 * LOG2E)
    k, kmean = _odd_proj(hn, w_qkv, 1, tabs_cos, tabs_sin, rope=True, kmean=True)
    (v,) = _odd_proj(hn, w_qkv, 2, tabs_cos, tabs_sin, rope=False)
    o = _moba_attn(q, k, v, kmean.reshape(seq // MOBA_BLOCK, MOBA_HEADS * MOBA_HD))
    h, hn = _out_proj([o], odd_w_out[0].astype(BF16), h, odd_post_g[0], mlp_pre_g[1])
    h = _mlp(hn, mlp_w_up[1].astype(BF16), mlp_w_down[1].astype(BF16), h, mlp_post_g[1])
    h, _ = _ple(h, p[1, 0], ple_w_gate[1].astype(BF16), ple_b_gate[1], ple_w_proj[1].astype(BF16))
    return h.reshape(batch, seq, d)
```

```python
import functools

import jax
import jax.numpy as jnp
import numpy as np
from jax import lax
from jax.experimental import pallas as pl
from jax.experimental.pallas import tpu as pltpu
from jax.experimental.pallas import tpu_sc as plsc

F32 = jnp.float32
BF16 = jnp.bfloat16

NORM_EPS = 1e-6
ROPE_THETA = 10000.0

MLA_HEADS = 8
MLA_Q_RANK = 512
MLA_KV_RANK = 512
MLA_NOPE = 128
MLA_ROPE = 64
MLA_V = 128
SWA_HEADS = 16
SWA_KV_HEADS = 2
SWA_HD = 64
SWA_BLOCK = 128
MOBA_HEADS = 16
MOBA_HD = 128
MOBA_BLOCK = 256
MOBA_TOPK = 3

LANES = 128
V7X_VMEM_BYTES = 64 * 1024 * 1024
VMEM_LIMIT = 56 * 1024 * 1024
MASK_BIAS = -1e9
LOG2E = 1.4426950408889634
FLASH_TQ = 2048
FLASH_TK = 512


def _params(sem):
    return pltpu.CompilerParams(dimension_semantics=sem, vmem_limit_bytes=VMEM_LIMIT)


def _rms(t, g):
    return t * lax.rsqrt(jnp.mean(t * t, axis=-1, keepdims=True) + NORM_EPS) * g


def _rope(t, cos, sin_signed, half):
    width = t.shape[1]
    reps = width // LANES
    if reps > 1:
        cos = jnp.concatenate([cos] * reps, axis=1)
        sin_signed = jnp.concatenate([sin_signed] * reps, axis=1)
    lane = lax.broadcasted_iota(jnp.int32, t.shape, 1)
    first = (lane % (2 * half)) < half
    partner = jnp.where(first, pltpu.roll(t, width - half, 1), pltpu.roll(t, half, 1))
    return t * cos + partner * sin_signed


def _tables_kernel(pos_ref, invf_ref, sign_ref, cos_ref, sin_ref):
    ang = pos_ref[...].astype(F32) * invf_ref[...]
    cos_ref[...] = jnp.cos(ang)
    sin_ref[...] = jnp.sin(ang) * sign_ref[...]


def _rope_tables(positions, seq):
    def inv_freq(d):
        half = d // 2
        return jnp.power(ROPE_THETA, -jnp.arange(half, dtype=F32) * (2.0 / d))

    f64, f128 = inv_freq(64), inv_freq(128)
    invf = jnp.concatenate([f64, f64, f64, f64, f128, f128])[None, :]
    sign = np.concatenate([-np.ones(32), np.ones(32), -np.ones(32), np.ones(32),
                           -np.ones(64), np.ones(64)]).astype(np.float32)[None, :]
    tm = min(seq, 1024)
    return pl.pallas_call(
        _tables_kernel,
        grid=(seq // tm,),
        in_specs=[pl.BlockSpec((tm, 1), lambda i: (i, 0)),
                  pl.BlockSpec((1, 256), lambda i: (0, 0)),
                  pl.BlockSpec((1, 256), lambda i: (0, 0))],
        out_specs=[pl.BlockSpec((tm, 256), lambda i: (i, 0)),
                   pl.BlockSpec((tm, 256), lambda i: (i, 0))],
        out_shape=[jax.ShapeDtypeStruct((seq, 256), F32)] * 2,
        compiler_params=_params(("parallel",)),
        name="rope_tables",
    )(positions.reshape(seq, 1), invf, jnp.asarray(sign))


_EVEN_COLS = (0, 512, 1024, 1152, 2176, 2304, 2432)


def _even_proj_kernel(x_ref, gpre_ref, w_ref, gq_ref, gkv_ref, cos_ref, sin_ref,
                      cq_ref, ckv_ref, kr_ref, qs_ref, ks_ref, vs_ref):
    x = _rms(x_ref[...], gpre_ref[...]).astype(BF16)
    cos, sin = cos_ref[...], sin_ref[...]
    c = _EVEN_COLS

    def mm(k):
        return jnp.dot(x, w_ref[:, c[k]:c[k + 1]], preferred_element_type=F32)

    cq_ref[...] = _rms(mm(0), gq_ref[...]).astype(BF16)
    ckv_ref[...] = _rms(mm(1), gkv_ref[...]).astype(BF16)
    kr_ref[...] = _rope(mm(2), cos, sin, 32).astype(BF16)
    qs_ref[...] = (_rope(mm(3), cos, sin, 32) * (SWA_HD ** -0.5)).astype(BF16)
    ks_ref[...] = _rope(mm(4), cos, sin, 32).astype(BF16)
    vs_ref[...] = mm(5).astype(BF16)


def _even_proj(x, g_pre, w, gq, gkv, tabs_cos, tabs_sin):
    seq, d = x.shape
    tm = min(seq, 512)
    widths = [_EVEN_COLS[k + 1] - _EVEN_COLS[k] for k in range(6)]
    row = lambda i: (i, 0)
    fixed = lambda i: (0, 0)
    return pl.pallas_call(
        _even_proj_kernel,
        grid=(seq // tm,),
        in_specs=[pl.BlockSpec((tm, d), row),
                  pl.BlockSpec((1, d), fixed),
                  pl.BlockSpec(w.shape, fixed),
                  pl.BlockSpec((1, 512), fixed),
                  pl.BlockSpec((1, 512), fixed),
                  pl.BlockSpec((tm, LANES), row),
                  pl.BlockSpec((tm, LANES), row)],
        out_specs=[pl.BlockSpec((tm, n), row) for n in widths],
        out_shape=[jax.ShapeDtypeStruct((seq, n), BF16) for n in widths],
        compiler_params=_params(("parallel",)),
        name="even_proj",
    )(x, g_pre[None, :], w, gq[None, :], gkv[None, :], tabs_cos, tabs_sin)


def _mla_up_kernel(cq_ref, ckv_ref, wq_ref, wkv_ref, cos_ref, sin_ref,
                   qn_ref, qr_ref, kn_ref, v_ref, *, scale):
    cq, ckv = cq_ref[...], ckv_ref[...]
    n = MLA_HEADS * LANES
    qn = jnp.dot(cq, wq_ref[:, :n], preferred_element_type=F32)
    qr = jnp.dot(cq, wq_ref[:, n:], preferred_element_type=F32)
    qn_ref[...] = (qn * scale).astype(BF16)
    qr_ref[...] = (_rope(qr, cos_ref[...], sin_ref[...], 32) * scale).astype(BF16)
    kn_ref[...] = jnp.dot(ckv, wkv_ref[:, :n], preferred_element_type=F32).astype(BF16)
    v_ref[...] = jnp.dot(ckv, wkv_ref[:, n:], preferred_element_type=F32).astype(BF16)


def _mla_up(cq, ckv, wq, wkv, tabs_cos, tabs_sin):
    seq = cq.shape[0]
    tm = min(seq, 512)
    n = MLA_HEADS * LANES
    row = lambda i: (i, 0)
    fixed = lambda i: (0, 0)
    scale = (MLA_NOPE + MLA_ROPE) ** -0.5 * LOG2E
    return pl.pallas_call(
        functools.partial(_mla_up_kernel, scale=scale),
        grid=(seq // tm,),
        in_specs=[pl.BlockSpec((tm, MLA_Q_RANK), row),
                  pl.BlockSpec((tm, MLA_KV_RANK), row),
                  pl.BlockSpec(wq.shape, fixed),
                  pl.BlockSpec(wkv.shape, fixed),
                  pl.BlockSpec((tm, LANES), row),
                  pl.BlockSpec((tm, LANES), row)],
        out_specs=[pl.BlockSpec((tm, n), row)] * 4,
        out_shape=[jax.ShapeDtypeStruct((seq, n), BF16)] * 4,
        compiler_params=_params(("parallel",)),
        name="mla_up",
    )(cq, ckv, wq, wkv, tabs_cos, tabs_sin)


def _flash_step(q, k, v1, m_ref, acc_ref, rows, mask=None):
    s = lax.dot_general(q, k, (((1,), (1,)), ((), ())), preferred_element_type=F32)
    if mask is not None:
        s = jnp.where(mask, s, -jnp.inf)
    m_old = m_ref[rows, :]
    m_new = jnp.maximum(m_old, jnp.max(s, axis=1, keepdims=True))
    alpha = jnp.exp2(m_old - m_new)
    p = jnp.exp2(s - jnp.tile(m_new, (1, s.shape[1] // LANES)))
    pv = jnp.dot(p.astype(BF16), v1[:, :LANES], preferred_element_type=F32)
    acc_ref[rows, :LANES] = alpha * acc_ref[rows, :LANES] + pv
    acc_ref[rows, LANES:] = alpha * acc_ref[rows, LANES:] + jnp.sum(p, axis=1, keepdims=True)
    m_ref[rows, :] = m_new


def _flash_attend(q, kv, i, tq, tk, m_ref, acc_ref, o_ref):
    m_ref[...] = jnp.full(m_ref.shape, -jnp.inf, F32)
    acc_ref[...] = jnp.zeros(acc_ref.shape, F32)
    sub = tq // tk

    def past(jj, carry):
        for d in range(sub):
            k, v1 = kv(jj * sub + d)
            _flash_step(q, k, v1, m_ref, acc_ref, slice(None))
        return carry

    lax.fori_loop(0, i, past, 0)

    for d in range(sub):
        n = tq - d * tk
        row = lax.broadcasted_iota(jnp.int32, (n, tk), 0)
        col = lax.broadcasted_iota(jnp.int32, (n, tk), 1)
        k, v1 = kv(i * sub + d)
        _flash_step(q[d * tk:], k, v1, m_ref, acc_ref, slice(d * tk, tq), mask=col <= row)

    acc = acc_ref[...]
    o_ref[...] = (acc[:, :LANES] / acc[:, LANES:]).astype(BF16)


def _mla_attn_kernel(qn_ref, qr_ref, kn_ref, kr_ref, v_ref, o_ref, m_ref, acc_ref, *, tq, tk):
    i = pl.program_id(1)
    q = jnp.concatenate([qn_ref[...], qr_ref[...]], axis=1)
    ones = jnp.ones((tk, LANES), BF16)

    def kv(j):
        rows = pl.ds(pl.multiple_of(j * tk, tk), tk)
        return (jnp.concatenate([kn_ref[rows, :], kr_ref[rows, :]], axis=1),
                jnp.concatenate([v_ref[rows, :], ones], axis=1))

    _flash_attend(q, kv, i, tq, tk, m_ref, acc_ref, o_ref)


def _mla_attn(qn, qr, kn, kr, v):
    seq = qn.shape[0]
    tq = min(seq, FLASH_TQ)
    tk = min(seq, FLASH_TK)
    return pl.pallas_call(
        functools.partial(_mla_attn_kernel, tq=tq, tk=tk),
        grid=(MLA_HEADS, seq // tq),
        in_specs=[pl.BlockSpec((tq, LANES), lambda h, i: (i, h)),
                  pl.BlockSpec((tq, LANES), lambda h, i: (i, h)),
                  pl.BlockSpec((seq, LANES), lambda h, i: (0, h)),
                  pl.BlockSpec((seq, LANES), lambda h, i: (0, 0)),
                  pl.BlockSpec((seq, LANES), lambda h, i: (0, h))],
        out_specs=pl.BlockSpec((tq, LANES), lambda h, i: (i, h)),
        out_shape=jax.ShapeDtypeStruct((seq, MLA_HEADS * MLA_V), BF16),
        scratch_shapes=[pltpu.VMEM((tq, LANES), F32), pltpu.VMEM((tq, 2 * LANES), F32)],
        compiler_params=_params(("parallel", "arbitrary")),
        name="mla_attn",
    )(qn, qr, kn, kr, v)


def _swa_kernel(sink_ref, q_ref, kc_ref, kp_ref, vc_ref, vp_ref, o_ref, *, tq):
    i = pl.program_id(0)
    nb = tq // SWA_BLOCK
    group = SWA_HEADS // SWA_KV_HEADS
    lane = lax.broadcasted_iota(jnp.int32, (2 * SWA_BLOCK, LANES), 1)
    qp = lax.broadcasted_iota(jnp.int32, (SWA_BLOCK, 2 * SWA_BLOCK), 0) + SWA_BLOCK
    kp = lax.broadcasted_iota(jnp.int32, (SWA_BLOCK, 2 * SWA_BLOCK), 1)
    band = (kp <= qp) & (qp - kp < SWA_BLOCK)

    def split(t, c):
        mine = jnp.where((lane >= c * SWA_HD) & (lane < (c + 1) * SWA_HD), t, 0.0)
        other = pltpu.roll(mine, SWA_HD, 1)
        lo, hi = (mine, other) if c == 0 else (other, mine)
        return jnp.concatenate([lo, hi], axis=0).astype(BF16)

    for b in range(nb):
        rows = slice(b * SWA_BLOCK, (b + 1) * SWA_BLOCK)
        if b == 0:
            k_prev, v_prev = kp_ref[...], vp_ref[...]
        else:
            prev = slice((b - 1) * SWA_BLOCK, b * SWA_BLOCK)
            k_prev, v_prev = kc_ref[prev, :], vc_ref[prev, :]
        kw = jnp.concatenate([k_prev, kc_ref[rows, :]], axis=0).astype(F32)
        vw = jnp.concatenate([v_prev, vc_ref[rows, :]], axis=0).astype(F32)
        first_key = jnp.where(i * nb + b == 0, SWA_BLOCK, 0)
        valid = band & (kp >= first_key)
        for c in range(SWA_KV_HEADS):
            kcat = split(kw, c)
            vcat = split(vw, c)
            for a in range(group // 2):
                cols = slice((c * (group // 2) + a) * LANES, (c * (group // 2) + a + 1) * LANES)
                s = lax.dot_general(q_ref[rows, cols], kcat, (((1,), (1,)), ((), ())),
                                    preferred_element_type=F32)
                ws = []
                for e in range(2):
                    sink = sink_ref[c * group + 2 * a + e]
                    se = jnp.where(valid, s[:, e * 2 * SWA_BLOCK:(e + 1) * 2 * SWA_BLOCK], -jnp.inf)
                    m = jnp.maximum(jnp.max(se, axis=1, keepdims=True), sink)
                    p = jnp.exp(se - m)
                    den = jnp.sum(p, axis=1, keepdims=True) + jnp.exp(sink - m)
                    ws.append((p / den).astype(BF16))
                w = jnp.concatenate(ws, axis=1)
                o_ref[rows, cols] = jnp.dot(w, vcat, preferred_element_type=F32).astype(BF16)


def _swa_attn(qs, ks, vs, sinks):
    seq = qs.shape[0]
    tq = min(seq, 512)
    nb = tq // SWA_BLOCK
    cur = lambda i: (i, 0)
    prev = lambda i: (jnp.maximum(i * nb - 1, 0), 0)
    return pl.pallas_call(
        functools.partial(_swa_kernel, tq=tq),
        grid=(seq // tq,),
        in_specs=[pl.BlockSpec(memory_space=pltpu.SMEM),
                  pl.BlockSpec((tq, SWA_HEADS * SWA_HD), cur),
                  pl.BlockSpec((tq, LANES), cur),
                  pl.BlockSpec((SWA_BLOCK, LANES), prev),
                  pl.BlockSpec((tq, LANES), cur),
                  pl.BlockSpec((SWA_BLOCK, LANES), prev)],
        out_specs=pl.BlockSpec((tq, SWA_HEADS * SWA_HD), cur),
        out_shape=jax.ShapeDtypeStruct((seq, SWA_HEADS * SWA_HD), BF16),
        compiler_params=_params(("parallel",)),
        name="swa_attn",
    )(sinks, qs, ks, ks, vs, vs)


def _out_proj_kernel(*refs, n_a):
    a_refs = refs[:n_a]
    w_ref, h_ref, gpost_ref, gnext_ref, hout_ref, hn_ref = refs[n_a:]
    m = None
    k0 = 0
    for a_ref in a_refs:
        kw = a_ref.shape[1]
        part = jnp.dot(a_ref[...], w_ref[k0:k0 + kw, :], preferred_element_type=F32)
        m = part if m is None else m + part
        k0 += kw
    h = h_ref[...] + _rms(m, gpost_ref[...])
    hout_ref[...] = h
    hn_ref[...] = _rms(h, gnext_ref[...]).astype(BF16)


def _out_proj(a_list, w, h, g_post, g_next):
    seq, d = h.shape
    tm = min(seq, 512)
    row = lambda i: (i, 0)
    fixed = lambda i: (0, 0)
    return pl.pallas_call(
        functools.partial(_out_proj_kernel, n_a=len(a_list)),
        grid=(seq // tm,),
        in_specs=[pl.BlockSpec((tm, a.shape[1]), row) for a in a_list]
        + [pl.BlockSpec(w.shape, fixed), pl.BlockSpec((tm, d), row),
           pl.BlockSpec((1, d), fixed), pl.BlockSpec((1, d), fixed)],
        out_specs=[pl.BlockSpec((tm, d), row), pl.BlockSpec((tm, d), row)],
        out_shape=[jax.ShapeDtypeStruct((seq, d), F32), jax.ShapeDtypeStruct((seq, d), BF16)],
        compiler_params=_params(("parallel",)),
        name="out_proj",
    )(*a_list, w, h, g_post[None, :], g_next[None, :])


def _mlp_kernel(x_ref, wu_ref, wd_ref, h_ref, g_ref, o_ref, acc_ref):
    f = pl.program_id(1)

    @pl.when(f == 0)
    def _():
        acc_ref[...] = jnp.zeros(acc_ref.shape, F32)

    a = jnp.maximum(jnp.dot(x_ref[...], wu_ref[...], preferred_element_type=F32), 0.0)
    acc_ref[...] += jnp.dot((a * a).astype(BF16), wd_ref[...], preferred_element_type=F32)

    @pl.when(f == pl.num_programs(1) - 1)
    def _():
        o_ref[...] = h_ref[...] + _rms(acc_ref[...], g_ref[...])


def _mlp(xn, w_up, w_down, h, g_post):
    seq, d = h.shape
    d_ff = w_up.shape[1]
    tm = min(seq, 512)
    tf = min(d_ff, 1024)
    return pl.pallas_call(
        _mlp_kernel,
        grid=(seq // tm, d_ff // tf),
        in_specs=[pl.BlockSpec((tm, d), lambda i, f: (i, 0)),
                  pl.BlockSpec((d, tf), lambda i, f: (0, f)),
                  pl.BlockSpec((tf, d), lambda i, f: (f, 0)),
                  pl.BlockSpec((tm, d), lambda i, f: (i, 0)),
                  pl.BlockSpec((1, d), lambda i, f: (0, 0))],
        out_specs=pl.BlockSpec((tm, d), lambda i, f: (i, 0)),
        out_shape=jax.ShapeDtypeStruct((seq, d), F32),
        scratch_shapes=[pltpu.VMEM((tm, d), F32)],
        compiler_params=_params(("parallel", "arbitrary")),
        name="mlp",
    )(xn, w_up, w_down, h, g_post[None, :])


def _ple_kernel(h_ref, p_ref, wg_ref, b_ref, wp_ref, *rest, with_next):
    h = h_ref[...]
    z = jnp.dot(h.astype(BF16), wg_ref[...], preferred_element_type=F32) + b_ref[...]
    gate = 1.0 / (1.0 + jnp.exp(-z))
    e = jnp.dot(p_ref[...].astype(BF16), wp_ref[...], preferred_element_type=F32)
    out = h + gate * e
    if with_next:
        gnext_ref, o_ref, hn_ref = rest
        hn_ref[...] = _rms(out, gnext_ref[...]).astype(BF16)
    else:
        (o_ref,) = rest
    o_ref[...] = out


def _ple(h, p, w_gate, b_gate, w_proj, g_next=None):
    seq, d = h.shape
    tm = min(seq, 512)
    row = lambda i: (i, 0)
    fixed = lambda i: (0, 0)
    with_next = g_next is not None
    in_specs = [pl.BlockSpec((tm, d), row), pl.BlockSpec((tm, p.shape[1]), row),
                pl.BlockSpec(w_gate.shape, fixed), pl.BlockSpec((1, d), fixed),
                pl.BlockSpec(w_proj.shape, fixed)]
    args = [h, p, w_gate, b_gate[None, :], w_proj]
    out_specs = [pl.BlockSpec((tm, d), row)]
    out_shape = [jax.ShapeDtypeStruct((seq, d), F32)]
    if with_next:
        in_specs.append(pl.BlockSpec((1, d), fixed))
        args.append(g_next[None, :])
        out_specs.append(pl.BlockSpec((tm, d), row))
        out_shape.append(jax.ShapeDtypeStruct((seq, d), BF16))
    res = pl.pallas_call(
        functools.partial(_ple_kernel, with_next=with_next),
        grid=(seq // tm,),
        in_specs=in_specs, out_specs=out_specs, out_shape=out_shape,
        compiler_params=_params(("parallel",)),
        name="ple",
    )(*args)
    return res if with_next else (res[0], None)


def _odd_proj_kernel(x_ref, w_ref, cos_ref, sin_ref, o_ref, *rest, rope, scale, kmean, tm):
    t = jnp.dot(x_ref[...], w_ref[...], preferred_element_type=F32)
    if rope:
        t = _rope(t, cos_ref[...], sin_ref[...], MOBA_HD // 2)
    if kmean:
        (km_ref,) = rest
        for b in range(tm // MOBA_BLOCK):
            blk = t[b * MOBA_BLOCK:(b + 1) * MOBA_BLOCK, :]
            km_ref[b] = jnp.sum(blk, axis=0, keepdims=True) * (1.0 / MOBA_BLOCK)
    if scale != 1.0:
        t = t * scale
    o_ref[...] = t.astype(BF16)


def _odd_proj(xn, w, col, tabs_cos, tabs_sin, *, rope, scale=1.0, kmean=False):
    seq, d = xn.shape
    n = d
    tm = min(seq, 512)
    row = lambda i: (i, 0)
    fixed = lambda i: (0, 0)
    out_specs = [pl.BlockSpec((tm, n), row)]
    out_shape = [jax.ShapeDtypeStruct((seq, n), BF16)]
    if kmean:
        nb = tm // MOBA_BLOCK
        out_specs.append(pl.BlockSpec((nb, 1, n), lambda i: (i, 0, 0)))
        out_shape.append(jax.ShapeDtypeStruct((seq // MOBA_BLOCK, 1, n), F32))
    res = pl.pallas_call(
        functools.partial(_odd_proj_kernel, rope=rope, scale=scale, kmean=kmean, tm=tm),
        grid=(seq // tm,),
        in_specs=[pl.BlockSpec((tm, d), row), pl.BlockSpec((d, n), lambda i: (0, col)),
                  pl.BlockSpec((tm, LANES), lambda i: (i, 1)),
                  pl.BlockSpec((tm, LANES), lambda i: (i, 1))],
        out_specs=out_specs, out_shape=out_shape,
        compiler_params=_params(("parallel",)),
        name="odd_proj",
    )(xn, w, tabs_cos, tabs_sin)
    return res


def _moba_kernel(q_ref, k_ref, v_ref, km_ref, o_ref, m_ref, acc_ref, *, tq, tk, nkb):
    i = pl.program_id(1)
    q = q_ref[...]

    km = km_ref[...]
    if nkb < 64:
        km = jnp.concatenate([km, jnp.zeros((64 - nkb, LANES), F32)], axis=0)
    km_hi = km.astype(BF16).astype(F32)
    r1 = km - km_hi
    km_mid = r1.astype(BF16).astype(F32)
    km_lo = (r1 - km_mid).astype(BF16).astype(F32)
    kcat = jnp.concatenate([km_hi, km_mid, km_lo, jnp.zeros((64, LANES), F32)], axis=0).astype(BF16)
    g3 = lax.dot_general(q, kcat, (((1,), (1,)), ((), ())), preferred_element_type=F32)
    g01 = g3[:, :LANES]
    gate = g01 + pltpu.roll(g01, 64, 1) + g3[:, LANES:]

    lane = lax.broadcasted_iota(jnp.int32, (tq, LANES), 1)
    qblk = (i * tq + lax.broadcasted_iota(jnp.int32, (tq, LANES), 0)) // MOBA_BLOCK
    gm = jnp.where(lane < qblk, gate, -jnp.inf)
    sel = lane == qblk
    for _ in range(MOBA_TOPK):
        mx = jnp.max(gm, axis=1, keepdims=True)
        is_max = (gm == mx) & (mx > -jnp.inf)
        idx = jnp.min(jnp.where(is_max, lane, LANES), axis=1, keepdims=True)
        pick = lane == idx
        sel = sel | pick
        gm = jnp.where(pick, -jnp.inf, gm)
    bias = jnp.where(sel, 0.0, MASK_BIAS).astype(BF16)
    qa = jnp.concatenate([q, bias], axis=1)

    kb_per_tile = tk // MOBA_BLOCK
    krow = lax.broadcasted_iota(jnp.int32, (tk, LANES), 0) // MOBA_BLOCK
    klane = lax.broadcasted_iota(jnp.int32, (tk, LANES), 1)

    ones = jnp.ones((tk, LANES), BF16)

    def kv(j):
        rows = pl.ds(pl.multiple_of(j * tk, tk), tk)
        onehot = jnp.where(klane == krow + j * kb_per_tile, 1.0, 0.0).astype(BF16)
        return (jnp.concatenate([k_ref[rows, :], onehot], axis=1),
                jnp.concatenate([v_ref[rows, :], ones], axis=1))

    _flash_attend(qa, kv, i, tq, tk, m_ref, acc_ref, o_ref)


def _moba_attn(q, k, v, kmean):
    seq = q.shape[0]
    nkb = seq // MOBA_BLOCK
    assert nkb <= 64 and nkb % 8 == 0, "the gate matmul packs three 64-lane groups"
    tq = min(seq, FLASH_TQ)
    tk = min(seq, FLASH_TK)
    return pl.pallas_call(
        functools.partial(_moba_kernel, tq=tq, tk=tk, nkb=nkb),
        grid=(MOBA_HEADS, seq // tq),
        in_specs=[pl.BlockSpec((tq, LANES), lambda h, i: (i, h)),
                  pl.BlockSpec((seq, LANES), lambda h, i: (0, h)),
                  pl.BlockSpec((seq, LANES), lambda h, i: (0, h)),
                  pl.BlockSpec((nkb, LANES), lambda h, i: (0, h))],
        out_specs=pl.BlockSpec((tq, LANES), lambda h, i: (i, h)),
        out_shape=jax.ShapeDtypeStruct((seq, MOBA_HEADS * MOBA_HD), BF16),
        scratch_shapes=[pltpu.VMEM((tq, LANES), F32), pltpu.VMEM((tq, 2 * LANES), F32)],
        compiler_params=_params(("parallel", "arbitrary")),
        name="moba_attn",
    )(q, k, v, kmean)


SC_WINDOW = 128


def _sc_gather(table, idx):
    n = idx.shape[0]
    width = table.shape[1]
    mesh = plsc.VectorSubcoreMesh(core_axis_name="core", subcore_axis_name="subcore")
    n_cores = mesh.num_cores
    steps = n // (SC_WINDOW * n_cores)
    idx2 = idx.reshape(1, n)

    @functools.partial(pl.kernel, out_type=jax.ShapeDtypeStruct((n, width), table.dtype), mesh=mesh)
    def gather_kernel(x_hbm, i_hbm, o_hbm):
        def body(i_vmem, o_vmem):
            pltpu.sync_copy(x_hbm.at[i_vmem.at[0]], o_vmem)

        pltpu.emit_pipeline(
            body,
            grid=(n_cores, steps),
            in_specs=[pl.BlockSpec((1, SC_WINDOW), lambda c, i: (0, c * steps + i))],
            out_specs=[pl.BlockSpec((SC_WINDOW, width), lambda c, i: (c * steps + i, 0))],
            core_axis_name=("core", "subcore"),
            dimension_semantics=(pltpu.PARALLEL, pltpu.PARALLEL),
        )(i_hbm, o_hbm)

    return gather_kernel(table, idx2)


def _even_in_weight(w_in):
    d = w_in.shape[0]
    cq, ckv, kr, qs, ks, vs = jnp.split(w_in, [512, 1024, 1088, 2112, 2240], axis=1)
    return jnp.concatenate([cq, ckv, kr, jnp.zeros((d, 64), w_in.dtype), qs, ks, vs], axis=1).astype(BF16)


def _mla_q_weight(w_q_up):
    r = w_q_up.shape[0]
    w = w_q_up.reshape(r, MLA_HEADS, MLA_NOPE + MLA_ROPE)
    nope = w[:, :, :MLA_NOPE].reshape(r, MLA_HEADS * MLA_NOPE)
    rope = jnp.pad(w[:, :, MLA_NOPE:], ((0, 0), (0, 0), (0, LANES - MLA_ROPE))).reshape(r, MLA_HEADS * LANES)
    return jnp.concatenate([nope, rope], axis=1).astype(BF16)


def _mla_kv_weight(w_kv_up):
    r = w_kv_up.shape[0]
    w = w_kv_up.reshape(r, MLA_HEADS, MLA_NOPE + MLA_V)
    nope = w[:, :, :MLA_NOPE].reshape(r, MLA_HEADS * MLA_NOPE)
    val = w[:, :, MLA_NOPE:].reshape(r, MLA_HEADS * MLA_V)
    return jnp.concatenate([nope, val], axis=1).astype(BF16)


def kernel(x, p, positions, even_pre_g, even_w_in, mla_q_norm_g, mla_w_q_up, mla_kv_norm_g, mla_w_kv_up, swa_sinks, even_w_out, even_post_g, odd_pre_g, moba_w_qkv, odd_w_out, odd_post_g, mlp_pre_g, mlp_w_up, mlp_w_down, mlp_post_g, ple_w_gate, ple_b_gate, ple_w_proj):
    batch, seq, d = x.shape
    assert batch == 1
    h = x.reshape(seq, d)
    tabs_cos, tabs_sin = _rope_tables(positions, seq)

    cq, ckv, kr, qs, ks, vs = _even_proj(h, even_pre_g[0], _even_in_weight(even_w_in[0]),
                                         mla_q_norm_g[0], mla_kv_norm_g[0], tabs_cos, tabs_sin)
    qn, qr, kn, v = _mla_up(cq, ckv, _mla_q_weight(mla_w_q_up[0]), _mla_kv_weight(mla_w_kv_up[0]),
                            tabs_cos, tabs_sin)
    o_mla = _mla_attn(qn, qr, kn, kr, v)
    o_swa = _swa_attn(qs, ks, vs, swa_sinks[0])
    h, hn = _out_proj([o_mla, o_swa], even_w_out[0].astype(BF16), h, even_post_g[0], mlp_pre_g[0])
    h = _mlp(hn, mlp_w_up[0].astype(BF16), mlp_w_down[0].astype(BF16), h, mlp_post_g[0])
    h, hn = _ple(h, p[0, 0], ple_w_gate[0].astype(BF16), ple_b_gate[0], ple_w_proj[0].astype(BF16),
                 g_next=odd_pre_g[0])

    w_qkv = moba_w_qkv[0].astype(BF16)
    (q,) = _odd_proj(hn, w_qkv, 0, tabs_cos, tabs_sin, rope=True, scale=MOBA_HD ** -0.5 * LOG2E)
    k, kmean = _odd_proj(hn, w_qkv, 1, tabs_cos, tabs_sin, rope=True, kmean=True)
    (v,) = _odd_proj(hn, w_qkv, 2, tabs_cos, tabs_sin, rope=False)
    o = _moba_attn(q, k, v, kmean.reshape(seq // MOBA_BLOCK, MOBA_HEADS * MOBA_HD))
    h, hn = _out_proj([o], odd_w_out[0].astype(BF16), h, odd_post_g[0], mlp_pre_g[1])
    h = _mlp(hn, mlp_w_up[1].astype(BF16), mlp_w_down[1].astype(BF16), h, mlp_post_g[1])
    h, _ = _ple(h, p[1, 0], ple_w_gate[1].astype(BF16), ple_b_gate[1], ple_w_proj[1].astype(BF16))
    rows = seq * d // LANES
    h = _sc_gather(h.reshape(rows, LANES), jnp.arange(rows, dtype=jnp.int32))
    return h.reshape(batch, seq, d)
```

```python
import functools

import jax
import jax.numpy as jnp
import numpy as np
from jax import lax
from jax.experimental import pallas as pl
from jax.experimental.pallas import tpu as pltpu

F32 = jnp.float32
BF16 = jnp.bfloat16

NORM_EPS = 1e-6
ROPE_THETA = 10000.0

MLA_HEADS = 8
MLA_Q_RANK = 512
MLA_KV_RANK = 512
MLA_NOPE = 128
MLA_ROPE = 64
MLA_V = 128
SWA_HEADS = 16
SWA_KV_HEADS = 2
SWA_HD = 64
SWA_BLOCK = 128
MOBA_HEADS = 16
MOBA_HD = 128
MOBA_BLOCK = 256
MOBA_TOPK = 3

LANES = 128
V7X_VMEM_BYTES = 64 * 1024 * 1024
VMEM_LIMIT = 56 * 1024 * 1024
MASK_BIAS = -1e9
LOG2E = 1.4426950408889634
FLASH_TQ = 2048
FLASH_TK = 512


def _params(sem):
    return pltpu.CompilerParams(dimension_semantics=sem, vmem_limit_bytes=VMEM_LIMIT)


def _rms(t, g):
    return t * lax.rsqrt(jnp.mean(t * t, axis=-1, keepdims=True) + NORM_EPS) * g


def _rope(t, cos, sin_signed, half):
    width = t.shape[1]
    reps = width // LANES
    if reps > 1:
        cos = jnp.concatenate([cos] * reps, axis=1)
        sin_signed = jnp.concatenate([sin_signed] * reps, axis=1)
    lane = lax.broadcasted_iota(jnp.int32, t.shape, 1)
    first = (lane % (2 * half)) < half
    partner = jnp.where(first, pltpu.roll(t, width - half, 1), pltpu.roll(t, half, 1))
    return t * cos + partner * sin_signed


def _tables_kernel(pos_ref, invf_ref, sign_ref, cos_ref, sin_ref):
    ang = pos_ref[...].astype(F32) * invf_ref[...]
    cos_ref[...] = jnp.cos(ang)
    sin_ref[...] = jnp.sin(ang) * sign_ref[...]


def _rope_tables(positions, seq):
    def inv_freq(d):
        half = d // 2
        return jnp.power(ROPE_THETA, -jnp.arange(half, dtype=F32) * (2.0 / d))

    f64, f128 = inv_freq(64), inv_freq(128)
    invf = jnp.concatenate([f64, f64, f64, f64, f128, f128])[None, :]
    sign = np.concatenate([-np.ones(32), np.ones(32), -np.ones(32), np.ones(32),
                           -np.ones(64), np.ones(64)]).astype(np.float32)[None, :]
    tm = min(seq, 1024)
    return pl.pallas_call(
        _tables_kernel,
        grid=(seq // tm,),
        in_specs=[pl.BlockSpec((tm, 1), lambda i: (i, 0)),
                  pl.BlockSpec((1, 256), lambda i: (0, 0)),
                  pl.BlockSpec((1, 256), lambda i: (0, 0))],
        out_specs=[pl.BlockSpec((tm, 256), lambda i: (i, 0)),
                   pl.BlockSpec((tm, 256), lambda i: (i, 0))],
        out_shape=[jax.ShapeDtypeStruct((seq, 256), F32)] * 2,
        compiler_params=_params(("parallel",)),
        name="rope_tables",
    )(positions.reshape(seq, 1), invf, jnp.asarray(sign))


_EVEN_COLS = (0, 512, 1024, 1152, 2176, 2304, 2432)


def _even_proj_kernel(x_ref, gpre_ref, w_ref, gq_ref, gkv_ref, cos_ref, sin_ref,
                      cq_ref, ckv_ref, kr_ref, qs_ref, ks_ref, vs_ref):
    x = _rms(x_ref[...], gpre_ref[...]).astype(BF16)
    cos, sin = cos_ref[...], sin_ref[...]
    c = _EVEN_COLS

    def mm(k):
        return jnp.dot(x, w_ref[:, c[k]:c[k + 1]], preferred_element_type=F32)

    cq_ref[...] = _rms(mm(0), gq_ref[...]).astype(BF16)
    ckv_ref[...] = _rms(mm(1), gkv_ref[...]).astype(BF16)
    kr_ref[...] = _rope(mm(2), cos, sin, 32).astype(BF16)
    qs_ref[...] = (_rope(mm(3), cos, sin, 32) * (SWA_HD ** -0.5)).astype(BF16)
    ks_ref[...] = _rope(mm(4), cos, sin, 32).astype(BF16)
    vs_ref[...] = mm(5).astype(BF16)


def _even_proj(x, g_pre, w, gq, gkv, tabs_cos, tabs_sin):
    seq, d = x.shape
    tm = min(seq, 512)
    widths = [_EVEN_COLS[k + 1] - _EVEN_COLS[k] for k in range(6)]
    row = lambda i: (i, 0)
    fixed = lambda i: (0, 0)
    return pl.pallas_call(
        _even_proj_kernel,
        grid=(seq // tm,),
        in_specs=[pl.BlockSpec((tm, d), row),
                  pl.BlockSpec((1, d), fixed),
                  pl.BlockSpec(w.shape, fixed),
                  pl.BlockSpec((1, 512), fixed),
                  pl.BlockSpec((1, 512), fixed),
                  pl.BlockSpec((tm, LANES), row),
                  pl.BlockSpec((tm, LANES), row)],
        out_specs=[pl.BlockSpec((tm, n), row) for n in widths],
        out_shape=[jax.ShapeDtypeStruct((seq, n), BF16) for n in widths],
        compiler_params=_params(("parallel",)),
        name="even_proj",
    )(x, g_pre[None, :], w, gq[None, :], gkv[None, :], tabs_cos, tabs_sin)


def _mla_up_kernel(cq_ref, ckv_ref, wq_ref, wkv_ref, cos_ref, sin_ref,
                   qn_ref, qr_ref, kn_ref, v_ref, *, scale):
    cq, ckv = cq_ref[...], ckv_ref[...]
    n = MLA_HEADS * LANES
    qn = jnp.dot(cq, wq_ref[:, :n], preferred_element_type=F32)
    qr = jnp.dot(cq, wq_ref[:, n:], preferred_element_type=F32)
    qn_ref[...] = (qn * scale).astype(BF16)
    qr_ref[...] = (_rope(qr, cos_ref[...], sin_ref[...], 32) * scale).astype(BF16)
    kn_ref[...] = jnp.dot(ckv, wkv_ref[:, :n], preferred_element_type=F32).astype(BF16)
    v_ref[...] = jnp.dot(ckv, wkv_ref[:, n:], preferred_element_type=F32).astype(BF16)


def _mla_up(cq, ckv, wq, wkv, tabs_cos, tabs_sin):
    seq = cq.shape[0]
    tm = min(seq, 512)
    n = MLA_HEADS * LANES
    row = lambda i: (i, 0)
    fixed = lambda i: (0, 0)
    scale = (MLA_NOPE + MLA_ROPE) ** -0.5 * LOG2E
    return pl.pallas_call(
        functools.partial(_mla_up_kernel, scale=scale),
        grid=(seq // tm,),
        in_specs=[pl.BlockSpec((tm, MLA_Q_RANK), row),
                  pl.BlockSpec((tm, MLA_KV_RANK), row),
                  pl.BlockSpec(wq.shape, fixed),
                  pl.BlockSpec(wkv.shape, fixed),
                  pl.BlockSpec((tm, LANES), row),
                  pl.BlockSpec((tm, LANES), row)],
        out_specs=[pl.BlockSpec((tm, n), row)] * 4,
        out_shape=[jax.ShapeDtypeStruct((seq, n), BF16)] * 4,
        compiler_params=_params(("parallel",)),
        name="mla_up",
    )(cq, ckv, wq, wkv, tabs_cos, tabs_sin)


def _flash_step(q, k, v1, m_ref, acc_ref, rows, mask=None):
    s = lax.dot_general(q, k, (((1,), (1,)), ((), ())), preferred_element_type=F32)
    if mask is not None:
        s = jnp.where(mask, s, -jnp.inf)
    m_old = m_ref[rows, :]
    m_new = jnp.maximum(m_old, jnp.max(s, axis=1, keepdims=True))
    alpha = jnp.exp2(m_old - m_new)
    p = jnp.exp2(s - jnp.tile(m_new, (1, s.shape[1] // LANES)))
    pv = jnp.dot(p.astype(BF16), v1[:, :LANES], preferred_element_type=F32)
    acc_ref[rows, :LANES] = alpha * acc_ref[rows, :LANES] + pv
    acc_ref[rows, LANES:] = alpha * acc_ref[rows, LANES:] + jnp.sum(p, axis=1, keepdims=True)
    m_ref[rows, :] = m_new


def _flash_attend(q, kv, i, tq, tk, m_ref, acc_ref, o_ref):
    m_ref[...] = jnp.full(m_ref.shape, -jnp.inf, F32)
    acc_ref[...] = jnp.zeros(acc_ref.shape, F32)
    sub = tq // tk

    def past(jj, carry):
        for d in range(sub):
            k, v1 = kv(jj * sub + d)
            _flash_step(q, k, v1, m_ref, acc_ref, slice(None))
        return carry

    lax.fori_loop(0, i, past, 0)

    for d in range(sub):
        n = tq - d * tk
        row = lax.broadcasted_iota(jnp.int32, (n, tk), 0)
        col = lax.broadcasted_iota(jnp.int32, (n, tk), 1)
        k, v1 = kv(i * sub + d)
        _flash_step(q[d * tk:], k, v1, m_ref, acc_ref, slice(d * tk, tq), mask=col <= row)

    acc = acc_ref[...]
    o_ref[...] = (acc[:, :LANES] / acc[:, LANES:]).astype(BF16)


def _mla_attn_kernel(qn_ref, qr_ref, kn_ref, kr_ref, v_ref, o_ref, m_ref, acc_ref, *, tq, tk):
    i = pl.program_id(1)
    q = jnp.concatenate([qn_ref[...], qr_ref[...]], axis=1)
    ones = jnp.ones((tk, LANES), BF16)

    def kv(j):
        rows = pl.ds(pl.multiple_of(j * tk, tk), tk)
        return (jnp.concatenate([kn_ref[rows, :], kr_ref[rows, :]], axis=1),
                jnp.concatenate([v_ref[rows, :], ones], axis=1))

    _flash_attend(q, kv, i, tq, tk, m_ref, acc_ref, o_ref)


def _mla_attn(qn, qr, kn, kr, v):
    seq = qn.shape[0]
    tq = min(seq, FLASH_TQ)
    tk = min(seq, FLASH_TK)
    return pl.pallas_call(
        functools.partial(_mla_attn_kernel, tq=tq, tk=tk),
        grid=(MLA_HEADS, seq // tq),
        in_specs=[pl.BlockSpec((tq, LANES), lambda h, i: (i, h)),
                  pl.BlockSpec((tq, LANES), lambda h, i: (i, h)),
                  pl.BlockSpec((seq, LANES), lambda h, i: (0, h)),
                  pl.BlockSpec((seq, LANES), lambda h, i: (0, 0)),
                  pl.BlockSpec((seq, LANES), lambda h, i: (0, h))],
        out_specs=pl.BlockSpec((tq, LANES), lambda h, i: (i, h)),
        out_shape=jax.ShapeDtypeStruct((seq, MLA_HEADS * MLA_V), BF16),
        scratch_shapes=[pltpu.VMEM((tq, LANES), F32), pltpu.VMEM((tq, 2 * LANES), F32)],
        compiler_params=_params(("parallel", "arbitrary")),
        name="mla_attn",
    )(qn, qr, kn, kr, v)


def _swa_kernel(sink_ref, q_ref, kc_ref, kp_ref, vc_ref, vp_ref, o_ref, *, tq):
    i = pl.program_id(0)
    nb = tq // SWA_BLOCK
    group = SWA_HEADS // SWA_KV_HEADS
    pairs = group // 2
    lane = lax.broadcasted_iota(jnp.int32, (2 * SWA_BLOCK, LANES), 1)
    shape = (pairs * SWA_BLOCK, 2 * SWA_BLOCK)
    qp = lax.broadcasted_iota(jnp.int32, shape, 0) % SWA_BLOCK + SWA_BLOCK
    kp = lax.broadcasted_iota(jnp.int32, shape, 1)
    band = (kp <= qp) & (qp - kp < SWA_BLOCK)
    out_lane = lax.broadcasted_iota(jnp.int32, (pairs * SWA_BLOCK, LANES), 1)
    key_row = lax.broadcasted_iota(jnp.int32, (4 * SWA_BLOCK, LANES), 0)
    key_lane = lax.broadcasted_iota(jnp.int32, (4 * SWA_BLOCK, LANES), 1)
    ones_by_head = jnp.where((key_row < 2 * SWA_BLOCK) == (key_lane < SWA_HD), 1.0, 0.0).astype(BF16)

    def split(t, c):
        mine = jnp.where((lane >= c * SWA_HD) & (lane < (c + 1) * SWA_HD), t, 0.0)
        other = pltpu.roll(mine, SWA_HD, 1)
        lo, hi = (mine, other) if c == 0 else (other, mine)
        return jnp.concatenate([lo, hi], axis=0).astype(BF16)

    for b in range(nb):
        rows = slice(b * SWA_BLOCK, (b + 1) * SWA_BLOCK)
        if b == 0:
            k_prev, v_prev = kp_ref[...], vp_ref[...]
        else:
            prev = slice((b - 1) * SWA_BLOCK, b * SWA_BLOCK)
            k_prev, v_prev = kc_ref[prev, :], vc_ref[prev, :]
        kw = jnp.concatenate([k_prev, kc_ref[rows, :]], axis=0).astype(F32)
        vw = jnp.concatenate([v_prev, vc_ref[rows, :]], axis=0).astype(F32)
        first_key = jnp.where(i * nb + b == 0, SWA_BLOCK, 0)
        valid = band & (kp >= first_key)
        for c in range(SWA_KV_HEADS):
            kcat = split(kw, c)
            vcat = split(vw, c)
            cols = [slice((c * pairs + a) * LANES, (c * pairs + a + 1) * LANES) for a in range(pairs)]
            q4 = jnp.concatenate([q_ref[rows, cs] for cs in cols], axis=0)
            s = lax.dot_general(q4, kcat, (((1,), (1,)), ((), ())), preferred_element_type=F32)
            ps, sink_terms = [], []
            for e in range(2):
                sink = jnp.concatenate(
                    [jnp.full((SWA_BLOCK, 1), sink_ref[c * group + 2 * a + e], F32) for a in range(pairs)], axis=0)
                se = jnp.where(valid, s[:, e * 2 * SWA_BLOCK:(e + 1) * 2 * SWA_BLOCK], -jnp.inf)
                m = jnp.maximum(jnp.max(se, axis=1, keepdims=True), sink)
                ps.append(jnp.exp(se - m).astype(BF16))
                sink_terms.append(jnp.exp(sink - m))
            pv = jnp.dot(jnp.concatenate(ps, axis=1), jnp.concatenate([vcat, ones_by_head], axis=1),
                         preferred_element_type=F32)
            den = pv[:, LANES:] + jnp.where(out_lane < SWA_HD, sink_terms[0], sink_terms[1])
            o4 = (pv[:, :LANES] / den).astype(BF16)
            for a, cs in enumerate(cols):
                o_ref[rows, cs] = o4[a * SWA_BLOCK:(a + 1) * SWA_BLOCK]


def _swa_attn(qs, ks, vs, sinks):
    seq = qs.shape[0]
    tq = min(seq, 512)
    nb = tq // SWA_BLOCK
    cur = lambda i: (i, 0)
    prev = lambda i: (jnp.maximum(i * nb - 1, 0), 0)
    return pl.pallas_call(
        functools.partial(_swa_kernel, tq=tq),
        grid=(seq // tq,),
        in_specs=[pl.BlockSpec(memory_space=pltpu.SMEM),
                  pl.BlockSpec((tq, SWA_HEADS * SWA_HD), cur),
                  pl.BlockSpec((tq, LANES), cur),
                  pl.BlockSpec((SWA_BLOCK, LANES), prev),
                  pl.BlockSpec((tq, LANES), cur),
                  pl.BlockSpec((SWA_BLOCK, LANES), prev)],
        out_specs=pl.BlockSpec((tq, SWA_HEADS * SWA_HD), cur),
        out_shape=jax.ShapeDtypeStruct((seq, SWA_HEADS * SWA_HD), BF16),
        compiler_params=_params(("parallel",)),
        name="swa_attn",
    )(sinks, qs, ks, ks, vs, vs)


def _out_proj_kernel(*refs, n_a):
    a_refs = refs[:n_a]
    w_ref, h_ref, gpost_ref, gnext_ref, hout_ref, hn_ref = refs[n_a:]
    m = None
    k0 = 0
    for a_ref in a_refs:
        kw = a_ref.shape[1]
        part = jnp.dot(a_ref[...], w_ref[k0:k0 + kw, :], preferred_element_type=F32)
        m = part if m is None else m + part
        k0 += kw
    h = h_ref[...] + _rms(m, gpost_ref[...])
    hout_ref[...] = h
    hn_ref[...] = _rms(h, gnext_ref[...]).astype(BF16)


def _out_proj(a_list, w, h, g_post, g_next):
    seq, d = h.shape
    tm = min(seq, 512)
    row = lambda i: (i, 0)
    fixed = lambda i: (0, 0)
    return pl.pallas_call(
        functools.partial(_out_proj_kernel, n_a=len(a_list)),
        grid=(seq // tm,),
        in_specs=[pl.BlockSpec((tm, a.shape[1]), row) for a in a_list]
        + [pl.BlockSpec(w.shape, fixed), pl.BlockSpec((tm, d), row),
           pl.BlockSpec((1, d), fixed), pl.BlockSpec((1, d), fixed)],
        out_specs=[pl.BlockSpec((tm, d), row), pl.BlockSpec((tm, d), row)],
        out_shape=[jax.ShapeDtypeStruct((seq, d), F32), jax.ShapeDtypeStruct((seq, d), BF16)],
        compiler_params=_params(("parallel",)),
        name="out_proj",
    )(*a_list, w, h, g_post[None, :], g_next[None, :])


def _mlp_kernel(x_ref, wu_ref, wd_ref, h_ref, g_ref, o_ref, acc_ref):
    f = pl.program_id(1)

    @pl.when(f == 0)
    def _():
        acc_ref[...] = jnp.zeros(acc_ref.shape, F32)

    a = jnp.maximum(jnp.dot(x_ref[...], wu_ref[...], preferred_element_type=F32), 0.0)
    acc_ref[...] += jnp.dot((a * a).astype(BF16), wd_ref[...], preferred_element_type=F32)

    @pl.when(f == pl.num_programs(1) - 1)
    def _():
        o_ref[...] = h_ref[...] + _rms(acc_ref[...], g_ref[...])


def _mlp(xn, w_up, w_down, h, g_post):
    seq, d = h.shape
    d_ff = w_up.shape[1]
    tm = min(seq, 512)
    tf = min(d_ff, 1024)
    return pl.pallas_call(
        _mlp_kernel,
        grid=(seq // tm, d_ff // tf),
        in_specs=[pl.BlockSpec((tm, d), lambda i, f: (i, 0)),
                  pl.BlockSpec((d, tf), lambda i, f: (0, f)),
                  pl.BlockSpec((tf, d), lambda i, f: (f, 0)),
                  pl.BlockSpec((tm, d), lambda i, f: (i, 0)),
                  pl.BlockSpec((1, d), lambda i, f: (0, 0))],
        out_specs=pl.BlockSpec((tm, d), lambda i, f: (i, 0)),
        out_shape=jax.ShapeDtypeStruct((seq, d), F32),
        scratch_shapes=[pltpu.VMEM((tm, d), F32)],
        compiler_params=_params(("parallel", "arbitrary")),
        name="mlp",
    )(xn, w_up, w_down, h, g_post[None, :])


def _ple_kernel(h_ref, p_ref, wg_ref, b_ref, wp_ref, *rest, with_next):
    h = h_ref[...]
    z = jnp.dot(h.astype(BF16), wg_ref[...], preferred_element_type=F32) + b_ref[...]
    gate = 1.0 / (1.0 + jnp.exp(-z))
    e = jnp.dot(p_ref[...].astype(BF16), wp_ref[...], preferred_element_type=F32)
    out = h + gate * e
    if with_next:
        gnext_ref, o_ref, hn_ref = rest
        hn_ref[...] = _rms(out, gnext_ref[...]).astype(BF16)
    else:
        (o_ref,) = rest
    o_ref[...] = out


def _ple(h, p, w_gate, b_gate, w_proj, g_next=None):
    seq, d = h.shape
    tm = min(seq, 512)
    row = lambda i: (i, 0)
    fixed = lambda i: (0, 0)
    with_next = g_next is not None
    in_specs = [pl.BlockSpec((tm, d), row), pl.BlockSpec((tm, p.shape[1]), row),
                pl.BlockSpec(w_gate.shape, fixed), pl.BlockSpec((1, d), fixed),
                pl.BlockSpec(w_proj.shape, fixed)]
    args = [h, p, w_gate, b_gate[None, :], w_proj]
    out_specs = [pl.BlockSpec((tm, d), row)]
    out_shape = [jax.ShapeDtypeStruct((seq, d), F32)]
    if with_next:
        in_specs.append(pl.BlockSpec((1, d), fixed))
        args.append(g_next[None, :])
        out_specs.append(pl.BlockSpec((tm, d), row))
        out_shape.append(jax.ShapeDtypeStruct((seq, d), BF16))
    res = pl.pallas_call(
        functools.partial(_ple_kernel, with_next=with_next),
        grid=(seq // tm,),
        in_specs=in_specs, out_specs=out_specs, out_shape=out_shape,
        compiler_params=_params(("parallel",)),
        name="ple",
    )(*args)
    return res if with_next else (res[0], None)


def _odd_proj_kernel(x_ref, w_ref, cos_ref, sin_ref, o_ref, *rest, rope, scale, kmean, tm):
    t = jnp.dot(x_ref[...], w_ref[...], preferred_element_type=F32)
    if rope:
        t = _rope(t, cos_ref[...], sin_ref[...], MOBA_HD // 2)
    if kmean:
        (km_ref,) = rest
        for b in range(tm // MOBA_BLOCK):
            blk = t[b * MOBA_BLOCK:(b + 1) * MOBA_BLOCK, :]
            km_ref[b] = jnp.sum(blk, axis=0, keepdims=True) * (1.0 / MOBA_BLOCK)
    if scale != 1.0:
        t = t * scale
    o_ref[...] = t.astype(BF16)


def _odd_proj(xn, w, col, tabs_cos, tabs_sin, *, rope, scale=1.0, kmean=False):
    seq, d = xn.shape
    n = d
    tm = min(seq, 512)
    row = lambda i: (i, 0)
    fixed = lambda i: (0, 0)
    out_specs = [pl.BlockSpec((tm, n), row)]
    out_shape = [jax.ShapeDtypeStruct((seq, n), BF16)]
    if kmean:
        nb = tm // MOBA_BLOCK
        out_specs.append(pl.BlockSpec((nb, 1, n), lambda i: (i, 0, 0)))
        out_shape.append(jax.ShapeDtypeStruct((seq // MOBA_BLOCK, 1, n), F32))
    res = pl.pallas_call(
        functools.partial(_odd_proj_kernel, rope=rope, scale=scale, kmean=kmean, tm=tm),
        grid=(seq // tm,),
        in_specs=[pl.BlockSpec((tm, d), row), pl.BlockSpec((d, n), lambda i: (0, col)),
                  pl.BlockSpec((tm, LANES), lambda i: (i, 1)),
                  pl.BlockSpec((tm, LANES), lambda i: (i, 1))],
        out_specs=out_specs, out_shape=out_shape,
        compiler_params=_params(("parallel",)),
        name="odd_proj",
    )(xn, w, tabs_cos, tabs_sin)
    return res


def _moba_kernel(q_ref, k_ref, v_ref, km_ref, o_ref, m_ref, acc_ref, *, tq, tk, nkb):
    i = pl.program_id(1)
    q = q_ref[...]

    km = km_ref[...]
    if nkb < 64:
        km = jnp.concatenate([km, jnp.zeros((64 - nkb, LANES), F32)], axis=0)
    km_hi = km.astype(BF16)
    km_mid = (km - km_hi.astype(F32)).astype(BF16)
    gate = lax.dot_general(jnp.concatenate([km_hi, km_mid], axis=1), jnp.concatenate([q, q], axis=1),
                           (((1,), (1,)), ((), ())), preferred_element_type=F32)

    blk = lax.broadcasted_iota(jnp.int32, (64, tq), 0)
    qblk = (i * tq + lax.broadcasted_iota(jnp.int32, (64, tq), 1)) // MOBA_BLOCK
    gm = jnp.where(blk < qblk, gate, -jnp.inf)
    sel = blk == qblk
    for r in range(MOBA_TOPK):
        mx = jnp.max(gm, axis=0, keepdims=True)
        idx = jnp.min(jnp.where(gm == mx, blk, 64), axis=0, keepdims=True)
        pick = (blk == idx) & (qblk > r)
        sel = sel | pick
        gm = jnp.where(pick, -jnp.inf, gm)
    bias_t = jnp.where(sel, 0.0, MASK_BIAS)
    bias = jnp.concatenate([bias_t, jnp.zeros((64, tq), F32)], axis=0).T.astype(BF16)
    qa = jnp.concatenate([q, bias], axis=1)

    kb_per_tile = tk // MOBA_BLOCK
    krow = lax.broadcasted_iota(jnp.int32, (tk, LANES), 0) // MOBA_BLOCK
    klane = lax.broadcasted_iota(jnp.int32, (tk, LANES), 1)

    ones = jnp.ones((tk, LANES), BF16)

    def kv(j):
        rows = pl.ds(pl.multiple_of(j * tk, tk), tk)
        onehot = jnp.where(klane == krow + j * kb_per_tile, 1.0, 0.0).astype(BF16)
        return (jnp.concatenate([k_ref[rows, :], onehot], axis=1),
                jnp.concatenate([v_ref[rows, :], ones], axis=1))

    _flash_attend(qa, kv, i, tq, tk, m_ref, acc_ref, o_ref)


def _moba_attn(q, k, v, kmean):
    seq = q.shape[0]
    nkb = seq // MOBA_BLOCK
    assert nkb <= 64 and nkb % 8 == 0, "the gate matmul packs three 64-lane groups"
    tq = min(seq, FLASH_TQ)
    tk = min(seq, FLASH_TK)
    return pl.pallas_call(
        functools.partial(_moba_kernel, tq=tq, tk=tk, nkb=nkb),
        grid=(MOBA_HEADS, seq // tq),
        in_specs=[pl.BlockSpec((tq, LANES), lambda h, i: (i, h)),
                  pl.BlockSpec((seq, LANES), lambda h, i: (0, h)),
                  pl.BlockSpec((seq, LANES), lambda h, i: (0, h)),
                  pl.BlockSpec((nkb, LANES), lambda h, i: (0, h))],
        out_specs=pl.BlockSpec((tq, LANES), lambda h, i: (i, h)),
        out_shape=jax.ShapeDtypeStruct((seq, MOBA_HEADS * MOBA_HD), BF16),
        scratch_shapes=[pltpu.VMEM((tq, LANES), F32), pltpu.VMEM((tq, 2 * LANES), F32)],
        compiler_params=_params(("parallel", "arbitrary")),
        name="moba_attn",
    )(q, k, v, kmean)


def _even_in_weight(w_in):
    d = w_in.shape[0]
    cq, ckv, kr, qs, ks, vs = jnp.split(w_in, [512, 1024, 1088, 2112, 2240], axis=1)
    return jnp.concatenate([cq, ckv, kr, jnp.zeros((d, 64), w_in.dtype), qs, ks, vs], axis=1).astype(BF16)


def _mla_q_weight(w_q_up):
    r = w_q_up.shape[0]
    w = w_q_up.reshape(r, MLA_HEADS, MLA_NOPE + MLA_ROPE)
    nope = w[:, :, :MLA_NOPE].reshape(r, MLA_HEADS * MLA_NOPE)
    rope = jnp.pad(w[:, :, MLA_NOPE:], ((0, 0), (0, 0), (0, LANES - MLA_ROPE))).reshape(r, MLA_HEADS * LANES)
    return jnp.concatenate([nope, rope], axis=1).astype(BF16)


def _mla_kv_weight(w_kv_up):
    r = w_kv_up.shape[0]
    w = w_kv_up.reshape(r, MLA_HEADS, MLA_NOPE + MLA_V)
    nope = w[:, :, :MLA_NOPE].reshape(r, MLA_HEADS * MLA_NOPE)
    val = w[:, :, MLA_NOPE:].reshape(r, MLA_HEADS * MLA_V)
    return jnp.concatenate([nope, val], axis=1).astype(BF16)


def kernel(x, p, positions, even_pre_g, even_w_in, mla_q_norm_g, mla_w_q_up, mla_kv_norm_g, mla_w_kv_up, swa_sinks, even_w_out, even_post_g, odd_pre_g, moba_w_qkv, odd_w_out, odd_post_g, mlp_pre_g, mlp_w_up, mlp_w_down, mlp_post_g, ple_w_gate, ple_b_gate, ple_w_proj):
    batch, seq, d = x.shape
    assert batch == 1
    h = x.reshape(seq, d)
    tabs_cos, tabs_sin = _rope_tables(positions, seq)

    cq, ckv, kr, qs, ks, vs = _even_proj(h, even_pre_g[0], _even_in_weight(even_w_in[0]),
                                         mla_q_norm_g[0], mla_kv_norm_g[0], tabs_cos, tabs_sin)
    qn, qr, kn, v = _mla_up(cq, ckv, _mla_q_weight(mla_w_q_up[0]), _mla_kv_weight(mla_w_kv_up[0]),
                            tabs_cos, tabs_sin)
    o_mla = _mla_attn(qn, qr, kn, kr, v)
    o_swa = _swa_attn(qs, ks, vs, swa_sinks[0])
    h, hn = _out_proj([o_mla, o_swa], even_w_out[0].astype(BF16), h, even_post_g[0], mlp_pre_g[0])
    h = _mlp(hn, mlp_w_up[0].astype(BF16), mlp_w_down[0].astype(BF16), h, mlp_post_g[0])
    h, hn = _ple(h, p[0, 0], ple_w_gate[0].astype(BF16), ple_b_gate[0], ple_w_proj[0].astype(BF16),
                 g_next=odd_pre_g[0])

    w_qkv = moba_w_qkv[0].astype(BF16)
    (q,) = _odd_proj(hn, w_qkv, 0, tabs_cos, tabs_sin, rope=True, scale=MOBA_HD ** -0.5 * LOG2E)
    k, kmean = _odd_proj(hn, w_qkv, 1, tabs_cos, tabs_sin, rope=True, kmean=True)
    (v,) = _odd_proj(hn, w_qkv, 2, tabs_cos, tabs_sin, rope=False)
    o = _moba_attn(q, k, v, kmean.reshape(seq // MOBA_BLOCK, MOBA_HEADS * MOBA_HD))
    h, hn = _out_proj([o], odd_w_out[0].astype(BF16), h, odd_post_g[0], mlp_pre_g[1])
    h = _mlp(hn, mlp_w_up[1].astype(BF16), mlp_w_down[1].astype(BF16), h, mlp_post_g[1])
    h, _ = _ple(h, p[1, 0], ple_w_gate[1].astype(BF16), ple_b_gate[1], ple_w_proj[1].astype(BF16))
    return h.reshape(batch, seq, d)
```

```python
import functools

import jax
import jax.numpy as jnp
import numpy as np
from jax import lax
from jax.experimental import pallas as pl
from jax.experimental.pallas import tpu as pltpu

F32 = jnp.float32
BF16 = jnp.bfloat16

NORM_EPS = 1e-6
ROPE_THETA = 10000.0

MLA_HEADS = 8
MLA_Q_RANK = 512
MLA_KV_RANK = 512
MLA_NOPE = 128
MLA_ROPE = 64
MLA_V = 128
SWA_HEADS = 16
SWA_KV_HEADS = 2
SWA_HD = 64
SWA_BLOCK = 128
MOBA_HEADS = 16
MOBA_HD = 128
MOBA_BLOCK = 256
MOBA_TOPK = 3

LANES = 128
V7X_VMEM_BYTES = 64 * 1024 * 1024
VMEM_LIMIT = 56 * 1024 * 1024
MASK_BIAS = -1e9
LOG2E = 1.4426950408889634
FLASH_TQ = 2048
FLASH_TK = 512


def _params(sem):
    return pltpu.CompilerParams(dimension_semantics=sem, vmem_limit_bytes=VMEM_LIMIT)


def _rms(t, g):
    return t * lax.rsqrt(jnp.mean(t * t, axis=-1, keepdims=True) + NORM_EPS) * g


def _rope(t, cos, sin_signed, half):
    width = t.shape[1]
    reps = width // LANES
    if reps > 1:
        cos = jnp.concatenate([cos] * reps, axis=1)
        sin_signed = jnp.concatenate([sin_signed] * reps, axis=1)
    lane = lax.broadcasted_iota(jnp.int32, t.shape, 1)
    first = (lane % (2 * half)) < half
    partner = jnp.where(first, pltpu.roll(t, width - half, 1), pltpu.roll(t, half, 1))
    return t * cos + partner * sin_signed


def _tables_kernel(pos_ref, invf_ref, sign_ref, cos_ref, sin_ref):
    ang = pos_ref[...].astype(F32) * invf_ref[...]
    cos_ref[...] = jnp.cos(ang)
    sin_ref[...] = jnp.sin(ang) * sign_ref[...]


def _rope_tables(positions, seq):
    def inv_freq(d):
        half = d // 2
        return jnp.power(ROPE_THETA, -jnp.arange(half, dtype=F32) * (2.0 / d))

    f64, f128 = inv_freq(64), inv_freq(128)
    invf = jnp.concatenate([f64, f64, f64, f64, f128, f128])[None, :]
    sign = np.concatenate([-np.ones(32), np.ones(32), -np.ones(32), np.ones(32),
                           -np.ones(64), np.ones(64)]).astype(np.float32)[None, :]
    tm = min(seq, 1024)
    return pl.pallas_call(
        _tables_kernel,
        grid=(seq // tm,),
        in_specs=[pl.BlockSpec((tm, 1), lambda i: (i, 0)),
                  pl.BlockSpec((1, 256), lambda i: (0, 0)),
                  pl.BlockSpec((1, 256), lambda i: (0, 0))],
        out_specs=[pl.BlockSpec((tm, 256), lambda i: (i, 0)),
                   pl.BlockSpec((tm, 256), lambda i: (i, 0))],
        out_shape=[jax.ShapeDtypeStruct((seq, 256), F32)] * 2,
        compiler_params=_params(("parallel",)),
        name="rope_tables",
    )(positions.reshape(seq, 1), invf, jnp.asarray(sign))


_EVEN_COLS = (0, 512, 1024, 1152, 2176, 2304, 2432)


def _even_proj_kernel(x_ref, gpre_ref, w_ref, gq_ref, gkv_ref, cos_ref, sin_ref,
                      cq_ref, ckv_ref, kr_ref, qs_ref, ks_ref, vs_ref):
    x = _rms(x_ref[...], gpre_ref[...]).astype(BF16)
    cos, sin = cos_ref[...], sin_ref[...]
    c = _EVEN_COLS

    def mm(k):
        return jnp.dot(x, w_ref[:, c[k]:c[k + 1]], preferred_element_type=F32)

    cq_ref[...] = _rms(mm(0), gq_ref[...]).astype(BF16)
    ckv_ref[...] = _rms(mm(1), gkv_ref[...]).astype(BF16)
    kr_ref[...] = _rope(mm(2), cos, sin, 32).astype(BF16)
    qs_ref[...] = (_rope(mm(3), cos, sin, 32) * (SWA_HD ** -0.5)).astype(BF16)
    ks_ref[...] = _rope(mm(4), cos, sin, 32).astype(BF16)
    vs_ref[...] = mm(5).astype(BF16)


def _even_proj(x, g_pre, w, gq, gkv, tabs_cos, tabs_sin):
    seq, d = x.shape
    tm = min(seq, 512)
    widths = [_EVEN_COLS[k + 1] - _EVEN_COLS[k] for k in range(6)]
    row = lambda i: (i, 0)
    fixed = lambda i: (0, 0)
    return pl.pallas_call(
        _even_proj_kernel,
        grid=(seq // tm,),
        in_specs=[pl.BlockSpec((tm, d), row),
                  pl.BlockSpec((1, d), fixed),
                  pl.BlockSpec(w.shape, fixed),
                  pl.BlockSpec((1, 512), fixed),
                  pl.BlockSpec((1, 512), fixed),
                  pl.BlockSpec((tm, LANES), row),
                  pl.BlockSpec((tm, LANES), row)],
        out_specs=[pl.BlockSpec((tm, n), row) for n in widths],
        out_shape=[jax.ShapeDtypeStruct((seq, n), BF16) for n in widths],
        compiler_params=_params(("parallel",)),
        name="even_proj",
    )(x, g_pre[None, :], w, gq[None, :], gkv[None, :], tabs_cos, tabs_sin)


def _mla_up_kernel(cq_ref, ckv_ref, wq_ref, wkv_ref, cos_ref, sin_ref,
                   qn_ref, qr_ref, kn_ref, v_ref, *, scale):
    cq, ckv = cq_ref[...], ckv_ref[...]
    n = MLA_HEADS * LANES
    qn = jnp.dot(cq, wq_ref[:, :n], preferred_element_type=F32)
    qr = jnp.dot(cq, wq_ref[:, n:], preferred_element_type=F32)
    qn_ref[...] = (qn * scale).astype(BF16)
    qr_ref[...] = (_rope(qr, cos_ref[...], sin_ref[...], 32) * scale).astype(BF16)
    kn_ref[...] = jnp.dot(ckv, wkv_ref[:, :n], preferred_element_type=F32).astype(BF16)
    v_ref[...] = jnp.dot(ckv, wkv_ref[:, n:], preferred_element_type=F32).astype(BF16)


def _mla_up(cq, ckv, wq, wkv, tabs_cos, tabs_sin):
    seq = cq.shape[0]
    tm = min(seq, 512)
    n = MLA_HEADS * LANES
    row = lambda i: (i, 0)
    fixed = lambda i: (0, 0)
    scale = (MLA_NOPE + MLA_ROPE) ** -0.5 * LOG2E
    return pl.pallas_call(
        functools.partial(_mla_up_kernel, scale=scale),
        grid=(seq // tm,),
        in_specs=[pl.BlockSpec((tm, MLA_Q_RANK), row),
                  pl.BlockSpec((tm, MLA_KV_RANK), row),
                  pl.BlockSpec(wq.shape, fixed),
                  pl.BlockSpec(wkv.shape, fixed),
                  pl.BlockSpec((tm, LANES), row),
                  pl.BlockSpec((tm, LANES), row)],
        out_specs=[pl.BlockSpec((tm, n), row)] * 4,
        out_shape=[jax.ShapeDtypeStruct((seq, n), BF16)] * 4,
        compiler_params=_params(("parallel",)),
        name="mla_up",
    )(cq, ckv, wq, wkv, tabs_cos, tabs_sin)


def _scores(q, k):
    return lax.dot_general(q, k, (((1,), (1,)), ((), ())), preferred_element_type=F32)


def _softmax_pv(s, v, m_ref, acc_ref, rows):
    m_old = m_ref[rows, :]
    m_new = jnp.maximum(m_old, jnp.max(s, axis=1, keepdims=True))
    alpha = jnp.exp2(m_old - m_new)
    p = jnp.exp2(s - jnp.tile(m_new, (1, s.shape[1] // LANES)))
    pv = jnp.dot(p.astype(BF16), v, preferred_element_type=F32)
    acc_ref[rows, :LANES] = alpha * acc_ref[rows, :LANES] + pv
    p_lanes = p[:, :LANES]
    for c in range(1, s.shape[1] // LANES):
        p_lanes = p_lanes + p[:, c * LANES:(c + 1) * LANES]
    acc_ref[rows, LANES:] = alpha * acc_ref[rows, LANES:] + p_lanes
    m_ref[rows, :] = m_new


def _flash_attend(q, kv, i, tq, tk, m_ref, acc_ref, o_ref):
    m_ref[...] = jnp.full(m_ref.shape, -jnp.inf, F32)
    acc_ref[...] = jnp.zeros(acc_ref.shape, F32)
    sub = tq // tk

    def past(g, carry):
        for d in range(sub):
            k, v = kv(g * sub + d)
            _softmax_pv(_scores(q, k), v, m_ref, acc_ref, slice(None))
        return carry

    lax.fori_loop(0, i, past, 0)

    for d in range(sub):
        n = tq - d * tk
        row = lax.broadcasted_iota(jnp.int32, (n, tk), 0)
        col = lax.broadcasted_iota(jnp.int32, (n, tk), 1)
        k, v = kv(i * sub + d)
        s = jnp.where(col <= row, _scores(q[d * tk:], k), -jnp.inf)
        _softmax_pv(s, v, m_ref, acc_ref, slice(d * tk, tq))

    acc = acc_ref[...]
    o_ref[...] = (acc[:, :LANES] / jnp.sum(acc[:, LANES:], axis=1, keepdims=True)).astype(BF16)


def _mla_attn_kernel(qn_ref, qr_ref, kn_ref, kr_ref, v_ref, o_ref, m_ref, acc_ref, *, tq, tk):
    i = pl.program_id(1)
    q = jnp.concatenate([qn_ref[...], qr_ref[...]], axis=1)

    def kv(j):
        rows = pl.ds(pl.multiple_of(j * tk, tk), tk)
        return jnp.concatenate([kn_ref[rows, :], kr_ref[rows, :]], axis=1), v_ref[rows, :]

    _flash_attend(q, kv, i, tq, tk, m_ref, acc_ref, o_ref)


def _mla_attn(qn, qr, kn, kr, v):
    seq = qn.shape[0]
    tq = min(seq, FLASH_TQ)
    tk = min(seq, FLASH_TK)
    return pl.pallas_call(
        functools.partial(_mla_attn_kernel, tq=tq, tk=tk),
        grid=(MLA_HEADS, seq // tq),
        in_specs=[pl.BlockSpec((tq, LANES), lambda h, i: (i, h)),
                  pl.BlockSpec((tq, LANES), lambda h, i: (i, h)),
                  pl.BlockSpec((seq, LANES), lambda h, i: (0, h)),
                  pl.BlockSpec((seq, LANES), lambda h, i: (0, 0)),
                  pl.BlockSpec((seq, LANES), lambda h, i: (0, h))],
        out_specs=pl.BlockSpec((tq, LANES), lambda h, i: (i, h)),
        out_shape=jax.ShapeDtypeStruct((seq, MLA_HEADS * MLA_V), BF16),
        scratch_shapes=[pltpu.VMEM((tq, LANES), F32), pltpu.VMEM((tq, 2 * LANES), F32)],
        compiler_params=_params(("parallel", "arbitrary")),
        name="mla_attn",
    )(qn, qr, kn, kr, v)


def _swa_kernel(sink_ref, q_ref, kc_ref, kp_ref, vc_ref, vp_ref, o_ref, *, tq):
    i = pl.program_id(0)
    nb = tq // SWA_BLOCK
    group = SWA_HEADS // SWA_KV_HEADS
    pairs = group // 2
    lane = lax.broadcasted_iota(jnp.int32, (2 * SWA_BLOCK, LANES), 1)
    shape = (pairs * SWA_BLOCK, 2 * SWA_BLOCK)
    qp = lax.broadcasted_iota(jnp.int32, shape, 0) % SWA_BLOCK + SWA_BLOCK
    kp = lax.broadcasted_iota(jnp.int32, shape, 1)
    band = (kp <= qp) & (qp - kp < SWA_BLOCK)
    out_lane = lax.broadcasted_iota(jnp.int32, (pairs * SWA_BLOCK, LANES), 1)
    key_row = lax.broadcasted_iota(jnp.int32, (4 * SWA_BLOCK, LANES), 0)
    key_lane = lax.broadcasted_iota(jnp.int32, (4 * SWA_BLOCK, LANES), 1)
    ones_by_head = jnp.where((key_row < 2 * SWA_BLOCK) == (key_lane < SWA_HD), 1.0, 0.0).astype(BF16)

    def split(t, c):
        mine = jnp.where((lane >= c * SWA_HD) & (lane < (c + 1) * SWA_HD), t, 0.0)
        other = pltpu.roll(mine, SWA_HD, 1)
        lo, hi = (mine, other) if c == 0 else (other, mine)
        return jnp.concatenate([lo, hi], axis=0).astype(BF16)

    for b in range(nb):
        rows = slice(b * SWA_BLOCK, (b + 1) * SWA_BLOCK)
        if b == 0:
            k_prev, v_prev = kp_ref[...], vp_ref[...]
        else:
            prev = slice((b - 1) * SWA_BLOCK, b * SWA_BLOCK)
            k_prev, v_prev = kc_ref[prev, :], vc_ref[prev, :]
        kw = jnp.concatenate([k_prev, kc_ref[rows, :]], axis=0).astype(F32)
        vw = jnp.concatenate([v_prev, vc_ref[rows, :]], axis=0).astype(F32)
        first_key = jnp.where(i * nb + b == 0, SWA_BLOCK, 0)
        valid = band & (kp >= first_key)
        for c in range(SWA_KV_HEADS):
            kcat = split(kw, c)
            vcat = split(vw, c)
            cols = [slice((c * pairs + a) * LANES, (c * pairs + a + 1) * LANES) for a in range(pairs)]
            q4 = jnp.concatenate([q_ref[rows, cs] for cs in cols], axis=0)
            s = lax.dot_general(q4, kcat, (((1,), (1,)), ((), ())), preferred_element_type=F32)
            ps, sink_terms = [], []
            for e in range(2):
                sink = jnp.concatenate(
                    [jnp.full((SWA_BLOCK, 1), sink_ref[c * group + 2 * a + e], F32) for a in range(pairs)], axis=0)
                se = jnp.where(valid, s[:, e * 2 * SWA_BLOCK:(e + 1) * 2 * SWA_BLOCK], -jnp.inf)
                m = jnp.maximum(jnp.max(se, axis=1, keepdims=True), sink)
                ps.append(jnp.exp(se - m).astype(BF16))
                sink_terms.append(jnp.exp(sink - m))
            pv = jnp.dot(jnp.concatenate(ps, axis=1), jnp.concatenate([vcat, ones_by_head], axis=1),
                         preferred_element_type=F32)
            den = pv[:, LANES:] + jnp.where(out_lane < SWA_HD, sink_terms[0], sink_terms[1])
            o4 = (pv[:, :LANES] / den).astype(BF16)
            for a, cs in enumerate(cols):
                o_ref[rows, cs] = o4[a * SWA_BLOCK:(a + 1) * SWA_BLOCK]


def _swa_attn(qs, ks, vs, sinks):
    seq = qs.shape[0]
    tq = min(seq, 512)
    nb = tq // SWA_BLOCK
    cur = lambda i: (i, 0)
    prev = lambda i: (jnp.maximum(i * nb - 1, 0), 0)
    return pl.pallas_call(
        functools.partial(_swa_kernel, tq=tq),
        grid=(seq // tq,),
        in_specs=[pl.BlockSpec(memory_space=pltpu.SMEM),
                  pl.BlockSpec((tq, SWA_HEADS * SWA_HD), cur),
                  pl.BlockSpec((tq, LANES), cur),
                  pl.BlockSpec((SWA_BLOCK, LANES), prev),
                  pl.BlockSpec((tq, LANES), cur),
                  pl.BlockSpec((SWA_BLOCK, LANES), prev)],
        out_specs=pl.BlockSpec((tq, SWA_HEADS * SWA_HD), cur),
        out_shape=jax.ShapeDtypeStruct((seq, SWA_HEADS * SWA_HD), BF16),
        compiler_params=_params(("parallel",)),
        name="swa_attn",
    )(sinks, qs, ks, ks, vs, vs)


def _out_proj_kernel(*refs, n_a):
    a_refs = refs[:n_a]
    w_ref, h_ref, gpost_ref, gnext_ref, hout_ref, hn_ref = refs[n_a:]
    m = None
    k0 = 0
    for a_ref in a_refs:
        kw = a_ref.shape[1]
        part = jnp.dot(a_ref[...], w_ref[k0:k0 + kw, :], preferred_element_type=F32)
        m = part if m is None else m + part
        k0 += kw
    h = h_ref[...] + _rms(m, gpost_ref[...])
    hout_ref[...] = h
    hn_ref[...] = _rms(h, gnext_ref[...]).astype(BF16)


def _out_proj(a_list, w, h, g_post, g_next):
    seq, d = h.shape
    tm = min(seq, 512)
    row = lambda i: (i, 0)
    fixed = lambda i: (0, 0)
    return pl.pallas_call(
        functools.partial(_out_proj_kernel, n_a=len(a_list)),
        grid=(seq // tm,),
        in_specs=[pl.BlockSpec((tm, a.shape[1]), row) for a in a_list]
        + [pl.BlockSpec(w.shape, fixed), pl.BlockSpec((tm, d), row),
           pl.BlockSpec((1, d), fixed), pl.BlockSpec((1, d), fixed)],
        out_specs=[pl.BlockSpec((tm, d), row), pl.BlockSpec((tm, d), row)],
        out_shape=[jax.ShapeDtypeStruct((seq, d), F32), jax.ShapeDtypeStruct((seq, d), BF16)],
        compiler_params=_params(("parallel",)),
        name="out_proj",
    )(*a_list, w, h, g_post[None, :], g_next[None, :])


def _mlp_kernel(x_ref, wu_ref, wd_ref, h_ref, g_ref, o_ref, acc_ref):
    f = pl.program_id(1)

    @pl.when(f == 0)
    def _():
        acc_ref[...] = jnp.zeros(acc_ref.shape, F32)

    a = jnp.maximum(jnp.dot(x_ref[...], wu_ref[...], preferred_element_type=F32), 0.0)
    acc_ref[...] += jnp.dot((a * a).astype(BF16), wd_ref[...], preferred_element_type=F32)

    @pl.when(f == pl.num_programs(1) - 1)
    def _():
        o_ref[...] = h_ref[...] + _rms(acc_ref[...], g_ref[...])


def _mlp(xn, w_up, w_down, h, g_post):
    seq, d = h.shape
    d_ff = w_up.shape[1]
    tm = min(seq, 512)
    tf = min(d_ff, 1024)
    return pl.pallas_call(
        _mlp_kernel,
        grid=(seq // tm, d_ff // tf),
        in_specs=[pl.BlockSpec((tm, d), lambda i, f: (i, 0)),
                  pl.BlockSpec((d, tf), lambda i, f: (0, f)),
                  pl.BlockSpec((tf, d), lambda i, f: (f, 0)),
                  pl.BlockSpec((tm, d), lambda i, f: (i, 0)),
                  pl.BlockSpec((1, d), lambda i, f: (0, 0))],
        out_specs=pl.BlockSpec((tm, d), lambda i, f: (i, 0)),
        out_shape=jax.ShapeDtypeStruct((seq, d), F32),
        scratch_shapes=[pltpu.VMEM((tm, d), F32)],
        compiler_params=_params(("parallel", "arbitrary")),
        name="mlp",
    )(xn, w_up, w_down, h, g_post[None, :])


def _ple_kernel(h_ref, p_ref, wg_ref, b_ref, wp_ref, *rest, with_next):
    h = h_ref[...]
    z = jnp.dot(h.astype(BF16), wg_ref[...], preferred_element_type=F32) + b_ref[...]
    gate = 1.0 / (1.0 + jnp.exp(-z))
    e = jnp.dot(p_ref[...].astype(BF16), wp_ref[...], preferred_element_type=F32)
    out = h + gate * e
    if with_next:
        gnext_ref, o_ref, hn_ref = rest
        hn_ref[...] = _rms(out, gnext_ref[...]).astype(BF16)
    else:
        (o_ref,) = rest
    o_ref[...] = out


def _ple(h, p, w_gate, b_gate, w_proj, g_next=None):
    seq, d = h.shape
    tm = min(seq, 512)
    row = lambda i: (i, 0)
    fixed = lambda i: (0, 0)
    with_next = g_next is not None
    in_specs = [pl.BlockSpec((tm, d), row), pl.BlockSpec((tm, p.shape[1]), row),
                pl.BlockSpec(w_gate.shape, fixed), pl.BlockSpec((1, d), fixed),
                pl.BlockSpec(w_proj.shape, fixed)]
    args = [h, p, w_gate, b_gate[None, :], w_proj]
    out_specs = [pl.BlockSpec((tm, d), row)]
    out_shape = [jax.ShapeDtypeStruct((seq, d), F32)]
    if with_next:
        in_specs.append(pl.BlockSpec((1, d), fixed))
        args.append(g_next[None, :])
        out_specs.append(pl.BlockSpec((tm, d), row))
        out_shape.append(jax.ShapeDtypeStruct((seq, d), BF16))
    res = pl.pallas_call(
        functools.partial(_ple_kernel, with_next=with_next),
        grid=(seq // tm,),
        in_specs=in_specs, out_specs=out_specs, out_shape=out_shape,
        compiler_params=_params(("parallel",)),
        name="ple",
    )(*args)
    return res if with_next else (res[0], None)


def _odd_proj_kernel(x_ref, w_ref, cos_ref, sin_ref, o_ref, *rest, rope, scale, kmean, tm):
    t = jnp.dot(x_ref[...], w_ref[...], preferred_element_type=F32)
    if rope:
        t = _rope(t, cos_ref[...], sin_ref[...], MOBA_HD // 2)
    if kmean:
        (km_ref,) = rest
        for b in range(tm // MOBA_BLOCK):
            blk = t[b * MOBA_BLOCK:(b + 1) * MOBA_BLOCK, :]
            km_ref[b] = jnp.sum(blk, axis=0, keepdims=True) * (1.0 / MOBA_BLOCK)
    if scale != 1.0:
        t = t * scale
    o_ref[...] = t.astype(BF16)


def _odd_proj(xn, w, col, tabs_cos, tabs_sin, *, rope, scale=1.0, kmean=False):
    seq, d = xn.shape
    n = d
    tm = min(seq, 512)
    row = lambda i: (i, 0)
    fixed = lambda i: (0, 0)
    out_specs = [pl.BlockSpec((tm, n), row)]
    out_shape = [jax.ShapeDtypeStruct((seq, n), BF16)]
    if kmean:
        nb = tm // MOBA_BLOCK
        out_specs.append(pl.BlockSpec((nb, 1, n), lambda i: (i, 0, 0)))
        out_shape.append(jax.ShapeDtypeStruct((seq // MOBA_BLOCK, 1, n), F32))
    res = pl.pallas_call(
        functools.partial(_odd_proj_kernel, rope=rope, scale=scale, kmean=kmean, tm=tm),
        grid=(seq // tm,),
        in_specs=[pl.BlockSpec((tm, d), row), pl.BlockSpec((d, n), lambda i: (0, col)),
                  pl.BlockSpec((tm, LANES), lambda i: (i, 1)),
                  pl.BlockSpec((tm, LANES), lambda i: (i, 1))],
        out_specs=out_specs, out_shape=out_shape,
        compiler_params=_params(("parallel",)),
        name="odd_proj",
    )(xn, w, tabs_cos, tabs_sin)
    return res


def _moba_kernel(q_ref, k_ref, v_ref, km_ref, o_ref, m_ref, acc_ref, *, tq, tk, nkb):
    i = pl.program_id(1)
    q = q_ref[...]

    km = km_ref[...]
    if nkb < 64:
        km = jnp.concatenate([km, jnp.zeros((64 - nkb, LANES), F32)], axis=0)
    km_hi = km.astype(BF16)
    km_mid = (km - km_hi.astype(F32)).astype(BF16)
    gate = lax.dot_general(jnp.concatenate([km_hi, km_mid], axis=1), jnp.concatenate([q, q], axis=1),
                           (((1,), (1,)), ((), ())), preferred_element_type=F32)

    blk = lax.broadcasted_iota(jnp.int32, (64, tq), 0)
    qblk = (i * tq + lax.broadcasted_iota(jnp.int32, (64, tq), 1)) // MOBA_BLOCK
    gm = jnp.where(blk < qblk, gate, -jnp.inf)
    sel = blk == qblk
    for r in range(MOBA_TOPK):
        mx = jnp.max(gm, axis=0, keepdims=True)
        idx = jnp.min(jnp.where(gm == mx, blk, 64), axis=0, keepdims=True)
        pick = (blk == idx) & (qblk > r)
        sel = sel | pick
        gm = jnp.where(pick, -jnp.inf, gm)
    bias_t = jnp.where(sel, 0.0, MASK_BIAS)
    bias = jnp.concatenate([bias_t, jnp.zeros((64, tq), F32)], axis=0).T.astype(BF16)
    qa = jnp.concatenate([q, bias], axis=1)

    kb_per_tile = tk // MOBA_BLOCK
    krow = lax.broadcasted_iota(jnp.int32, (tk, LANES), 0) // MOBA_BLOCK
    klane = lax.broadcasted_iota(jnp.int32, (tk, LANES), 1)

    def kv(j):
        rows = pl.ds(pl.multiple_of(j * tk, tk), tk)
        onehot = jnp.where(klane == krow + j * kb_per_tile, 1.0, 0.0).astype(BF16)
        return jnp.concatenate([k_ref[rows, :], onehot], axis=1), v_ref[rows, :]

    _flash_attend(qa, kv, i, tq, tk, m_ref, acc_ref, o_ref)


def _moba_attn(q, k, v, kmean):
    seq = q.shape[0]
    nkb = seq // MOBA_BLOCK
    assert nkb <= 64 and nkb % 8 == 0, "the gate matmul packs three 64-lane groups"
    tq = min(seq, FLASH_TQ)
    tk = min(seq, FLASH_TK)
    return pl.pallas_call(
        functools.partial(_moba_kernel, tq=tq, tk=tk, nkb=nkb),
        grid=(MOBA_HEADS, seq // tq),
        in_specs=[pl.BlockSpec((tq, LANES), lambda h, i: (i, h)),
                  pl.BlockSpec((seq, LANES), lambda h, i: (0, h)),
                  pl.BlockSpec((seq, LANES), lambda h, i: (0, h)),
                  pl.BlockSpec((nkb, LANES), lambda h, i: (0, h))],
        out_specs=pl.BlockSpec((tq, LANES), lambda h, i: (i, h)),
        out_shape=jax.ShapeDtypeStruct((seq, MOBA_HEADS * MOBA_HD), BF16),
        scratch_shapes=[pltpu.VMEM((tq, LANES), F32), pltpu.VMEM((tq, 2 * LANES), F32)],
        compiler_params=_params(("parallel", "arbitrary")),
        name="moba_attn",
    )(q, k, v, kmean)


def _even_in_weight(w_in):
    d = w_in.shape[0]
    cq, ckv, kr, qs, ks, vs = jnp.split(w_in, [512, 1024, 1088, 2112, 2240], axis=1)
    return jnp.concatenate([cq, ckv, kr, jnp.zeros((d, 64), w_in.dtype), qs, ks, vs], axis=1).astype(BF16)


def _mla_q_weight(w_q_up):
    r = w_q_up.shape[0]
    w = w_q_up.reshape(r, MLA_HEADS, MLA_NOPE + MLA_ROPE)
    nope = w[:, :, :MLA_NOPE].reshape(r, MLA_HEADS * MLA_NOPE)
    rope = jnp.pad(w[:, :, MLA_NOPE:], ((0, 0), (0, 0), (0, LANES - MLA_ROPE))).reshape(r, MLA_HEADS * LANES)
    return jnp.concatenate([nope, rope], axis=1).astype(BF16)


def _mla_kv_weight(w_kv_up):
    r = w_kv_up.shape[0]
    w = w_kv_up.reshape(r, MLA_HEADS, MLA_NOPE + MLA_V)
    nope = w[:, :, :MLA_NOPE].reshape(r, MLA_HEADS * MLA_NOPE)
    val = w[:, :, MLA_NOPE:].reshape(r, MLA_HEADS * MLA_V)
    return jnp.concatenate([nope, val], axis=1).astype(BF16)


def kernel(x, p, positions, even_pre_g, even_w_in, mla_q_norm_g, mla_w_q_up, mla_kv_norm_g, mla_w_kv_up, swa_sinks, even_w_out, even_post_g, odd_pre_g, moba_w_qkv, odd_w_out, odd_post_g, mlp_pre_g, mlp_w_up, mlp_w_down, mlp_post_g, ple_w_gate, ple_b_gate, ple_w_proj):
    batch, seq, d = x.shape
    assert batch == 1
    h = x.reshape(seq, d)
    tabs_cos, tabs_sin = _rope_tables(positions, seq)

    cq, ckv, kr, qs, ks, vs = _even_proj(h, even_pre_g[0], _even_in_weight(even_w_in[0]),
                                         mla_q_norm_g[0], mla_kv_norm_g[0], tabs_cos, tabs_sin)
    qn, qr, kn, v = _mla_up(cq, ckv, _mla_q_weight(mla_w_q_up[0]), _mla_kv_weight(mla_w_kv_up[0]),
                            tabs_cos, tabs_sin)
    o_mla = _mla_attn(qn, qr, kn, kr, v)
    o_swa = _swa_attn(qs, ks, vs, swa_sinks[0])
    h, hn = _out_proj([o_mla, o_swa], even_w_out[0].astype(BF16), h, even_post_g[0], mlp_pre_g[0])
    h = _mlp(hn, mlp_w_up[0].astype(BF16), mlp_w_down[0].astype(BF16), h, mlp_post_g[0])
    h, hn = _ple(h, p[0, 0], ple_w_gate[0].astype(BF16), ple_b_gate[0], ple_w_proj[0].astype(BF16),
                 g_next=odd_pre_g[0])

    w_qkv = moba_w_qkv[0].astype(BF16)
    (q,) = _odd_proj(hn, w_qkv, 0, tabs_cos, tabs_sin, rope=True, scale=MOBA_HD ** -0.5 * LOG2E)
    k, kmean = _odd_proj(hn, w_qkv, 1, tabs_cos, tabs_sin, rope=True, kmean=True)
    (v,) = _odd_proj(hn, w_qkv, 2, tabs_cos, tabs_sin, rope=False)
    o = _moba_attn(q, k, v, kmean.reshape(seq // MOBA_BLOCK, MOBA_HEADS * MOBA_HD))
    h, hn = _out_proj([o], odd_w_out[0].astype(BF16), h, odd_post_g[0], mlp_pre_g[1])
    h = _mlp(hn, mlp_w_up[1].astype(BF16), mlp_w_down[1].astype(BF16), h, mlp_post_g[1])
    h, _ = _ple(h, p[1, 0], ple_w_gate[1].astype(BF16), ple_b_gate[1], ple_w_proj[1].astype(BF16))
    return h.reshape(batch, seq, d)
```

```python
import functools

import jax
import jax.numpy as jnp
import numpy as np
from jax import lax
from jax.experimental import pallas as pl
from jax.experimental.pallas import tpu as pltpu

F32 = jnp.float32
BF16 = jnp.bfloat16

NORM_EPS = 1e-6
ROPE_THETA = 10000.0

MLA_HEADS = 8
MLA_Q_RANK = 512
MLA_KV_RANK = 512
MLA_NOPE = 128
MLA_ROPE = 64
MLA_V = 128
SWA_HEADS = 16
SWA_KV_HEADS = 2
SWA_HD = 64
SWA_BLOCK = 128
MOBA_HEADS = 16
MOBA_HD = 128
MOBA_BLOCK = 256
MOBA_TOPK = 3

LANES = 128
V7X_VMEM_BYTES = 64 * 1024 * 1024
VMEM_LIMIT = 56 * 1024 * 1024
MASK_BIAS = -1e30
LOG2E = 1.4426950408889634
FLASH_TQ = 2048
FLASH_TK = 512


def _params(sem):
    return pltpu.CompilerParams(dimension_semantics=sem, vmem_limit_bytes=VMEM_LIMIT)


def _rms(t, g):
    return t * lax.rsqrt(jnp.mean(t * t, axis=-1, keepdims=True) + NORM_EPS) * g


def _rope(t, cos, sin_signed, half):
    width = t.shape[1]
    reps = width // LANES
    if reps > 1:
        cos = jnp.concatenate([cos] * reps, axis=1)
        sin_signed = jnp.concatenate([sin_signed] * reps, axis=1)
    lane = lax.broadcasted_iota(jnp.int32, t.shape, 1)
    first = (lane % (2 * half)) < half
    partner = jnp.where(first, pltpu.roll(t, width - half, 1), pltpu.roll(t, half, 1))
    return t * cos + partner * sin_signed


def _tables_kernel(pos_ref, invf_ref, sign_ref, cos_ref, sin_ref):
    ang = pos_ref[...].astype(F32) * invf_ref[...]
    cos_ref[...] = jnp.cos(ang)
    sin_ref[...] = jnp.sin(ang) * sign_ref[...]


def _rope_tables(positions, seq):
    def inv_freq(d):
        half = d // 2
        return jnp.power(ROPE_THETA, -jnp.arange(half, dtype=F32) * (2.0 / d))

    f64, f128 = inv_freq(64), inv_freq(128)
    invf = jnp.concatenate([f64, f64, f64, f64, f128, f128])[None, :]
    sign = np.concatenate([-np.ones(32), np.ones(32), -np.ones(32), np.ones(32),
                           -np.ones(64), np.ones(64)]).astype(np.float32)[None, :]
    tm = min(seq, 1024)
    return pl.pallas_call(
        _tables_kernel,
        grid=(seq // tm,),
        in_specs=[pl.BlockSpec((tm, 1), lambda i: (i, 0)),
                  pl.BlockSpec((1, 256), lambda i: (0, 0)),
                  pl.BlockSpec((1, 256), lambda i: (0, 0))],
        out_specs=[pl.BlockSpec((tm, 256), lambda i: (i, 0)),
                   pl.BlockSpec((tm, 256), lambda i: (i, 0))],
        out_shape=[jax.ShapeDtypeStruct((seq, 256), F32)] * 2,
        compiler_params=_params(("parallel",)),
        name="rope_tables",
    )(positions.reshape(seq, 1), invf, jnp.asarray(sign))


_EVEN_COLS = (0, 512, 1024, 1152, 2176, 2304, 2432)


def _even_proj_kernel(x_ref, gpre_ref, w_ref, gq_ref, gkv_ref, cos_ref, sin_ref,
                      cq_ref, ckv_ref, kr_ref, qs_ref, ks_ref, vs_ref):
    x = _rms(x_ref[...], gpre_ref[...]).astype(BF16)
    cos, sin = cos_ref[...], sin_ref[...]
    c = _EVEN_COLS

    def mm(k):
        return jnp.dot(x, w_ref[:, c[k]:c[k + 1]], preferred_element_type=F32)

    cq_ref[...] = _rms(mm(0), gq_ref[...]).astype(BF16)
    ckv_ref[...] = _rms(mm(1), gkv_ref[...]).astype(BF16)
    kr_ref[...] = _rope(mm(2), cos, sin, 32).astype(BF16)
    qs_ref[...] = (_rope(mm(3), cos, sin, 32) * (SWA_HD ** -0.5)).astype(BF16)
    ks_ref[...] = _rope(mm(4), cos, sin, 32).astype(BF16)
    vs_ref[...] = mm(5).astype(BF16)


def _even_proj(x, g_pre, w, gq, gkv, tabs_cos, tabs_sin):
    seq, d = x.shape
    tm = min(seq, 512)
    widths = [_EVEN_COLS[k + 1] - _EVEN_COLS[k] for k in range(6)]
    row = lambda i: (i, 0)
    fixed = lambda i: (0, 0)
    return pl.pallas_call(
        _even_proj_kernel,
        grid=(seq // tm,),
        in_specs=[pl.BlockSpec((tm, d), row),
                  pl.BlockSpec((1, d), fixed),
                  pl.BlockSpec(w.shape, fixed),
                  pl.BlockSpec((1, 512), fixed),
                  pl.BlockSpec((1, 512), fixed),
                  pl.BlockSpec((tm, LANES), row),
                  pl.BlockSpec((tm, LANES), row)],
        out_specs=[pl.BlockSpec((tm, n), row) for n in widths],
        out_shape=[jax.ShapeDtypeStruct((seq, n), BF16) for n in widths],
        compiler_params=_params(("parallel",)),
        name="even_proj",
    )(x, g_pre[None, :], w, gq[None, :], gkv[None, :], tabs_cos, tabs_sin)


def _mla_up_kernel(cq_ref, ckv_ref, wq_ref, wkv_ref, cos_ref, sin_ref,
                   qn_ref, qr_ref, kn_ref, v_ref, *, scale):
    cq, ckv = cq_ref[...], ckv_ref[...]
    n = MLA_HEADS * LANES
    qn = jnp.dot(cq, wq_ref[:, :n], preferred_element_type=F32)
    qr = jnp.dot(cq, wq_ref[:, n:], preferred_element_type=F32)
    qn_ref[...] = (qn * scale).astype(BF16)
    qr_ref[...] = (_rope(qr, cos_ref[...], sin_ref[...], 32) * scale).astype(BF16)
    kn_ref[...] = jnp.dot(ckv, wkv_ref[:, :n], preferred_element_type=F32).astype(BF16)
    v_ref[...] = jnp.dot(ckv, wkv_ref[:, n:], preferred_element_type=F32).astype(BF16)


def _mla_up(cq, ckv, wq, wkv, tabs_cos, tabs_sin):
    seq = cq.shape[0]
    tm = min(seq, 512)
    n = MLA_HEADS * LANES
    row = lambda i: (i, 0)
    fixed = lambda i: (0, 0)
    scale = (MLA_NOPE + MLA_ROPE) ** -0.5 * LOG2E
    return pl.pallas_call(
        functools.partial(_mla_up_kernel, scale=scale),
        grid=(seq // tm,),
        in_specs=[pl.BlockSpec((tm, MLA_Q_RANK), row),
                  pl.BlockSpec((tm, MLA_KV_RANK), row),
                  pl.BlockSpec(wq.shape, fixed),
                  pl.BlockSpec(wkv.shape, fixed),
                  pl.BlockSpec((tm, LANES), row),
                  pl.BlockSpec((tm, LANES), row)],
        out_specs=[pl.BlockSpec((tm, n), row)] * 4,
        out_shape=[jax.ShapeDtypeStruct((seq, n), BF16)] * 4,
        compiler_params=_params(("parallel",)),
        name="mla_up",
    )(cq, ckv, wq, wkv, tabs_cos, tabs_sin)


def _scores(q, k):
    return lax.dot_general(q, k, (((1,), (1,)), ((), ())), preferred_element_type=F32)


def _softmax_pv(s, v, m_ref, acc_ref, rows):
    m_old = m_ref[rows, :]
    m_new = jnp.maximum(m_old, jnp.max(s, axis=1, keepdims=True))
    alpha = jnp.exp2(m_old - m_new)
    p = jnp.exp2(s - jnp.tile(m_new, (1, s.shape[1] // LANES)))
    pv = jnp.dot(p.astype(BF16), v, preferred_element_type=F32)
    acc_ref[rows, :LANES] = alpha * acc_ref[rows, :LANES] + pv
    p_lanes = p[:, :LANES]
    for c in range(1, s.shape[1] // LANES):
        p_lanes = p_lanes + p[:, c * LANES:(c + 1) * LANES]
    acc_ref[rows, LANES:] = alpha * acc_ref[rows, LANES:] + p_lanes
    m_ref[rows, :] = m_new


def _flash_attend(q, kv, i, tq, tk, m_ref, acc_ref, o_ref):
    m_ref[...] = jnp.full(m_ref.shape, -jnp.inf, F32)
    acc_ref[...] = jnp.zeros(acc_ref.shape, F32)
    sub = tq // tk

    def past(g, carry):
        for d in range(sub):
            k, v = kv(g * sub + d)
            _softmax_pv(_scores(q, k), v, m_ref, acc_ref, slice(None))
        return carry

    lax.fori_loop(0, i, past, 0)

    for d in range(sub):
        n = tq - d * tk
        row = lax.broadcasted_iota(jnp.int32, (n, tk), 0)
        col = lax.broadcasted_iota(jnp.int32, (n, tk), 1)
        k, v = kv(i * sub + d)
        s = jnp.where(col <= row, _scores(q[d * tk:], k), -jnp.inf)
        _softmax_pv(s, v, m_ref, acc_ref, slice(d * tk, tq))

    acc = acc_ref[...]
    o_ref[...] = (acc[:, :LANES] / jnp.sum(acc[:, LANES:], axis=1, keepdims=True)).astype(BF16)


def _mla_attn_kernel(qn_ref, qr_ref, kn_ref, kr_ref, v_ref, o_ref, m_ref, acc_ref, *, tq, tk):
    i = pl.program_id(1)
    q = jnp.concatenate([qn_ref[...], qr_ref[...]], axis=1)

    def kv(j):
        rows = pl.ds(pl.multiple_of(j * tk, tk), tk)
        return jnp.concatenate([kn_ref[rows, :], kr_ref[rows, :]], axis=1), v_ref[rows, :]

    _flash_attend(q, kv, i, tq, tk, m_ref, acc_ref, o_ref)


def _mla_attn(qn, qr, kn, kr, v):
    seq = qn.shape[0]
    tq = min(seq, FLASH_TQ)
    tk = min(seq, FLASH_TK)
    return pl.pallas_call(
        functools.partial(_mla_attn_kernel, tq=tq, tk=tk),
        grid=(MLA_HEADS, seq // tq),
        in_specs=[pl.BlockSpec((tq, LANES), lambda h, i: (i, h)),
                  pl.BlockSpec((tq, LANES), lambda h, i: (i, h)),
                  pl.BlockSpec((seq, LANES), lambda h, i: (0, h)),
                  pl.BlockSpec((seq, LANES), lambda h, i: (0, 0)),
                  pl.BlockSpec((seq, LANES), lambda h, i: (0, h))],
        out_specs=pl.BlockSpec((tq, LANES), lambda h, i: (i, h)),
        out_shape=jax.ShapeDtypeStruct((seq, MLA_HEADS * MLA_V), BF16),
        scratch_shapes=[pltpu.VMEM((tq, LANES), F32), pltpu.VMEM((tq, 2 * LANES), F32)],
        compiler_params=_params(("parallel", "arbitrary")),
        name="mla_attn",
    )(qn, qr, kn, kr, v)


def _swa_kernel(sink_ref, q_ref, kc_ref, kp_ref, vc_ref, vp_ref, o_ref, *, tq):
    i = pl.program_id(0)
    nb = tq // SWA_BLOCK
    group = SWA_HEADS // SWA_KV_HEADS
    pairs = group // 2
    lane = lax.broadcasted_iota(jnp.int32, (2 * SWA_BLOCK, LANES), 1)
    shape = (pairs * SWA_BLOCK, 2 * SWA_BLOCK)
    qp = lax.broadcasted_iota(jnp.int32, shape, 0) % SWA_BLOCK + SWA_BLOCK
    kp = lax.broadcasted_iota(jnp.int32, shape, 1)
    band = (kp <= qp) & (qp - kp < SWA_BLOCK)
    out_lane = lax.broadcasted_iota(jnp.int32, (pairs * SWA_BLOCK, LANES), 1)
    key_row = lax.broadcasted_iota(jnp.int32, (4 * SWA_BLOCK, LANES), 0)
    key_lane = lax.broadcasted_iota(jnp.int32, (4 * SWA_BLOCK, LANES), 1)
    ones_by_head = jnp.where((key_row < 2 * SWA_BLOCK) == (key_lane < SWA_HD), 1.0, 0.0).astype(BF16)

    def split(t, c):
        mine = jnp.where((lane >= c * SWA_HD) & (lane < (c + 1) * SWA_HD), t, 0.0)
        other = pltpu.roll(mine, SWA_HD, 1)
        lo, hi = (mine, other) if c == 0 else (other, mine)
        return jnp.concatenate([lo, hi], axis=0).astype(BF16)

    for b in range(nb):
        rows = slice(b * SWA_BLOCK, (b + 1) * SWA_BLOCK)
        if b == 0:
            k_prev, v_prev = kp_ref[...], vp_ref[...]
        else:
            prev = slice((b - 1) * SWA_BLOCK, b * SWA_BLOCK)
            k_prev, v_prev = kc_ref[prev, :], vc_ref[prev, :]
        kw = jnp.concatenate([k_prev, kc_ref[rows, :]], axis=0).astype(F32)
        vw = jnp.concatenate([v_prev, vc_ref[rows, :]], axis=0).astype(F32)
        first_key = jnp.where(i * nb + b == 0, SWA_BLOCK, 0)
        valid = band & (kp >= first_key)
        for c in range(SWA_KV_HEADS):
            kcat = split(kw, c)
            vcat = split(vw, c)
            cols = [slice((c * pairs + a) * LANES, (c * pairs + a + 1) * LANES) for a in range(pairs)]
            q4 = jnp.concatenate([q_ref[rows, cs] for cs in cols], axis=0)
            s = lax.dot_general(q4, kcat, (((1,), (1,)), ((), ())), preferred_element_type=F32)
            ps, sink_terms = [], []
            for e in range(2):
                sink = jnp.concatenate(
                    [jnp.full((SWA_BLOCK, 1), sink_ref[c * group + 2 * a + e], F32) for a in range(pairs)], axis=0)
                se = jnp.where(valid, s[:, e * 2 * SWA_BLOCK:(e + 1) * 2 * SWA_BLOCK], -jnp.inf)
                m = jnp.maximum(jnp.max(se, axis=1, keepdims=True), sink)
                ps.append(jnp.exp(se - m).astype(BF16))
                sink_terms.append(jnp.exp(sink - m))
            pv = jnp.dot(jnp.concatenate(ps, axis=1), jnp.concatenate([vcat, ones_by_head], axis=1),
                         preferred_element_type=F32)
            den = pv[:, LANES:] + jnp.where(out_lane < SWA_HD, sink_terms[0], sink_terms[1])
            o4 = (pv[:, :LANES] / den).astype(BF16)
            for a, cs in enumerate(cols):
                o_ref[rows, cs] = o4[a * SWA_BLOCK:(a + 1) * SWA_BLOCK]


def _swa_attn(qs, ks, vs, sinks):
    seq = qs.shape[0]
    tq = min(seq, 512)
    nb = tq // SWA_BLOCK
    cur = lambda i: (i, 0)
    prev = lambda i: (jnp.maximum(i * nb - 1, 0), 0)
    return pl.pallas_call(
        functools.partial(_swa_kernel, tq=tq),
        grid=(seq // tq,),
        in_specs=[pl.BlockSpec(memory_space=pltpu.SMEM),
                  pl.BlockSpec((tq, SWA_HEADS * SWA_HD), cur),
                  pl.BlockSpec((tq, LANES), cur),
                  pl.BlockSpec((SWA_BLOCK, LANES), prev),
                  pl.BlockSpec((tq, LANES), cur),
                  pl.BlockSpec((SWA_BLOCK, LANES), prev)],
        out_specs=pl.BlockSpec((tq, SWA_HEADS * SWA_HD), cur),
        out_shape=jax.ShapeDtypeStruct((seq, SWA_HEADS * SWA_HD), BF16),
        compiler_params=_params(("parallel",)),
        name="swa_attn",
    )(sinks, qs, ks, ks, vs, vs)


def _out_proj_kernel(*refs, n_a):
    a_refs = refs[:n_a]
    w_ref, h_ref, gpost_ref, gnext_ref, hout_ref, hn_ref = refs[n_a:]
    tm = h_ref.shape[0]
    for rows in [slice(r * tm // 4, (r + 1) * tm // 4) for r in range(4)]:
        m = None
        k0 = 0
        for a_ref in a_refs:
            kw = a_ref.shape[1]
            part = jnp.dot(a_ref[rows, :], w_ref[k0:k0 + kw, :], preferred_element_type=F32)
            m = part if m is None else m + part
            k0 += kw
        h = h_ref[rows, :] + _rms(m, gpost_ref[...])
        hout_ref[rows, :] = h
        hn_ref[rows, :] = _rms(h, gnext_ref[...]).astype(BF16)


def _out_proj(a_list, w, h, g_post, g_next):
    seq, d = h.shape
    tm = min(seq, 512)
    row = lambda i: (i, 0)
    fixed = lambda i: (0, 0)
    return pl.pallas_call(
        functools.partial(_out_proj_kernel, n_a=len(a_list)),
        grid=(seq // tm,),
        in_specs=[pl.BlockSpec((tm, a.shape[1]), row) for a in a_list]
        + [pl.BlockSpec(w.shape, fixed), pl.BlockSpec((tm, d), row),
           pl.BlockSpec((1, d), fixed), pl.BlockSpec((1, d), fixed)],
        out_specs=[pl.BlockSpec((tm, d), row), pl.BlockSpec((tm, d), row)],
        out_shape=[jax.ShapeDtypeStruct((seq, d), F32), jax.ShapeDtypeStruct((seq, d), BF16)],
        compiler_params=_params(("parallel",)),
        name="out_proj",
    )(*a_list, w, h, g_post[None, :], g_next[None, :])


def _mlp_kernel(x_ref, wu_ref, wd_ref, h_ref, g_ref, o_ref, acc_ref):
    f = pl.program_id(1)

    @pl.when(f == 0)
    def _():
        acc_ref[...] = jnp.zeros(acc_ref.shape, F32)

    a = jnp.maximum(jnp.dot(x_ref[...], wu_ref[...], preferred_element_type=F32), 0.0)
    acc_ref[...] += jnp.dot((a * a).astype(BF16), wd_ref[...], preferred_element_type=F32)

    @pl.when(f == pl.num_programs(1) - 1)
    def _():
        o_ref[...] = h_ref[...] + _rms(acc_ref[...], g_ref[...])


def _mlp(xn, w_up, w_down, h, g_post):
    seq, d = h.shape
    d_ff = w_up.shape[1]
    tm = min(seq, 512)
    tf = min(d_ff, 1024)
    return pl.pallas_call(
        _mlp_kernel,
        grid=(seq // tm, d_ff // tf),
        in_specs=[pl.BlockSpec((tm, d), lambda i, f: (i, 0)),
                  pl.BlockSpec((d, tf), lambda i, f: (0, f)),
                  pl.BlockSpec((tf, d), lambda i, f: (f, 0)),
                  pl.BlockSpec((tm, d), lambda i, f: (i, 0)),
                  pl.BlockSpec((1, d), lambda i, f: (0, 0))],
        out_specs=pl.BlockSpec((tm, d), lambda i, f: (i, 0)),
        out_shape=jax.ShapeDtypeStruct((seq, d), F32),
        scratch_shapes=[pltpu.VMEM((tm, d), F32)],
        compiler_params=_params(("parallel", "arbitrary")),
        name="mlp",
    )(xn, w_up, w_down, h, g_post[None, :])


def _ple_kernel(h_ref, p_ref, wg_ref, b_ref, wp_ref, *rest, with_next):
    h = h_ref[...]
    z = jnp.dot(h.astype(BF16), wg_ref[...], preferred_element_type=F32) + b_ref[...]
    gate = 1.0 / (1.0 + jnp.exp(-z))
    e = jnp.dot(p_ref[...].astype(BF16), wp_ref[...], preferred_element_type=F32)
    out = h + gate * e
    if with_next:
        gnext_ref, o_ref, hn_ref = rest
        hn_ref[...] = _rms(out, gnext_ref[...]).astype(BF16)
    else:
        (o_ref,) = rest
    o_ref[...] = out


def _ple(h, p, w_gate, b_gate, w_proj, g_next=None):
    seq, d = h.shape
    tm = min(seq, 512)
    row = lambda i: (i, 0)
    fixed = lambda i: (0, 0)
    with_next = g_next is not None
    in_specs = [pl.BlockSpec((tm, d), row), pl.BlockSpec((tm, p.shape[1]), row),
                pl.BlockSpec(w_gate.shape, fixed), pl.BlockSpec((1, d), fixed),
                pl.BlockSpec(w_proj.shape, fixed)]
    args = [h, p, w_gate, b_gate[None, :], w_proj]
    out_specs = [pl.BlockSpec((tm, d), row)]
    out_shape = [jax.ShapeDtypeStruct((seq, d), F32)]
    if with_next:
        in_specs.append(pl.BlockSpec((1, d), fixed))
        args.append(g_next[None, :])
        out_specs.append(pl.BlockSpec((tm, d), row))
        out_shape.append(jax.ShapeDtypeStruct((seq, d), BF16))
    res = pl.pallas_call(
        functools.partial(_ple_kernel, with_next=with_next),
        grid=(seq // tm,),
        in_specs=in_specs, out_specs=out_specs, out_shape=out_shape,
        compiler_params=_params(("parallel",)),
        name="ple",
    )(*args)
    return res if with_next else (res[0], None)


def _odd_proj_kernel(x_ref, w_ref, cos_ref, sin_ref, o_ref, *rest, rope, scale, kmean, tm):
    t = jnp.dot(x_ref[...], w_ref[...], preferred_element_type=F32)
    if rope:
        t = _rope(t, cos_ref[...], sin_ref[...], MOBA_HD // 2)
    if kmean:
        (km_ref,) = rest
        for b in range(tm // MOBA_BLOCK):
            blk = t[b * MOBA_BLOCK:(b + 1) * MOBA_BLOCK, :]
            km_ref[b] = jnp.sum(blk, axis=0, keepdims=True) * (1.0 / MOBA_BLOCK)
    if scale != 1.0:
        t = t * scale
    o_ref[...] = t.astype(BF16)


def _odd_proj(xn, w, col, tabs_cos, tabs_sin, *, rope, scale=1.0, kmean=False):
    seq, d = xn.shape
    n = d
    tm = min(seq, 512)
    row = lambda i: (i, 0)
    fixed = lambda i: (0, 0)
    out_specs = [pl.BlockSpec((tm, n), row)]
    out_shape = [jax.ShapeDtypeStruct((seq, n), BF16)]
    if kmean:
        nb = tm // MOBA_BLOCK
        out_specs.append(pl.BlockSpec((nb, 1, n), lambda i: (i, 0, 0)))
        out_shape.append(jax.ShapeDtypeStruct((seq // MOBA_BLOCK, 1, n), F32))
    res = pl.pallas_call(
        functools.partial(_odd_proj_kernel, rope=rope, scale=scale, kmean=kmean, tm=tm),
        grid=(seq // tm,),
        in_specs=[pl.BlockSpec((tm, d), row), pl.BlockSpec((d, n), lambda i: (0, col)),
                  pl.BlockSpec((tm, LANES), lambda i: (i, 1)),
                  pl.BlockSpec((tm, LANES), lambda i: (i, 1))],
        out_specs=out_specs, out_shape=out_shape,
        compiler_params=_params(("parallel",)),
        name="odd_proj",
    )(xn, w, tabs_cos, tabs_sin)
    return res


def _moba_kernel(q_ref, k_ref, v_ref, km_ref, o_ref, m_ref, acc_ref, *, tq, tk, nkb):
    i = pl.program_id(1)
    q = q_ref[...]

    km = km_ref[...]
    if nkb < 64:
        km = jnp.concatenate([km, jnp.zeros((64 - nkb, LANES), F32)], axis=0)
    km_hi = km.astype(BF16)
    km_mid = (km - km_hi.astype(F32)).astype(BF16)
    gate = lax.dot_general(jnp.concatenate([km_hi, km_mid], axis=1), jnp.concatenate([q, q], axis=1),
                           (((1,), (1,)), ((), ())), preferred_element_type=F32)

    blk = lax.broadcasted_iota(jnp.int32, (64, tq), 0)
    qblk = (i * tq + lax.broadcasted_iota(jnp.int32, (64, tq), 1)) // MOBA_BLOCK
    gm = jnp.where(blk < qblk, gate, -jnp.inf)
    sel = blk == qblk
    for r in range(MOBA_TOPK):
        mx = jnp.max(gm, axis=0, keepdims=True)
        idx = jnp.min(jnp.where(gm == mx, blk, 64), axis=0, keepdims=True)
        pick = (blk == idx) & (qblk > r)
        sel = sel | pick
        gm = jnp.where(pick, -jnp.inf, gm)
    bias_t = jnp.where(sel, 0.0, MASK_BIAS)
    bias = jnp.concatenate([bias_t, jnp.zeros((64, tq), F32)], axis=0).T.astype(BF16)
    qa = jnp.concatenate([q, bias], axis=1)

    kb_per_tile = tk // MOBA_BLOCK
    krow = lax.broadcasted_iota(jnp.int32, (tk, LANES), 0) // MOBA_BLOCK
    klane = lax.broadcasted_iota(jnp.int32, (tk, LANES), 1)

    def kv(j):
        rows = pl.ds(pl.multiple_of(j * tk, tk), tk)
        onehot = jnp.where(klane == krow + j * kb_per_tile, 1.0, 0.0).astype(BF16)
        return jnp.concatenate([k_ref[rows, :], onehot], axis=1), v_ref[rows, :]

    _flash_attend(qa, kv, i, tq, tk, m_ref, acc_ref, o_ref)


def _moba_attn(q, k, v, kmean):
    seq = q.shape[0]
    nkb = seq // MOBA_BLOCK
    assert nkb <= 64 and nkb % 8 == 0, "the gate matmul packs three 64-lane groups"
    tq = min(seq, FLASH_TQ)
    tk = min(seq, FLASH_TK)
    return pl.pallas_call(
        functools.partial(_moba_kernel, tq=tq, tk=tk, nkb=nkb),
        grid=(MOBA_HEADS, seq // tq),
        in_specs=[pl.BlockSpec((tq, LANES), lambda h, i: (i, h)),
                  pl.BlockSpec((seq, LANES), lambda h, i: (0, h)),
                  pl.BlockSpec((seq, LANES), lambda h, i: (0, h)),
                  pl.BlockSpec((nkb, LANES), lambda h, i: (0, h))],
        out_specs=pl.BlockSpec((tq, LANES), lambda h, i: (i, h)),
        out_shape=jax.ShapeDtypeStruct((seq, MOBA_HEADS * MOBA_HD), BF16),
        scratch_shapes=[pltpu.VMEM((tq, LANES), F32), pltpu.VMEM((tq, 2 * LANES), F32)],
        compiler_params=_params(("parallel", "arbitrary")),
        name="moba_attn",
    )(q, k, v, kmean)


def _even_in_weight(w_in):
    d = w_in.shape[0]
    cq, ckv, kr, qs, ks, vs = jnp.split(w_in, [512, 1024, 1088, 2112, 2240], axis=1)
    return jnp.concatenate([cq, ckv, kr, jnp.zeros((d, 64), w_in.dtype), qs, ks, vs], axis=1).astype(BF16)


def _mla_q_weight(w_q_up):
    r = w_q_up.shape[0]
    w = w_q_up.reshape(r, MLA_HEADS, MLA_NOPE + MLA_ROPE)
    nope = w[:, :, :MLA_NOPE].reshape(r, MLA_HEADS * MLA_NOPE)
    rope = jnp.pad(w[:, :, MLA_NOPE:], ((0, 0), (0, 0), (0, LANES - MLA_ROPE))).reshape(r, MLA_HEADS * LANES)
    return jnp.concatenate([nope, rope], axis=1).astype(BF16)


def _mla_kv_weight(w_kv_up):
    r = w_kv_up.shape[0]
    w = w_kv_up.reshape(r, MLA_HEADS, MLA_NOPE + MLA_V)
    nope = w[:, :, :MLA_NOPE].reshape(r, MLA_HEADS * MLA_NOPE)
    val = w[:, :, MLA_NOPE:].reshape(r, MLA_HEADS * MLA_V)
    return jnp.concatenate([nope, val], axis=1).astype(BF16)


def kernel(x, p, positions, even_pre_g, even_w_in, mla_q_norm_g, mla_w_q_up, mla_kv_norm_g, mla_w_kv_up, swa_sinks, even_w_out, even_post_g, odd_pre_g, moba_w_qkv, odd_w_out, odd_post_g, mlp_pre_g, mlp_w_up, mlp_w_down, mlp_post_g, ple_w_gate, ple_b_gate, ple_w_proj):
    batch, seq, d = x.shape
    assert batch == 1
    h = x.reshape(seq, d)
    tabs_cos, tabs_sin = _rope_tables(positions, seq)

    cq, ckv, kr, qs, ks, vs = _even_proj(h, even_pre_g[0], _even_in_weight(even_w_in[0]),
                                         mla_q_norm_g[0], mla_kv_norm_g[0], tabs_cos, tabs_sin)
    qn, qr, kn, v = _mla_up(cq, ckv, _mla_q_weight(mla_w_q_up[0]), _mla_kv_weight(mla_w_kv_up[0]),
                            tabs_cos, tabs_sin)
    o_mla = _mla_attn(qn, qr, kn, kr, v)
    o_swa = _swa_attn(qs, ks, vs, swa_sinks[0])
    h, hn = _out_proj([o_mla, o_swa], even_w_out[0].astype(BF16), h, even_post_g[0], mlp_pre_g[0])
    h = _mlp(hn, mlp_w_up[0].astype(BF16), mlp_w_down[0].astype(BF16), h, mlp_post_g[0])
    h, hn = _ple(h, p[0, 0], ple_w_gate[0].astype(BF16), ple_b_gate[0], ple_w_proj[0].astype(BF16),
                 g_next=odd_pre_g[0])

    w_qkv = moba_w_qkv[0].astype(BF16)
    (q,) = _odd_proj(hn, w_qkv, 0, tabs_cos, tabs_sin, rope=True, scale=MOBA_HD ** -0.5 * LOG2E)
    k, kmean = _odd_proj(hn, w_qkv, 1, tabs_cos, tabs_sin, rope=True, kmean=True)
    (v,) = _odd_proj(hn, w_qkv, 2, tabs_cos, tabs_sin, rope=False)
    o = _moba_attn(q, k, v, kmean.reshape(seq // MOBA_BLOCK, MOBA_HEADS * MOBA_HD))
    h, hn = _out_proj([o], odd_w_out[0].astype(BF16), h, odd_post_g[0], mlp_pre_g[1])
    h = _mlp(hn, mlp_w_up[1].astype(BF16), mlp_w_down[1].astype(BF16), h, mlp_post_g[1])
    h, _ = _ple(h, p[1, 0], ple_w_gate[1].astype(BF16), ple_b_gate[1], ple_w_proj[1].astype(BF16))
    return h.reshape(batch, seq, d)
```

```python
import functools

import jax
import jax.numpy as jnp
import numpy as np
from jax import lax
from jax.experimental import pallas as pl
from jax.experimental.pallas import tpu as pltpu

F32 = jnp.float32
BF16 = jnp.bfloat16

NORM_EPS = 1e-6
ROPE_THETA = 10000.0

MLA_HEADS = 8
MLA_Q_RANK = 512
MLA_KV_RANK = 512
MLA_NOPE = 128
MLA_ROPE = 64
MLA_V = 128
SWA_HEADS = 16
SWA_KV_HEADS = 2
SWA_HD = 64
SWA_BLOCK = 128
MOBA_HEADS = 16
MOBA_HD = 128
MOBA_BLOCK = 256
MOBA_TOPK = 3

LANES = 128
V7X_VMEM_BYTES = 64 * 1024 * 1024
VMEM_LIMIT = 56 * 1024 * 1024
MASK_BIAS = -1e30
LOG2E = 1.4426950408889634
FLASH_TQ = 2048
FLASH_TK = 512
T_CHUNK = 256
T_AHEAD = 3


def _params(sem, flags=None):
    return pltpu.CompilerParams(dimension_semantics=sem, vmem_limit_bytes=VMEM_LIMIT, flags=flags)


def _rms(t, g):
    return t * lax.rsqrt(jnp.mean(t * t, axis=-1, keepdims=True) + NORM_EPS) * g


def _rope(t, cos, sin_signed, half):
    width = t.shape[1]
    reps = width // LANES
    if reps > 1:
        cos = jnp.concatenate([cos] * reps, axis=1)
        sin_signed = jnp.concatenate([sin_signed] * reps, axis=1)
    lane = lax.broadcasted_iota(jnp.int32, t.shape, 1)
    first = (lane % (2 * half)) < half
    partner = jnp.where(first, pltpu.roll(t, width - half, 1), pltpu.roll(t, half, 1))
    return t * cos + partner * sin_signed


def _tables_kernel(pos_ref, invf_ref, sign_ref, cos_ref, sin_ref):
    ang = pos_ref[...].astype(F32) * invf_ref[...]
    cos_ref[...] = jnp.cos(ang)
    sin_ref[...] = jnp.sin(ang) * sign_ref[...]


def _rope_tables(positions, seq):
    def inv_freq(d):
        half = d // 2
        return jnp.power(ROPE_THETA, -jnp.arange(half, dtype=F32) * (2.0 / d))

    f64, f128 = inv_freq(64), inv_freq(128)
    invf = jnp.concatenate([f64, f64, f64, f64, f128, f128])[None, :]
    sign = np.concatenate([-np.ones(32), np.ones(32), -np.ones(32), np.ones(32),
                           -np.ones(64), np.ones(64)]).astype(np.float32)[None, :]
    tm = min(seq, 1024)
    return pl.pallas_call(
        _tables_kernel,
        grid=(seq // tm,),
        in_specs=[pl.BlockSpec((tm, 1), lambda i: (i, 0)),
                  pl.BlockSpec((1, 256), lambda i: (0, 0)),
                  pl.BlockSpec((1, 256), lambda i: (0, 0))],
        out_specs=[pl.BlockSpec((tm, 256), lambda i: (i, 0)),
                   pl.BlockSpec((tm, 256), lambda i: (i, 0))],
        out_shape=[jax.ShapeDtypeStruct((seq, 256), F32)] * 2,
        compiler_params=_params(("parallel",)),
        name="rope_tables",
    )(positions.reshape(seq, 1), invf, jnp.asarray(sign))


_EVEN_COLS = (0, 512, 1024, 1152, 2176, 2304, 2432)


def _even_proj_kernel(x_ref, gpre_ref, w_ref, gq_ref, gkv_ref, cos_ref, sin_ref,
                      cq_ref, ckv_ref, kr_ref, qs_ref, ks_ref, vs_ref):
    x = _rms(x_ref[...], gpre_ref[...]).astype(BF16)
    cos, sin = cos_ref[...], sin_ref[...]
    c = _EVEN_COLS

    def mm(k):
        return jnp.dot(x, w_ref[:, c[k]:c[k + 1]], preferred_element_type=F32)

    cq_ref[...] = _rms(mm(0), gq_ref[...]).astype(BF16)
    ckv_ref[...] = _rms(mm(1), gkv_ref[...]).astype(BF16)
    kr_ref[...] = _rope(mm(2), cos, sin, 32).astype(BF16)
    qs_ref[...] = (_rope(mm(3), cos, sin, 32) * (SWA_HD ** -0.5)).astype(BF16)
    ks_ref[...] = _rope(mm(4), cos, sin, 32).astype(BF16)
    vs_ref[...] = mm(5).astype(BF16)


def _even_proj(x, g_pre, w, gq, gkv, tabs_cos, tabs_sin):
    seq, d = x.shape
    tm = min(seq, 512)
    widths = [_EVEN_COLS[k + 1] - _EVEN_COLS[k] for k in range(6)]
    row = lambda i: (i, 0)
    fixed = lambda i: (0, 0)
    return pl.pallas_call(
        _even_proj_kernel,
        grid=(seq // tm,),
        in_specs=[pl.BlockSpec((tm, d), row),
                  pl.BlockSpec((1, d), fixed),
                  pl.BlockSpec(w.shape, fixed),
                  pl.BlockSpec((1, 512), fixed),
                  pl.BlockSpec((1, 512), fixed),
                  pl.BlockSpec((tm, LANES), row),
                  pl.BlockSpec((tm, LANES), row)],
        out_specs=[pl.BlockSpec((tm, n), row) for n in widths],
        out_shape=[jax.ShapeDtypeStruct((seq, n), BF16) for n in widths],
        compiler_params=_params(("parallel",)),
        name="even_proj",
    )(x, g_pre[None, :], w, gq[None, :], gkv[None, :], tabs_cos, tabs_sin)


def _mla_up_kernel(cq_ref, ckv_ref, wq_ref, wkv_ref, cos_ref, sin_ref,
                   qn_ref, qr_ref, kn_ref, v_ref, *, scale):
    cq, ckv = cq_ref[...], ckv_ref[...]
    n = MLA_HEADS * LANES
    qn = jnp.dot(cq, wq_ref[:, :n], preferred_element_type=F32)
    qr = jnp.dot(cq, wq_ref[:, n:], preferred_element_type=F32)
    qn_ref[...] = (qn * scale).astype(BF16)
    qr_ref[...] = (_rope(qr, cos_ref[...], sin_ref[...], 32) * scale).astype(BF16)
    kn_ref[...] = jnp.dot(ckv, wkv_ref[:, :n], preferred_element_type=F32).astype(BF16)
    v_ref[...] = jnp.dot(ckv, wkv_ref[:, n:], preferred_element_type=F32).astype(BF16)


def _mla_up(cq, ckv, wq, wkv, tabs_cos, tabs_sin):
    seq = cq.shape[0]
    tm = min(seq, 512)
    n = MLA_HEADS * LANES
    row = lambda i: (i, 0)
    fixed = lambda i: (0, 0)
    scale = (MLA_NOPE + MLA_ROPE) ** -0.5 * LOG2E
    return pl.pallas_call(
        functools.partial(_mla_up_kernel, scale=scale),
        grid=(seq // tm,),
        in_specs=[pl.BlockSpec((tm, MLA_Q_RANK), row),
                  pl.BlockSpec((tm, MLA_KV_RANK), row),
                  pl.BlockSpec(wq.shape, fixed),
                  pl.BlockSpec(wkv.shape, fixed),
                  pl.BlockSpec((tm, LANES), row),
                  pl.BlockSpec((tm, LANES), row)],
        out_specs=[pl.BlockSpec((tm, n), row)] * 4,
        out_shape=[jax.ShapeDtypeStruct((seq, n), BF16)] * 4,
        compiler_params=_params(("parallel",)),
        name="mla_up",
    )(cq, ckv, wq, wkv, tabs_cos, tabs_sin)


def _scores(q, k):
    return lax.dot_general(q, k, (((1,), (1,)), ((), ())), preferred_element_type=F32)


def _softmax_pv(s, v, m_ref, acc_ref, rows):
    m_old = m_ref[rows, :]
    m_new = jnp.maximum(m_old, jnp.max(s, axis=1, keepdims=True))
    alpha = jnp.exp2(m_old - m_new)
    p = jnp.exp2(s - jnp.tile(m_new, (1, s.shape[1] // LANES)))
    pv = jnp.dot(p.astype(BF16), v, preferred_element_type=F32)
    acc_ref[rows, :LANES] = alpha * acc_ref[rows, :LANES] + pv
    p_lanes = p[:, :LANES]
    for c in range(1, s.shape[1] // LANES):
        p_lanes = p_lanes + p[:, c * LANES:(c + 1) * LANES]
    acc_ref[rows, LANES:] = alpha * acc_ref[rows, LANES:] + p_lanes
    m_ref[rows, :] = m_new


def _flash_attend(q, kv, i, tq, tk, m_ref, acc_ref, o_ref):
    m_ref[...] = jnp.full(m_ref.shape, -jnp.inf, F32)
    acc_ref[...] = jnp.zeros(acc_ref.shape, F32)
    sub = tq // tk

    def past(g, carry):
        for d in range(sub):
            k, v = kv(g * sub + d)
            _softmax_pv(_scores(q, k), v, m_ref, acc_ref, slice(None))
        return carry

    lax.fori_loop(0, i, past, 0)

    for d in range(sub):
        n = tq - d * tk
        row = lax.broadcasted_iota(jnp.int32, (n, tk), 0)
        col = lax.broadcasted_iota(jnp.int32, (n, tk), 1)
        k, v = kv(i * sub + d)
        s = jnp.where(col <= row, _scores(q[d * tk:], k), -jnp.inf)
        _softmax_pv(s, v, m_ref, acc_ref, slice(d * tk, tq))

    acc = acc_ref[...]
    o_ref[...] = (acc[:, :LANES] / jnp.sum(acc[:, LANES:], axis=1, keepdims=True)).astype(BF16)


def _mla_attn_kernel(qn_ref, qr_ref, kn_ref, kr_ref, v_ref, o_ref, m_ref, acc_ref, *, tq, tk):
    i = pl.program_id(1)
    q = jnp.concatenate([qn_ref[...], qr_ref[...]], axis=1)

    def kv(j):
        rows = pl.ds(pl.multiple_of(j * tk, tk), tk)
        return jnp.concatenate([kn_ref[rows, :], kr_ref[rows, :]], axis=1), v_ref[rows, :]

    _flash_attend(q, kv, i, tq, tk, m_ref, acc_ref, o_ref)


def _mla_attn(qn, qr, kn, kr, v):
    seq = qn.shape[0]
    tq = min(seq, FLASH_TQ)
    tk = min(seq, FLASH_TK)
    return pl.pallas_call(
        functools.partial(_mla_attn_kernel, tq=tq, tk=tk),
        grid=(MLA_HEADS, seq // tq),
        in_specs=[pl.BlockSpec((tq, LANES), lambda h, i: (i, h)),
                  pl.BlockSpec((tq, LANES), lambda h, i: (i, h)),
                  pl.BlockSpec((seq, LANES), lambda h, i: (0, h)),
                  pl.BlockSpec((seq, LANES), lambda h, i: (0, 0)),
                  pl.BlockSpec((seq, LANES), lambda h, i: (0, h))],
        out_specs=pl.BlockSpec((tq, LANES), lambda h, i: (i, h)),
        out_shape=jax.ShapeDtypeStruct((seq, MLA_HEADS * MLA_V), BF16),
        scratch_shapes=[pltpu.VMEM((tq, LANES), F32), pltpu.VMEM((tq, 2 * LANES), F32)],
        compiler_params=_params(("parallel", "arbitrary")),
        name="mla_attn",
    )(qn, qr, kn, kr, v)


def _softmax_pv_t(st, vt, m_ref, l_ref, acc_ref, cols):
    m_old = m_ref[:, cols]
    m_new = jnp.maximum(m_old, jnp.max(st, axis=0, keepdims=True))
    alpha = jnp.exp2(m_old - m_new)
    pt = jnp.exp2(st - m_new)
    l_ref[:, cols] = alpha * l_ref[:, cols] + jnp.sum(pt, axis=0, keepdims=True)
    acc_ref[:, cols] = alpha * acc_ref[:, cols] + jnp.dot(vt, pt.astype(BF16), preferred_element_type=F32)
    m_ref[:, cols] = m_new


def _flash_attend_t(qt, kv, i, tq, tk, stats, o_ref):
    nc = tq // T_CHUNK
    m_refs, l_refs, acc_refs = stats[:nc], stats[nc:2 * nc], stats[2 * nc:]
    for c in range(nc):
        m_refs[c][...] = jnp.full(m_refs[c].shape, -jnp.inf, F32)
        l_refs[c][...] = jnp.zeros(l_refs[c].shape, F32)
        acc_refs[c][...] = jnp.zeros(acc_refs[c].shape, F32)
    sub = tq // tk
    qts = [qt[:, c * T_CHUNK:(c + 1) * T_CHUNK] for c in range(nc)]
    key = lax.broadcasted_iota(jnp.int32, (tk, T_CHUNK), 0)
    qry = lax.broadcasted_iota(jnp.int32, (tk, T_CHUNK), 1)

    def run(work, kvs, diagonal):
        def score(n):
            d, c = work[n]
            st = jnp.dot(kvs[d][0], qts[c], preferred_element_type=F32)
            off = c * T_CHUNK - d * tk
            if diagonal and off < tk:
                st = jnp.where(key <= qry + off, st, -jnp.inf)
            return st

        ahead = [score(n) for n in range(min(T_AHEAD, len(work)))]
        for n, (d, c) in enumerate(work):
            if n + T_AHEAD < len(work):
                ahead.append(score(n + T_AHEAD))
            _softmax_pv_t(ahead.pop(0), kvs[d][1], m_refs[c], l_refs[c], acc_refs[c], slice(None))

    def past(g, carry):
        run([(d, c) for d in range(sub) for c in range(nc)], [kv(g * sub + d) for d in range(sub)], False)
        return carry

    lax.fori_loop(0, i, past, 0)

    run([(d, c) for d in range(sub) for c in range(d * tk // T_CHUNK, nc)],
        [kv(i * sub + d) for d in range(sub)], True)


    out_t = jnp.concatenate([acc_refs[c][...] / l_refs[c][...] for c in range(nc)], axis=1)
    o_ref[...] = out_t.T.astype(BF16)


def _flash_t_scratch(tq):
    nc = tq // T_CHUNK
    return [pltpu.VMEM((1, T_CHUNK), F32)] * (2 * nc) + [pltpu.VMEM((LANES, T_CHUNK), F32)] * nc


def _mla_attn_t_kernel(qn_ref, qr_ref, kn_ref, kr_ref, vt_ref, o_ref, *stats, tq, tk):
    i = pl.program_id(1)
    q = jnp.concatenate([qn_ref[...], qr_ref[...]], axis=1)
    qt = q.astype(F32).T.astype(BF16)

    def kv(j):
        rows = pl.ds(pl.multiple_of(j * tk, tk), tk)
        return jnp.concatenate([kn_ref[rows, :], kr_ref[rows, :]], axis=1), vt_ref[j]

    _flash_attend_t(qt, kv, i, tq, tk, stats, o_ref)


def _mla_attn_t(qn, qr, kn, kr, v):
    seq = qn.shape[0]
    tq = min(seq, 2048)
    tk = min(seq, 512)
    nk = seq // tk
    vt = v.reshape(nk, tk, MLA_HEADS, LANES).transpose(2, 0, 3, 1)
    return pl.pallas_call(
        functools.partial(_mla_attn_t_kernel, tq=tq, tk=tk),
        grid=(MLA_HEADS, seq // tq),
        in_specs=[pl.BlockSpec((tq, LANES), lambda h, i: (i, h)),
                  pl.BlockSpec((tq, LANES), lambda h, i: (i, h)),
                  pl.BlockSpec((seq, LANES), lambda h, i: (0, h)),
                  pl.BlockSpec((seq, LANES), lambda h, i: (0, 0)),
                  pl.BlockSpec((None, nk, LANES, tk), lambda h, i: (h, 0, 0, 0))],
        out_specs=pl.BlockSpec((tq, LANES), lambda h, i: (i, h)),
        out_shape=jax.ShapeDtypeStruct((seq, MLA_HEADS * MLA_V), BF16),
        scratch_shapes=_flash_t_scratch(tq),
        compiler_params=_params(("parallel", "arbitrary")),
        name="mla_attn_t",
    )(qn, qr, kn, kr, vt)


def _swa_kernel(sink_ref, q_ref, kc_ref, kp_ref, vc_ref, vp_ref, o_ref, *, tq):
    i = pl.program_id(0)
    nb = tq // SWA_BLOCK
    group = SWA_HEADS // SWA_KV_HEADS
    pairs = group // 2
    lane = lax.broadcasted_iota(jnp.int32, (2 * SWA_BLOCK, LANES), 1)
    shape = (pairs * SWA_BLOCK, 2 * SWA_BLOCK)
    qp = lax.broadcasted_iota(jnp.int32, shape, 0) % SWA_BLOCK + SWA_BLOCK
    kp = lax.broadcasted_iota(jnp.int32, shape, 1)
    band = (kp <= qp) & (qp - kp < SWA_BLOCK)
    out_lane = lax.broadcasted_iota(jnp.int32, (pairs * SWA_BLOCK, LANES), 1)
    key_row = lax.broadcasted_iota(jnp.int32, (4 * SWA_BLOCK, LANES), 0)
    key_lane = lax.broadcasted_iota(jnp.int32, (4 * SWA_BLOCK, LANES), 1)
    ones_by_head = jnp.where((key_row < 2 * SWA_BLOCK) == (key_lane < SWA_HD), 1.0, 0.0).astype(BF16)

    def split(t, c):
        mine = jnp.where((lane >= c * SWA_HD) & (lane < (c + 1) * SWA_HD), t, 0.0)
        other = pltpu.roll(mine, SWA_HD, 1)
        lo, hi = (mine, other) if c == 0 else (other, mine)
        return jnp.concatenate([lo, hi], axis=0).astype(BF16)

    for b in range(nb):
        rows = slice(b * SWA_BLOCK, (b + 1) * SWA_BLOCK)
        if b == 0:
            k_prev, v_prev = kp_ref[...], vp_ref[...]
        else:
            prev = slice((b - 1) * SWA_BLOCK, b * SWA_BLOCK)
            k_prev, v_prev = kc_ref[prev, :], vc_ref[prev, :]
        kw = jnp.concatenate([k_prev, kc_ref[rows, :]], axis=0).astype(F32)
        vw = jnp.concatenate([v_prev, vc_ref[rows, :]], axis=0).astype(F32)
        first_key = jnp.where(i * nb + b == 0, SWA_BLOCK, 0)
        valid = band & (kp >= first_key)
        for c in range(SWA_KV_HEADS):
            kcat = split(kw, c)
            vcat = split(vw, c)
            cols = [slice((c * pairs + a) * LANES, (c * pairs + a + 1) * LANES) for a in range(pairs)]
            q4 = jnp.concatenate([q_ref[rows, cs] for cs in cols], axis=0)
            s = lax.dot_general(q4, kcat, (((1,), (1,)), ((), ())), preferred_element_type=F32)
            ps, sink_terms = [], []
            for e in range(2):
                sink = jnp.concatenate(
                    [jnp.full((SWA_BLOCK, 1), sink_ref[c * group + 2 * a + e], F32) for a in range(pairs)], axis=0)
                se = jnp.where(valid, s[:, e * 2 * SWA_BLOCK:(e + 1) * 2 * SWA_BLOCK], -jnp.inf)
                m = jnp.maximum(jnp.max(se, axis=1, keepdims=True), sink)
                ps.append(jnp.exp(se - m).astype(BF16))
                sink_terms.append(jnp.exp(sink - m))
            pv = jnp.dot(jnp.concatenate(ps, axis=1), jnp.concatenate([vcat, ones_by_head], axis=1),
                         preferred_element_type=F32)
            den = pv[:, LANES:] + jnp.where(out_lane < SWA_HD, sink_terms[0], sink_terms[1])
            o4 = (pv[:, :LANES] / den).astype(BF16)
            for a, cs in enumerate(cols):
                o_ref[rows, cs] = o4[a * SWA_BLOCK:(a + 1) * SWA_BLOCK]


def _swa_attn(qs, ks, vs, sinks):
    seq = qs.shape[0]
    tq = min(seq, 512)
    nb = tq // SWA_BLOCK
    cur = lambda i: (i, 0)
    prev = lambda i: (jnp.maximum(i * nb - 1, 0), 0)
    return pl.pallas_call(
        functools.partial(_swa_kernel, tq=tq),
        grid=(seq // tq,),
        in_specs=[pl.BlockSpec(memory_space=pltpu.SMEM),
                  pl.BlockSpec((tq, SWA_HEADS * SWA_HD), cur),
                  pl.BlockSpec((tq, LANES), cur),
                  pl.BlockSpec((SWA_BLOCK, LANES), prev),
                  pl.BlockSpec((tq, LANES), cur),
                  pl.BlockSpec((SWA_BLOCK, LANES), prev)],
        out_specs=pl.BlockSpec((tq, SWA_HEADS * SWA_HD), cur),
        out_shape=jax.ShapeDtypeStruct((seq, SWA_HEADS * SWA_HD), BF16),
        compiler_params=_params(("parallel",)),
        name="swa_attn",
    )(sinks, qs, ks, ks, vs, vs)


def _out_proj_kernel(*refs, n_a):
    a_refs = refs[:n_a]
    w_ref, h_ref, gpost_ref, gnext_ref, hout_ref, hn_ref = refs[n_a:]
    tm = h_ref.shape[0]
    for rows in [slice(r * tm // 4, (r + 1) * tm // 4) for r in range(4)]:
        m = None
        k0 = 0
        for a_ref in a_refs:
            kw = a_ref.shape[1]
            part = jnp.dot(a_ref[rows, :], w_ref[k0:k0 + kw, :], preferred_element_type=F32)
            m = part if m is None else m + part
            k0 += kw
        h = h_ref[rows, :] + _rms(m, gpost_ref[...])
        hout_ref[rows, :] = h
        hn_ref[rows, :] = _rms(h, gnext_ref[...]).astype(BF16)


def _out_proj(a_list, w, h, g_post, g_next):
    seq, d = h.shape
    tm = min(seq, 512)
    row = lambda i: (i, 0)
    fixed = lambda i: (0, 0)
    return pl.pallas_call(
        functools.partial(_out_proj_kernel, n_a=len(a_list)),
        grid=(seq // tm,),
        in_specs=[pl.BlockSpec((tm, a.shape[1]), row) for a in a_list]
        + [pl.BlockSpec(w.shape, fixed), pl.BlockSpec((tm, d), row),
           pl.BlockSpec((1, d), fixed), pl.BlockSpec((1, d), fixed)],
        out_specs=[pl.BlockSpec((tm, d), row), pl.BlockSpec((tm, d), row)],
        out_shape=[jax.ShapeDtypeStruct((seq, d), F32), jax.ShapeDtypeStruct((seq, d), BF16)],
        compiler_params=_params(("parallel",)),
        name="out_proj",
    )(*a_list, w, h, g_post[None, :], g_next[None, :])


def _mlp_kernel(x_ref, wu_ref, wd_ref, h_ref, g_ref, o_ref, acc_ref):
    f = pl.program_id(1)

    @pl.when(f == 0)
    def _():
        acc_ref[...] = jnp.zeros(acc_ref.shape, F32)

    a = jnp.maximum(jnp.dot(x_ref[...], wu_ref[...], preferred_element_type=F32), 0.0)
    acc_ref[...] += jnp.dot((a * a).astype(BF16), wd_ref[...], preferred_element_type=F32)

    @pl.when(f == pl.num_programs(1) - 1)
    def _():
        o_ref[...] = h_ref[...] + _rms(acc_ref[...], g_ref[...])


def _mlp(xn, w_up, w_down, h, g_post):
    seq, d = h.shape
    d_ff = w_up.shape[1]
    tm = min(seq, 512)
    tf = min(d_ff, 1024)
    return pl.pallas_call(
        _mlp_kernel,
        grid=(seq // tm, d_ff // tf),
        in_specs=[pl.BlockSpec((tm, d), lambda i, f: (i, 0)),
                  pl.BlockSpec((d, tf), lambda i, f: (0, f)),
                  pl.BlockSpec((tf, d), lambda i, f: (f, 0)),
                  pl.BlockSpec((tm, d), lambda i, f: (i, 0)),
                  pl.BlockSpec((1, d), lambda i, f: (0, 0))],
        out_specs=pl.BlockSpec((tm, d), lambda i, f: (i, 0)),
        out_shape=jax.ShapeDtypeStruct((seq, d), F32),
        scratch_shapes=[pltpu.VMEM((tm, d), F32)],
        compiler_params=_params(("parallel", "arbitrary")),
        name="mlp",
    )(xn, w_up, w_down, h, g_post[None, :])


def _ple_kernel(h_ref, p_ref, wg_ref, b_ref, wp_ref, *rest, with_next):
    h = h_ref[...]
    z = jnp.dot(h.astype(BF16), wg_ref[...], preferred_element_type=F32) + b_ref[...]
    gate = 1.0 / (1.0 + jnp.exp(-z))
    e = jnp.dot(p_ref[...].astype(BF16), wp_ref[...], preferred_element_type=F32)
    out = h + gate * e
    if with_next:
        gnext_ref, o_ref, hn_ref = rest
        hn_ref[...] = _rms(out, gnext_ref[...]).astype(BF16)
    else:
        (o_ref,) = rest
    o_ref[...] = out


def _ple(h, p, w_gate, b_gate, w_proj, g_next=None):
    seq, d = h.shape
    tm = min(seq, 512)
    row = lambda i: (i, 0)
    fixed = lambda i: (0, 0)
    with_next = g_next is not None
    in_specs = [pl.BlockSpec((tm, d), row), pl.BlockSpec((tm, p.shape[1]), row),
                pl.BlockSpec(w_gate.shape, fixed), pl.BlockSpec((1, d), fixed),
                pl.BlockSpec(w_proj.shape, fixed)]
    args = [h, p, w_gate, b_gate[None, :], w_proj]
    out_specs = [pl.BlockSpec((tm, d), row)]
    out_shape = [jax.ShapeDtypeStruct((seq, d), F32)]
    if with_next:
        in_specs.append(pl.BlockSpec((1, d), fixed))
        args.append(g_next[None, :])
        out_specs.append(pl.BlockSpec((tm, d), row))
        out_shape.append(jax.ShapeDtypeStruct((seq, d), BF16))
    res = pl.pallas_call(
        functools.partial(_ple_kernel, with_next=with_next),
        grid=(seq // tm,),
        in_specs=in_specs, out_specs=out_specs, out_shape=out_shape,
        compiler_params=_params(("parallel",)),
        name="ple",
    )(*args)
    return res if with_next else (res[0], None)


def _odd_proj_kernel(x_ref, w_ref, cos_ref, sin_ref, o_ref, *rest, rope, scale, kmean, tm):
    t = jnp.dot(x_ref[...], w_ref[...], preferred_element_type=F32)
    if rope:
        t = _rope(t, cos_ref[...], sin_ref[...], MOBA_HD // 2)
    if kmean:
        (km_ref,) = rest
        for b in range(tm // MOBA_BLOCK):
            blk = t[b * MOBA_BLOCK:(b + 1) * MOBA_BLOCK, :]
            km_ref[b] = jnp.sum(blk, axis=0, keepdims=True) * (1.0 / MOBA_BLOCK)
    if scale != 1.0:
        t = t * scale
    o_ref[...] = t.astype(BF16)


def _odd_proj(xn, w, col, tabs_cos, tabs_sin, *, rope, scale=1.0, kmean=False):
    seq, d = xn.shape
    n = d
    tm = min(seq, 512)
    row = lambda i: (i, 0)
    fixed = lambda i: (0, 0)
    out_specs = [pl.BlockSpec((tm, n), row)]
    out_shape = [jax.ShapeDtypeStruct((seq, n), BF16)]
    if kmean:
        nb = tm // MOBA_BLOCK
        out_specs.append(pl.BlockSpec((nb, 1, n), lambda i: (i, 0, 0)))
        out_shape.append(jax.ShapeDtypeStruct((seq // MOBA_BLOCK, 1, n), F32))
    res = pl.pallas_call(
        functools.partial(_odd_proj_kernel, rope=rope, scale=scale, kmean=kmean, tm=tm),
        grid=(seq // tm,),
        in_specs=[pl.BlockSpec((tm, d), row), pl.BlockSpec((d, n), lambda i: (0, col)),
                  pl.BlockSpec((tm, LANES), lambda i: (i, 1)),
                  pl.BlockSpec((tm, LANES), lambda i: (i, 1))],
        out_specs=out_specs, out_shape=out_shape,
        compiler_params=_params(("parallel",)),
        name="odd_proj",
    )(xn, w, tabs_cos, tabs_sin)
    return res


def _moba_kernel(q_ref, k_ref, vt_ref, km_ref, o_ref, *stats, tq, tk, nkb):
    i = pl.program_id(1)
    q = q_ref[...]

    km = km_ref[...]
    if nkb < 64:
        km = jnp.concatenate([km, jnp.zeros((64 - nkb, LANES), F32)], axis=0)
    km_hi = km.astype(BF16)
    km_mid = (km - km_hi.astype(F32)).astype(BF16)
    gate = lax.dot_general(jnp.concatenate([km_hi, km_mid], axis=1), jnp.concatenate([q, q], axis=1),
                           (((1,), (1,)), ((), ())), preferred_element_type=F32)

    blk = lax.broadcasted_iota(jnp.int32, (64, tq), 0)
    qblk = (i * tq + lax.broadcasted_iota(jnp.int32, (64, tq), 1)) // MOBA_BLOCK
    gm = jnp.where(blk < qblk, gate, -jnp.inf)
    sel = blk == qblk
    for r in range(MOBA_TOPK):
        mx = jnp.max(gm, axis=0, keepdims=True)
        idx = jnp.min(jnp.where(gm == mx, blk, 64), axis=0, keepdims=True)
        pick = (blk == idx) & (qblk > r)
        sel = sel | pick
        gm = jnp.where(pick, -jnp.inf, gm)
    bias_t = jnp.where(sel, 0.0, MASK_BIAS)
    qat = jnp.concatenate([q.astype(F32).T, bias_t, jnp.zeros((64, tq), F32)], axis=0).astype(BF16)

    kb_per_tile = tk // MOBA_BLOCK
    krow = lax.broadcasted_iota(jnp.int32, (tk, LANES), 0) // MOBA_BLOCK
    klane = lax.broadcasted_iota(jnp.int32, (tk, LANES), 1)

    def kv(j):
        rows = pl.ds(pl.multiple_of(j * tk, tk), tk)
        onehot = jnp.where(klane == krow + j * kb_per_tile, 1.0, 0.0).astype(BF16)
        return jnp.concatenate([k_ref[rows, :], onehot], axis=1), vt_ref[j]

    _flash_attend_t(qat, kv, i, tq, tk, stats, o_ref)


def _moba_attn(q, k, v, kmean):
    seq = q.shape[0]
    nkb = seq // MOBA_BLOCK
    assert nkb <= 64 and nkb % 8 == 0, "the gate matmul packs three 64-lane groups"
    tq = min(seq, FLASH_TQ)
    tk = min(seq, FLASH_TK)
    nk = seq // tk
    vt = v.reshape(nk, tk, MOBA_HEADS, LANES).transpose(2, 0, 3, 1)
    return pl.pallas_call(
        functools.partial(_moba_kernel, tq=tq, tk=tk, nkb=nkb),
        grid=(MOBA_HEADS, seq // tq),
        in_specs=[pl.BlockSpec((tq, LANES), lambda h, i: (i, h)),
                  pl.BlockSpec((seq, LANES), lambda h, i: (0, h)),
                  pl.BlockSpec((None, nk, LANES, tk), lambda h, i: (h, 0, 0, 0)),
                  pl.BlockSpec((nkb, LANES), lambda h, i: (0, h))],
        out_specs=pl.BlockSpec((tq, LANES), lambda h, i: (i, h)),
        out_shape=jax.ShapeDtypeStruct((seq, MOBA_HEADS * MOBA_HD), BF16),
        scratch_shapes=_flash_t_scratch(tq),
        compiler_params=_params(("parallel", "arbitrary")),
        name="moba_attn",
    )(q, k, vt, kmean)


def _even_in_weight(w_in):
    d = w_in.shape[0]
    cq, ckv, kr, qs, ks, vs = jnp.split(w_in, [512, 1024, 1088, 2112, 2240], axis=1)
    return jnp.concatenate([cq, ckv, kr, jnp.zeros((d, 64), w_in.dtype), qs, ks, vs], axis=1).astype(BF16)


def _mla_q_weight(w_q_up):
    r = w_q_up.shape[0]
    w = w_q_up.reshape(r, MLA_HEADS, MLA_NOPE + MLA_ROPE)
    nope = w[:, :, :MLA_NOPE].reshape(r, MLA_HEADS * MLA_NOPE)
    rope = jnp.pad(w[:, :, MLA_NOPE:], ((0, 0), (0, 0), (0, LANES - MLA_ROPE))).reshape(r, MLA_HEADS * LANES)
    return jnp.concatenate([nope, rope], axis=1).astype(BF16)


def _mla_kv_weight(w_kv_up):
    r = w_kv_up.shape[0]
    w = w_kv_up.reshape(r, MLA_HEADS, MLA_NOPE + MLA_V)
    nope = w[:, :, :MLA_NOPE].reshape(r, MLA_HEADS * MLA_NOPE)
    val = w[:, :, MLA_NOPE:].reshape(r, MLA_HEADS * MLA_V)
    return jnp.concatenate([nope, val], axis=1).astype(BF16)


def kernel(x, p, positions, even_pre_g, even_w_in, mla_q_norm_g, mla_w_q_up, mla_kv_norm_g, mla_w_kv_up, swa_sinks, even_w_out, even_post_g, odd_pre_g, moba_w_qkv, odd_w_out, odd_post_g, mlp_pre_g, mlp_w_up, mlp_w_down, mlp_post_g, ple_w_gate, ple_b_gate, ple_w_proj):
    batch, seq, d = x.shape
    assert batch == 1
    h = x.reshape(seq, d)
    tabs_cos, tabs_sin = _rope_tables(positions, seq)

    cq, ckv, kr, qs, ks, vs = _even_proj(h, even_pre_g[0], _even_in_weight(even_w_in[0]),
                                         mla_q_norm_g[0], mla_kv_norm_g[0], tabs_cos, tabs_sin)
    qn, qr, kn, v = _mla_up(cq, ckv, _mla_q_weight(mla_w_q_up[0]), _mla_kv_weight(mla_w_kv_up[0]),
                            tabs_cos, tabs_sin)
    o_mla = _mla_attn_t(qn, qr, kn, kr, v)
    o_swa = _swa_attn(qs, ks, vs, swa_sinks[0])
    h, hn = _out_proj([o_mla, o_swa], even_w_out[0].astype(BF16), h, even_post_g[0], mlp_pre_g[0])
    h = _mlp(hn, mlp_w_up[0].astype(BF16), mlp_w_down[0].astype(BF16), h, mlp_post_g[0])
    h, hn = _ple(h, p[0, 0], ple_w_gate[0].astype(BF16), ple_b_gate[0], ple_w_proj[0].astype(BF16),
                 g_next=odd_pre_g[0])

    w_qkv = moba_w_qkv[0].astype(BF16)
    (q,) = _odd_proj(hn, w_qkv, 0, tabs_cos, tabs_sin, rope=True, scale=MOBA_HD ** -0.5 * LOG2E)
    k, kmean = _odd_proj(hn, w_qkv, 1, tabs_cos, tabs_sin, rope=True, kmean=True)
    (v,) = _odd_proj(hn, w_qkv, 2, tabs_cos, tabs_sin, rope=False)
    o = _moba_attn(q, k, v, kmean.reshape(seq // MOBA_BLOCK, MOBA_HEADS * MOBA_HD))
    h, hn = _out_proj([o], odd_w_out[0].astype(BF16), h, odd_post_g[0], mlp_pre_g[1])
    h = _mlp(hn, mlp_w_up[1].astype(BF16), mlp_w_down[1].astype(BF16), h, mlp_post_g[1])
    h, _ = _ple(h, p[1, 0], ple_w_gate[1].astype(BF16), ple_b_gate[1], ple_w_proj[1].astype(BF16))
    return h.reshape(batch, seq, d)
```

```python
import functools

import jax
import jax.numpy as jnp
import numpy as np
from jax import lax
from jax.experimental import pallas as pl
from jax.experimental.pallas import tpu as pltpu

F32 = jnp.float32
BF16 = jnp.bfloat16

NORM_EPS = 1e-6
ROPE_THETA = 10000.0

MLA_HEADS = 8
MLA_Q_RANK = 512
MLA_KV_RANK = 512
MLA_NOPE = 128
MLA_ROPE = 64
MLA_V = 128
SWA_HEADS = 16
SWA_KV_HEADS = 2
SWA_HD = 64
SWA_BLOCK = 128
MOBA_HEADS = 16
MOBA_HD = 128
MOBA_BLOCK = 256
MOBA_TOPK = 3

LANES = 128
V7X_VMEM_BYTES = 64 * 1024 * 1024
VMEM_LIMIT = 56 * 1024 * 1024
MASK_BIAS = -1e30
LOG2E = 1.4426950408889634
FLASH_TQ = 2048
FLASH_TK = 512
T_CHUNK = 256
T_AHEAD = 3


def _params(sem):
    return pltpu.CompilerParams(dimension_semantics=sem, vmem_limit_bytes=VMEM_LIMIT)


def _rms(t, g):
    return t * lax.rsqrt(jnp.mean(t * t, axis=-1, keepdims=True) + NORM_EPS) * g


def _rope(t, cos, sin_signed, half):
    width = t.shape[1]
    reps = width // LANES
    if reps > 1:
        cos = jnp.concatenate([cos] * reps, axis=1)
        sin_signed = jnp.concatenate([sin_signed] * reps, axis=1)
    lane = lax.broadcasted_iota(jnp.int32, t.shape, 1)
    first = (lane % (2 * half)) < half
    partner = jnp.where(first, pltpu.roll(t, width - half, 1), pltpu.roll(t, half, 1))
    return t * cos + partner * sin_signed


def _tables_kernel(pos_ref, invf_ref, sign_ref, cos_ref, sin_ref):
    ang = pos_ref[...].astype(F32) * invf_ref[...]
    cos_ref[...] = jnp.cos(ang)
    sin_ref[...] = jnp.sin(ang) * sign_ref[...]


def _rope_tables(positions, seq):
    def inv_freq(d):
        half = d // 2
        return jnp.power(ROPE_THETA, -jnp.arange(half, dtype=F32) * (2.0 / d))

    f64, f128 = inv_freq(64), inv_freq(128)
    invf = jnp.concatenate([f64, f64, f64, f64, f128, f128])[None, :]
    sign = np.concatenate([-np.ones(32), np.ones(32), -np.ones(32), np.ones(32),
                           -np.ones(64), np.ones(64)]).astype(np.float32)[None, :]
    tm = min(seq, 1024)
    return pl.pallas_call(
        _tables_kernel,
        grid=(seq // tm,),
        in_specs=[pl.BlockSpec((tm, 1), lambda i: (i, 0)),
                  pl.BlockSpec((1, 256), lambda i: (0, 0)),
                  pl.BlockSpec((1, 256), lambda i: (0, 0))],
        out_specs=[pl.BlockSpec((tm, 256), lambda i: (i, 0)),
                   pl.BlockSpec((tm, 256), lambda i: (i, 0))],
        out_shape=[jax.ShapeDtypeStruct((seq, 256), F32)] * 2,
        compiler_params=_params(("parallel",)),
        name="rope_tables",
    )(positions.reshape(seq, 1), invf, jnp.asarray(sign))


_EVEN_COLS = (0, 512, 1024, 1152, 2176, 2304, 2432)


def _even_proj_kernel(x_ref, gpre_ref, w_ref, gq_ref, gkv_ref, cos_ref, sin_ref,
                      cq_ref, ckv_ref, kr_ref, qs_ref, ks_ref, vs_ref):
    x = _rms(x_ref[...], gpre_ref[...]).astype(BF16)
    cos, sin = cos_ref[...], sin_ref[...]
    c = _EVEN_COLS

    def mm(k):
        return jnp.dot(x, w_ref[:, c[k]:c[k + 1]], preferred_element_type=F32)

    cq_ref[...] = _rms(mm(0), gq_ref[...]).astype(BF16)
    ckv_ref[...] = _rms(mm(1), gkv_ref[...]).astype(BF16)
    kr_ref[...] = _rope(mm(2), cos, sin, 32).astype(BF16)
    qs_ref[...] = (_rope(mm(3), cos, sin, 32) * (SWA_HD ** -0.5)).astype(BF16)
    ks_ref[...] = _rope(mm(4), cos, sin, 32).astype(BF16)
    vs_ref[...] = mm(5).astype(BF16)


def _even_proj(x, g_pre, w, gq, gkv, tabs_cos, tabs_sin):
    seq, d = x.shape
    tm = min(seq, 512)
    widths = [_EVEN_COLS[k + 1] - _EVEN_COLS[k] for k in range(6)]
    row = lambda i: (i, 0)
    fixed = lambda i: (0, 0)
    return pl.pallas_call(
        _even_proj_kernel,
        grid=(seq // tm,),
        in_specs=[pl.BlockSpec((tm, d), row),
                  pl.BlockSpec((1, d), fixed),
                  pl.BlockSpec(w.shape, fixed),
                  pl.BlockSpec((1, 512), fixed),
                  pl.BlockSpec((1, 512), fixed),
                  pl.BlockSpec((tm, LANES), row),
                  pl.BlockSpec((tm, LANES), row)],
        out_specs=[pl.BlockSpec((tm, n), row) for n in widths],
        out_shape=[jax.ShapeDtypeStruct((seq, n), BF16) for n in widths],
        compiler_params=_params(("parallel",)),
        name="even_proj",
    )(x, g_pre[None, :], w, gq[None, :], gkv[None, :], tabs_cos, tabs_sin)


def _mla_up_kernel(cq_ref, ckv_ref, wq_ref, wk_ref, wvt_ref, cos_ref, sin_ref,
                   qn_ref, qr_ref, kn_ref, vt_ref, *, scale):
    cq, ckv = cq_ref[...], ckv_ref[...]
    n = MLA_HEADS * LANES
    qn = jnp.dot(cq, wq_ref[:, :n], preferred_element_type=F32)
    qr = jnp.dot(cq, wq_ref[:, n:], preferred_element_type=F32)
    qn_ref[...] = (qn * scale).astype(BF16)
    qr_ref[...] = (_rope(qr, cos_ref[...], sin_ref[...], 32) * scale).astype(BF16)
    kn_ref[...] = jnp.dot(ckv, wk_ref[...], preferred_element_type=F32).astype(BF16)
    vt = lax.dot_general(wvt_ref[...], ckv, (((1,), (1,)), ((), ())), preferred_element_type=F32)
    vt_ref[...] = vt.astype(BF16).reshape(vt_ref.shape)


def _mla_up(cq, ckv, wq, wk, wvt, tabs_cos, tabs_sin):
    seq = cq.shape[0]
    tm = min(seq, FLASH_TK)
    n = MLA_HEADS * LANES
    row = lambda i: (i, 0)
    fixed = lambda i: (0, 0)
    scale = (MLA_NOPE + MLA_ROPE) ** -0.5 * LOG2E
    return pl.pallas_call(
        functools.partial(_mla_up_kernel, scale=scale),
        grid=(seq // tm,),
        in_specs=[pl.BlockSpec((tm, MLA_Q_RANK), row),
                  pl.BlockSpec((tm, MLA_KV_RANK), row),
                  pl.BlockSpec(wq.shape, fixed),
                  pl.BlockSpec(wk.shape, fixed),
                  pl.BlockSpec(wvt.shape, fixed),
                  pl.BlockSpec((tm, LANES), row),
                  pl.BlockSpec((tm, LANES), row)],
        out_specs=[pl.BlockSpec((tm, n), row)] * 3
        + [pl.BlockSpec((MLA_HEADS, None, LANES, tm), lambda i: (0, i, 0, 0))],
        out_shape=[jax.ShapeDtypeStruct((seq, n), BF16)] * 3
        + [jax.ShapeDtypeStruct((MLA_HEADS, seq // tm, LANES, tm), BF16)],
        compiler_params=_params(("parallel",)),
        name="mla_up",
    )(cq, ckv, wq, wk, wvt, tabs_cos, tabs_sin)


def _softmax_pv_t(st, vt, m_ref, l_ref, acc_ref):
    m_old = m_ref[...]
    m_new = jnp.maximum(m_old, jnp.max(st, axis=0, keepdims=True))
    alpha = jnp.exp2(m_old - m_new)
    pt = jnp.exp2(st - m_new)
    l_ref[...] = alpha * l_ref[...] + jnp.sum(pt, axis=0, keepdims=True)
    acc_ref[...] = alpha * acc_ref[...] + jnp.dot(vt, pt.astype(BF16), preferred_element_type=F32)
    m_ref[...] = m_new


def _flash_attend_t(qt, kv, i, tq, tk, stats, o_ref):
    nc = tq // T_CHUNK
    m_refs, l_refs, acc_refs = stats[:nc], stats[nc:2 * nc], stats[2 * nc:]
    for c in range(nc):
        m_refs[c][...] = jnp.full(m_refs[c].shape, -jnp.inf, F32)
        l_refs[c][...] = jnp.zeros(l_refs[c].shape, F32)
        acc_refs[c][...] = jnp.zeros(acc_refs[c].shape, F32)
    sub = tq // tk
    qts = [qt[:, c * T_CHUNK:(c + 1) * T_CHUNK] for c in range(nc)]
    key = lax.broadcasted_iota(jnp.int32, (tk, T_CHUNK), 0)
    qry = lax.broadcasted_iota(jnp.int32, (tk, T_CHUNK), 1)

    def run(work, kvs, diagonal):
        def score(n):
            d, c = work[n]
            st = jnp.dot(kvs[d][0], qts[c], preferred_element_type=F32)
            off = c * T_CHUNK - d * tk
            if diagonal and off < tk:
                st = jnp.where(key <= qry + off, st, -jnp.inf)
            return st

        ahead = [score(n) for n in range(min(T_AHEAD, len(work)))]
        for n, (d, c) in enumerate(work):
            if n + T_AHEAD < len(work):
                ahead.append(score(n + T_AHEAD))
            _softmax_pv_t(ahead.pop(0), kvs[d][1], m_refs[c], l_refs[c], acc_refs[c])

    def past(g, carry):
        run([(d, c) for d in range(sub) for c in range(nc)], [kv(g * sub + d) for d in range(sub)], False)
        return carry

    lax.fori_loop(0, i, past, 0)

    run([(d, c) for d in range(sub) for c in range(d * tk // T_CHUNK, nc)],
        [kv(i * sub + d) for d in range(sub)], True)


    out_t = jnp.concatenate([acc_refs[c][...] / l_refs[c][...] for c in range(nc)], axis=1)
    o_ref[...] = out_t.T.astype(BF16)


def _flash_t_scratch(tq):
    nc = tq // T_CHUNK
    return [pltpu.VMEM((1, T_CHUNK), F32)] * (2 * nc) + [pltpu.VMEM((LANES, T_CHUNK), F32)] * nc


def _mla_attn_t_kernel(qn_ref, qr_ref, kn_ref, kr_ref, vt_ref, o_ref, *stats, tq, tk):
    i = pl.program_id(1)
    q = jnp.concatenate([qn_ref[...], qr_ref[...]], axis=1)
    qt = q.astype(F32).T.astype(BF16)

    def kv(j):
        rows = pl.ds(pl.multiple_of(j * tk, tk), tk)
        return jnp.concatenate([kn_ref[rows, :], kr_ref[rows, :]], axis=1), vt_ref[j]

    _flash_attend_t(qt, kv, i, tq, tk, stats, o_ref)


def _mla_attn_t(qn, qr, kn, kr, vt):
    seq = qn.shape[0]
    tq = min(seq, FLASH_TQ)
    tk = min(seq, FLASH_TK)
    nk = seq // tk
    return pl.pallas_call(
        functools.partial(_mla_attn_t_kernel, tq=tq, tk=tk),
        grid=(MLA_HEADS, seq // tq),
        in_specs=[pl.BlockSpec((tq, LANES), lambda h, i: (i, h)),
                  pl.BlockSpec((tq, LANES), lambda h, i: (i, h)),
                  pl.BlockSpec((seq, LANES), lambda h, i: (0, h)),
                  pl.BlockSpec((seq, LANES), lambda h, i: (0, 0)),
                  pl.BlockSpec((None, nk, LANES, tk), lambda h, i: (h, 0, 0, 0))],
        out_specs=pl.BlockSpec((tq, LANES), lambda h, i: (i, h)),
        out_shape=jax.ShapeDtypeStruct((seq, MLA_HEADS * MLA_V), BF16),
        scratch_shapes=_flash_t_scratch(tq),
        compiler_params=_params(("parallel", "arbitrary")),
        name="mla_attn_t",
    )(qn, qr, kn, kr, vt)


def _swa_kernel(sink_ref, q_ref, kc_ref, kp_ref, vc_ref, vp_ref, o_ref, *, tq):
    i = pl.program_id(0)
    nb = tq // SWA_BLOCK
    group = SWA_HEADS // SWA_KV_HEADS
    pairs = group // 2
    lane = lax.broadcasted_iota(jnp.int32, (2 * SWA_BLOCK, LANES), 1)
    shape = (pairs * SWA_BLOCK, 2 * SWA_BLOCK)
    qp = lax.broadcasted_iota(jnp.int32, shape, 0) % SWA_BLOCK + SWA_BLOCK
    kp = lax.broadcasted_iota(jnp.int32, shape, 1)
    band = (kp <= qp) & (qp - kp < SWA_BLOCK)
    out_lane = lax.broadcasted_iota(jnp.int32, (pairs * SWA_BLOCK, LANES), 1)
    key_row = lax.broadcasted_iota(jnp.int32, (4 * SWA_BLOCK, LANES), 0)
    key_lane = lax.broadcasted_iota(jnp.int32, (4 * SWA_BLOCK, LANES), 1)
    ones_by_head = jnp.where((key_row < 2 * SWA_BLOCK) == (key_lane < SWA_HD), 1.0, 0.0).astype(BF16)

    def split(t, c):
        mine = jnp.where((lane >= c * SWA_HD) & (lane < (c + 1) * SWA_HD), t, 0.0)
        other = pltpu.roll(mine, SWA_HD, 1)
        lo, hi = (mine, other) if c == 0 else (other, mine)
        return jnp.concatenate([lo, hi], axis=0).astype(BF16)

    for b in range(nb):
        rows = slice(b * SWA_BLOCK, (b + 1) * SWA_BLOCK)
        if b == 0:
            k_prev, v_prev = kp_ref[...], vp_ref[...]
        else:
            prev = slice((b - 1) * SWA_BLOCK, b * SWA_BLOCK)
            k_prev, v_prev = kc_ref[prev, :], vc_ref[prev, :]
        kw = jnp.concatenate([k_prev, kc_ref[rows, :]], axis=0).astype(F32)
        vw = jnp.concatenate([v_prev, vc_ref[rows, :]], axis=0).astype(F32)
        first_key = jnp.where(i * nb + b == 0, SWA_BLOCK, 0)
        valid = band & (kp >= first_key)
        for c in range(SWA_KV_HEADS):
            kcat = split(kw, c)
            vcat = split(vw, c)
            cols = [slice((c * pairs + a) * LANES, (c * pairs + a + 1) * LANES) for a in range(pairs)]
            q4 = jnp.concatenate([q_ref[rows, cs] for cs in cols], axis=0)
            s = lax.dot_general(q4, kcat, (((1,), (1,)), ((), ())), preferred_element_type=F32)
            ps, sink_terms = [], []
            for e in range(2):
                sink = jnp.concatenate(
                    [jnp.full((SWA_BLOCK, 1), sink_ref[c * group + 2 * a + e], F32) for a in range(pairs)], axis=0)
                se = jnp.where(valid, s[:, e * 2 * SWA_BLOCK:(e + 1) * 2 * SWA_BLOCK], -jnp.inf)
                m = jnp.maximum(jnp.max(se, axis=1, keepdims=True), sink)
                ps.append(jnp.exp(se - m).astype(BF16))
                sink_terms.append(jnp.exp(sink - m))
            pv = jnp.dot(jnp.concatenate(ps, axis=1), jnp.concatenate([vcat, ones_by_head], axis=1),
                         preferred_element_type=F32)
            den = pv[:, LANES:] + jnp.where(out_lane < SWA_HD, sink_terms[0], sink_terms[1])
            o4 = (pv[:, :LANES] / den).astype(BF16)
            for a, cs in enumerate(cols):
                o_ref[rows, cs] = o4[a * SWA_BLOCK:(a + 1) * SWA_BLOCK]


def _swa_attn(qs, ks, vs, sinks):
    seq = qs.shape[0]
    tq = min(seq, 512)
    nb = tq // SWA_BLOCK
    cur = lambda i: (i, 0)
    prev = lambda i: (jnp.maximum(i * nb - 1, 0), 0)
    return pl.pallas_call(
        functools.partial(_swa_kernel, tq=tq),
        grid=(seq // tq,),
        in_specs=[pl.BlockSpec(memory_space=pltpu.SMEM),
                  pl.BlockSpec((tq, SWA_HEADS * SWA_HD), cur),
                  pl.BlockSpec((tq, LANES), cur),
                  pl.BlockSpec((SWA_BLOCK, LANES), prev),
                  pl.BlockSpec((tq, LANES), cur),
                  pl.BlockSpec((SWA_BLOCK, LANES), prev)],
        out_specs=pl.BlockSpec((tq, SWA_HEADS * SWA_HD), cur),
        out_shape=jax.ShapeDtypeStruct((seq, SWA_HEADS * SWA_HD), BF16),
        compiler_params=_params(("parallel",)),
        name="swa_attn",
    )(sinks, qs, ks, ks, vs, vs)


def _out_proj_kernel(*refs, n_a):
    a_refs = refs[:n_a]
    w_ref, h_ref, gpost_ref, gnext_ref, hout_ref, hn_ref = refs[n_a:]
    tm = h_ref.shape[0]
    for rows in [slice(r * tm // 4, (r + 1) * tm // 4) for r in range(4)]:
        m = None
        k0 = 0
        for a_ref in a_refs:
            kw = a_ref.shape[1]
            part = jnp.dot(a_ref[rows, :], w_ref[k0:k0 + kw, :], preferred_element_type=F32)
            m = part if m is None else m + part
            k0 += kw
        h = h_ref[rows, :] + _rms(m, gpost_ref[...])
        hout_ref[rows, :] = h
        hn_ref[rows, :] = _rms(h, gnext_ref[...]).astype(BF16)


def _out_proj(a_list, w, h, g_post, g_next):
    seq, d = h.shape
    tm = min(seq, 512)
    row = lambda i: (i, 0)
    fixed = lambda i: (0, 0)
    return pl.pallas_call(
        functools.partial(_out_proj_kernel, n_a=len(a_list)),
        grid=(seq // tm,),
        in_specs=[pl.BlockSpec((tm, a.shape[1]), row) for a in a_list]
        + [pl.BlockSpec(w.shape, fixed), pl.BlockSpec((tm, d), row),
           pl.BlockSpec((1, d), fixed), pl.BlockSpec((1, d), fixed)],
        out_specs=[pl.BlockSpec((tm, d), row), pl.BlockSpec((tm, d), row)],
        out_shape=[jax.ShapeDtypeStruct((seq, d), F32), jax.ShapeDtypeStruct((seq, d), BF16)],
        compiler_params=_params(("parallel",)),
        name="out_proj",
    )(*a_list, w, h, g_post[None, :], g_next[None, :])


def _mlp_kernel(x_ref, wu_ref, wd_ref, h_ref, g_ref, o_ref, acc_ref):
    f = pl.program_id(1)

    @pl.when(f == 0)
    def _():
        acc_ref[...] = jnp.zeros(acc_ref.shape, F32)

    a = jnp.maximum(jnp.dot(x_ref[...], wu_ref[...], preferred_element_type=F32), 0.0)
    acc_ref[...] += jnp.dot((a * a).astype(BF16), wd_ref[...], preferred_element_type=F32)

    @pl.when(f == pl.num_programs(1) - 1)
    def _():
        o_ref[...] = h_ref[...] + _rms(acc_ref[...], g_ref[...])


def _mlp(xn, w_up, w_down, h, g_post):
    seq, d = h.shape
    d_ff = w_up.shape[1]
    tm = min(seq, 512)
    tf = min(d_ff, 1024)
    return pl.pallas_call(
        _mlp_kernel,
        grid=(seq // tm, d_ff // tf),
        in_specs=[pl.BlockSpec((tm, d), lambda i, f: (i, 0)),
                  pl.BlockSpec((d, tf), lambda i, f: (0, f)),
                  pl.BlockSpec((tf, d), lambda i, f: (f, 0)),
                  pl.BlockSpec((tm, d), lambda i, f: (i, 0)),
                  pl.BlockSpec((1, d), lambda i, f: (0, 0))],
        out_specs=pl.BlockSpec((tm, d), lambda i, f: (i, 0)),
        out_shape=jax.ShapeDtypeStruct((seq, d), F32),
        scratch_shapes=[pltpu.VMEM((tm, d), F32)],
        compiler_params=_params(("parallel", "arbitrary")),
        name="mlp",
    )(xn, w_up, w_down, h, g_post[None, :])


def _ple_kernel(h_ref, p_ref, wg_ref, b_ref, wp_ref, *rest, with_next):
    h = h_ref[...]
    z = jnp.dot(h.astype(BF16), wg_ref[...], preferred_element_type=F32) + b_ref[...]
    gate = 1.0 / (1.0 + jnp.exp(-z))
    e = jnp.dot(p_ref[...].astype(BF16), wp_ref[...], preferred_element_type=F32)
    out = h + gate * e
    if with_next:
        gnext_ref, o_ref, hn_ref = rest
        hn_ref[...] = _rms(out, gnext_ref[...]).astype(BF16)
    else:
        (o_ref,) = rest
    o_ref[...] = out


def _ple(h, p, w_gate, b_gate, w_proj, g_next=None):
    seq, d = h.shape
    tm = min(seq, 512)
    row = lambda i: (i, 0)
    fixed = lambda i: (0, 0)
    with_next = g_next is not None
    in_specs = [pl.BlockSpec((tm, d), row), pl.BlockSpec((tm, p.shape[1]), row),
                pl.BlockSpec(w_gate.shape, fixed), pl.BlockSpec((1, d), fixed),
                pl.BlockSpec(w_proj.shape, fixed)]
    args = [h, p, w_gate, b_gate[None, :], w_proj]
    out_specs = [pl.BlockSpec((tm, d), row)]
    out_shape = [jax.ShapeDtypeStruct((seq, d), F32)]
    if with_next:
        in_specs.append(pl.BlockSpec((1, d), fixed))
        args.append(g_next[None, :])
        out_specs.append(pl.BlockSpec((tm, d), row))
        out_shape.append(jax.ShapeDtypeStruct((seq, d), BF16))
    res = pl.pallas_call(
        functools.partial(_ple_kernel, with_next=with_next),
        grid=(seq // tm,),
        in_specs=in_specs, out_specs=out_specs, out_shape=out_shape,
        compiler_params=_params(("parallel",)),
        name="ple",
    )(*args)
    return res if with_next else (res[0], None)


def _odd_proj_kernel(x_ref, w_ref, cos_ref, sin_ref, o_ref, *rest, rope, scale, kmean, tm):
    t = jnp.dot(x_ref[...], w_ref[...], preferred_element_type=F32)
    if rope:
        t = _rope(t, cos_ref[...], sin_ref[...], MOBA_HD // 2)
    if kmean:
        (km_ref,) = rest
        for b in range(tm // MOBA_BLOCK):
            blk = t[b * MOBA_BLOCK:(b + 1) * MOBA_BLOCK, :]
            km_ref[b] = jnp.sum(blk, axis=0, keepdims=True) * (1.0 / MOBA_BLOCK)
    if scale != 1.0:
        t = t * scale
    o_ref[...] = t.astype(BF16)


def _odd_proj(xn, w, col, tabs_cos, tabs_sin, *, rope, scale=1.0, kmean=False):
    seq, d = xn.shape
    n = d
    tm = min(seq, 512)
    row = lambda i: (i, 0)
    fixed = lambda i: (0, 0)
    out_specs = [pl.BlockSpec((tm, n), row)]
    out_shape = [jax.ShapeDtypeStruct((seq, n), BF16)]
    if kmean:
        nb = tm // MOBA_BLOCK
        out_specs.append(pl.BlockSpec((nb, 1, n), lambda i: (i, 0, 0)))
        out_shape.append(jax.ShapeDtypeStruct((seq // MOBA_BLOCK, 1, n), F32))
    res = pl.pallas_call(
        functools.partial(_odd_proj_kernel, rope=rope, scale=scale, kmean=kmean, tm=tm),
        grid=(seq // tm,),
        in_specs=[pl.BlockSpec((tm, d), row), pl.BlockSpec((d, n), lambda i: (0, col)),
                  pl.BlockSpec((tm, LANES), lambda i: (i, 1)),
                  pl.BlockSpec((tm, LANES), lambda i: (i, 1))],
        out_specs=out_specs, out_shape=out_shape,
        compiler_params=_params(("parallel",)),
        name="odd_proj",
    )(xn, w, tabs_cos, tabs_sin)
    return res


def _odd_vt_kernel(x_ref, wt_ref, vt_ref):
    vt = lax.dot_general(wt_ref[...], x_ref[...], (((1,), (1,)), ((), ())), preferred_element_type=F32)
    vt_ref[...] = vt.astype(BF16).reshape(vt_ref.shape)


def _odd_proj_vt(xn, wt):
    seq, d = xn.shape
    tm = min(seq, FLASH_TK)
    heads = wt.shape[0] // LANES
    return pl.pallas_call(
        _odd_vt_kernel,
        grid=(seq // tm,),
        in_specs=[pl.BlockSpec((tm, d), lambda i: (i, 0)), pl.BlockSpec(wt.shape, lambda i: (0, 0))],
        out_specs=pl.BlockSpec((heads, None, LANES, tm), lambda i: (0, i, 0, 0)),
        out_shape=jax.ShapeDtypeStruct((heads, seq // tm, LANES, tm), BF16),
        compiler_params=_params(("parallel",)),
        name="odd_proj_vt",
    )(xn, wt)


def _moba_kernel(q_ref, k_ref, vt_ref, km_ref, o_ref, *stats, tq, tk, nkb):
    i = pl.program_id(1)
    q = q_ref[...]

    km = km_ref[...]
    if nkb < 64:
        km = jnp.concatenate([km, jnp.zeros((64 - nkb, LANES), F32)], axis=0)
    km_hi = km.astype(BF16)
    km_mid = (km - km_hi.astype(F32)).astype(BF16)
    gate = lax.dot_general(jnp.concatenate([km_hi, km_mid], axis=1), jnp.concatenate([q, q], axis=1),
                           (((1,), (1,)), ((), ())), preferred_element_type=F32)

    blk = lax.broadcasted_iota(jnp.int32, (64, tq), 0)
    qblk = (i * tq + lax.broadcasted_iota(jnp.int32, (64, tq), 1)) // MOBA_BLOCK
    gm = jnp.where(blk < qblk, gate, -jnp.inf)
    sel = blk == qblk
    for r in range(MOBA_TOPK):
        mx = jnp.max(gm, axis=0, keepdims=True)
        idx = jnp.min(jnp.where(gm == mx, blk, 64), axis=0, keepdims=True)
        pick = (blk == idx) & (qblk > r)
        sel = sel | pick
        gm = jnp.where(pick, -jnp.inf, gm)
    bias_t = jnp.where(sel, 0.0, MASK_BIAS)
    qat = jnp.concatenate([q.astype(F32).T, bias_t, jnp.zeros((64, tq), F32)], axis=0).astype(BF16)

    kb_per_tile = tk // MOBA_BLOCK
    krow = lax.broadcasted_iota(jnp.int32, (tk, LANES), 0) // MOBA_BLOCK
    klane = lax.broadcasted_iota(jnp.int32, (tk, LANES), 1)

    def kv(j):
        rows = pl.ds(pl.multiple_of(j * tk, tk), tk)
        onehot = jnp.where(klane == krow + j * kb_per_tile, 1.0, 0.0).astype(BF16)
        return jnp.concatenate([k_ref[rows, :], onehot], axis=1), vt_ref[j]

    _flash_attend_t(qat, kv, i, tq, tk, stats, o_ref)


def _moba_attn(q, k, vt, kmean):
    seq = q.shape[0]
    nkb = seq // MOBA_BLOCK
    assert nkb <= 64 and nkb % 8 == 0, "the gate matmul stacks blocks on 64 sublanes"
    tq = min(seq, FLASH_TQ)
    tk = min(seq, FLASH_TK)
    nk = seq // tk
    return pl.pallas_call(
        functools.partial(_moba_kernel, tq=tq, tk=tk, nkb=nkb),
        grid=(MOBA_HEADS, seq // tq),
        in_specs=[pl.BlockSpec((tq, LANES), lambda h, i: (i, h)),
                  pl.BlockSpec((seq, LANES), lambda h, i: (0, h)),
                  pl.BlockSpec((None, nk, LANES, tk), lambda h, i: (h, 0, 0, 0)),
                  pl.BlockSpec((nkb, LANES), lambda h, i: (0, h))],
        out_specs=pl.BlockSpec((tq, LANES), lambda h, i: (i, h)),
        out_shape=jax.ShapeDtypeStruct((seq, MOBA_HEADS * MOBA_HD), BF16),
        scratch_shapes=_flash_t_scratch(tq),
        compiler_params=_params(("parallel", "arbitrary")),
        name="moba_attn",
    )(q, k, vt, kmean)


def _even_in_weight(w_in):
    d = w_in.shape[0]
    cq, ckv, kr, qs, ks, vs = jnp.split(w_in, [512, 1024, 1088, 2112, 2240], axis=1)
    return jnp.concatenate([cq, ckv, kr, jnp.zeros((d, 64), w_in.dtype), qs, ks, vs], axis=1).astype(BF16)


def _mla_q_weight(w_q_up):
    r = w_q_up.shape[0]
    w = w_q_up.reshape(r, MLA_HEADS, MLA_NOPE + MLA_ROPE)
    nope = w[:, :, :MLA_NOPE].reshape(r, MLA_HEADS * MLA_NOPE)
    rope = jnp.pad(w[:, :, MLA_NOPE:], ((0, 0), (0, 0), (0, LANES - MLA_ROPE))).reshape(r, MLA_HEADS * LANES)
    return jnp.concatenate([nope, rope], axis=1).astype(BF16)


def _mla_kv_weight(w_kv_up):
    r = w_kv_up.shape[0]
    w = w_kv_up.reshape(r, MLA_HEADS, MLA_NOPE + MLA_V)
    nope = w[:, :, :MLA_NOPE].reshape(r, MLA_HEADS * MLA_NOPE)
    val = w[:, :, MLA_NOPE:].reshape(r, MLA_HEADS * MLA_V)
    return nope.astype(BF16), val.T.astype(BF16)


def kernel(x, p, positions, even_pre_g, even_w_in, mla_q_norm_g, mla_w_q_up, mla_kv_norm_g, mla_w_kv_up, swa_sinks, even_w_out, even_post_g, odd_pre_g, moba_w_qkv, odd_w_out, odd_post_g, mlp_pre_g, mlp_w_up, mlp_w_down, mlp_post_g, ple_w_gate, ple_b_gate, ple_w_proj):
    batch, seq, d = x.shape
    assert batch == 1
    h = x.reshape(seq, d)
    tabs_cos, tabs_sin = _rope_tables(positions, seq)

    cq, ckv, kr, qs, ks, vs = _even_proj(h, even_pre_g[0], _even_in_weight(even_w_in[0]),
                                         mla_q_norm_g[0], mla_kv_norm_g[0], tabs_cos, tabs_sin)
    qn, qr, kn, vt = _mla_up(cq, ckv, _mla_q_weight(mla_w_q_up[0]), *_mla_kv_weight(mla_w_kv_up[0]),
                             tabs_cos, tabs_sin)
    o_mla = _mla_attn_t(qn, qr, kn, kr, vt)
    o_swa = _swa_attn(qs, ks, vs, swa_sinks[0])
    h, hn = _out_proj([o_mla, o_swa], even_w_out[0].astype(BF16), h, even_post_g[0], mlp_pre_g[0])
    h = _mlp(hn, mlp_w_up[0].astype(BF16), mlp_w_down[0].astype(BF16), h, mlp_post_g[0])
    h, hn = _ple(h, p[0, 0], ple_w_gate[0].astype(BF16), ple_b_gate[0], ple_w_proj[0].astype(BF16),
                 g_next=odd_pre_g[0])

    w_qkv = moba_w_qkv[0].astype(BF16)
    (q,) = _odd_proj(hn, w_qkv, 0, tabs_cos, tabs_sin, rope=True, scale=MOBA_HD ** -0.5 * LOG2E)
    k, kmean = _odd_proj(hn, w_qkv, 1, tabs_cos, tabs_sin, rope=True, kmean=True)
    vt = _odd_proj_vt(hn, moba_w_qkv[0][:, 2 * d:].T.astype(BF16))
    o = _moba_attn(q, k, vt, kmean.reshape(seq // MOBA_BLOCK, MOBA_HEADS * MOBA_HD))
    h, hn = _out_proj([o], odd_w_out[0].astype(BF16), h, odd_post_g[0], mlp_pre_g[1])
    h = _mlp(hn, mlp_w_up[1].astype(BF16), mlp_w_down[1].astype(BF16), h, mlp_post_g[1])
    h, _ = _ple(h, p[1, 0], ple_w_gate[1].astype(BF16), ple_b_gate[1], ple_w_proj[1].astype(BF16))
    return h.reshape(batch, seq, d)
```

```python
import functools

import jax
import jax.numpy as jnp
import numpy as np
from jax import lax
from jax.experimental import pallas as pl
from jax.experimental.pallas import tpu as pltpu

F32 = jnp.float32
BF16 = jnp.bfloat16

NORM_EPS = 1e-6
ROPE_THETA = 10000.0

MLA_HEADS = 8
MLA_Q_RANK = 512
MLA_KV_RANK = 512
MLA_NOPE = 128
MLA_ROPE = 64
MLA_V = 128
SWA_HEADS = 16
SWA_KV_HEADS = 2
SWA_HD = 64
SWA_BLOCK = 128
MOBA_HEADS = 16
MOBA_HD = 128
MOBA_BLOCK = 256
MOBA_TOPK = 3

LANES = 128
V7X_VMEM_BYTES = 64 * 1024 * 1024
VMEM_LIMIT = 56 * 1024 * 1024
MASK_BIAS = -1e30
LOG2E = 1.4426950408889634
FLASH_TQ = 2048
FLASH_TK = 512
T_CHUNK = 256
T_AHEAD = 3


def _params(sem):
    return pltpu.CompilerParams(dimension_semantics=sem, vmem_limit_bytes=VMEM_LIMIT)


def _rms(t, g):
    return t * lax.rsqrt(jnp.mean(t * t, axis=-1, keepdims=True) + NORM_EPS) * g


def _rope(t, cos, sin_signed, half):
    width = t.shape[1]
    reps = width // LANES
    if reps > 1:
        cos = jnp.concatenate([cos] * reps, axis=1)
        sin_signed = jnp.concatenate([sin_signed] * reps, axis=1)
    lane = lax.broadcasted_iota(jnp.int32, t.shape, 1)
    first = (lane % (2 * half)) < half
    partner = jnp.where(first, pltpu.roll(t, width - half, 1), pltpu.roll(t, half, 1))
    return t * cos + partner * sin_signed


def _tables_kernel(pos_ref, invf_ref, sign_ref, cos_ref, sin_ref):
    ang = pos_ref[...].astype(F32) * invf_ref[...]
    cos_ref[...] = jnp.cos(ang)
    sin_ref[...] = jnp.sin(ang) * sign_ref[...]


def _rope_tables(positions, seq):
    def inv_freq(d):
        half = d // 2
        return jnp.power(ROPE_THETA, -jnp.arange(half, dtype=F32) * (2.0 / d))

    f64, f128 = inv_freq(64), inv_freq(128)
    invf = jnp.concatenate([f64, f64, f64, f64, f128, f128])[None, :]
    sign = np.concatenate([-np.ones(32), np.ones(32), -np.ones(32), np.ones(32),
                           -np.ones(64), np.ones(64)]).astype(np.float32)[None, :]
    tm = min(seq, 1024)
    return pl.pallas_call(
        _tables_kernel,
        grid=(seq // tm,),
        in_specs=[pl.BlockSpec((tm, 1), lambda i: (i, 0)),
                  pl.BlockSpec((1, 256), lambda i: (0, 0)),
                  pl.BlockSpec((1, 256), lambda i: (0, 0))],
        out_specs=[pl.BlockSpec((tm, 256), lambda i: (i, 0)),
                   pl.BlockSpec((tm, 256), lambda i: (i, 0))],
        out_shape=[jax.ShapeDtypeStruct((seq, 256), F32)] * 2,
        compiler_params=_params(("parallel",)),
        name="rope_tables",
    )(positions.reshape(seq, 1), invf, jnp.asarray(sign))


_EVEN_COLS = (0, 512, 1024, 1152, 2176, 2304, 2432)


def _even_proj_kernel(x_ref, gpre_ref, w_ref, gq_ref, gkv_ref, cos_ref, sin_ref,
                      cq_ref, ckv_ref, kr_ref, qs_ref, ks_ref, vs_ref):
    x = _rms(x_ref[...], gpre_ref[...]).astype(BF16)
    cos, sin = cos_ref[...], sin_ref[...]
    c = _EVEN_COLS

    def mm(k):
        return jnp.dot(x, w_ref[:, c[k]:c[k + 1]], preferred_element_type=F32)

    cq_ref[...] = _rms(mm(0), gq_ref[...]).astype(BF16)
    ckv_ref[...] = _rms(mm(1), gkv_ref[...]).astype(BF16)
    kr_ref[...] = _rope(mm(2), cos, sin, 32).astype(BF16)
    qs_ref[...] = (_rope(mm(3), cos, sin, 32) * (SWA_HD ** -0.5)).astype(BF16)
    ks_ref[...] = _rope(mm(4), cos, sin, 32).astype(BF16)
    vs_ref[...] = mm(5).astype(BF16)


def _even_proj(x, g_pre, w, gq, gkv, tabs_cos, tabs_sin):
    seq, d = x.shape
    tm = min(seq, 1024)
    widths = [_EVEN_COLS[k + 1] - _EVEN_COLS[k] for k in range(6)]
    row = lambda i: (i, 0)
    fixed = lambda i: (0, 0)
    return pl.pallas_call(
        _even_proj_kernel,
        grid=(seq // tm,),
        in_specs=[pl.BlockSpec((tm, d), row),
                  pl.BlockSpec((1, d), fixed),
                  pl.BlockSpec(w.shape, fixed, pipeline_mode=pl.Buffered(1)),
                  pl.BlockSpec((1, 512), fixed),
                  pl.BlockSpec((1, 512), fixed),
                  pl.BlockSpec((tm, LANES), row),
                  pl.BlockSpec((tm, LANES), row)],
        out_specs=[pl.BlockSpec((tm, n), row) for n in widths],
        out_shape=[jax.ShapeDtypeStruct((seq, n), BF16) for n in widths],
        compiler_params=_params(("parallel",)),
        name="even_proj",
    )(x, g_pre[None, :], w, gq[None, :], gkv[None, :], tabs_cos, tabs_sin)


def _mla_up_kernel(cq_ref, ckv_ref, wq_ref, wk_ref, wvt_ref, cos_ref, sin_ref,
                   qn_ref, qr_ref, kn_ref, vt_ref, *, scale):
    cq, ckv = cq_ref[...], ckv_ref[...]
    n = MLA_HEADS * LANES
    qn = jnp.dot(cq, wq_ref[:, :n], preferred_element_type=F32)
    qr = jnp.dot(cq, wq_ref[:, n:], preferred_element_type=F32)
    qn_ref[...] = (qn * scale).astype(BF16)
    qr_ref[...] = (_rope(qr, cos_ref[...], sin_ref[...], 32) * scale).astype(BF16)
    kn_ref[...] = jnp.dot(ckv, wk_ref[...], preferred_element_type=F32).astype(BF16)
    vt = lax.dot_general(wvt_ref[...], ckv, (((1,), (1,)), ((), ())), preferred_element_type=F32)
    vt_ref[...] = vt.astype(BF16).reshape(vt_ref.shape)


def _mla_up(cq, ckv, wq, wk, wvt, tabs_cos, tabs_sin):
    seq = cq.shape[0]
    tm = min(seq, FLASH_TK)
    n = MLA_HEADS * LANES
    row = lambda i: (i, 0)
    fixed = lambda i: (0, 0)
    scale = (MLA_NOPE + MLA_ROPE) ** -0.5 * LOG2E
    return pl.pallas_call(
        functools.partial(_mla_up_kernel, scale=scale),
        grid=(seq // tm,),
        in_specs=[pl.BlockSpec((tm, MLA_Q_RANK), row),
                  pl.BlockSpec((tm, MLA_KV_RANK), row),
                  pl.BlockSpec(wq.shape, fixed),
                  pl.BlockSpec(wk.shape, fixed),
                  pl.BlockSpec(wvt.shape, fixed),
                  pl.BlockSpec((tm, LANES), row),
                  pl.BlockSpec((tm, LANES), row)],
        out_specs=[pl.BlockSpec((tm, n), row)] * 3
        + [pl.BlockSpec((MLA_HEADS, None, LANES, tm), lambda i: (0, i, 0, 0))],
        out_shape=[jax.ShapeDtypeStruct((seq, n), BF16)] * 3
        + [jax.ShapeDtypeStruct((MLA_HEADS, seq // tm, LANES, tm), BF16)],
        compiler_params=_params(("parallel",)),
        name="mla_up",
    )(cq, ckv, wq, wk, wvt, tabs_cos, tabs_sin)


def _softmax_pv_t(st, vt, m_ref, l_ref, acc_ref):
    m_old = m_ref[...]
    m_new = jnp.maximum(m_old, jnp.max(st, axis=0, keepdims=True))
    alpha = jnp.exp2(m_old - m_new)
    pt = jnp.exp2(st - m_new)
    l_ref[...] = alpha * l_ref[...] + jnp.sum(pt, axis=0, keepdims=True)
    acc_ref[...] = alpha * acc_ref[...] + jnp.dot(vt, pt.astype(BF16), preferred_element_type=F32)
    m_ref[...] = m_new


def _flash_attend_t(qt, kv, i, tq, tk, stats, o_ref):
    nc = tq // T_CHUNK
    m_refs, l_refs, acc_refs = stats[:nc], stats[nc:2 * nc], stats[2 * nc:]
    for c in range(nc):
        m_refs[c][...] = jnp.full(m_refs[c].shape, -jnp.inf, F32)
        l_refs[c][...] = jnp.zeros(l_refs[c].shape, F32)
        acc_refs[c][...] = jnp.zeros(acc_refs[c].shape, F32)
    sub = tq // tk
    qts = [qt[:, c * T_CHUNK:(c + 1) * T_CHUNK] for c in range(nc)]
    key = lax.broadcasted_iota(jnp.int32, (tk, T_CHUNK), 0)
    qry = lax.broadcasted_iota(jnp.int32, (tk, T_CHUNK), 1)

    def run(work, kvs, diagonal):
        def score(n):
            d, c = work[n]
            st = jnp.dot(kvs[d][0], qts[c], preferred_element_type=F32)
            off = c * T_CHUNK - d * tk
            if diagonal and off < tk:
                st = jnp.where(key <= qry + off, st, -jnp.inf)
            return st

        ahead = [score(n) for n in range(min(T_AHEAD, len(work)))]
        for n, (d, c) in enumerate(work):
            if n + T_AHEAD < len(work):
                ahead.append(score(n + T_AHEAD))
            _softmax_pv_t(ahead.pop(0), kvs[d][1], m_refs[c], l_refs[c], acc_refs[c])

    def past(g, carry):
        run([(d, c) for d in range(sub) for c in range(nc)], [kv(g * sub + d) for d in range(sub)], False)
        return carry

    lax.fori_loop(0, i, past, 0)

    run([(d, c) for d in range(sub) for c in range(d * tk // T_CHUNK, nc)],
        [kv(i * sub + d) for d in range(sub)], True)


    out_t = jnp.concatenate([acc_refs[c][...] / l_refs[c][...] for c in range(nc)], axis=1)
    o_ref[...] = out_t.T.astype(BF16)


def _flash_t_scratch(tq):
    nc = tq // T_CHUNK
    return [pltpu.VMEM((1, T_CHUNK), F32)] * (2 * nc) + [pltpu.VMEM((LANES, T_CHUNK), F32)] * nc


def _mla_attn_t_kernel(qn_ref, qr_ref, kn_ref, kr_ref, vt_ref, o_ref, *stats, tq, tk):
    i = pl.program_id(1)
    q = jnp.concatenate([qn_ref[...], qr_ref[...]], axis=1)
    qt = q.astype(F32).T.astype(BF16)

    def kv(j):
        rows = pl.ds(pl.multiple_of(j * tk, tk), tk)
        return jnp.concatenate([kn_ref[rows, :], kr_ref[rows, :]], axis=1), vt_ref[j]

    _flash_attend_t(qt, kv, i, tq, tk, stats, o_ref)


def _mla_attn_t(qn, qr, kn, kr, vt):
    seq = qn.shape[0]
    tq = min(seq, FLASH_TQ)
    tk = min(seq, FLASH_TK)
    nk = seq // tk
    return pl.pallas_call(
        functools.partial(_mla_attn_t_kernel, tq=tq, tk=tk),
        grid=(MLA_HEADS, seq // tq),
        in_specs=[pl.BlockSpec((tq, LANES), lambda h, i: (i, h)),
                  pl.BlockSpec((tq, LANES), lambda h, i: (i, h)),
                  pl.BlockSpec((seq, LANES), lambda h, i: (0, h)),
                  pl.BlockSpec((seq, LANES), lambda h, i: (0, 0)),
                  pl.BlockSpec((None, nk, LANES, tk), lambda h, i: (h, 0, 0, 0))],
        out_specs=pl.BlockSpec((tq, LANES), lambda h, i: (i, h)),
        out_shape=jax.ShapeDtypeStruct((seq, MLA_HEADS * MLA_V), BF16),
        scratch_shapes=_flash_t_scratch(tq),
        compiler_params=_params(("parallel", "arbitrary")),
        name="mla_attn_t",
    )(qn, qr, kn, kr, vt)


def _swa_kernel(sink_ref, q_ref, kc_ref, kp_ref, vc_ref, vp_ref, o_ref, *, tq):
    i = pl.program_id(0)
    nb = tq // SWA_BLOCK
    group = SWA_HEADS // SWA_KV_HEADS
    pairs = group // 2
    lane = lax.broadcasted_iota(jnp.int32, (2 * SWA_BLOCK, LANES), 1)
    shape = (pairs * SWA_BLOCK, 2 * SWA_BLOCK)
    qp = lax.broadcasted_iota(jnp.int32, shape, 0) % SWA_BLOCK + SWA_BLOCK
    kp = lax.broadcasted_iota(jnp.int32, shape, 1)
    band = (kp <= qp) & (qp - kp < SWA_BLOCK)
    out_lane = lax.broadcasted_iota(jnp.int32, (pairs * SWA_BLOCK, LANES), 1)
    key_row = lax.broadcasted_iota(jnp.int32, (4 * SWA_BLOCK, LANES), 0)
    key_lane = lax.broadcasted_iota(jnp.int32, (4 * SWA_BLOCK, LANES), 1)
    ones_by_head = jnp.where((key_row < 2 * SWA_BLOCK) == (key_lane < SWA_HD), 1.0, 0.0).astype(BF16)

    def split(t, c):
        mine = jnp.where((lane >= c * SWA_HD) & (lane < (c + 1) * SWA_HD), t, 0.0)
        other = pltpu.roll(mine, SWA_HD, 1)
        lo, hi = (mine, other) if c == 0 else (other, mine)
        return jnp.concatenate([lo, hi], axis=0).astype(BF16)

    for b in range(nb):
        rows = slice(b * SWA_BLOCK, (b + 1) * SWA_BLOCK)
        if b == 0:
            k_prev, v_prev = kp_ref[...], vp_ref[...]
        else:
            prev = slice((b - 1) * SWA_BLOCK, b * SWA_BLOCK)
            k_prev, v_prev = kc_ref[prev, :], vc_ref[prev, :]
        kw = jnp.concatenate([k_prev, kc_ref[rows, :]], axis=0).astype(F32)
        vw = jnp.concatenate([v_prev, vc_ref[rows, :]], axis=0).astype(F32)
        first_key = jnp.where(i * nb + b == 0, SWA_BLOCK, 0)
        valid = band & (kp >= first_key)
        for c in range(SWA_KV_HEADS):
            kcat = split(kw, c)
            vcat = split(vw, c)
            cols = [slice((c * pairs + a) * LANES, (c * pairs + a + 1) * LANES) for a in range(pairs)]
            q4 = jnp.concatenate([q_ref[rows, cs] for cs in cols], axis=0)
            s = lax.dot_general(q4, kcat, (((1,), (1,)), ((), ())), preferred_element_type=F32)
            ps, sink_terms = [], []
            for e in range(2):
                sink = jnp.concatenate(
                    [jnp.full((SWA_BLOCK, 1), sink_ref[c * group + 2 * a + e], F32) for a in range(pairs)], axis=0)
                se = jnp.where(valid, s[:, e * 2 * SWA_BLOCK:(e + 1) * 2 * SWA_BLOCK], -jnp.inf)
                m = jnp.maximum(jnp.max(se, axis=1, keepdims=True), sink)
                ps.append(jnp.exp(se - m).astype(BF16))
                sink_terms.append(jnp.exp(sink - m))
            pv = jnp.dot(jnp.concatenate(ps, axis=1), jnp.concatenate([vcat, ones_by_head], axis=1),
                         preferred_element_type=F32)
            den = pv[:, LANES:] + jnp.where(out_lane < SWA_HD, sink_terms[0], sink_terms[1])
            o4 = (pv[:, :LANES] / den).astype(BF16)
            for a, cs in enumerate(cols):
                o_ref[rows, cs] = o4[a * SWA_BLOCK:(a + 1) * SWA_BLOCK]


def _swa_attn(qs, ks, vs, sinks):
    seq = qs.shape[0]
    tq = min(seq, 512)
    nb = tq // SWA_BLOCK
    cur = lambda i: (i, 0)
    prev = lambda i: (jnp.maximum(i * nb - 1, 0), 0)
    return pl.pallas_call(
        functools.partial(_swa_kernel, tq=tq),
        grid=(seq // tq,),
        in_specs=[pl.BlockSpec(memory_space=pltpu.SMEM),
                  pl.BlockSpec((tq, SWA_HEADS * SWA_HD), cur),
                  pl.BlockSpec((tq, LANES), cur),
                  pl.BlockSpec((SWA_BLOCK, LANES), prev),
                  pl.BlockSpec((tq, LANES), cur),
                  pl.BlockSpec((SWA_BLOCK, LANES), prev)],
        out_specs=pl.BlockSpec((tq, SWA_HEADS * SWA_HD), cur),
        out_shape=jax.ShapeDtypeStruct((seq, SWA_HEADS * SWA_HD), BF16),
        compiler_params=_params(("parallel",)),
        name="swa_attn",
    )(sinks, qs, ks, ks, vs, vs)


def _out_proj_kernel(*refs, n_a):
    a_refs = refs[:n_a]
    w_ref, h_ref, gpost_ref, gnext_ref, hout_ref, hn_ref = refs[n_a:]
    tm = h_ref.shape[0]
    for rows in [slice(r * tm // 4, (r + 1) * tm // 4) for r in range(4)]:
        m = None
        k0 = 0
        for a_ref in a_refs:
            kw = a_ref.shape[1]
            part = jnp.dot(a_ref[rows, :], w_ref[k0:k0 + kw, :], preferred_element_type=F32)
            m = part if m is None else m + part
            k0 += kw
        h = h_ref[rows, :] + _rms(m, gpost_ref[...])
        hout_ref[rows, :] = h
        hn_ref[rows, :] = _rms(h, gnext_ref[...]).astype(BF16)


def _out_proj(a_list, w, h, g_post, g_next):
    seq, d = h.shape
    tm = min(seq, 512)
    row = lambda i: (i, 0)
    fixed = lambda i: (0, 0)
    return pl.pallas_call(
        functools.partial(_out_proj_kernel, n_a=len(a_list)),
        grid=(seq // tm,),
        in_specs=[pl.BlockSpec((tm, a.shape[1]), row) for a in a_list]
        + [pl.BlockSpec(w.shape, fixed), pl.BlockSpec((tm, d), row),
           pl.BlockSpec((1, d), fixed), pl.BlockSpec((1, d), fixed)],
        out_specs=[pl.BlockSpec((tm, d), row), pl.BlockSpec((tm, d), row)],
        out_shape=[jax.ShapeDtypeStruct((seq, d), F32), jax.ShapeDtypeStruct((seq, d), BF16)],
        compiler_params=_params(("parallel",)),
        name="out_proj",
    )(*a_list, w, h, g_post[None, :], g_next[None, :])


def _mlp_kernel(x_ref, wu_ref, wd_ref, h_ref, g_ref, o_ref, acc_ref):
    f = pl.program_id(1)

    @pl.when(f == 0)
    def _():
        acc_ref[...] = jnp.zeros(acc_ref.shape, F32)

    a = jnp.maximum(jnp.dot(x_ref[...], wu_ref[...], preferred_element_type=F32), 0.0)
    acc_ref[...] += jnp.dot((a * a).astype(BF16), wd_ref[...], preferred_element_type=F32)

    @pl.when(f == pl.num_programs(1) - 1)
    def _():
        o_ref[...] = h_ref[...] + _rms(acc_ref[...], g_ref[...])


def _mlp(xn, w_up, w_down, h, g_post):
    seq, d = h.shape
    d_ff = w_up.shape[1]
    tm = min(seq, 1024)
    tf = min(d_ff, 512)
    return pl.pallas_call(
        _mlp_kernel,
        grid=(seq // tm, d_ff // tf),
        in_specs=[pl.BlockSpec((tm, d), lambda i, f: (i, 0), pipeline_mode=pl.Buffered(1)),
                  pl.BlockSpec((d, tf), lambda i, f: (0, f)),
                  pl.BlockSpec((tf, d), lambda i, f: (f, 0)),
                  pl.BlockSpec((tm, d), lambda i, f: (i, 0)),
                  pl.BlockSpec((1, d), lambda i, f: (0, 0))],
        out_specs=pl.BlockSpec((tm, d), lambda i, f: (i, 0), pipeline_mode=pl.Buffered(1)),
        out_shape=jax.ShapeDtypeStruct((seq, d), F32),
        scratch_shapes=[pltpu.VMEM((tm, d), F32)],
        compiler_params=_params(("parallel", "arbitrary")),
        name="mlp",
    )(xn, w_up, w_down, h, g_post[None, :])


def _ple_kernel(h_ref, p_ref, wg_ref, b_ref, wp_ref, *rest, with_next):
    h = h_ref[...]
    z = jnp.dot(h.astype(BF16), wg_ref[...], preferred_element_type=F32) + b_ref[...]
    gate = 1.0 / (1.0 + jnp.exp(-z))
    e = jnp.dot(p_ref[...].astype(BF16), wp_ref[...], preferred_element_type=F32)
    out = h + gate * e
    if with_next:
        gnext_ref, o_ref, hn_ref = rest
        hn_ref[...] = _rms(out, gnext_ref[...]).astype(BF16)
    else:
        (o_ref,) = rest
    o_ref[...] = out


def _ple(h, p, w_gate, b_gate, w_proj, g_next=None):
    seq, d = h.shape
    tm = min(seq, 512)
    row = lambda i: (i, 0)
    fixed = lambda i: (0, 0)
    with_next = g_next is not None
    in_specs = [pl.BlockSpec((tm, d), row), pl.BlockSpec((tm, p.shape[1]), row),
                pl.BlockSpec(w_gate.shape, fixed), pl.BlockSpec((1, d), fixed),
                pl.BlockSpec(w_proj.shape, fixed)]
    args = [h, p, w_gate, b_gate[None, :], w_proj]
    out_specs = [pl.BlockSpec((tm, d), row)]
    out_shape = [jax.ShapeDtypeStruct((seq, d), F32)]
    if with_next:
        in_specs.append(pl.BlockSpec((1, d), fixed))
        args.append(g_next[None, :])
        out_specs.append(pl.BlockSpec((tm, d), row))
        out_shape.append(jax.ShapeDtypeStruct((seq, d), BF16))
    res = pl.pallas_call(
        functools.partial(_ple_kernel, with_next=with_next),
        grid=(seq // tm,),
        in_specs=in_specs, out_specs=out_specs, out_shape=out_shape,
        compiler_params=_params(("parallel",)),
        name="ple",
    )(*args)
    return res if with_next else (res[0], None)


def _odd_proj_kernel(x_ref, w_ref, cos_ref, sin_ref, o_ref, *rest, rope, scale, kmean, tm):
    t = jnp.dot(x_ref[...], w_ref[...], preferred_element_type=F32)
    if rope:
        t = _rope(t, cos_ref[...], sin_ref[...], MOBA_HD // 2)
    if kmean:
        (km_ref,) = rest
        for b in range(tm // MOBA_BLOCK):
            blk = t[b * MOBA_BLOCK:(b + 1) * MOBA_BLOCK, :]
            km_ref[b] = jnp.sum(blk, axis=0, keepdims=True) * (1.0 / MOBA_BLOCK)
    if scale != 1.0:
        t = t * scale
    o_ref[...] = t.astype(BF16)


def _odd_proj(xn, w, col, tabs_cos, tabs_sin, *, rope, scale=1.0, kmean=False):
    seq, d = xn.shape
    n = d
    tm = min(seq, 1024)
    row = lambda i: (i, 0)
    out_specs = [pl.BlockSpec((tm, n), row)]
    out_shape = [jax.ShapeDtypeStruct((seq, n), BF16)]
    if kmean:
        nb = tm // MOBA_BLOCK
        out_specs.append(pl.BlockSpec((nb, 1, n), lambda i: (i, 0, 0)))
        out_shape.append(jax.ShapeDtypeStruct((seq // MOBA_BLOCK, 1, n), F32))
    res = pl.pallas_call(
        functools.partial(_odd_proj_kernel, rope=rope, scale=scale, kmean=kmean, tm=tm),
        grid=(seq // tm,),
        in_specs=[pl.BlockSpec((tm, d), row),
                  pl.BlockSpec((d, n), lambda i: (0, col), pipeline_mode=pl.Buffered(1)),
                  pl.BlockSpec((tm, LANES), lambda i: (i, 1)),
                  pl.BlockSpec((tm, LANES), lambda i: (i, 1))],
        out_specs=out_specs, out_shape=out_shape,
        compiler_params=_params(("parallel",)),
        name="odd_proj",
    )(xn, w, tabs_cos, tabs_sin)
    return res


def _odd_vt_kernel(x_ref, wt_ref, vt_ref):
    vt = lax.dot_general(wt_ref[...], x_ref[...], (((1,), (1,)), ((), ())), preferred_element_type=F32)
    vt_ref[...] = vt.astype(BF16).reshape(vt_ref.shape)


def _odd_proj_vt(xn, wt):
    seq, d = xn.shape
    tm = min(seq, FLASH_TK)
    heads = wt.shape[0] // LANES
    return pl.pallas_call(
        _odd_vt_kernel,
        grid=(seq // tm,),
        in_specs=[pl.BlockSpec((tm, d), lambda i: (i, 0)), pl.BlockSpec(wt.shape, lambda i: (0, 0))],
        out_specs=pl.BlockSpec((heads, None, LANES, tm), lambda i: (0, i, 0, 0)),
        out_shape=jax.ShapeDtypeStruct((heads, seq // tm, LANES, tm), BF16),
        compiler_params=_params(("parallel",)),
        name="odd_proj_vt",
    )(xn, wt)


def _moba_kernel(q_ref, k_ref, vt_ref, km_ref, o_ref, *stats, tq, tk, nkb):
    i = pl.program_id(1)
    q = q_ref[...]

    km = km_ref[...]
    if nkb < 64:
        km = jnp.concatenate([km, jnp.zeros((64 - nkb, LANES), F32)], axis=0)
    km_hi = km.astype(BF16)
    km_mid = (km - km_hi.astype(F32)).astype(BF16)
    gate = lax.dot_general(jnp.concatenate([km_hi, km_mid], axis=1), jnp.concatenate([q, q], axis=1),
                           (((1,), (1,)), ((), ())), preferred_element_type=F32)

    blk = lax.broadcasted_iota(jnp.int32, (64, tq), 0)
    qblk = (i * tq + lax.broadcasted_iota(jnp.int32, (64, tq), 1)) // MOBA_BLOCK
    gm = jnp.where(blk < qblk, gate, -jnp.inf)
    sel = blk == qblk
    for r in range(MOBA_TOPK):
        mx = jnp.max(gm, axis=0, keepdims=True)
        idx = jnp.min(jnp.where(gm == mx, blk, 64), axis=0, keepdims=True)
        pick = (blk == idx) & (qblk > r)
        sel = sel | pick
        gm = jnp.where(pick, -jnp.inf, gm)
    bias_t = jnp.where(sel, 0.0, MASK_BIAS)
    qat = jnp.concatenate([q.astype(F32).T, bias_t, jnp.zeros((64, tq), F32)], axis=0).astype(BF16)

    kb_per_tile = tk // MOBA_BLOCK
    krow = lax.broadcasted_iota(jnp.int32, (tk, LANES), 0) // MOBA_BLOCK
    klane = lax.broadcasted_iota(jnp.int32, (tk, LANES), 1)

    def kv(j):
        rows = pl.ds(pl.multiple_of(j * tk, tk), tk)
        onehot = jnp.where(klane == krow + j * kb_per_tile, 1.0, 0.0).astype(BF16)
        return jnp.concatenate([k_ref[rows, :], onehot], axis=1), vt_ref[j]

    _flash_attend_t(qat, kv, i, tq, tk, stats, o_ref)


def _moba_attn(q, k, vt, kmean):
    seq = q.shape[0]
    nkb = seq // MOBA_BLOCK
    assert nkb <= 64 and nkb % 8 == 0, "the gate matmul stacks blocks on 64 sublanes"
    tq = min(seq, FLASH_TQ)
    tk = min(seq, FLASH_TK)
    nk = seq // tk
    return pl.pallas_call(
        functools.partial(_moba_kernel, tq=tq, tk=tk, nkb=nkb),
        grid=(MOBA_HEADS, seq // tq),
        in_specs=[pl.BlockSpec((tq, LANES), lambda h, i: (i, h)),
                  pl.BlockSpec((seq, LANES), lambda h, i: (0, h)),
                  pl.BlockSpec((None, nk, LANES, tk), lambda h, i: (h, 0, 0, 0)),
                  pl.BlockSpec((nkb, LANES), lambda h, i: (0, h))],
        out_specs=pl.BlockSpec((tq, LANES), lambda h, i: (i, h)),
        out_shape=jax.ShapeDtypeStruct((seq, MOBA_HEADS * MOBA_HD), BF16),
        scratch_shapes=_flash_t_scratch(tq),
        compiler_params=_params(("parallel", "arbitrary")),
        name="moba_attn",
    )(q, k, vt, kmean)


def _even_in_weight(w_in):
    d = w_in.shape[0]
    cq, ckv, kr, qs, ks, vs = jnp.split(w_in, [512, 1024, 1088, 2112, 2240], axis=1)
    return jnp.concatenate([cq, ckv, kr, jnp.zeros((d, 64), w_in.dtype), qs, ks, vs], axis=1).astype(BF16)


def _mla_q_weight(w_q_up):
    r = w_q_up.shape[0]
    w = w_q_up.reshape(r, MLA_HEADS, MLA_NOPE + MLA_ROPE)
    nope = w[:, :, :MLA_NOPE].reshape(r, MLA_HEADS * MLA_NOPE)
    rope = jnp.pad(w[:, :, MLA_NOPE:], ((0, 0), (0, 0), (0, LANES - MLA_ROPE))).reshape(r, MLA_HEADS * LANES)
    return jnp.concatenate([nope, rope], axis=1).astype(BF16)


def _mla_kv_weight(w_kv_up):
    r = w_kv_up.shape[0]
    w = w_kv_up.reshape(r, MLA_HEADS, MLA_NOPE + MLA_V)
    nope = w[:, :, :MLA_NOPE].reshape(r, MLA_HEADS * MLA_NOPE)
    val = w[:, :, MLA_NOPE:].reshape(r, MLA_HEADS * MLA_V)
    return nope.astype(BF16), val.T.astype(BF16)


def kernel(x, p, positions, even_pre_g, even_w_in, mla_q_norm_g, mla_w_q_up, mla_kv_norm_g, mla_w_kv_up, swa_sinks, even_w_out, even_post_g, odd_pre_g, moba_w_qkv, odd_w_out, odd_post_g, mlp_pre_g, mlp_w_up, mlp_w_down, mlp_post_g, ple_w_gate, ple_b_gate, ple_w_proj):
    batch, seq, d = x.shape
    assert batch == 1
    h = x.reshape(seq, d)
    tabs_cos, tabs_sin = _rope_tables(positions, seq)

    cq, ckv, kr, qs, ks, vs = _even_proj(h, even_pre_g[0], _even_in_weight(even_w_in[0]),
                                         mla_q_norm_g[0], mla_kv_norm_g[0], tabs_cos, tabs_sin)
    qn, qr, kn, vt = _mla_up(cq, ckv, _mla_q_weight(mla_w_q_up[0]), *_mla_kv_weight(mla_w_kv_up[0]),
                             tabs_cos, tabs_sin)
    o_mla = _mla_attn_t(qn, qr, kn, kr, vt)
    o_swa = _swa_attn(qs, ks, vs, swa_sinks[0])
    h, hn = _out_proj([o_mla, o_swa], even_w_out[0].astype(BF16), h, even_post_g[0], mlp_pre_g[0])
    h = _mlp(hn, mlp_w_up[0].astype(BF16), mlp_w_down[0].astype(BF16), h, mlp_post_g[0])
    h, hn = _ple(h, p[0, 0], ple_w_gate[0].astype(BF16), ple_b_gate[0], ple_w_proj[0].astype(BF16),
                 g_next=odd_pre_g[0])

    w_qkv = moba_w_qkv[0].astype(BF16)
    (q,) = _odd_proj(hn, w_qkv, 0, tabs_cos, tabs_sin, rope=True, scale=MOBA_HD ** -0.5 * LOG2E)
    k, kmean = _odd_proj(hn, w_qkv, 1, tabs_cos, tabs_sin, rope=True, kmean=True)
    vt = _odd_proj_vt(hn, moba_w_qkv[0][:, 2 * d:].T.astype(BF16))
    o = _moba_attn(q, k, vt, kmean.reshape(seq // MOBA_BLOCK, MOBA_HEADS * MOBA_HD))
    h, hn = _out_proj([o], odd_w_out[0].astype(BF16), h, odd_post_g[0], mlp_pre_g[1])
    h = _mlp(hn, mlp_w_up[1].astype(BF16), mlp_w_down[1].astype(BF16), h, mlp_post_g[1])
    h, _ = _ple(h, p[1, 0], ple_w_gate[1].astype(BF16), ple_b_gate[1], ple_w_proj[1].astype(BF16))
    return h.reshape(batch, seq, d)
```

```python
import functools

import jax
import jax.numpy as jnp
import numpy as np
from jax import lax
from jax.experimental import pallas as pl
from jax.experimental.pallas import tpu as pltpu

F32 = jnp.float32
BF16 = jnp.bfloat16

NORM_EPS = 1e-6
ROPE_THETA = 10000.0

MLA_HEADS = 8
MLA_Q_RANK = 512
MLA_KV_RANK = 512
MLA_NOPE = 128
MLA_ROPE = 64
MLA_V = 128
SWA_HEADS = 16
SWA_KV_HEADS = 2
SWA_HD = 64
SWA_BLOCK = 128
MOBA_HEADS = 16
MOBA_HD = 128
MOBA_BLOCK = 256
MOBA_TOPK = 3

LANES = 128
V7X_VMEM_BYTES = 64 * 1024 * 1024
VMEM_LIMIT = 56 * 1024 * 1024
MASK_BIAS = -1e30
LOG2E = 1.4426950408889634
FLASH_TQ = 2048
FLASH_TK = 512
T_CHUNK = 256
T_AHEAD = 3
ONES_ROWS = 16


def _params(sem):
    return pltpu.CompilerParams(dimension_semantics=sem, vmem_limit_bytes=VMEM_LIMIT)


def _rms(t, g):
    return t * lax.rsqrt(jnp.mean(t * t, axis=-1, keepdims=True) + NORM_EPS) * g


def _rope(t, cos, sin_signed, half):
    width = t.shape[1]
    reps = width // LANES
    if reps > 1:
        cos = jnp.concatenate([cos] * reps, axis=1)
        sin_signed = jnp.concatenate([sin_signed] * reps, axis=1)
    lane = lax.broadcasted_iota(jnp.int32, t.shape, 1)
    first = (lane % (2 * half)) < half
    partner = jnp.where(first, pltpu.roll(t, width - half, 1), pltpu.roll(t, half, 1))
    return t * cos + partner * sin_signed


def _tables_kernel(pos_ref, invf_ref, sign_ref, cos_ref, sin_ref):
    ang = pos_ref[...].astype(F32) * invf_ref[...]
    cos_ref[...] = jnp.cos(ang)
    sin_ref[...] = jnp.sin(ang) * sign_ref[...]


def _rope_tables(positions, seq):
    def inv_freq(d):
        half = d // 2
        return jnp.power(ROPE_THETA, -jnp.arange(half, dtype=F32) * (2.0 / d))

    f64, f128 = inv_freq(64), inv_freq(128)
    invf = jnp.concatenate([f64, f64, f64, f64, f128, f128])[None, :]
    sign = np.concatenate([-np.ones(32), np.ones(32), -np.ones(32), np.ones(32),
                           -np.ones(64), np.ones(64)]).astype(np.float32)[None, :]
    tm = min(seq, 1024)
    return pl.pallas_call(
        _tables_kernel,
        grid=(seq // tm,),
        in_specs=[pl.BlockSpec((tm, 1), lambda i: (i, 0)),
                  pl.BlockSpec((1, 256), lambda i: (0, 0)),
                  pl.BlockSpec((1, 256), lambda i: (0, 0))],
        out_specs=[pl.BlockSpec((tm, 256), lambda i: (i, 0)),
                   pl.BlockSpec((tm, 256), lambda i: (i, 0))],
        out_shape=[jax.ShapeDtypeStruct((seq, 256), F32)] * 2,
        compiler_params=_params(("parallel",)),
        name="rope_tables",
    )(positions.reshape(seq, 1), invf, jnp.asarray(sign))


_EVEN_COLS = (0, 512, 1024, 1152, 2176, 2304, 2432)


def _even_proj_kernel(x_ref, gpre_ref, w_ref, gq_ref, gkv_ref, cos_ref, sin_ref,
                      cq_ref, ckv_ref, kr_ref, qs_ref, ks_ref, vs_ref):
    x = _rms(x_ref[...], gpre_ref[...]).astype(BF16)
    cos, sin = cos_ref[...], sin_ref[...]
    c = _EVEN_COLS

    def mm(k):
        return jnp.dot(x, w_ref[:, c[k]:c[k + 1]], preferred_element_type=F32)

    cq_ref[...] = _rms(mm(0), gq_ref[...]).astype(BF16)
    ckv_ref[...] = _rms(mm(1), gkv_ref[...]).astype(BF16)
    kr_ref[...] = _rope(mm(2), cos, sin, 32).astype(BF16)
    qs_ref[...] = (_rope(mm(3), cos, sin, 32) * (SWA_HD ** -0.5)).astype(BF16)
    ks_ref[...] = _rope(mm(4), cos, sin, 32).astype(BF16)
    vs_ref[...] = mm(5).astype(BF16)


def _even_proj(x, g_pre, w, gq, gkv, tabs_cos, tabs_sin):
    seq, d = x.shape
    tm = min(seq, 1024)
    widths = [_EVEN_COLS[k + 1] - _EVEN_COLS[k] for k in range(6)]
    row = lambda i: (i, 0)
    fixed = lambda i: (0, 0)
    return pl.pallas_call(
        _even_proj_kernel,
        grid=(seq // tm,),
        in_specs=[pl.BlockSpec((tm, d), row),
                  pl.BlockSpec((1, d), fixed),
                  pl.BlockSpec(w.shape, fixed, pipeline_mode=pl.Buffered(1)),
                  pl.BlockSpec((1, 512), fixed),
                  pl.BlockSpec((1, 512), fixed),
                  pl.BlockSpec((tm, LANES), row),
                  pl.BlockSpec((tm, LANES), row)],
        out_specs=[pl.BlockSpec((tm, n), row) for n in widths],
        out_shape=[jax.ShapeDtypeStruct((seq, n), BF16) for n in widths],
        compiler_params=_params(("parallel",)),
        name="even_proj",
    )(x, g_pre[None, :], w, gq[None, :], gkv[None, :], tabs_cos, tabs_sin)


def _mla_up_kernel(cq_ref, ckv_ref, wq_ref, wk_ref, wvt_ref, cos_ref, sin_ref,
                   qn_ref, qr_ref, kn_ref, vt_ref, *, scale):
    cq, ckv = cq_ref[...], ckv_ref[...]
    n = MLA_HEADS * LANES
    qn = jnp.dot(cq, wq_ref[:, :n], preferred_element_type=F32)
    qr = jnp.dot(cq, wq_ref[:, n:], preferred_element_type=F32)
    qn_ref[...] = (qn * scale).astype(BF16)
    qr_ref[...] = (_rope(qr, cos_ref[...], sin_ref[...], 32) * scale).astype(BF16)
    kn_ref[...] = jnp.dot(ckv, wk_ref[...], preferred_element_type=F32).astype(BF16)
    vt = lax.dot_general(wvt_ref[...], ckv, (((1,), (1,)), ((), ())), preferred_element_type=F32)
    vt_ref[...] = vt.astype(BF16).reshape(vt_ref.shape)


def _mla_up(cq, ckv, wq, wk, wvt, tabs_cos, tabs_sin):
    seq = cq.shape[0]
    tm = min(seq, FLASH_TK)
    n = MLA_HEADS * LANES
    row = lambda i: (i, 0)
    fixed = lambda i: (0, 0)
    scale = (MLA_NOPE + MLA_ROPE) ** -0.5 * LOG2E
    return pl.pallas_call(
        functools.partial(_mla_up_kernel, scale=scale),
        grid=(seq // tm,),
        in_specs=[pl.BlockSpec((tm, MLA_Q_RANK), row),
                  pl.BlockSpec((tm, MLA_KV_RANK), row),
                  pl.BlockSpec(wq.shape, fixed),
                  pl.BlockSpec(wk.shape, fixed),
                  pl.BlockSpec(wvt.shape, fixed),
                  pl.BlockSpec((tm, LANES), row),
                  pl.BlockSpec((tm, LANES), row)],
        out_specs=[pl.BlockSpec((tm, n), row)] * 3
        + [pl.BlockSpec((MLA_HEADS, None, LANES, tm), lambda i: (0, i, 0, 0))],
        out_shape=[jax.ShapeDtypeStruct((seq, n), BF16)] * 3
        + [jax.ShapeDtypeStruct((MLA_HEADS, seq // tm, LANES, tm), BF16)],
        compiler_params=_params(("parallel",)),
        name="mla_up",
    )(cq, ckv, wq, wk, wvt, tabs_cos, tabs_sin)


def _softmax_pv_t(st, vt1, m_ref, acc_ref):
    m_old = m_ref[...]
    m_new = jnp.maximum(m_old, jnp.max(st, axis=0, keepdims=True))
    alpha = jnp.exp2(m_old - m_new)
    pt = jnp.exp2(st - m_new)
    acc_ref[...] = alpha * acc_ref[...] + jnp.dot(vt1, pt.astype(BF16), preferred_element_type=F32)
    m_ref[...] = m_new


def _flash_attend_t(qt, kv, i, tq, tk, stats, o_ref):
    nc = tq // T_CHUNK
    m_refs, acc_refs = stats[:nc], stats[nc:]
    for c in range(nc):
        m_refs[c][...] = jnp.full(m_refs[c].shape, -jnp.inf, F32)
        acc_refs[c][...] = jnp.zeros(acc_refs[c].shape, F32)
    sub = tq // tk
    qts = [qt[:, c * T_CHUNK:(c + 1) * T_CHUNK] for c in range(nc)]
    key = lax.broadcasted_iota(jnp.int32, (tk, T_CHUNK), 0)
    qry = lax.broadcasted_iota(jnp.int32, (tk, T_CHUNK), 1)
    ones = jnp.ones((ONES_ROWS, tk), BF16)

    def run(work, kvs, diagonal):
        vt1 = [jnp.concatenate([vt, ones], axis=0) for _, vt in kvs]

        def score(n):
            d, c = work[n]
            st = jnp.dot(kvs[d][0], qts[c], preferred_element_type=F32)
            off = c * T_CHUNK - d * tk
            if diagonal and off < tk:
                st = jnp.where(key <= qry + off, st, -jnp.inf)
            return st

        ahead = [score(n) for n in range(min(T_AHEAD, len(work)))]
        for n, (d, c) in enumerate(work):
            if n + T_AHEAD < len(work):
                ahead.append(score(n + T_AHEAD))
            _softmax_pv_t(ahead.pop(0), vt1[d], m_refs[c], acc_refs[c])

    def past(g, carry):
        run([(d, c) for d in range(sub) for c in range(nc)], [kv(g * sub + d) for d in range(sub)], False)
        return carry

    lax.fori_loop(0, i, past, 0)

    run([(d, c) for d in range(sub) for c in range(d * tk // T_CHUNK, nc)],
        [kv(i * sub + d) for d in range(sub)], True)

    out_t = jnp.concatenate([acc_refs[c][:LANES, :] / acc_refs[c][LANES:LANES + 1, :] for c in range(nc)], axis=1)
    o_ref[...] = out_t.T.astype(BF16)


def _flash_t_scratch(tq):
    nc = tq // T_CHUNK
    return [pltpu.VMEM((1, T_CHUNK), F32)] * nc + [pltpu.VMEM((LANES + ONES_ROWS, T_CHUNK), F32)] * nc


def _mla_attn_t_kernel(qn_ref, qr_ref, kn_ref, kr_ref, vt_ref, o_ref, *stats, tq, tk):
    i = pl.program_id(1)
    q = jnp.concatenate([qn_ref[...], qr_ref[...]], axis=1)
    qt = q.astype(F32).T.astype(BF16)

    def kv(j):
        rows = pl.ds(pl.multiple_of(j * tk, tk), tk)
        return jnp.concatenate([kn_ref[rows, :], kr_ref[rows, :]], axis=1), vt_ref[j]

    _flash_attend_t(qt, kv, i, tq, tk, stats, o_ref)


def _mla_attn_t(qn, qr, kn, kr, vt):
    seq = qn.shape[0]
    tq = min(seq, FLASH_TQ)
    tk = min(seq, FLASH_TK)
    nk = seq // tk
    return pl.pallas_call(
        functools.partial(_mla_attn_t_kernel, tq=tq, tk=tk),
        grid=(MLA_HEADS, seq // tq),
        in_specs=[pl.BlockSpec((tq, LANES), lambda h, i: (i, h)),
                  pl.BlockSpec((tq, LANES), lambda h, i: (i, h)),
                  pl.BlockSpec((seq, LANES), lambda h, i: (0, h)),
                  pl.BlockSpec((seq, LANES), lambda h, i: (0, 0)),
                  pl.BlockSpec((None, nk, LANES, tk), lambda h, i: (h, 0, 0, 0))],
        out_specs=pl.BlockSpec((tq, LANES), lambda h, i: (i, h)),
        out_shape=jax.ShapeDtypeStruct((seq, MLA_HEADS * MLA_V), BF16),
        scratch_shapes=_flash_t_scratch(tq),
        compiler_params=_params(("parallel", "arbitrary")),
        name="mla_attn_t",
    )(qn, qr, kn, kr, vt)


def _swa_kernel(sink_ref, q_ref, kc_ref, kp_ref, vc_ref, vp_ref, o_ref, *, tq):
    i = pl.program_id(0)
    nb = tq // SWA_BLOCK
    group = SWA_HEADS // SWA_KV_HEADS
    pairs = group // 2
    lane = lax.broadcasted_iota(jnp.int32, (2 * SWA_BLOCK, LANES), 1)
    shape = (pairs * SWA_BLOCK, 2 * SWA_BLOCK)
    qp = lax.broadcasted_iota(jnp.int32, shape, 0) % SWA_BLOCK + SWA_BLOCK
    kp = lax.broadcasted_iota(jnp.int32, shape, 1)
    band = (kp <= qp) & (qp - kp < SWA_BLOCK)
    out_lane = lax.broadcasted_iota(jnp.int32, (pairs * SWA_BLOCK, LANES), 1)
    key_row = lax.broadcasted_iota(jnp.int32, (4 * SWA_BLOCK, LANES), 0)
    key_lane = lax.broadcasted_iota(jnp.int32, (4 * SWA_BLOCK, LANES), 1)
    ones_by_head = jnp.where((key_row < 2 * SWA_BLOCK) == (key_lane < SWA_HD), 1.0, 0.0).astype(BF16)

    def split(t, c):
        mine = jnp.where((lane >= c * SWA_HD) & (lane < (c + 1) * SWA_HD), t, 0.0)
        other = pltpu.roll(mine, SWA_HD, 1)
        lo, hi = (mine, other) if c == 0 else (other, mine)
        return jnp.concatenate([lo, hi], axis=0).astype(BF16)

    for b in range(nb):
        rows = slice(b * SWA_BLOCK, (b + 1) * SWA_BLOCK)
        if b == 0:
            k_prev, v_prev = kp_ref[...], vp_ref[...]
        else:
            prev = slice((b - 1) * SWA_BLOCK, b * SWA_BLOCK)
            k_prev, v_prev = kc_ref[prev, :], vc_ref[prev, :]
        kw = jnp.concatenate([k_prev, kc_ref[rows, :]], axis=0).astype(F32)
        vw = jnp.concatenate([v_prev, vc_ref[rows, :]], axis=0).astype(F32)
        first_key = jnp.where(i * nb + b == 0, SWA_BLOCK, 0)
        valid = band & (kp >= first_key)
        for c in range(SWA_KV_HEADS):
            kcat = split(kw, c)
            vcat = split(vw, c)
            cols = [slice((c * pairs + a) * LANES, (c * pairs + a + 1) * LANES) for a in range(pairs)]
            q4 = jnp.concatenate([q_ref[rows, cs] for cs in cols], axis=0)
            s = lax.dot_general(q4, kcat, (((1,), (1,)), ((), ())), preferred_element_type=F32)
            ps, sink_terms = [], []
            for e in range(2):
                sink = jnp.concatenate(
                    [jnp.full((SWA_BLOCK, 1), sink_ref[c * group + 2 * a + e], F32) for a in range(pairs)], axis=0)
                se = jnp.where(valid, s[:, e * 2 * SWA_BLOCK:(e + 1) * 2 * SWA_BLOCK], -jnp.inf)
                m = jnp.maximum(jnp.max(se, axis=1, keepdims=True), sink)
                ps.append(jnp.exp(se - m).astype(BF16))
                sink_terms.append(jnp.exp(sink - m))
            pv = jnp.dot(jnp.concatenate(ps, axis=1), jnp.concatenate([vcat, ones_by_head], axis=1),
                         preferred_element_type=F32)
            den = pv[:, LANES:] + jnp.where(out_lane < SWA_HD, sink_terms[0], sink_terms[1])
            o4 = (pv[:, :LANES] / den).astype(BF16)
            for a, cs in enumerate(cols):
                o_ref[rows, cs] = o4[a * SWA_BLOCK:(a + 1) * SWA_BLOCK]


def _swa_attn(qs, ks, vs, sinks):
    seq = qs.shape[0]
    tq = min(seq, 512)
    nb = tq // SWA_BLOCK
    cur = lambda i: (i, 0)
    prev = lambda i: (jnp.maximum(i * nb - 1, 0), 0)
    return pl.pallas_call(
        functools.partial(_swa_kernel, tq=tq),
        grid=(seq // tq,),
        in_specs=[pl.BlockSpec(memory_space=pltpu.SMEM),
                  pl.BlockSpec((tq, SWA_HEADS * SWA_HD), cur),
                  pl.BlockSpec((tq, LANES), cur),
                  pl.BlockSpec((SWA_BLOCK, LANES), prev),
                  pl.BlockSpec((tq, LANES), cur),
                  pl.BlockSpec((SWA_BLOCK, LANES), prev)],
        out_specs=pl.BlockSpec((tq, SWA_HEADS * SWA_HD), cur),
        out_shape=jax.ShapeDtypeStruct((seq, SWA_HEADS * SWA_HD), BF16),
        compiler_params=_params(("parallel",)),
        name="swa_attn",
    )(sinks, qs, ks, ks, vs, vs)


def _out_proj_kernel(*refs, n_a):
    a_refs = refs[:n_a]
    w_ref, h_ref, gpost_ref, gnext_ref, hout_ref, hn_ref = refs[n_a:]
    tm = h_ref.shape[0]
    for rows in [slice(r * tm // 4, (r + 1) * tm // 4) for r in range(4)]:
        m = None
        k0 = 0
        for a_ref in a_refs:
            kw = a_ref.shape[1]
            part = jnp.dot(a_ref[rows, :], w_ref[k0:k0 + kw, :], preferred_element_type=F32)
            m = part if m is None else m + part
            k0 += kw
        h = h_ref[rows, :] + _rms(m, gpost_ref[...])
        hout_ref[rows, :] = h
        hn_ref[rows, :] = _rms(h, gnext_ref[...]).astype(BF16)


def _out_proj(a_list, w, h, g_post, g_next):
    seq, d = h.shape
    tm = min(seq, 512)
    row = lambda i: (i, 0)
    fixed = lambda i: (0, 0)
    return pl.pallas_call(
        functools.partial(_out_proj_kernel, n_a=len(a_list)),
        grid=(seq // tm,),
        in_specs=[pl.BlockSpec((tm, a.shape[1]), row) for a in a_list]
        + [pl.BlockSpec(w.shape, fixed), pl.BlockSpec((tm, d), row),
           pl.BlockSpec((1, d), fixed), pl.BlockSpec((1, d), fixed)],
        out_specs=[pl.BlockSpec((tm, d), row), pl.BlockSpec((tm, d), row)],
        out_shape=[jax.ShapeDtypeStruct((seq, d), F32), jax.ShapeDtypeStruct((seq, d), BF16)],
        compiler_params=_params(("parallel",)),
        name="out_proj",
    )(*a_list, w, h, g_post[None, :], g_next[None, :])


def _mlp_kernel(x_ref, wu_ref, wd_ref, h_ref, g_ref, o_ref, acc_ref):
    f = pl.program_id(1)

    @pl.when(f == 0)
    def _():
        acc_ref[...] = jnp.zeros(acc_ref.shape, F32)

    a = jnp.maximum(jnp.dot(x_ref[...], wu_ref[...], preferred_element_type=F32), 0.0)
    acc_ref[...] += jnp.dot((a * a).astype(BF16), wd_ref[...], preferred_element_type=F32)

    @pl.when(f == pl.num_programs(1) - 1)
    def _():
        o_ref[...] = h_ref[...] + _rms(acc_ref[...], g_ref[...])


def _mlp(xn, w_up, w_down, h, g_post):
    seq, d = h.shape
    d_ff = w_up.shape[1]
    tm = min(seq, 512)
    tf = min(d_ff, 1024)
    return pl.pallas_call(
        _mlp_kernel,
        grid=(seq // tm, d_ff // tf),
        in_specs=[pl.BlockSpec((tm, d), lambda i, f: (i, 0)),
                  pl.BlockSpec((d, tf), lambda i, f: (0, f)),
                  pl.BlockSpec((tf, d), lambda i, f: (f, 0)),
                  pl.BlockSpec((tm, d), lambda i, f: (i, 0)),
                  pl.BlockSpec((1, d), lambda i, f: (0, 0))],
        out_specs=pl.BlockSpec((tm, d), lambda i, f: (i, 0)),
        out_shape=jax.ShapeDtypeStruct((seq, d), F32),
        scratch_shapes=[pltpu.VMEM((tm, d), F32)],
        compiler_params=_params(("parallel", "arbitrary")),
        name="mlp",
    )(xn, w_up, w_down, h, g_post[None, :])


def _ple_kernel(h_ref, p_ref, wg_ref, b_ref, wp_ref, *rest, with_next):
    h = h_ref[...]
    z = jnp.dot(h.astype(BF16), wg_ref[...], preferred_element_type=F32) + b_ref[...]
    gate = 1.0 / (1.0 + jnp.exp(-z))
    e = jnp.dot(p_ref[...].astype(BF16), wp_ref[...], preferred_element_type=F32)
    out = h + gate * e
    if with_next:
        gnext_ref, o_ref, hn_ref = rest
        hn_ref[...] = _rms(out, gnext_ref[...]).astype(BF16)
    else:
        (o_ref,) = rest
    o_ref[...] = out


def _ple(h, p, w_gate, b_gate, w_proj, g_next=None):
    seq, d = h.shape
    tm = min(seq, 512)
    row = lambda i: (i, 0)
    fixed = lambda i: (0, 0)
    with_next = g_next is not None
    in_specs = [pl.BlockSpec((tm, d), row), pl.BlockSpec((tm, p.shape[1]), row),
                pl.BlockSpec(w_gate.shape, fixed), pl.BlockSpec((1, d), fixed),
                pl.BlockSpec(w_proj.shape, fixed)]
    args = [h, p, w_gate, b_gate[None, :], w_proj]
    out_specs = [pl.BlockSpec((tm, d), row)]
    out_shape = [jax.ShapeDtypeStruct((seq, d), F32)]
    if with_next:
        in_specs.append(pl.BlockSpec((1, d), fixed))
        args.append(g_next[None, :])
        out_specs.append(pl.BlockSpec((tm, d), row))
        out_shape.append(jax.ShapeDtypeStruct((seq, d), BF16))
    res = pl.pallas_call(
        functools.partial(_ple_kernel, with_next=with_next),
        grid=(seq // tm,),
        in_specs=in_specs, out_specs=out_specs, out_shape=out_shape,
        compiler_params=_params(("parallel",)),
        name="ple",
    )(*args)
    return res if with_next else (res[0], None)


def _odd_proj_kernel(x_ref, w_ref, cos_ref, sin_ref, o_ref, *rest, rope, scale, kmean, tm):
    t = jnp.dot(x_ref[...], w_ref[...], preferred_element_type=F32)
    if rope:
        t = _rope(t, cos_ref[...], sin_ref[...], MOBA_HD // 2)
    if kmean:
        (km_ref,) = rest
        for b in range(tm // MOBA_BLOCK):
            blk = t[b * MOBA_BLOCK:(b + 1) * MOBA_BLOCK, :]
            km_ref[b] = jnp.sum(blk, axis=0, keepdims=True) * (1.0 / MOBA_BLOCK)
    if scale != 1.0:
        t = t * scale
    o_ref[...] = t.astype(BF16)


def _odd_proj(xn, w, col, tabs_cos, tabs_sin, *, rope, scale=1.0, kmean=False):
    seq, d = xn.shape
    n = d
    tm = min(seq, 1024)
    row = lambda i: (i, 0)
    out_specs = [pl.BlockSpec((tm, n), row)]
    out_shape = [jax.ShapeDtypeStruct((seq, n), BF16)]
    if kmean:
        nb = tm // MOBA_BLOCK
        out_specs.append(pl.BlockSpec((nb, 1, n), lambda i: (i, 0, 0)))
        out_shape.append(jax.ShapeDtypeStruct((seq // MOBA_BLOCK, 1, n), F32))
    res = pl.pallas_call(
        functools.partial(_odd_proj_kernel, rope=rope, scale=scale, kmean=kmean, tm=tm),
        grid=(seq // tm,),
        in_specs=[pl.BlockSpec((tm, d), row),
                  pl.BlockSpec((d, n), lambda i: (0, col), pipeline_mode=pl.Buffered(1)),
                  pl.BlockSpec((tm, LANES), lambda i: (i, 1)),
                  pl.BlockSpec((tm, LANES), lambda i: (i, 1))],
        out_specs=out_specs, out_shape=out_shape,
        compiler_params=_params(("parallel",)),
        name="odd_proj",
    )(xn, w, tabs_cos, tabs_sin)
    return res


def _odd_vt_kernel(x_ref, wt_ref, vt_ref):
    vt = lax.dot_general(wt_ref[...], x_ref[...], (((1,), (1,)), ((), ())), preferred_element_type=F32)
    vt_ref[...] = vt.astype(BF16).reshape(vt_ref.shape)


def _odd_proj_vt(xn, wt):
    seq, d = xn.shape
    tm = min(seq, FLASH_TK)
    heads = wt.shape[0] // LANES
    return pl.pallas_call(
        _odd_vt_kernel,
        grid=(seq // tm,),
        in_specs=[pl.BlockSpec((tm, d), lambda i: (i, 0)), pl.BlockSpec(wt.shape, lambda i: (0, 0))],
        out_specs=pl.BlockSpec((heads, None, LANES, tm), lambda i: (0, i, 0, 0)),
        out_shape=jax.ShapeDtypeStruct((heads, seq // tm, LANES, tm), BF16),
        compiler_params=_params(("parallel",)),
        name="odd_proj_vt",
    )(xn, wt)


def _moba_kernel(q_ref, k_ref, vt_ref, km_ref, o_ref, *stats, tq, tk, nkb):
    i = pl.program_id(1)
    q = q_ref[...]

    km = km_ref[...]
    if nkb < 64:
        km = jnp.concatenate([km, jnp.zeros((64 - nkb, LANES), F32)], axis=0)
    km_hi = km.astype(BF16)
    km_mid = (km - km_hi.astype(F32)).astype(BF16)
    gate = lax.dot_general(jnp.concatenate([km_hi, km_mid], axis=1), jnp.concatenate([q, q], axis=1),
                           (((1,), (1,)), ((), ())), preferred_element_type=F32)

    blk = lax.broadcasted_iota(jnp.int32, (64, tq), 0)
    qblk = (i * tq + lax.broadcasted_iota(jnp.int32, (64, tq), 1)) // MOBA_BLOCK
    gm = jnp.where(blk < qblk, gate, -jnp.inf)
    sel = blk == qblk
    for r in range(MOBA_TOPK):
        mx = jnp.max(gm, axis=0, keepdims=True)
        idx = jnp.min(jnp.where(gm == mx, blk, 64), axis=0, keepdims=True)
        pick = (blk == idx) & (qblk > r)
        sel = sel | pick
        gm = jnp.where(pick, -jnp.inf, gm)
    bias_t = jnp.where(sel, 0.0, MASK_BIAS)
    qat = jnp.concatenate([q.astype(F32).T, bias_t, jnp.zeros((64, tq), F32)], axis=0).astype(BF16)

    kb_per_tile = tk // MOBA_BLOCK
    krow = lax.broadcasted_iota(jnp.int32, (tk, LANES), 0) // MOBA_BLOCK
    klane = lax.broadcasted_iota(jnp.int32, (tk, LANES), 1)

    def kv(j):
        rows = pl.ds(pl.multiple_of(j * tk, tk), tk)
        onehot = jnp.where(klane == krow + j * kb_per_tile, 1.0, 0.0).astype(BF16)
        return jnp.concatenate([k_ref[rows, :], onehot], axis=1), vt_ref[j]

    _flash_attend_t(qat, kv, i, tq, tk, stats, o_ref)


def _moba_attn(q, k, vt, kmean):
    seq = q.shape[0]
    nkb = seq // MOBA_BLOCK
    assert nkb <= 64 and nkb % 8 == 0, "the gate matmul stacks blocks on 64 sublanes"
    tq = min(seq, FLASH_TQ)
    tk = min(seq, FLASH_TK)
    nk = seq // tk
    return pl.pallas_call(
        functools.partial(_moba_kernel, tq=tq, tk=tk, nkb=nkb),
        grid=(MOBA_HEADS, seq // tq),
        in_specs=[pl.BlockSpec((tq, LANES), lambda h, i: (i, h)),
                  pl.BlockSpec((seq, LANES), lambda h, i: (0, h)),
                  pl.BlockSpec((None, nk, LANES, tk), lambda h, i: (h, 0, 0, 0)),
                  pl.BlockSpec((nkb, LANES), lambda h, i: (0, h))],
        out_specs=pl.BlockSpec((tq, LANES), lambda h, i: (i, h)),
        out_shape=jax.ShapeDtypeStruct((seq, MOBA_HEADS * MOBA_HD), BF16),
        scratch_shapes=_flash_t_scratch(tq),
        compiler_params=_params(("parallel", "arbitrary")),
        name="moba_attn",
    )(q, k, vt, kmean)


def _even_in_weight(w_in):
    d = w_in.shape[0]
    cq, ckv, kr, qs, ks, vs = jnp.split(w_in, [512, 1024, 1088, 2112, 2240], axis=1)
    return jnp.concatenate([cq, ckv, kr, jnp.zeros((d, 64), w_in.dtype), qs, ks, vs], axis=1).astype(BF16)


def _mla_q_weight(w_q_up):
    r = w_q_up.shape[0]
    w = w_q_up.reshape(r, MLA_HEADS, MLA_NOPE + MLA_ROPE)
    nope = w[:, :, :MLA_NOPE].reshape(r, MLA_HEADS * MLA_NOPE)
    rope = jnp.pad(w[:, :, MLA_NOPE:], ((0, 0), (0, 0), (0, LANES - MLA_ROPE))).reshape(r, MLA_HEADS * LANES)
    return jnp.concatenate([nope, rope], axis=1).astype(BF16)


def _mla_kv_weight(w_kv_up):
    r = w_kv_up.shape[0]
    w = w_kv_up.reshape(r, MLA_HEADS, MLA_NOPE + MLA_V)
    nope = w[:, :, :MLA_NOPE].reshape(r, MLA_HEADS * MLA_NOPE)
    val = w[:, :, MLA_NOPE:].reshape(r, MLA_HEADS * MLA_V)
    return nope.astype(BF16), val.T.astype(BF16)


def kernel(x, p, positions, even_pre_g, even_w_in, mla_q_norm_g, mla_w_q_up, mla_kv_norm_g, mla_w_kv_up, swa_sinks, even_w_out, even_post_g, odd_pre_g, moba_w_qkv, odd_w_out, odd_post_g, mlp_pre_g, mlp_w_up, mlp_w_down, mlp_post_g, ple_w_gate, ple_b_gate, ple_w_proj):
    batch, seq, d = x.shape
    assert batch == 1
    h = x.reshape(seq, d)
    tabs_cos, tabs_sin = _rope_tables(positions, seq)

    cq, ckv, kr, qs, ks, vs = _even_proj(h, even_pre_g[0], _even_in_weight(even_w_in[0]),
                                         mla_q_norm_g[0], mla_kv_norm_g[0], tabs_cos, tabs_sin)
    qn, qr, kn, vt = _mla_up(cq, ckv, _mla_q_weight(mla_w_q_up[0]), *_mla_kv_weight(mla_w_kv_up[0]),
                             tabs_cos, tabs_sin)
    o_mla = _mla_attn_t(qn, qr, kn, kr, vt)
    o_swa = _swa_attn(qs, ks, vs, swa_sinks[0])
    h, hn = _out_proj([o_mla, o_swa], even_w_out[0].astype(BF16), h, even_post_g[0], mlp_pre_g[0])
    h = _mlp(hn, mlp_w_up[0].astype(BF16), mlp_w_down[0].astype(BF16), h, mlp_post_g[0])
    h, hn = _ple(h, p[0, 0], ple_w_gate[0].astype(BF16), ple_b_gate[0], ple_w_proj[0].astype(BF16),
                 g_next=odd_pre_g[0])

    w_qkv = moba_w_qkv[0].astype(BF16)
    (q,) = _odd_proj(hn, w_qkv, 0, tabs_cos, tabs_sin, rope=True, scale=MOBA_HD ** -0.5 * LOG2E)
    k, kmean = _odd_proj(hn, w_qkv, 1, tabs_cos, tabs_sin, rope=True, kmean=True)
    vt = _odd_proj_vt(hn, moba_w_qkv[0][:, 2 * d:].T.astype(BF16))
    o = _moba_attn(q, k, vt, kmean.reshape(seq // MOBA_BLOCK, MOBA_HEADS * MOBA_HD))
    h, hn = _out_proj([o], odd_w_out[0].astype(BF16), h, odd_post_g[0], mlp_pre_g[1])
    h = _mlp(hn, mlp_w_up[1].astype(BF16), mlp_w_down[1].astype(BF16), h, mlp_post_g[1])
    h, _ = _ple(h, p[1, 0], ple_w_gate[1].astype(BF16), ple_b_gate[1], ple_w_proj[1].astype(BF16))
    return h.reshape(batch, seq, d)
```

```python
import functools

import jax
import jax.numpy as jnp
import numpy as np
from jax import lax
from jax.experimental import pallas as pl
from jax.experimental.pallas import tpu as pltpu

F32 = jnp.float32
BF16 = jnp.bfloat16

NORM_EPS = 1e-6
ROPE_THETA = 10000.0

MLA_HEADS = 8
MLA_Q_RANK = 512
MLA_KV_RANK = 512
MLA_NOPE = 128
MLA_ROPE = 64
MLA_V = 128
SWA_HEADS = 16
SWA_KV_HEADS = 2
SWA_HD = 64
SWA_BLOCK = 128
MOBA_HEADS = 16
MOBA_HD = 128
MOBA_BLOCK = 256
MOBA_TOPK = 3

LANES = 128
V7X_VMEM_BYTES = 64 * 1024 * 1024
VMEM_LIMIT = 56 * 1024 * 1024
MASK_BIAS = -1e30
LOG2E = 1.4426950408889634
FLASH_TQ = 4096
FLASH_TK = 512
T_CHUNK = 256
T_AHEAD = 3
ONES_ROWS = 16


def _params(sem):
    return pltpu.CompilerParams(dimension_semantics=sem, vmem_limit_bytes=VMEM_LIMIT)


def _rms(t, g):
    return t * lax.rsqrt(jnp.mean(t * t, axis=-1, keepdims=True) + NORM_EPS) * g


def _rope(t, cos, sin_signed, half):
    width = t.shape[1]
    reps = width // LANES
    if reps > 1:
        cos = jnp.concatenate([cos] * reps, axis=1)
        sin_signed = jnp.concatenate([sin_signed] * reps, axis=1)
    lane = lax.broadcasted_iota(jnp.int32, t.shape, 1)
    first = (lane % (2 * half)) < half
    partner = jnp.where(first, pltpu.roll(t, width - half, 1), pltpu.roll(t, half, 1))
    return t * cos + partner * sin_signed


def _tables_kernel(pos_ref, invf_ref, sign_ref, cos_ref, sin_ref):
    ang = pos_ref[...].astype(F32) * invf_ref[...]
    cos_ref[...] = jnp.cos(ang)
    sin_ref[...] = jnp.sin(ang) * sign_ref[...]


def _rope_tables(positions, seq):
    def inv_freq(d):
        half = d // 2
        return jnp.power(ROPE_THETA, -jnp.arange(half, dtype=F32) * (2.0 / d))

    f64, f128 = inv_freq(64), inv_freq(128)
    invf = jnp.concatenate([f64, f64, f64, f64, f128, f128])[None, :]
    sign = np.concatenate([-np.ones(32), np.ones(32), -np.ones(32), np.ones(32),
                           -np.ones(64), np.ones(64)]).astype(np.float32)[None, :]
    tm = min(seq, 1024)
    return pl.pallas_call(
        _tables_kernel,
        grid=(seq // tm,),
        in_specs=[pl.BlockSpec((tm, 1), lambda i: (i, 0)),
                  pl.BlockSpec((1, 256), lambda i: (0, 0)),
                  pl.BlockSpec((1, 256), lambda i: (0, 0))],
        out_specs=[pl.BlockSpec((tm, 256), lambda i: (i, 0)),
                   pl.BlockSpec((tm, 256), lambda i: (i, 0))],
        out_shape=[jax.ShapeDtypeStruct((seq, 256), F32)] * 2,
        compiler_params=_params(("parallel",)),
        name="rope_tables",
    )(positions.reshape(seq, 1), invf, jnp.asarray(sign))


_EVEN_COLS = (0, 512, 1024, 1152, 2176, 2304, 2432)


def _even_proj_kernel(x_ref, gpre_ref, w_ref, gq_ref, gkv_ref, cos_ref, sin_ref,
                      cq_ref, ckv_ref, kr_ref, qs_ref, ks_ref, vs_ref):
    x = _rms(x_ref[...], gpre_ref[...]).astype(BF16)
    cos, sin = cos_ref[...], sin_ref[...]
    c = _EVEN_COLS

    def mm(k):
        return jnp.dot(x, w_ref[:, c[k]:c[k + 1]], preferred_element_type=F32)

    cq_ref[...] = _rms(mm(0), gq_ref[...]).astype(BF16)
    ckv_ref[...] = _rms(mm(1), gkv_ref[...]).astype(BF16)
    kr_ref[...] = _rope(mm(2), cos, sin, 32).astype(BF16)
    qs_ref[...] = (_rope(mm(3), cos, sin, 32) * (SWA_HD ** -0.5)).astype(BF16)
    ks_ref[...] = _rope(mm(4), cos, sin, 32).astype(BF16)
    vs_ref[...] = mm(5).astype(BF16)


def _even_proj(x, g_pre, w, gq, gkv, tabs_cos, tabs_sin):
    seq, d = x.shape
    tm = min(seq, 1024)
    widths = [_EVEN_COLS[k + 1] - _EVEN_COLS[k] for k in range(6)]
    row = lambda i: (i, 0)
    fixed = lambda i: (0, 0)
    return pl.pallas_call(
        _even_proj_kernel,
        grid=(seq // tm,),
        in_specs=[pl.BlockSpec((tm, d), row),
                  pl.BlockSpec((1, d), fixed),
                  pl.BlockSpec(w.shape, fixed, pipeline_mode=pl.Buffered(1)),
                  pl.BlockSpec((1, 512), fixed),
                  pl.BlockSpec((1, 512), fixed),
                  pl.BlockSpec((tm, LANES), row),
                  pl.BlockSpec((tm, LANES), row)],
        out_specs=[pl.BlockSpec((tm, n), row) for n in widths],
        out_shape=[jax.ShapeDtypeStruct((seq, n), BF16) for n in widths],
        compiler_params=_params(("parallel",)),
        name="even_proj",
    )(x, g_pre[None, :], w, gq[None, :], gkv[None, :], tabs_cos, tabs_sin)


def _mla_up_kernel(cq_ref, ckv_ref, wq_ref, wk_ref, wvt_ref, cos_ref, sin_ref,
                   qn_ref, qr_ref, kn_ref, vt_ref, *, scale):
    cq, ckv = cq_ref[...], ckv_ref[...]
    n = MLA_HEADS * LANES
    qn = jnp.dot(cq, wq_ref[:, :n], preferred_element_type=F32)
    qr = jnp.dot(cq, wq_ref[:, n:], preferred_element_type=F32)
    qn_ref[...] = (qn * scale).astype(BF16)
    qr_ref[...] = (_rope(qr, cos_ref[...], sin_ref[...], 32) * scale).astype(BF16)
    kn_ref[...] = jnp.dot(ckv, wk_ref[...], preferred_element_type=F32).astype(BF16)
    vt = lax.dot_general(wvt_ref[...], ckv, (((1,), (1,)), ((), ())), preferred_element_type=F32)
    vt_ref[...] = vt.astype(BF16).reshape(vt_ref.shape)


def _mla_up(cq, ckv, wq, wk, wvt, tabs_cos, tabs_sin):
    seq = cq.shape[0]
    tm = min(seq, FLASH_TK)
    n = MLA_HEADS * LANES
    row = lambda i: (i, 0)
    fixed = lambda i: (0, 0)
    scale = (MLA_NOPE + MLA_ROPE) ** -0.5 * LOG2E
    return pl.pallas_call(
        functools.partial(_mla_up_kernel, scale=scale),
        grid=(seq // tm,),
        in_specs=[pl.BlockSpec((tm, MLA_Q_RANK), row),
                  pl.BlockSpec((tm, MLA_KV_RANK), row),
                  pl.BlockSpec(wq.shape, fixed),
                  pl.BlockSpec(wk.shape, fixed),
                  pl.BlockSpec(wvt.shape, fixed),
                  pl.BlockSpec((tm, LANES), row),
                  pl.BlockSpec((tm, LANES), row)],
        out_specs=[pl.BlockSpec((tm, n), row)] * 3
        + [pl.BlockSpec((MLA_HEADS, None, LANES, tm), lambda i: (0, i, 0, 0))],
        out_shape=[jax.ShapeDtypeStruct((seq, n), BF16)] * 3
        + [jax.ShapeDtypeStruct((MLA_HEADS, seq // tm, LANES, tm), BF16)],
        compiler_params=_params(("parallel",)),
        name="mla_up",
    )(cq, ckv, wq, wk, wvt, tabs_cos, tabs_sin)


def _softmax_pv_t(st, vt1, m_ref, acc_ref):
    m_old = m_ref[...]
    m_new = jnp.maximum(m_old, jnp.max(st, axis=0, keepdims=True))
    alpha = jnp.exp2(m_old - m_new)
    pt = jnp.exp2(st - m_new)
    acc_ref[...] = alpha * acc_ref[...] + jnp.dot(vt1, pt.astype(BF16), preferred_element_type=F32)
    m_ref[...] = m_new


def _flash_attend_t(qt, kv, i, tq, tk, stats, o_ref):
    nc = tq // T_CHUNK
    m_refs, acc_refs = stats[:nc], stats[nc:]
    for c in range(nc):
        m_refs[c][...] = jnp.full(m_refs[c].shape, -jnp.inf, F32)
        acc_refs[c][...] = jnp.zeros(acc_refs[c].shape, F32)
    sub = tq // tk
    qts = [qt[:, c * T_CHUNK:(c + 1) * T_CHUNK] for c in range(nc)]
    key = lax.broadcasted_iota(jnp.int32, (tk, T_CHUNK), 0)
    qry = lax.broadcasted_iota(jnp.int32, (tk, T_CHUNK), 1)
    ones = jnp.ones((ONES_ROWS, tk), BF16)

    def run(work, kvs, diagonal):
        vt1 = [jnp.concatenate([vt, ones], axis=0) for _, vt in kvs]

        def score(n):
            d, c = work[n]
            st = jnp.dot(kvs[d][0], qts[c], preferred_element_type=F32)
            off = c * T_CHUNK - d * tk
            if diagonal and off < tk:
                st = jnp.where(key <= qry + off, st, -jnp.inf)
            return st

        ahead = [score(n) for n in range(min(T_AHEAD, len(work)))]
        for n, (d, c) in enumerate(work):
            if n + T_AHEAD < len(work):
                ahead.append(score(n + T_AHEAD))
            _softmax_pv_t(ahead.pop(0), vt1[d], m_refs[c], acc_refs[c])

    def past(g, carry):
        run([(d, c) for d in range(sub) for c in range(nc)], [kv(g * sub + d) for d in range(sub)], False)
        return carry

    lax.fori_loop(0, i, past, 0)

    run([(d, c) for d in range(sub) for c in range(d * tk // T_CHUNK, nc)],
        [kv(i * sub + d) for d in range(sub)], True)

    out_t = jnp.concatenate([acc_refs[c][:LANES, :] / acc_refs[c][LANES:LANES + 1, :] for c in range(nc)], axis=1)
    o_ref[...] = out_t.T.astype(BF16)


def _flash_t_scratch(tq):
    nc = tq // T_CHUNK
    return [pltpu.VMEM((1, T_CHUNK), F32)] * nc + [pltpu.VMEM((LANES + ONES_ROWS, T_CHUNK), F32)] * nc


def _mla_attn_t_kernel(qn_ref, qr_ref, kn_ref, kr_ref, vt_ref, o_ref, *stats, tq, tk):
    i = pl.program_id(1)
    q = jnp.concatenate([qn_ref[...], qr_ref[...]], axis=1)
    qt = q.astype(F32).T.astype(BF16)

    def kv(j):
        rows = pl.ds(pl.multiple_of(j * tk, tk), tk)
        return jnp.concatenate([kn_ref[rows, :], kr_ref[rows, :]], axis=1), vt_ref[j]

    _flash_attend_t(qt, kv, i, tq, tk, stats, o_ref)


def _mla_attn_t(qn, qr, kn, kr, vt):
    seq = qn.shape[0]
    tq = min(seq, FLASH_TQ)
    tk = min(seq, FLASH_TK)
    nk = seq // tk
    return pl.pallas_call(
        functools.partial(_mla_attn_t_kernel, tq=tq, tk=tk),
        grid=(MLA_HEADS, seq // tq),
        in_specs=[pl.BlockSpec((tq, LANES), lambda h, i: (i, h)),
                  pl.BlockSpec((tq, LANES), lambda h, i: (i, h)),
                  pl.BlockSpec((seq, LANES), lambda h, i: (0, h)),
                  pl.BlockSpec((seq, LANES), lambda h, i: (0, 0)),
                  pl.BlockSpec((None, nk, LANES, tk), lambda h, i: (h, 0, 0, 0))],
        out_specs=pl.BlockSpec((tq, LANES), lambda h, i: (i, h)),
        out_shape=jax.ShapeDtypeStruct((seq, MLA_HEADS * MLA_V), BF16),
        scratch_shapes=_flash_t_scratch(tq),
        compiler_params=_params(("parallel", "arbitrary")),
        name="mla_attn_t",
    )(qn, qr, kn, kr, vt)


def _swa_kernel(sink_ref, q_ref, kc_ref, kp_ref, vc_ref, vp_ref, o_ref, *, tq):
    i = pl.program_id(0)
    nb = tq // SWA_BLOCK
    group = SWA_HEADS // SWA_KV_HEADS
    pairs = group // 2
    lane = lax.broadcasted_iota(jnp.int32, (2 * SWA_BLOCK, LANES), 1)
    shape = (pairs * SWA_BLOCK, 2 * SWA_BLOCK)
    qp = lax.broadcasted_iota(jnp.int32, shape, 0) % SWA_BLOCK + SWA_BLOCK
    kp = lax.broadcasted_iota(jnp.int32, shape, 1)
    band = (kp <= qp) & (qp - kp < SWA_BLOCK)
    out_lane = lax.broadcasted_iota(jnp.int32, (pairs * SWA_BLOCK, LANES), 1)
    key_row = lax.broadcasted_iota(jnp.int32, (4 * SWA_BLOCK, LANES), 0)
    key_lane = lax.broadcasted_iota(jnp.int32, (4 * SWA_BLOCK, LANES), 1)
    ones_by_head = jnp.where((key_row < 2 * SWA_BLOCK) == (key_lane < SWA_HD), 1.0, 0.0).astype(BF16)

    def split(t, c):
        mine = jnp.where((lane >= c * SWA_HD) & (lane < (c + 1) * SWA_HD), t, 0.0)
        other = pltpu.roll(mine, SWA_HD, 1)
        lo, hi = (mine, other) if c == 0 else (other, mine)
        return jnp.concatenate([lo, hi], axis=0).astype(BF16)

    for b in range(nb):
        rows = slice(b * SWA_BLOCK, (b + 1) * SWA_BLOCK)
        if b == 0:
            k_prev, v_prev = kp_ref[...], vp_ref[...]
        else:
            prev = slice((b - 1) * SWA_BLOCK, b * SWA_BLOCK)
            k_prev, v_prev = kc_ref[prev, :], vc_ref[prev, :]
        kw = jnp.concatenate([k_prev, kc_ref[rows, :]], axis=0).astype(F32)
        vw = jnp.concatenate([v_prev, vc_ref[rows, :]], axis=0).astype(F32)
        first_key = jnp.where(i * nb + b == 0, SWA_BLOCK, 0)
        valid = band & (kp >= first_key)
        for c in range(SWA_KV_HEADS):
            kcat = split(kw, c)
            vcat = split(vw, c)
            cols = [slice((c * pairs + a) * LANES, (c * pairs + a + 1) * LANES) for a in range(pairs)]
            q4 = jnp.concatenate([q_ref[rows, cs] for cs in cols], axis=0)
            s = lax.dot_general(q4, kcat, (((1,), (1,)), ((), ())), preferred_element_type=F32)
            ps, sink_terms = [], []
            for e in range(2):
                sink = jnp.concatenate(
                    [jnp.full((SWA_BLOCK, 1), sink_ref[c * group + 2 * a + e], F32) for a in range(pairs)], axis=0)
                se = jnp.where(valid, s[:, e * 2 * SWA_BLOCK:(e + 1) * 2 * SWA_BLOCK], -jnp.inf)
                m = jnp.maximum(jnp.max(se, axis=1, keepdims=True), sink)
                ps.append(jnp.exp(se - m).astype(BF16))
                sink_terms.append(jnp.exp(sink - m))
            pv = jnp.dot(jnp.concatenate(ps, axis=1), jnp.concatenate([vcat, ones_by_head], axis=1),
                         preferred_element_type=F32)
            den = pv[:, LANES:] + jnp.where(out_lane < SWA_HD, sink_terms[0], sink_terms[1])
            o4 = (pv[:, :LANES] / den).astype(BF16)
            for a, cs in enumerate(cols):
                o_ref[rows, cs] = o4[a * SWA_BLOCK:(a + 1) * SWA_BLOCK]


def _swa_attn(qs, ks, vs, sinks):
    seq = qs.shape[0]
    tq = min(seq, 512)
    nb = tq // SWA_BLOCK
    cur = lambda i: (i, 0)
    prev = lambda i: (jnp.maximum(i * nb - 1, 0), 0)
    return pl.pallas_call(
        functools.partial(_swa_kernel, tq=tq),
        grid=(seq // tq,),
        in_specs=[pl.BlockSpec(memory_space=pltpu.SMEM),
                  pl.BlockSpec((tq, SWA_HEADS * SWA_HD), cur),
                  pl.BlockSpec((tq, LANES), cur),
                  pl.BlockSpec((SWA_BLOCK, LANES), prev),
                  pl.BlockSpec((tq, LANES), cur),
                  pl.BlockSpec((SWA_BLOCK, LANES), prev)],
        out_specs=pl.BlockSpec((tq, SWA_HEADS * SWA_HD), cur),
        out_shape=jax.ShapeDtypeStruct((seq, SWA_HEADS * SWA_HD), BF16),
        compiler_params=_params(("parallel",)),
        name="swa_attn",
    )(sinks, qs, ks, ks, vs, vs)


def _out_proj_kernel(*refs, n_a):
    a_refs = refs[:n_a]
    w_ref, h_ref, gpost_ref, gnext_ref, hout_ref, hn_ref = refs[n_a:]
    tm = h_ref.shape[0]
    for rows in [slice(r * tm // 4, (r + 1) * tm // 4) for r in range(4)]:
        m = None
        k0 = 0
        for a_ref in a_refs:
            kw = a_ref.shape[1]
            part = jnp.dot(a_ref[rows, :], w_ref[k0:k0 + kw, :], preferred_element_type=F32)
            m = part if m is None else m + part
            k0 += kw
        h = h_ref[rows, :] + _rms(m, gpost_ref[...])
        hout_ref[rows, :] = h
        hn_ref[rows, :] = _rms(h, gnext_ref[...]).astype(BF16)


def _out_proj(a_list, w, h, g_post, g_next):
    seq, d = h.shape
    tm = min(seq, 512)
    row = lambda i: (i, 0)
    fixed = lambda i: (0, 0)
    return pl.pallas_call(
        functools.partial(_out_proj_kernel, n_a=len(a_list)),
        grid=(seq // tm,),
        in_specs=[pl.BlockSpec((tm, a.shape[1]), row) for a in a_list]
        + [pl.BlockSpec(w.shape, fixed), pl.BlockSpec((tm, d), row),
           pl.BlockSpec((1, d), fixed), pl.BlockSpec((1, d), fixed)],
        out_specs=[pl.BlockSpec((tm, d), row), pl.BlockSpec((tm, d), row)],
        out_shape=[jax.ShapeDtypeStruct((seq, d), F32), jax.ShapeDtypeStruct((seq, d), BF16)],
        compiler_params=_params(("parallel",)),
        name="out_proj",
    )(*a_list, w, h, g_post[None, :], g_next[None, :])


def _mlp_kernel(x_ref, wu_ref, wd_ref, h_ref, g_ref, o_ref, acc_ref):
    f = pl.program_id(1)

    @pl.when(f == 0)
    def _():
        acc_ref[...] = jnp.zeros(acc_ref.shape, F32)

    a = jnp.maximum(jnp.dot(x_ref[...], wu_ref[...], preferred_element_type=F32), 0.0)
    acc_ref[...] += jnp.dot((a * a).astype(BF16), wd_ref[...], preferred_element_type=F32)

    @pl.when(f == pl.num_programs(1) - 1)
    def _():
        o_ref[...] = h_ref[...] + _rms(acc_ref[...], g_ref[...])


def _mlp(xn, w_up, w_down, h, g_post):
    seq, d = h.shape
    d_ff = w_up.shape[1]
    tm = min(seq, 512)
    tf = min(d_ff, 1024)
    return pl.pallas_call(
        _mlp_kernel,
        grid=(seq // tm, d_ff // tf),
        in_specs=[pl.BlockSpec((tm, d), lambda i, f: (i, 0)),
                  pl.BlockSpec((d, tf), lambda i, f: (0, f)),
                  pl.BlockSpec((tf, d), lambda i, f: (f, 0)),
                  pl.BlockSpec((tm, d), lambda i, f: (i, 0)),
                  pl.BlockSpec((1, d), lambda i, f: (0, 0))],
        out_specs=pl.BlockSpec((tm, d), lambda i, f: (i, 0)),
        out_shape=jax.ShapeDtypeStruct((seq, d), F32),
        scratch_shapes=[pltpu.VMEM((tm, d), F32)],
        compiler_params=_params(("parallel", "arbitrary")),
        name="mlp",
    )(xn, w_up, w_down, h, g_post[None, :])


def _ple_kernel(h_ref, p_ref, wg_ref, b_ref, wp_ref, *rest, with_next):
    h = h_ref[...]
    z = jnp.dot(h.astype(BF16), wg_ref[...], preferred_element_type=F32) + b_ref[...]
    gate = 1.0 / (1.0 + jnp.exp(-z))
    e = jnp.dot(p_ref[...].astype(BF16), wp_ref[...], preferred_element_type=F32)
    out = h + gate * e
    if with_next:
        gnext_ref, o_ref, hn_ref = rest
        hn_ref[...] = _rms(out, gnext_ref[...]).astype(BF16)
    else:
        (o_ref,) = rest
    o_ref[...] = out


def _ple(h, p, w_gate, b_gate, w_proj, g_next=None):
    seq, d = h.shape
    tm = min(seq, 512)
    row = lambda i: (i, 0)
    fixed = lambda i: (0, 0)
    with_next = g_next is not None
    in_specs = [pl.BlockSpec((tm, d), row), pl.BlockSpec((tm, p.shape[1]), row),
                pl.BlockSpec(w_gate.shape, fixed), pl.BlockSpec((1, d), fixed),
                pl.BlockSpec(w_proj.shape, fixed)]
    args = [h, p, w_gate, b_gate[None, :], w_proj]
    out_specs = [pl.BlockSpec((tm, d), row)]
    out_shape = [jax.ShapeDtypeStruct((seq, d), F32)]
    if with_next:
        in_specs.append(pl.BlockSpec((1, d), fixed))
        args.append(g_next[None, :])
        out_specs.append(pl.BlockSpec((tm, d), row))
        out_shape.append(jax.ShapeDtypeStruct((seq, d), BF16))
    res = pl.pallas_call(
        functools.partial(_ple_kernel, with_next=with_next),
        grid=(seq // tm,),
        in_specs=in_specs, out_specs=out_specs, out_shape=out_shape,
        compiler_params=_params(("parallel",)),
        name="ple",
    )(*args)
    return res if with_next else (res[0], None)


def _odd_proj_kernel(x_ref, w_ref, cos_ref, sin_ref, o_ref, *rest, rope, scale, kmean, tm):
    t = jnp.dot(x_ref[...], w_ref[...], preferred_element_type=F32)
    if rope:
        t = _rope(t, cos_ref[...], sin_ref[...], MOBA_HD // 2)
    if kmean:
        (km_ref,) = rest
        for b in range(tm // MOBA_BLOCK):
            blk = t[b * MOBA_BLOCK:(b + 1) * MOBA_BLOCK, :]
            km_ref[b] = jnp.sum(blk, axis=0, keepdims=True) * (1.0 / MOBA_BLOCK)
    if scale != 1.0:
        t = t * scale
    o_ref[...] = t.astype(BF16)


def _odd_proj(xn, w, col, tabs_cos, tabs_sin, *, rope, scale=1.0, kmean=False):
    seq, d = xn.shape
    n = d
    tm = min(seq, 1024)
    row = lambda i: (i, 0)
    out_specs = [pl.BlockSpec((tm, n), row)]
    out_shape = [jax.ShapeDtypeStruct((seq, n), BF16)]
    if kmean:
        nb = tm // MOBA_BLOCK
        out_specs.append(pl.BlockSpec((nb, 1, n), lambda i: (i, 0, 0)))
        out_shape.append(jax.ShapeDtypeStruct((seq // MOBA_BLOCK, 1, n), F32))
    res = pl.pallas_call(
        functools.partial(_odd_proj_kernel, rope=rope, scale=scale, kmean=kmean, tm=tm),
        grid=(seq // tm,),
        in_specs=[pl.BlockSpec((tm, d), row),
                  pl.BlockSpec((d, n), lambda i: (0, col), pipeline_mode=pl.Buffered(1)),
                  pl.BlockSpec((tm, LANES), lambda i: (i, 1)),
                  pl.BlockSpec((tm, LANES), lambda i: (i, 1))],
        out_specs=out_specs, out_shape=out_shape,
        compiler_params=_params(("parallel",)),
        name="odd_proj",
    )(xn, w, tabs_cos, tabs_sin)
    return res


def _odd_vt_kernel(x_ref, wt_ref, vt_ref):
    vt = lax.dot_general(wt_ref[...], x_ref[...], (((1,), (1,)), ((), ())), preferred_element_type=F32)
    vt_ref[...] = vt.astype(BF16).reshape(vt_ref.shape)


def _odd_proj_vt(xn, wt):
    seq, d = xn.shape
    tm = min(seq, FLASH_TK)
    heads = wt.shape[0] // LANES
    return pl.pallas_call(
        _odd_vt_kernel,
        grid=(seq // tm,),
        in_specs=[pl.BlockSpec((tm, d), lambda i: (i, 0)), pl.BlockSpec(wt.shape, lambda i: (0, 0))],
        out_specs=pl.BlockSpec((heads, None, LANES, tm), lambda i: (0, i, 0, 0)),
        out_shape=jax.ShapeDtypeStruct((heads, seq // tm, LANES, tm), BF16),
        compiler_params=_params(("parallel",)),
        name="odd_proj_vt",
    )(xn, wt)


def _moba_kernel(q_ref, k_ref, vt_ref, km_ref, o_ref, *stats, tq, tk, nkb):
    i = pl.program_id(1)
    q = q_ref[...]

    km = km_ref[...]
    if nkb < 64:
        km = jnp.concatenate([km, jnp.zeros((64 - nkb, LANES), F32)], axis=0)
    km_hi = km.astype(BF16)
    km_mid = (km - km_hi.astype(F32)).astype(BF16)
    gate = lax.dot_general(jnp.concatenate([km_hi, km_mid], axis=1), jnp.concatenate([q, q], axis=1),
                           (((1,), (1,)), ((), ())), preferred_element_type=F32)

    blk = lax.broadcasted_iota(jnp.int32, (64, tq), 0)
    qblk = (i * tq + lax.broadcasted_iota(jnp.int32, (64, tq), 1)) // MOBA_BLOCK
    gm = jnp.where(blk < qblk, gate, -jnp.inf)
    sel = blk == qblk
    for r in range(MOBA_TOPK):
        mx = jnp.max(gm, axis=0, keepdims=True)
        idx = jnp.min(jnp.where(gm == mx, blk, 64), axis=0, keepdims=True)
        pick = (blk == idx) & (qblk > r)
        sel = sel | pick
        gm = jnp.where(pick, -jnp.inf, gm)
    bias_t = jnp.where(sel, 0.0, MASK_BIAS)
    qat = jnp.concatenate([q.astype(F32).T, bias_t, jnp.zeros((64, tq), F32)], axis=0).astype(BF16)

    kb_per_tile = tk // MOBA_BLOCK
    krow = lax.broadcasted_iota(jnp.int32, (tk, LANES), 0) // MOBA_BLOCK
    klane = lax.broadcasted_iota(jnp.int32, (tk, LANES), 1)

    def kv(j):
        rows = pl.ds(pl.multiple_of(j * tk, tk), tk)
        onehot = jnp.where(klane == krow + j * kb_per_tile, 1.0, 0.0).astype(BF16)
        return jnp.concatenate([k_ref[rows, :], onehot], axis=1), vt_ref[j]

    _flash_attend_t(qat, kv, i, tq, tk, stats, o_ref)


def _moba_attn(q, k, vt, kmean):
    seq = q.shape[0]
    nkb = seq // MOBA_BLOCK
    assert nkb <= 64 and nkb % 8 == 0, "the gate matmul stacks blocks on 64 sublanes"
    tq = min(seq, FLASH_TQ)
    tk = min(seq, FLASH_TK)
    nk = seq // tk
    return pl.pallas_call(
        functools.partial(_moba_kernel, tq=tq, tk=tk, nkb=nkb),
        grid=(MOBA_HEADS, seq // tq),
        in_specs=[pl.BlockSpec((tq, LANES), lambda h, i: (i, h)),
                  pl.BlockSpec((seq, LANES), lambda h, i: (0, h)),
                  pl.BlockSpec((None, nk, LANES, tk), lambda h, i: (h, 0, 0, 0)),
                  pl.BlockSpec((nkb, LANES), lambda h, i: (0, h))],
        out_specs=pl.BlockSpec((tq, LANES), lambda h, i: (i, h)),
        out_shape=jax.ShapeDtypeStruct((seq, MOBA_HEADS * MOBA_HD), BF16),
        scratch_shapes=_flash_t_scratch(tq),
        compiler_params=_params(("parallel", "arbitrary")),
        name="moba_attn",
    )(q, k, vt, kmean)


def _even_in_weight(w_in):
    d = w_in.shape[0]
    cq, ckv, kr, qs, ks, vs = jnp.split(w_in, [512, 1024, 1088, 2112, 2240], axis=1)
    return jnp.concatenate([cq, ckv, kr, jnp.zeros((d, 64), w_in.dtype), qs, ks, vs], axis=1).astype(BF16)


def _mla_q_weight(w_q_up):
    r = w_q_up.shape[0]
    w = w_q_up.reshape(r, MLA_HEADS, MLA_NOPE + MLA_ROPE)
    nope = w[:, :, :MLA_NOPE].reshape(r, MLA_HEADS * MLA_NOPE)
    rope = jnp.pad(w[:, :, MLA_NOPE:], ((0, 0), (0, 0), (0, LANES - MLA_ROPE))).reshape(r, MLA_HEADS * LANES)
    return jnp.concatenate([nope, rope], axis=1).astype(BF16)


def _mla_kv_weight(w_kv_up):
    r = w_kv_up.shape[0]
    w = w_kv_up.reshape(r, MLA_HEADS, MLA_NOPE + MLA_V)
    nope = w[:, :, :MLA_NOPE].reshape(r, MLA_HEADS * MLA_NOPE)
    val = w[:, :, MLA_NOPE:].reshape(r, MLA_HEADS * MLA_V)
    return nope.astype(BF16), val.T.astype(BF16)


def kernel(x, p, positions, even_pre_g, even_w_in, mla_q_norm_g, mla_w_q_up, mla_kv_norm_g, mla_w_kv_up, swa_sinks, even_w_out, even_post_g, odd_pre_g, moba_w_qkv, odd_w_out, odd_post_g, mlp_pre_g, mlp_w_up, mlp_w_down, mlp_post_g, ple_w_gate, ple_b_gate, ple_w_proj):
    batch, seq, d = x.shape
    assert batch == 1
    h = x.reshape(seq, d)
    tabs_cos, tabs_sin = _rope_tables(positions, seq)

    cq, ckv, kr, qs, ks, vs = _even_proj(h, even_pre_g[0], _even_in_weight(even_w_in[0]),
                                         mla_q_norm_g[0], mla_kv_norm_g[0], tabs_cos, tabs_sin)
    qn, qr, kn, vt = _mla_up(cq, ckv, _mla_q_weight(mla_w_q_up[0]), *_mla_kv_weight(mla_w_kv_up[0]),
                             tabs_cos, tabs_sin)
    o_mla = _mla_attn_t(qn, qr, kn, kr, vt)
    o_swa = _swa_attn(qs, ks, vs, swa_sinks[0])
    h, hn = _out_proj([o_mla, o_swa], even_w_out[0].astype(BF16), h, even_post_g[0], mlp_pre_g[0])
    h = _mlp(hn, mlp_w_up[0].astype(BF16), mlp_w_down[0].astype(BF16), h, mlp_post_g[0])
    h, hn = _ple(h, p[0, 0], ple_w_gate[0].astype(BF16), ple_b_gate[0], ple_w_proj[0].astype(BF16),
                 g_next=odd_pre_g[0])

    w_qkv = moba_w_qkv[0].astype(BF16)
    (q,) = _odd_proj(hn, w_qkv, 0, tabs_cos, tabs_sin, rope=True, scale=MOBA_HD ** -0.5 * LOG2E)
    k, kmean = _odd_proj(hn, w_qkv, 1, tabs_cos, tabs_sin, rope=True, kmean=True)
    vt = _odd_proj_vt(hn, moba_w_qkv[0][:, 2 * d:].T.astype(BF16))
    o = _moba_attn(q, k, vt, kmean.reshape(seq // MOBA_BLOCK, MOBA_HEADS * MOBA_HD))
    h, hn = _out_proj([o], odd_w_out[0].astype(BF16), h, odd_post_g[0], mlp_pre_g[1])
    h = _mlp(hn, mlp_w_up[1].astype(BF16), mlp_w_down[1].astype(BF16), h, mlp_post_g[1])
    h, _ = _ple(h, p[1, 0], ple_w_gate[1].astype(BF16), ple_b_gate[1], ple_w_proj[1].astype(BF16))
    return h.reshape(batch, seq, d)
```

```python
import functools

import jax
import jax.numpy as jnp
import numpy as np
from jax import lax
from jax.experimental import pallas as pl
from jax.experimental.pallas import tpu as pltpu

F32 = jnp.float32
BF16 = jnp.bfloat16

NORM_EPS = 1e-6
ROPE_THETA = 10000.0

MLA_HEADS = 8
MLA_Q_RANK = 512
MLA_KV_RANK = 512
MLA_NOPE = 128
MLA_ROPE = 64
MLA_V = 128
SWA_HEADS = 16
SWA_KV_HEADS = 2
SWA_HD = 64
SWA_BLOCK = 128
MOBA_HEADS = 16
MOBA_HD = 128
MOBA_BLOCK = 256
MOBA_TOPK = 3

LANES = 128
V7X_VMEM_BYTES = 64 * 1024 * 1024
VMEM_LIMIT = 56 * 1024 * 1024
MASK_BIAS = -1e30
LOG2E = 1.4426950408889634
FLASH_TQ = 4096
FLASH_TK = 512
T_CHUNK = 256
T_AHEAD = 4
ONES_ROWS = 16


def _params(sem):
    return pltpu.CompilerParams(dimension_semantics=sem, vmem_limit_bytes=VMEM_LIMIT)


def _rms(t, g):
    return t * lax.rsqrt(jnp.mean(t * t, axis=-1, keepdims=True) + NORM_EPS) * g


def _rope(t, cos, sin_signed, half):
    width = t.shape[1]
    reps = width // LANES
    if reps > 1:
        cos = jnp.concatenate([cos] * reps, axis=1)
        sin_signed = jnp.concatenate([sin_signed] * reps, axis=1)
    lane = lax.broadcasted_iota(jnp.int32, t.shape, 1)
    first = (lane % (2 * half)) < half
    partner = jnp.where(first, pltpu.roll(t, width - half, 1), pltpu.roll(t, half, 1))
    return t * cos + partner * sin_signed


def _tables_kernel(pos_ref, invf_ref, sign_ref, cos_ref, sin_ref):
    ang = pos_ref[...].astype(F32) * invf_ref[...]
    cos_ref[...] = jnp.cos(ang)
    sin_ref[...] = jnp.sin(ang) * sign_ref[...]


def _rope_tables(positions, seq):
    def inv_freq(d):
        half = d // 2
        return jnp.power(ROPE_THETA, -jnp.arange(half, dtype=F32) * (2.0 / d))

    f64, f128 = inv_freq(64), inv_freq(128)
    invf = jnp.concatenate([f64, f64, f64, f64, f128, f128])[None, :]
    sign = np.concatenate([-np.ones(32), np.ones(32), -np.ones(32), np.ones(32),
                           -np.ones(64), np.ones(64)]).astype(np.float32)[None, :]
    tm = min(seq, 1024)
    return pl.pallas_call(
        _tables_kernel,
        grid=(seq // tm,),
        in_specs=[pl.BlockSpec((tm, 1), lambda i: (i, 0)),
                  pl.BlockSpec((1, 256), lambda i: (0, 0)),
                  pl.BlockSpec((1, 256), lambda i: (0, 0))],
        out_specs=[pl.BlockSpec((tm, 256), lambda i: (i, 0)),
                   pl.BlockSpec((tm, 256), lambda i: (i, 0))],
        out_shape=[jax.ShapeDtypeStruct((seq, 256), F32)] * 2,
        compiler_params=_params(("parallel",)),
        name="rope_tables",
    )(positions.reshape(seq, 1), invf, jnp.asarray(sign))


_EVEN_COLS = (0, 512, 1024, 1152, 2176, 2304, 2432)


def _even_proj_kernel(x_ref, gpre_ref, w_ref, gq_ref, gkv_ref, cos_ref, sin_ref,
                      cq_ref, ckv_ref, kr_ref, qs_ref, ks_ref, vs_ref):
    x = _rms(x_ref[...], gpre_ref[...]).astype(BF16)
    cos, sin = cos_ref[...], sin_ref[...]
    c = _EVEN_COLS

    def mm(k):
        return jnp.dot(x, w_ref[:, c[k]:c[k + 1]], preferred_element_type=F32)

    cq_ref[...] = _rms(mm(0), gq_ref[...]).astype(BF16)
    ckv_ref[...] = _rms(mm(1), gkv_ref[...]).astype(BF16)
    kr_ref[...] = _rope(mm(2), cos, sin, 32).astype(BF16)
    qs_ref[...] = (_rope(mm(3), cos, sin, 32) * (SWA_HD ** -0.5)).astype(BF16)
    ks_ref[...] = _rope(mm(4), cos, sin, 32).astype(BF16)
    vs_ref[...] = mm(5).astype(BF16)


def _even_proj(x, g_pre, w, gq, gkv, tabs_cos, tabs_sin):
    seq, d = x.shape
    tm = min(seq, 1024)
    widths = [_EVEN_COLS[k + 1] - _EVEN_COLS[k] for k in range(6)]
    row = lambda i: (i, 0)
    fixed = lambda i: (0, 0)
    return pl.pallas_call(
        _even_proj_kernel,
        grid=(seq // tm,),
        in_specs=[pl.BlockSpec((tm, d), row),
                  pl.BlockSpec((1, d), fixed),
                  pl.BlockSpec(w.shape, fixed, pipeline_mode=pl.Buffered(1)),
                  pl.BlockSpec((1, 512), fixed),
                  pl.BlockSpec((1, 512), fixed),
                  pl.BlockSpec((tm, LANES), row),
                  pl.BlockSpec((tm, LANES), row)],
        out_specs=[pl.BlockSpec((tm, n), row) for n in widths],
        out_shape=[jax.ShapeDtypeStruct((seq, n), BF16) for n in widths],
        compiler_params=_params(("parallel",)),
        name="even_proj",
    )(x, g_pre[None, :], w, gq[None, :], gkv[None, :], tabs_cos, tabs_sin)


def _mla_up_kernel(cq_ref, ckv_ref, wq_ref, wk_ref, wvt_ref, cos_ref, sin_ref,
                   qn_ref, qr_ref, kn_ref, vt_ref, *, scale):
    cq, ckv = cq_ref[...], ckv_ref[...]
    n = MLA_HEADS * LANES
    qn = jnp.dot(cq, wq_ref[:, :n], preferred_element_type=F32)
    qr = jnp.dot(cq, wq_ref[:, n:], preferred_element_type=F32)
    qn_ref[...] = (qn * scale).astype(BF16)
    qr_ref[...] = (_rope(qr, cos_ref[...], sin_ref[...], 32) * scale).astype(BF16)
    kn_ref[...] = jnp.dot(ckv, wk_ref[...], preferred_element_type=F32).astype(BF16)
    vt = lax.dot_general(wvt_ref[...], ckv, (((1,), (1,)), ((), ())), preferred_element_type=F32)
    vt_ref[...] = vt.astype(BF16).reshape(vt_ref.shape)


def _mla_up(cq, ckv, wq, wk, wvt, tabs_cos, tabs_sin):
    seq = cq.shape[0]
    tm = min(seq, FLASH_TK)
    n = MLA_HEADS * LANES
    row = lambda i: (i, 0)
    fixed = lambda i: (0, 0)
    scale = (MLA_NOPE + MLA_ROPE) ** -0.5 * LOG2E
    return pl.pallas_call(
        functools.partial(_mla_up_kernel, scale=scale),
        grid=(seq // tm,),
        in_specs=[pl.BlockSpec((tm, MLA_Q_RANK), row),
                  pl.BlockSpec((tm, MLA_KV_RANK), row),
                  pl.BlockSpec(wq.shape, fixed),
                  pl.BlockSpec(wk.shape, fixed),
                  pl.BlockSpec(wvt.shape, fixed),
                  pl.BlockSpec((tm, LANES), row),
                  pl.BlockSpec((tm, LANES), row)],
        out_specs=[pl.BlockSpec((tm, n), row)] * 3
        + [pl.BlockSpec((MLA_HEADS, None, LANES, tm), lambda i: (0, i, 0, 0))],
        out_shape=[jax.ShapeDtypeStruct((seq, n), BF16)] * 3
        + [jax.ShapeDtypeStruct((MLA_HEADS, seq // tm, LANES, tm), BF16)],
        compiler_params=_params(("parallel",)),
        name="mla_up",
    )(cq, ckv, wq, wk, wvt, tabs_cos, tabs_sin)


def _softmax_pv_t(st, vt1, m_ref, acc_ref):
    m_old = m_ref[...]
    m_new = jnp.maximum(m_old, jnp.max(st, axis=0, keepdims=True))
    alpha = jnp.exp2(m_old - m_new)
    pt = jnp.exp2(st - m_new)
    acc_ref[...] = alpha * acc_ref[...] + jnp.dot(vt1, pt.astype(BF16), preferred_element_type=F32)
    m_ref[...] = m_new


def _flash_attend_t(qt, kv, i, tq, tk, stats, o_ref):
    nc = tq // T_CHUNK
    m_refs, acc_refs = stats[:nc], stats[nc:]
    for c in range(nc):
        m_refs[c][...] = jnp.full(m_refs[c].shape, -jnp.inf, F32)
        acc_refs[c][...] = jnp.zeros(acc_refs[c].shape, F32)
    sub = tq // tk
    qts = [qt[:, c * T_CHUNK:(c + 1) * T_CHUNK] for c in range(nc)]
    key = lax.broadcasted_iota(jnp.int32, (tk, T_CHUNK), 0)
    qry = lax.broadcasted_iota(jnp.int32, (tk, T_CHUNK), 1)
    ones = jnp.ones((ONES_ROWS, tk), BF16)

    def run(work, kvs, diagonal):
        vt1 = [jnp.concatenate([vt, ones], axis=0) for _, vt in kvs]

        def score(n):
            d, c = work[n]
            st = jnp.dot(kvs[d][0], qts[c], preferred_element_type=F32)
            off = c * T_CHUNK - d * tk
            if diagonal and off < tk:
                st = jnp.where(key <= qry + off, st, -jnp.inf)
            return st

        ahead = [score(n) for n in range(min(T_AHEAD, len(work)))]
        for n, (d, c) in enumerate(work):
            if n + T_AHEAD < len(work):
                ahead.append(score(n + T_AHEAD))
            _softmax_pv_t(ahead.pop(0), vt1[d], m_refs[c], acc_refs[c])

    def past(g, carry):
        run([(d, c) for d in range(sub) for c in range(nc)], [kv(g * sub + d) for d in range(sub)], False)
        return carry

    lax.fori_loop(0, i, past, 0)

    run([(d, c) for d in range(sub) for c in range(d * tk // T_CHUNK, nc)],
        [kv(i * sub + d) for d in range(sub)], True)

    out_t = jnp.concatenate([acc_refs[c][:LANES, :] / acc_refs[c][LANES:LANES + 1, :] for c in range(nc)], axis=1)
    o_ref[...] = out_t.T.astype(BF16)


def _flash_t_scratch(tq):
    nc = tq // T_CHUNK
    return [pltpu.VMEM((1, T_CHUNK), F32)] * nc + [pltpu.VMEM((LANES + ONES_ROWS, T_CHUNK), F32)] * nc


def _mla_attn_t_kernel(qn_ref, qr_ref, kn_ref, kr_ref, vt_ref, o_ref, *stats, tq, tk):
    i = pl.program_id(1)
    q = jnp.concatenate([qn_ref[...], qr_ref[...]], axis=1)
    qt = q.astype(F32).T.astype(BF16)

    def kv(j):
        rows = pl.ds(pl.multiple_of(j * tk, tk), tk)
        return jnp.concatenate([kn_ref[rows, :], kr_ref[rows, :]], axis=1), vt_ref[j]

    _flash_attend_t(qt, kv, i, tq, tk, stats, o_ref)


def _mla_attn_t(qn, qr, kn, kr, vt):
    seq = qn.shape[0]
    tq = min(seq, FLASH_TQ)
    tk = min(seq, FLASH_TK)
    nk = seq // tk
    return pl.pallas_call(
        functools.partial(_mla_attn_t_kernel, tq=tq, tk=tk),
        grid=(MLA_HEADS, seq // tq),
        in_specs=[pl.BlockSpec((tq, LANES), lambda h, i: (i, h)),
                  pl.BlockSpec((tq, LANES), lambda h, i: (i, h)),
                  pl.BlockSpec((seq, LANES), lambda h, i: (0, h)),
                  pl.BlockSpec((seq, LANES), lambda h, i: (0, 0)),
                  pl.BlockSpec((None, nk, LANES, tk), lambda h, i: (h, 0, 0, 0))],
        out_specs=pl.BlockSpec((tq, LANES), lambda h, i: (i, h)),
        out_shape=jax.ShapeDtypeStruct((seq, MLA_HEADS * MLA_V), BF16),
        scratch_shapes=_flash_t_scratch(tq),
        compiler_params=_params(("parallel", "arbitrary")),
        name="mla_attn_t",
    )(qn, qr, kn, kr, vt)


def _swa_kernel(sink_ref, q_ref, kc_ref, kp_ref, vc_ref, vp_ref, o_ref, *, tq):
    i = pl.program_id(0)
    nb = tq // SWA_BLOCK
    group = SWA_HEADS // SWA_KV_HEADS
    pairs = group // 2
    lane = lax.broadcasted_iota(jnp.int32, (2 * SWA_BLOCK, LANES), 1)
    shape = (pairs * SWA_BLOCK, 2 * SWA_BLOCK)
    qp = lax.broadcasted_iota(jnp.int32, shape, 0) % SWA_BLOCK + SWA_BLOCK
    kp = lax.broadcasted_iota(jnp.int32, shape, 1)
    band = (kp <= qp) & (qp - kp < SWA_BLOCK)
    out_lane = lax.broadcasted_iota(jnp.int32, (pairs * SWA_BLOCK, LANES), 1)
    key_row = lax.broadcasted_iota(jnp.int32, (4 * SWA_BLOCK, LANES), 0)
    key_lane = lax.broadcasted_iota(jnp.int32, (4 * SWA_BLOCK, LANES), 1)
    ones_by_head = jnp.where((key_row < 2 * SWA_BLOCK) == (key_lane < SWA_HD), 1.0, 0.0).astype(BF16)

    def split(t, c):
        mine = jnp.where((lane >= c * SWA_HD) & (lane < (c + 1) * SWA_HD), t, 0.0)
        other = pltpu.roll(mine, SWA_HD, 1)
        lo, hi = (mine, other) if c == 0 else (other, mine)
        return jnp.concatenate([lo, hi], axis=0).astype(BF16)

    for b in range(nb):
        rows = slice(b * SWA_BLOCK, (b + 1) * SWA_BLOCK)
        if b == 0:
            k_prev, v_prev = kp_ref[...], vp_ref[...]
        else:
            prev = slice((b - 1) * SWA_BLOCK, b * SWA_BLOCK)
            k_prev, v_prev = kc_ref[prev, :], vc_ref[prev, :]
        kw = jnp.concatenate([k_prev, kc_ref[rows, :]], axis=0).astype(F32)
        vw = jnp.concatenate([v_prev, vc_ref[rows, :]], axis=0).astype(F32)
        first_key = jnp.where(i * nb + b == 0, SWA_BLOCK, 0)
        valid = band & (kp >= first_key)
        for c in range(SWA_KV_HEADS):
            kcat = split(kw, c)
            vcat = split(vw, c)
            cols = [slice((c * pairs + a) * LANES, (c * pairs + a + 1) * LANES) for a in range(pairs)]
            q4 = jnp.concatenate([q_ref[rows, cs] for cs in cols], axis=0)
            s = lax.dot_general(q4, kcat, (((1,), (1,)), ((), ())), preferred_element_type=F32)
            ps, sink_terms = [], []
            for e in range(2):
                sink = jnp.concatenate(
                    [jnp.full((SWA_BLOCK, 1), sink_ref[c * group + 2 * a + e], F32) for a in range(pairs)], axis=0)
                se = jnp.where(valid, s[:, e * 2 * SWA_BLOCK:(e + 1) * 2 * SWA_BLOCK], -jnp.inf)
                m = jnp.maximum(jnp.max(se, axis=1, keepdims=True), sink)
                ps.append(jnp.exp(se - m).astype(BF16))
                sink_terms.append(jnp.exp(sink - m))
            pv = jnp.dot(jnp.concatenate(ps, axis=1), jnp.concatenate([vcat, ones_by_head], axis=1),
                         preferred_element_type=F32)
            den = pv[:, LANES:] + jnp.where(out_lane < SWA_HD, sink_terms[0], sink_terms[1])
            o4 = (pv[:, :LANES] / den).astype(BF16)
            for a, cs in enumerate(cols):
                o_ref[rows, cs] = o4[a * SWA_BLOCK:(a + 1) * SWA_BLOCK]


def _swa_attn(qs, ks, vs, sinks):
    seq = qs.shape[0]
    tq = min(seq, 512)
    nb = tq // SWA_BLOCK
    cur = lambda i: (i, 0)
    prev = lambda i: (jnp.maximum(i * nb - 1, 0), 0)
    return pl.pallas_call(
        functools.partial(_swa_kernel, tq=tq),
        grid=(seq // tq,),
        in_specs=[pl.BlockSpec(memory_space=pltpu.SMEM),
                  pl.BlockSpec((tq, SWA_HEADS * SWA_HD), cur),
                  pl.BlockSpec((tq, LANES), cur),
                  pl.BlockSpec((SWA_BLOCK, LANES), prev),
                  pl.BlockSpec((tq, LANES), cur),
                  pl.BlockSpec((SWA_BLOCK, LANES), prev)],
        out_specs=pl.BlockSpec((tq, SWA_HEADS * SWA_HD), cur),
        out_shape=jax.ShapeDtypeStruct((seq, SWA_HEADS * SWA_HD), BF16),
        compiler_params=_params(("parallel",)),
        name="swa_attn",
    )(sinks, qs, ks, ks, vs, vs)


def _out_proj_kernel(*refs, n_a):
    a_refs = refs[:n_a]
    w_ref, h_ref, gpost_ref, gnext_ref, hout_ref, hn_ref = refs[n_a:]
    tm = h_ref.shape[0]
    for rows in [slice(r * tm // 4, (r + 1) * tm // 4) for r in range(4)]:
        m = None
        k0 = 0
        for a_ref in a_refs:
            kw = a_ref.shape[1]
            part = jnp.dot(a_ref[rows, :], w_ref[k0:k0 + kw, :], preferred_element_type=F32)
            m = part if m is None else m + part
            k0 += kw
        h = h_ref[rows, :] + _rms(m, gpost_ref[...])
        hout_ref[rows, :] = h
        hn_ref[rows, :] = _rms(h, gnext_ref[...]).astype(BF16)


def _out_proj(a_list, w, h, g_post, g_next):
    seq, d = h.shape
    tm = min(seq, 512)
    row = lambda i: (i, 0)
    fixed = lambda i: (0, 0)
    return pl.pallas_call(
        functools.partial(_out_proj_kernel, n_a=len(a_list)),
        grid=(seq // tm,),
        in_specs=[pl.BlockSpec((tm, a.shape[1]), row) for a in a_list]
        + [pl.BlockSpec(w.shape, fixed), pl.BlockSpec((tm, d), row),
           pl.BlockSpec((1, d), fixed), pl.BlockSpec((1, d), fixed)],
        out_specs=[pl.BlockSpec((tm, d), row), pl.BlockSpec((tm, d), row)],
        out_shape=[jax.ShapeDtypeStruct((seq, d), F32), jax.ShapeDtypeStruct((seq, d), BF16)],
        compiler_params=_params(("parallel",)),
        name="out_proj",
    )(*a_list, w, h, g_post[None, :], g_next[None, :])


def _mlp_kernel(x_ref, wu_ref, wd_ref, h_ref, g_ref, o_ref, acc_ref):
    f = pl.program_id(1)

    @pl.when(f == 0)
    def _():
        acc_ref[...] = jnp.zeros(acc_ref.shape, F32)

    a = jnp.maximum(jnp.dot(x_ref[...], wu_ref[...], preferred_element_type=F32), 0.0)
    acc_ref[...] += jnp.dot((a * a).astype(BF16), wd_ref[...], preferred_element_type=F32)

    @pl.when(f == pl.num_programs(1) - 1)
    def _():
        o_ref[...] = h_ref[...] + _rms(acc_ref[...], g_ref[...])


def _mlp(xn, w_up, w_down, h, g_post):
    seq, d = h.shape
    d_ff = w_up.shape[1]
    tm = min(seq, 512)
    tf = min(d_ff, 1024)
    return pl.pallas_call(
        _mlp_kernel,
        grid=(seq // tm, d_ff // tf),
        in_specs=[pl.BlockSpec((tm, d), lambda i, f: (i, 0)),
                  pl.BlockSpec((d, tf), lambda i, f: (0, f)),
                  pl.BlockSpec((tf, d), lambda i, f: (f, 0)),
                  pl.BlockSpec((tm, d), lambda i, f: (i, 0)),
                  pl.BlockSpec((1, d), lambda i, f: (0, 0))],
        out_specs=pl.BlockSpec((tm, d), lambda i, f: (i, 0)),
        out_shape=jax.ShapeDtypeStruct((seq, d), F32),
        scratch_shapes=[pltpu.VMEM((tm, d), F32)],
        compiler_params=_params(("parallel", "arbitrary")),
        name="mlp",
    )(xn, w_up, w_down, h, g_post[None, :])


def _ple_kernel(h_ref, p_ref, wg_ref, b_ref, wp_ref, *rest, with_next):
    h = h_ref[...]
    z = jnp.dot(h.astype(BF16), wg_ref[...], preferred_element_type=F32) + b_ref[...]
    gate = 1.0 / (1.0 + jnp.exp(-z))
    e = jnp.dot(p_ref[...].astype(BF16), wp_ref[...], preferred_element_type=F32)
    out = h + gate * e
    if with_next:
        gnext_ref, o_ref, hn_ref = rest
        hn_ref[...] = _rms(out, gnext_ref[...]).astype(BF16)
    else:
        (o_ref,) = rest
    o_ref[...] = out


def _ple(h, p, w_gate, b_gate, w_proj, g_next=None):
    seq, d = h.shape
    tm = min(seq, 512)
    row = lambda i: (i, 0)
    fixed = lambda i: (0, 0)
    with_next = g_next is not None
    in_specs = [pl.BlockSpec((tm, d), row), pl.BlockSpec((tm, p.shape[1]), row),
                pl.BlockSpec(w_gate.shape, fixed), pl.BlockSpec((1, d), fixed),
                pl.BlockSpec(w_proj.shape, fixed)]
    args = [h, p, w_gate, b_gate[None, :], w_proj]
    out_specs = [pl.BlockSpec((tm, d), row)]
    out_shape = [jax.ShapeDtypeStruct((seq, d), F32)]
    if with_next:
        in_specs.append(pl.BlockSpec((1, d), fixed))
        args.append(g_next[None, :])
        out_specs.append(pl.BlockSpec((tm, d), row))
        out_shape.append(jax.ShapeDtypeStruct((seq, d), BF16))
    res = pl.pallas_call(
        functools.partial(_ple_kernel, with_next=with_next),
        grid=(seq // tm,),
        in_specs=in_specs, out_specs=out_specs, out_shape=out_shape,
        compiler_params=_params(("parallel",)),
        name="ple",
    )(*args)
    return res if with_next else (res[0], None)


def _odd_proj_kernel(x_ref, w_ref, cos_ref, sin_ref, o_ref, *rest, rope, scale, kmean, tm):
    t = jnp.dot(x_ref[...], w_ref[...], preferred_element_type=F32)
    if rope:
        t = _rope(t, cos_ref[...], sin_ref[...], MOBA_HD // 2)
    if kmean:
        (km_ref,) = rest
        for b in range(tm // MOBA_BLOCK):
            blk = t[b * MOBA_BLOCK:(b + 1) * MOBA_BLOCK, :]
            km_ref[b] = jnp.sum(blk, axis=0, keepdims=True) * (1.0 / MOBA_BLOCK)
    if scale != 1.0:
        t = t * scale
    o_ref[...] = t.astype(BF16)


def _odd_proj(xn, w, col, tabs_cos, tabs_sin, *, rope, scale=1.0, kmean=False):
    seq, d = xn.shape
    n = d
    tm = min(seq, 1024)
    row = lambda i: (i, 0)
    out_specs = [pl.BlockSpec((tm, n), row)]
    out_shape = [jax.ShapeDtypeStruct((seq, n), BF16)]
    if kmean:
        nb = tm // MOBA_BLOCK
        out_specs.append(pl.BlockSpec((nb, 1, n), lambda i: (i, 0, 0)))
        out_shape.append(jax.ShapeDtypeStruct((seq // MOBA_BLOCK, 1, n), F32))
    res = pl.pallas_call(
        functools.partial(_odd_proj_kernel, rope=rope, scale=scale, kmean=kmean, tm=tm),
        grid=(seq // tm,),
        in_specs=[pl.BlockSpec((tm, d), row),
                  pl.BlockSpec((d, n), lambda i: (0, col), pipeline_mode=pl.Buffered(1)),
                  pl.BlockSpec((tm, LANES), lambda i: (i, 1)),
                  pl.BlockSpec((tm, LANES), lambda i: (i, 1))],
        out_specs=out_specs, out_shape=out_shape,
        compiler_params=_params(("parallel",)),
        name="odd_proj",
    )(xn, w, tabs_cos, tabs_sin)
    return res


def _odd_vt_kernel(x_ref, wt_ref, vt_ref):
    vt = lax.dot_general(wt_ref[...], x_ref[...], (((1,), (1,)), ((), ())), preferred_element_type=F32)
    vt_ref[...] = vt.astype(BF16).reshape(vt_ref.shape)


def _odd_proj_vt(xn, wt):
    seq, d = xn.shape
    tm = min(seq, FLASH_TK)
    heads = wt.shape[0] // LANES
    return pl.pallas_call(
        _odd_vt_kernel,
        grid=(seq // tm,),
        in_specs=[pl.BlockSpec((tm, d), lambda i: (i, 0)), pl.BlockSpec(wt.shape, lambda i: (0, 0))],
        out_specs=pl.BlockSpec((heads, None, LANES, tm), lambda i: (0, i, 0, 0)),
        out_shape=jax.ShapeDtypeStruct((heads, seq // tm, LANES, tm), BF16),
        compiler_params=_params(("parallel",)),
        name="odd_proj_vt",
    )(xn, wt)


def _moba_kernel(q_ref, k_ref, vt_ref, km_ref, o_ref, *stats, tq, tk, nkb):
    i = pl.program_id(1)
    q = q_ref[...]

    km = km_ref[...]
    if nkb < 64:
        km = jnp.concatenate([km, jnp.zeros((64 - nkb, LANES), F32)], axis=0)
    km_hi = km.astype(BF16)
    km_mid = (km - km_hi.astype(F32)).astype(BF16)
    gate = lax.dot_general(jnp.concatenate([km_hi, km_mid], axis=1), jnp.concatenate([q, q], axis=1),
                           (((1,), (1,)), ((), ())), preferred_element_type=F32)

    blk = lax.broadcasted_iota(jnp.int32, (64, tq), 0)
    qblk = (i * tq + lax.broadcasted_iota(jnp.int32, (64, tq), 1)) // MOBA_BLOCK
    gm = jnp.where(blk < qblk, gate, -jnp.inf)
    sel = blk == qblk
    for r in range(MOBA_TOPK):
        mx = jnp.max(gm, axis=0, keepdims=True)
        idx = jnp.min(jnp.where(gm == mx, blk, 64), axis=0, keepdims=True)
        pick = (blk == idx) & (qblk > r)
        sel = sel | pick
        gm = jnp.where(pick, -jnp.inf, gm)
    bias_t = jnp.where(sel, 0.0, MASK_BIAS)
    qat = jnp.concatenate([q.astype(F32).T, bias_t, jnp.zeros((64, tq), F32)], axis=0).astype(BF16)

    kb_per_tile = tk // MOBA_BLOCK
    krow = lax.broadcasted_iota(jnp.int32, (tk, LANES), 0) // MOBA_BLOCK
    klane = lax.broadcasted_iota(jnp.int32, (tk, LANES), 1)

    def kv(j):
        rows = pl.ds(pl.multiple_of(j * tk, tk), tk)
        onehot = jnp.where(klane == krow + j * kb_per_tile, 1.0, 0.0).astype(BF16)
        return jnp.concatenate([k_ref[rows, :], onehot], axis=1), vt_ref[j]

    _flash_attend_t(qat, kv, i, tq, tk, stats, o_ref)


def _moba_attn(q, k, vt, kmean):
    seq = q.shape[0]
    nkb = seq // MOBA_BLOCK
    assert nkb <= 64 and nkb % 8 == 0, "the gate matmul stacks blocks on 64 sublanes"
    tq = min(seq, FLASH_TQ)
    tk = min(seq, FLASH_TK)
    nk = seq // tk
    return pl.pallas_call(
        functools.partial(_moba_kernel, tq=tq, tk=tk, nkb=nkb),
        grid=(MOBA_HEADS, seq // tq),
        in_specs=[pl.BlockSpec((tq, LANES), lambda h, i: (i, h)),
                  pl.BlockSpec((seq, LANES), lambda h, i: (0, h)),
                  pl.BlockSpec((None, nk, LANES, tk), lambda h, i: (h, 0, 0, 0)),
                  pl.BlockSpec((nkb, LANES), lambda h, i: (0, h))],
        out_specs=pl.BlockSpec((tq, LANES), lambda h, i: (i, h)),
        out_shape=jax.ShapeDtypeStruct((seq, MOBA_HEADS * MOBA_HD), BF16),
        scratch_shapes=_flash_t_scratch(tq),
        compiler_params=_params(("parallel", "arbitrary")),
        name="moba_attn",
    )(q, k, vt, kmean)


def _even_in_weight(w_in):
    d = w_in.shape[0]
    cq, ckv, kr, qs, ks, vs = jnp.split(w_in, [512, 1024, 1088, 2112, 2240], axis=1)
    return jnp.concatenate([cq, ckv, kr, jnp.zeros((d, 64), w_in.dtype), qs, ks, vs], axis=1).astype(BF16)


def _mla_q_weight(w_q_up):
    r = w_q_up.shape[0]
    w = w_q_up.reshape(r, MLA_HEADS, MLA_NOPE + MLA_ROPE)
    nope = w[:, :, :MLA_NOPE].reshape(r, MLA_HEADS * MLA_NOPE)
    rope = jnp.pad(w[:, :, MLA_NOPE:], ((0, 0), (0, 0), (0, LANES - MLA_ROPE))).reshape(r, MLA_HEADS * LANES)
    return jnp.concatenate([nope, rope], axis=1).astype(BF16)


def _mla_kv_weight(w_kv_up):
    r = w_kv_up.shape[0]
    w = w_kv_up.reshape(r, MLA_HEADS, MLA_NOPE + MLA_V)
    nope = w[:, :, :MLA_NOPE].reshape(r, MLA_HEADS * MLA_NOPE)
    val = w[:, :, MLA_NOPE:].reshape(r, MLA_HEADS * MLA_V)
    return nope.astype(BF16), val.T.astype(BF16)


def kernel(x, p, positions, even_pre_g, even_w_in, mla_q_norm_g, mla_w_q_up, mla_kv_norm_g, mla_w_kv_up, swa_sinks, even_w_out, even_post_g, odd_pre_g, moba_w_qkv, odd_w_out, odd_post_g, mlp_pre_g, mlp_w_up, mlp_w_down, mlp_post_g, ple_w_gate, ple_b_gate, ple_w_proj):
    batch, seq, d = x.shape
    assert batch == 1
    h = x.reshape(seq, d)
    tabs_cos, tabs_sin = _rope_tables(positions, seq)

    cq, ckv, kr, qs, ks, vs = _even_proj(h, even_pre_g[0], _even_in_weight(even_w_in[0]),
                                         mla_q_norm_g[0], mla_kv_norm_g[0], tabs_cos, tabs_sin)
    qn, qr, kn, vt = _mla_up(cq, ckv, _mla_q_weight(mla_w_q_up[0]), *_mla_kv_weight(mla_w_kv_up[0]),
                             tabs_cos, tabs_sin)
    o_mla = _mla_attn_t(qn, qr, kn, kr, vt)
    o_swa = _swa_attn(qs, ks, vs, swa_sinks[0])
    h, hn = _out_proj([o_mla, o_swa], even_w_out[0].astype(BF16), h, even_post_g[0], mlp_pre_g[0])
    h = _mlp(hn, mlp_w_up[0].astype(BF16), mlp_w_down[0].astype(BF16), h, mlp_post_g[0])
    h, hn = _ple(h, p[0, 0], ple_w_gate[0].astype(BF16), ple_b_gate[0], ple_w_proj[0].astype(BF16),
                 g_next=odd_pre_g[0])

    w_qkv = moba_w_qkv[0].astype(BF16)
    (q,) = _odd_proj(hn, w_qkv, 0, tabs_cos, tabs_sin, rope=True, scale=MOBA_HD ** -0.5 * LOG2E)
    k, kmean = _odd_proj(hn, w_qkv, 1, tabs_cos, tabs_sin, rope=True, kmean=True)
    vt = _odd_proj_vt(hn, moba_w_qkv[0][:, 2 * d:].T.astype(BF16))
    o = _moba_attn(q, k, vt, kmean.reshape(seq // MOBA_BLOCK, MOBA_HEADS * MOBA_HD))
    h, hn = _out_proj([o], odd_w_out[0].astype(BF16), h, odd_post_g[0], mlp_pre_g[1])
    h = _mlp(hn, mlp_w_up[1].astype(BF16), mlp_w_down[1].astype(BF16), h, mlp_post_g[1])
    h, _ = _ple(h, p[1, 0], ple_w_gate[1].astype(BF16), ple_b_gate[1], ple_w_proj[1].astype(BF16))
    return h.reshape(batch, seq, d)
```

```python
import functools

import jax
import jax.numpy as jnp
import numpy as np
from jax import lax
from jax.experimental import pallas as pl
from jax.experimental.pallas import tpu as pltpu

F32 = jnp.float32
BF16 = jnp.bfloat16

NORM_EPS = 1e-6
ROPE_THETA = 10000.0

MLA_HEADS = 8
MLA_Q_RANK = 512
MLA_KV_RANK = 512
MLA_NOPE = 128
MLA_ROPE = 64
MLA_V = 128
SWA_HEADS = 16
SWA_KV_HEADS = 2
SWA_HD = 64
SWA_BLOCK = 128
MOBA_HEADS = 16
MOBA_HD = 128
MOBA_BLOCK = 256
MOBA_TOPK = 3

LANES = 128
V7X_VMEM_BYTES = 64 * 1024 * 1024
VMEM_LIMIT = 56 * 1024 * 1024
MASK_BIAS = -1e30
LOG2E = 1.4426950408889634
FLASH_TQ = 4096
FLASH_TK = 512
T_CHUNK = 256
T_AHEAD = 6
ONES_ROWS = 16


def _params(sem):
    return pltpu.CompilerParams(dimension_semantics=sem, vmem_limit_bytes=VMEM_LIMIT)


def _rms(t, g):
    return t * lax.rsqrt(jnp.mean(t * t, axis=-1, keepdims=True) + NORM_EPS) * g


def _rope(t, cos, sin_signed, half):
    width = t.shape[1]
    reps = width // LANES
    if reps > 1:
        cos = jnp.concatenate([cos] * reps, axis=1)
        sin_signed = jnp.concatenate([sin_signed] * reps, axis=1)
    lane = lax.broadcasted_iota(jnp.int32, t.shape, 1)
    first = (lane % (2 * half)) < half
    partner = jnp.where(first, pltpu.roll(t, width - half, 1), pltpu.roll(t, half, 1))
    return t * cos + partner * sin_signed


def _tables_kernel(pos_ref, invf_ref, sign_ref, cos_ref, sin_ref):
    ang = pos_ref[...].astype(F32) * invf_ref[...]
    cos_ref[...] = jnp.cos(ang)
    sin_ref[...] = jnp.sin(ang) * sign_ref[...]


def _rope_tables(positions, seq):
    def inv_freq(d):
        half = d // 2
        return jnp.power(ROPE_THETA, -jnp.arange(half, dtype=F32) * (2.0 / d))

    f64, f128 = inv_freq(64), inv_freq(128)
    invf = jnp.concatenate([f64, f64, f64, f64, f128, f128])[None, :]
    sign = np.concatenate([-np.ones(32), np.ones(32), -np.ones(32), np.ones(32),
                           -np.ones(64), np.ones(64)]).astype(np.float32)[None, :]
    tm = min(seq, 1024)
    return pl.pallas_call(
        _tables_kernel,
        grid=(seq // tm,),
        in_specs=[pl.BlockSpec((tm, 1), lambda i: (i, 0)),
                  pl.BlockSpec((1, 256), lambda i: (0, 0)),
                  pl.BlockSpec((1, 256), lambda i: (0, 0))],
        out_specs=[pl.BlockSpec((tm, 256), lambda i: (i, 0)),
                   pl.BlockSpec((tm, 256), lambda i: (i, 0))],
        out_shape=[jax.ShapeDtypeStruct((seq, 256), F32)] * 2,
        compiler_params=_params(("parallel",)),
        name="rope_tables",
    )(positions.reshape(seq, 1), invf, jnp.asarray(sign))


_EVEN_COLS = (0, 512, 1024, 1152, 2176, 2304, 2432)


def _even_proj_kernel(x_ref, gpre_ref, w_ref, gq_ref, gkv_ref, cos_ref, sin_ref,
                      cq_ref, ckv_ref, kr_ref, qs_ref, ks_ref, vs_ref):
    x = _rms(x_ref[...], gpre_ref[...]).astype(BF16)
    cos, sin = cos_ref[...], sin_ref[...]
    c = _EVEN_COLS

    def mm(k):
        return jnp.dot(x, w_ref[:, c[k]:c[k + 1]], preferred_element_type=F32)

    cq_ref[...] = _rms(mm(0), gq_ref[...]).astype(BF16)
    ckv_ref[...] = _rms(mm(1), gkv_ref[...]).astype(BF16)
    kr_ref[...] = _rope(mm(2), cos, sin, 32).astype(BF16)
    qs_ref[...] = (_rope(mm(3), cos, sin, 32) * (SWA_HD ** -0.5)).astype(BF16)
    ks_ref[...] = _rope(mm(4), cos, sin, 32).astype(BF16)
    vs_ref[...] = mm(5).astype(BF16)


def _even_proj(x, g_pre, w, gq, gkv, tabs_cos, tabs_sin):
    seq, d = x.shape
    tm = min(seq, 1024)
    widths = [_EVEN_COLS[k + 1] - _EVEN_COLS[k] for k in range(6)]
    row = lambda i: (i, 0)
    fixed = lambda i: (0, 0)
    return pl.pallas_call(
        _even_proj_kernel,
        grid=(seq // tm,),
        in_specs=[pl.BlockSpec((tm, d), row),
                  pl.BlockSpec((1, d), fixed),
                  pl.BlockSpec(w.shape, fixed, pipeline_mode=pl.Buffered(1)),
                  pl.BlockSpec((1, 512), fixed),
                  pl.BlockSpec((1, 512), fixed),
                  pl.BlockSpec((tm, LANES), row),
                  pl.BlockSpec((tm, LANES), row)],
        out_specs=[pl.BlockSpec((tm, n), row) for n in widths],
        out_shape=[jax.ShapeDtypeStruct((seq, n), BF16) for n in widths],
        compiler_params=_params(("parallel",)),
        name="even_proj",
    )(x, g_pre[None, :], w, gq[None, :], gkv[None, :], tabs_cos, tabs_sin)


def _mla_up_kernel(cq_ref, ckv_ref, wq_ref, wk_ref, wvt_ref, cos_ref, sin_ref,
                   qn_ref, qr_ref, kn_ref, vt_ref, *, scale):
    cq, ckv = cq_ref[...], ckv_ref[...]
    n = MLA_HEADS * LANES
    qn = jnp.dot(cq, wq_ref[:, :n], preferred_element_type=F32)
    qr = jnp.dot(cq, wq_ref[:, n:], preferred_element_type=F32)
    qn_ref[...] = (qn * scale).astype(BF16)
    qr_ref[...] = (_rope(qr, cos_ref[...], sin_ref[...], 32) * scale).astype(BF16)
    kn_ref[...] = jnp.dot(ckv, wk_ref[...], preferred_element_type=F32).astype(BF16)
    vt = lax.dot_general(wvt_ref[...], ckv, (((1,), (1,)), ((), ())), preferred_element_type=F32)
    vt_ref[...] = vt.astype(BF16).reshape(vt_ref.shape)


def _mla_up(cq, ckv, wq, wk, wvt, tabs_cos, tabs_sin):
    seq = cq.shape[0]
    tm = min(seq, FLASH_TK)
    n = MLA_HEADS * LANES
    row = lambda i: (i, 0)
    fixed = lambda i: (0, 0)
    scale = (MLA_NOPE + MLA_ROPE) ** -0.5 * LOG2E
    return pl.pallas_call(
        functools.partial(_mla_up_kernel, scale=scale),
        grid=(seq // tm,),
        in_specs=[pl.BlockSpec((tm, MLA_Q_RANK), row),
                  pl.BlockSpec((tm, MLA_KV_RANK), row),
                  pl.BlockSpec(wq.shape, fixed),
                  pl.BlockSpec(wk.shape, fixed),
                  pl.BlockSpec(wvt.shape, fixed),
                  pl.BlockSpec((tm, LANES), row),
                  pl.BlockSpec((tm, LANES), row)],
        out_specs=[pl.BlockSpec((tm, n), row)] * 3
        + [pl.BlockSpec((MLA_HEADS, None, LANES, tm), lambda i: (0, i, 0, 0))],
        out_shape=[jax.ShapeDtypeStruct((seq, n), BF16)] * 3
        + [jax.ShapeDtypeStruct((MLA_HEADS, seq // tm, LANES, tm), BF16)],
        compiler_params=_params(("parallel",)),
        name="mla_up",
    )(cq, ckv, wq, wk, wvt, tabs_cos, tabs_sin)


def _softmax_pv_t(st, vt1, m_ref, acc_ref):
    m_old = m_ref[...]
    m_new = jnp.maximum(m_old, jnp.max(st, axis=0, keepdims=True))
    alpha = jnp.exp2(m_old - m_new)
    pt = jnp.exp2(st - m_new)
    acc_ref[...] = alpha * acc_ref[...] + jnp.dot(vt1, pt.astype(BF16), preferred_element_type=F32)
    m_ref[...] = m_new


def _flash_attend_t(qt, kv, i, tq, tk, stats, o_ref):
    nc = tq // T_CHUNK
    m_refs, acc_refs = stats[:nc], stats[nc:]
    for c in range(nc):
        m_refs[c][...] = jnp.full(m_refs[c].shape, -jnp.inf, F32)
        acc_refs[c][...] = jnp.zeros(acc_refs[c].shape, F32)
    sub = tq // tk
    qts = [qt[:, c * T_CHUNK:(c + 1) * T_CHUNK] for c in range(nc)]
    key = lax.broadcasted_iota(jnp.int32, (tk, T_CHUNK), 0)
    qry = lax.broadcasted_iota(jnp.int32, (tk, T_CHUNK), 1)
    ones = jnp.ones((ONES_ROWS, tk), BF16)

    def run(work, kvs, diagonal):
        vt1 = [jnp.concatenate([vt, ones], axis=0) for _, vt in kvs]

        def score(n):
            d, c = work[n]
            st = jnp.dot(kvs[d][0], qts[c], preferred_element_type=F32)
            off = c * T_CHUNK - d * tk
            if diagonal and off < tk:
                st = jnp.where(key <= qry + off, st, -jnp.inf)
            return st

        ahead = [score(n) for n in range(min(T_AHEAD, len(work)))]
        for n, (d, c) in enumerate(work):
            if n + T_AHEAD < len(work):
                ahead.append(score(n + T_AHEAD))
            _softmax_pv_t(ahead.pop(0), vt1[d], m_refs[c], acc_refs[c])

    def past(g, carry):
        run([(d, c) for d in range(sub) for c in range(nc)], [kv(g * sub + d) for d in range(sub)], False)
        return carry

    lax.fori_loop(0, i, past, 0)

    run([(d, c) for d in range(sub) for c in range(d * tk // T_CHUNK, nc)],
        [kv(i * sub + d) for d in range(sub)], True)

    out_t = jnp.concatenate([acc_refs[c][:LANES, :] / acc_refs[c][LANES:LANES + 1, :] for c in range(nc)], axis=1)
    o_ref[...] = out_t.T.astype(BF16)


def _flash_t_scratch(tq):
    nc = tq // T_CHUNK
    return [pltpu.VMEM((1, T_CHUNK), F32)] * nc + [pltpu.VMEM((LANES + ONES_ROWS, T_CHUNK), F32)] * nc


def _mla_attn_t_kernel(qn_ref, qr_ref, kn_ref, kr_ref, vt_ref, o_ref, *stats, tq, tk):
    i = pl.program_id(1)
    q = jnp.concatenate([qn_ref[...], qr_ref[...]], axis=1)
    qt = q.astype(F32).T.astype(BF16)

    def kv(j):
        rows = pl.ds(pl.multiple_of(j * tk, tk), tk)
        return jnp.concatenate([kn_ref[rows, :], kr_ref[rows, :]], axis=1), vt_ref[j]

    _flash_attend_t(qt, kv, i, tq, tk, stats, o_ref)


def _mla_attn_t(qn, qr, kn, kr, vt):
    seq = qn.shape[0]
    tq = min(seq, FLASH_TQ)
    tk = min(seq, FLASH_TK)
    nk = seq // tk
    return pl.pallas_call(
        functools.partial(_mla_attn_t_kernel, tq=tq, tk=tk),
        grid=(MLA_HEADS, seq // tq),
        in_specs=[pl.BlockSpec((tq, LANES), lambda h, i: (i, h)),
                  pl.BlockSpec((tq, LANES), lambda h, i: (i, h)),
                  pl.BlockSpec((seq, LANES), lambda h, i: (0, h)),
                  pl.BlockSpec((seq, LANES), lambda h, i: (0, 0)),
                  pl.BlockSpec((None, nk, LANES, tk), lambda h, i: (h, 0, 0, 0))],
        out_specs=pl.BlockSpec((tq, LANES), lambda h, i: (i, h)),
        out_shape=jax.ShapeDtypeStruct((seq, MLA_HEADS * MLA_V), BF16),
        scratch_shapes=_flash_t_scratch(tq),
        compiler_params=_params(("parallel", "arbitrary")),
        name="mla_attn_t",
    )(qn, qr, kn, kr, vt)


def _swa_kernel(sink_ref, q_ref, kc_ref, kp_ref, vc_ref, vp_ref, o_ref, *, tq):
    i = pl.program_id(0)
    nb = tq // SWA_BLOCK
    group = SWA_HEADS // SWA_KV_HEADS
    pairs = group // 2
    lane = lax.broadcasted_iota(jnp.int32, (2 * SWA_BLOCK, LANES), 1)
    shape = (pairs * SWA_BLOCK, 2 * SWA_BLOCK)
    qp = lax.broadcasted_iota(jnp.int32, shape, 0) % SWA_BLOCK + SWA_BLOCK
    kp = lax.broadcasted_iota(jnp.int32, shape, 1)
    band = (kp <= qp) & (qp - kp < SWA_BLOCK)
    out_lane = lax.broadcasted_iota(jnp.int32, (pairs * SWA_BLOCK, LANES), 1)
    key_row = lax.broadcasted_iota(jnp.int32, (4 * SWA_BLOCK, LANES), 0)
    key_lane = lax.broadcasted_iota(jnp.int32, (4 * SWA_BLOCK, LANES), 1)
    ones_by_head = jnp.where((key_row < 2 * SWA_BLOCK) == (key_lane < SWA_HD), 1.0, 0.0).astype(BF16)

    def split(t, c):
        mine = jnp.where((lane >= c * SWA_HD) & (lane < (c + 1) * SWA_HD), t, 0.0)
        other = pltpu.roll(mine, SWA_HD, 1)
        lo, hi = (mine, other) if c == 0 else (other, mine)
        return jnp.concatenate([lo, hi], axis=0).astype(BF16)

    for b in range(nb):
        rows = slice(b * SWA_BLOCK, (b + 1) * SWA_BLOCK)
        if b == 0:
            k_prev, v_prev = kp_ref[...], vp_ref[...]
        else:
            prev = slice((b - 1) * SWA_BLOCK, b * SWA_BLOCK)
            k_prev, v_prev = kc_ref[prev, :], vc_ref[prev, :]
        kw = jnp.concatenate([k_prev, kc_ref[rows, :]], axis=0).astype(F32)
        vw = jnp.concatenate([v_prev, vc_ref[rows, :]], axis=0).astype(F32)
        first_key = jnp.where(i * nb + b == 0, SWA_BLOCK, 0)
        valid = band & (kp >= first_key)
        for c in range(SWA_KV_HEADS):
            kcat = split(kw, c)
            vcat = split(vw, c)
            cols = [slice((c * pairs + a) * LANES, (c * pairs + a + 1) * LANES) for a in range(pairs)]
            q4 = jnp.concatenate([q_ref[rows, cs] for cs in cols], axis=0)
            s = lax.dot_general(q4, kcat, (((1,), (1,)), ((), ())), preferred_element_type=F32)
            ps, sink_terms = [], []
            for e in range(2):
                sink = jnp.concatenate(
                    [jnp.full((SWA_BLOCK, 1), sink_ref[c * group + 2 * a + e], F32) for a in range(pairs)], axis=0)
                se = jnp.where(valid, s[:, e * 2 * SWA_BLOCK:(e + 1) * 2 * SWA_BLOCK], -jnp.inf)
                m = jnp.maximum(jnp.max(se, axis=1, keepdims=True), sink)
                ps.append(jnp.exp(se - m).astype(BF16))
                sink_terms.append(jnp.exp(sink - m))
            pv = jnp.dot(jnp.concatenate(ps, axis=1), jnp.concatenate([vcat, ones_by_head], axis=1),
                         preferred_element_type=F32)
            den = pv[:, LANES:] + jnp.where(out_lane < SWA_HD, sink_terms[0], sink_terms[1])
            o4 = (pv[:, :LANES] / den).astype(BF16)
            for a, cs in enumerate(cols):
                o_ref[rows, cs] = o4[a * SWA_BLOCK:(a + 1) * SWA_BLOCK]


def _swa_attn(qs, ks, vs, sinks):
    seq = qs.shape[0]
    tq = min(seq, 512)
    nb = tq // SWA_BLOCK
    cur = lambda i: (i, 0)
    prev = lambda i: (jnp.maximum(i * nb - 1, 0), 0)
    return pl.pallas_call(
        functools.partial(_swa_kernel, tq=tq),
        grid=(seq // tq,),
        in_specs=[pl.BlockSpec(memory_space=pltpu.SMEM),
                  pl.BlockSpec((tq, SWA_HEADS * SWA_HD), cur),
                  pl.BlockSpec((tq, LANES), cur),
                  pl.BlockSpec((SWA_BLOCK, LANES), prev),
                  pl.BlockSpec((tq, LANES), cur),
                  pl.BlockSpec((SWA_BLOCK, LANES), prev)],
        out_specs=pl.BlockSpec((tq, SWA_HEADS * SWA_HD), cur),
        out_shape=jax.ShapeDtypeStruct((seq, SWA_HEADS * SWA_HD), BF16),
        compiler_params=_params(("parallel",)),
        name="swa_attn",
    )(sinks, qs, ks, ks, vs, vs)


def _out_proj_kernel(*refs, n_a):
    a_refs = refs[:n_a]
    w_ref, h_ref, gpost_ref, gnext_ref, hout_ref, hn_ref = refs[n_a:]
    tm = h_ref.shape[0]
    for rows in [slice(r * tm // 4, (r + 1) * tm // 4) for r in range(4)]:
        m = None
        k0 = 0
        for a_ref in a_refs:
            kw = a_ref.shape[1]
            part = jnp.dot(a_ref[rows, :], w_ref[k0:k0 + kw, :], preferred_element_type=F32)
            m = part if m is None else m + part
            k0 += kw
        h = h_ref[rows, :] + _rms(m, gpost_ref[...])
        hout_ref[rows, :] = h
        hn_ref[rows, :] = _rms(h, gnext_ref[...]).astype(BF16)


def _out_proj(a_list, w, h, g_post, g_next):
    seq, d = h.shape
    tm = min(seq, 512)
    row = lambda i: (i, 0)
    fixed = lambda i: (0, 0)
    return pl.pallas_call(
        functools.partial(_out_proj_kernel, n_a=len(a_list)),
        grid=(seq // tm,),
        in_specs=[pl.BlockSpec((tm, a.shape[1]), row) for a in a_list]
        + [pl.BlockSpec(w.shape, fixed), pl.BlockSpec((tm, d), row),
           pl.BlockSpec((1, d), fixed), pl.BlockSpec((1, d), fixed)],
        out_specs=[pl.BlockSpec((tm, d), row), pl.BlockSpec((tm, d), row)],
        out_shape=[jax.ShapeDtypeStruct((seq, d), F32), jax.ShapeDtypeStruct((seq, d), BF16)],
        compiler_params=_params(("parallel",)),
        name="out_proj",
    )(*a_list, w, h, g_post[None, :], g_next[None, :])


def _mlp_kernel(x_ref, wu_ref, wd_ref, h_ref, g_ref, o_ref, acc_ref):
    f = pl.program_id(1)

    @pl.when(f == 0)
    def _():
        acc_ref[...] = jnp.zeros(acc_ref.shape, F32)

    a = jnp.maximum(jnp.dot(x_ref[...], wu_ref[...], preferred_element_type=F32), 0.0)
    acc_ref[...] += jnp.dot((a * a).astype(BF16), wd_ref[...], preferred_element_type=F32)

    @pl.when(f == pl.num_programs(1) - 1)
    def _():
        o_ref[...] = h_ref[...] + _rms(acc_ref[...], g_ref[...])


def _mlp(xn, w_up, w_down, h, g_post):
    seq, d = h.shape
    d_ff = w_up.shape[1]
    tm = min(seq, 512)
    tf = min(d_ff, 1024)
    return pl.pallas_call(
        _mlp_kernel,
        grid=(seq // tm, d_ff // tf),
        in_specs=[pl.BlockSpec((tm, d), lambda i, f: (i, 0)),
                  pl.BlockSpec((d, tf), lambda i, f: (0, f)),
                  pl.BlockSpec((tf, d), lambda i, f: (f, 0)),
                  pl.BlockSpec((tm, d), lambda i, f: (i, 0)),
                  pl.BlockSpec((1, d), lambda i, f: (0, 0))],
        out_specs=pl.BlockSpec((tm, d), lambda i, f: (i, 0)),
        out_shape=jax.ShapeDtypeStruct((seq, d), F32),
        scratch_shapes=[pltpu.VMEM((tm, d), F32)],
        compiler_params=_params(("parallel", "arbitrary")),
        name="mlp",
    )(xn, w_up, w_down, h, g_post[None, :])


def _ple_kernel(h_ref, p_ref, wg_ref, b_ref, wp_ref, *rest, with_next):
    h = h_ref[...]
    z = jnp.dot(h.astype(BF16), wg_ref[...], preferred_element_type=F32) + b_ref[...]
    gate = 1.0 / (1.0 + jnp.exp(-z))
    e = jnp.dot(p_ref[...].astype(BF16), wp_ref[...], preferred_element_type=F32)
    out = h + gate * e
    if with_next:
        gnext_ref, o_ref, hn_ref = rest
        hn_ref[...] = _rms(out, gnext_ref[...]).astype(BF16)
    else:
        (o_ref,) = rest
    o_ref[...] = out


def _ple(h, p, w_gate, b_gate, w_proj, g_next=None):
    seq, d = h.shape
    tm = min(seq, 512)
    row = lambda i: (i, 0)
    fixed = lambda i: (0, 0)
    with_next = g_next is not None
    in_specs = [pl.BlockSpec((tm, d), row), pl.BlockSpec((tm, p.shape[1]), row),
                pl.BlockSpec(w_gate.shape, fixed), pl.BlockSpec((1, d), fixed),
                pl.BlockSpec(w_proj.shape, fixed)]
    args = [h, p, w_gate, b_gate[None, :], w_proj]
    out_specs = [pl.BlockSpec((tm, d), row)]
    out_shape = [jax.ShapeDtypeStruct((seq, d), F32)]
    if with_next:
        in_specs.append(pl.BlockSpec((1, d), fixed))
        args.append(g_next[None, :])
        out_specs.append(pl.BlockSpec((tm, d), row))
        out_shape.append(jax.ShapeDtypeStruct((seq, d), BF16))
    res = pl.pallas_call(
        functools.partial(_ple_kernel, with_next=with_next),
        grid=(seq // tm,),
        in_specs=in_specs, out_specs=out_specs, out_shape=out_shape,
        compiler_params=_params(("parallel",)),
        name="ple",
    )(*args)
    return res if with_next else (res[0], None)


def _odd_proj_kernel(x_ref, w_ref, cos_ref, sin_ref, o_ref, *rest, rope, scale, kmean, tm):
    t = jnp.dot(x_ref[...], w_ref[...], preferred_element_type=F32)
    if rope:
        t = _rope(t, cos_ref[...], sin_ref[...], MOBA_HD // 2)
    if kmean:
        (km_ref,) = rest
        for b in range(tm // MOBA_BLOCK):
            blk = t[b * MOBA_BLOCK:(b + 1) * MOBA_BLOCK, :]
            km_ref[b] = jnp.sum(blk, axis=0, keepdims=True) * (1.0 / MOBA_BLOCK)
    if scale != 1.0:
        t = t * scale
    o_ref[...] = t.astype(BF16)


def _odd_proj(xn, w, col, tabs_cos, tabs_sin, *, rope, scale=1.0, kmean=False):
    seq, d = xn.shape
    n = d
    tm = min(seq, 1024)
    row = lambda i: (i, 0)
    out_specs = [pl.BlockSpec((tm, n), row)]
    out_shape = [jax.ShapeDtypeStruct((seq, n), BF16)]
    if kmean:
        nb = tm // MOBA_BLOCK
        out_specs.append(pl.BlockSpec((nb, 1, n), lambda i: (i, 0, 0)))
        out_shape.append(jax.ShapeDtypeStruct((seq // MOBA_BLOCK, 1, n), F32))
    res = pl.pallas_call(
        functools.partial(_odd_proj_kernel, rope=rope, scale=scale, kmean=kmean, tm=tm),
        grid=(seq // tm,),
        in_specs=[pl.BlockSpec((tm, d), row),
                  pl.BlockSpec((d, n), lambda i: (0, col), pipeline_mode=pl.Buffered(1)),
                  pl.BlockSpec((tm, LANES), lambda i: (i, 1)),
                  pl.BlockSpec((tm, LANES), lambda i: (i, 1))],
        out_specs=out_specs, out_shape=out_shape,
        compiler_params=_params(("parallel",)),
        name="odd_proj",
    )(xn, w, tabs_cos, tabs_sin)
    return res


def _odd_vt_kernel(x_ref, wt_ref, vt_ref):
    vt = lax.dot_general(wt_ref[...], x_ref[...], (((1,), (1,)), ((), ())), preferred_element_type=F32)
    vt_ref[...] = vt.astype(BF16).reshape(vt_ref.shape)


def _odd_proj_vt(xn, wt):
    seq, d = xn.shape
    tm = min(seq, FLASH_TK)
    heads = wt.shape[0] // LANES
    return pl.pallas_call(
        _odd_vt_kernel,
        grid=(seq // tm,),
        in_specs=[pl.BlockSpec((tm, d), lambda i: (i, 0)), pl.BlockSpec(wt.shape, lambda i: (0, 0))],
        out_specs=pl.BlockSpec((heads, None, LANES, tm), lambda i: (0, i, 0, 0)),
        out_shape=jax.ShapeDtypeStruct((heads, seq // tm, LANES, tm), BF16),
        compiler_params=_params(("parallel",)),
        name="odd_proj_vt",
    )(xn, wt)


def _moba_kernel(q_ref, k_ref, vt_ref, km_ref, o_ref, *stats, tq, tk, nkb):
    i = pl.program_id(1)
    q = q_ref[...]

    km = km_ref[...]
    if nkb < 64:
        km = jnp.concatenate([km, jnp.zeros((64 - nkb, LANES), F32)], axis=0)
    km_hi = km.astype(BF16)
    km_mid = (km - km_hi.astype(F32)).astype(BF16)
    gate = lax.dot_general(jnp.concatenate([km_hi, km_mid], axis=1), jnp.concatenate([q, q], axis=1),
                           (((1,), (1,)), ((), ())), preferred_element_type=F32)

    blk = lax.broadcasted_iota(jnp.int32, (64, tq), 0)
    qblk = (i * tq + lax.broadcasted_iota(jnp.int32, (64, tq), 1)) // MOBA_BLOCK
    gm = jnp.where(blk < qblk, gate, -jnp.inf)
    sel = blk == qblk
    for r in range(MOBA_TOPK):
        mx = jnp.max(gm, axis=0, keepdims=True)
        idx = jnp.min(jnp.where(gm == mx, blk, 64), axis=0, keepdims=True)
        pick = (blk == idx) & (qblk > r)
        sel = sel | pick
        gm = jnp.where(pick, -jnp.inf, gm)
    bias_t = jnp.where(sel, 0.0, MASK_BIAS)
    qat = jnp.concatenate([q.astype(F32).T, bias_t, jnp.zeros((64, tq), F32)], axis=0).astype(BF16)

    kb_per_tile = tk // MOBA_BLOCK
    krow = lax.broadcasted_iota(jnp.int32, (tk, LANES), 0) // MOBA_BLOCK
    klane = lax.broadcasted_iota(jnp.int32, (tk, LANES), 1)

    def kv(j):
        rows = pl.ds(pl.multiple_of(j * tk, tk), tk)
        onehot = jnp.where(klane == krow + j * kb_per_tile, 1.0, 0.0).astype(BF16)
        return jnp.concatenate([k_ref[rows, :], onehot], axis=1), vt_ref[j]

    _flash_attend_t(qat, kv, i, tq, tk, stats, o_ref)


def _moba_attn(q, k, vt, kmean):
    seq = q.shape[0]
    nkb = seq // MOBA_BLOCK
    assert nkb <= 64 and nkb % 8 == 0, "the gate matmul stacks blocks on 64 sublanes"
    tq = min(seq, FLASH_TQ)
    tk = min(seq, FLASH_TK)
    nk = seq // tk
    return pl.pallas_call(
        functools.partial(_moba_kernel, tq=tq, tk=tk, nkb=nkb),
        grid=(MOBA_HEADS, seq // tq),
        in_specs=[pl.BlockSpec((tq, LANES), lambda h, i: (i, h)),
                  pl.BlockSpec((seq, LANES), lambda h, i: (0, h)),
                  pl.BlockSpec((None, nk, LANES, tk), lambda h, i: (h, 0, 0, 0)),
                  pl.BlockSpec((nkb, LANES), lambda h, i: (0, h))],
        out_specs=pl.BlockSpec((tq, LANES), lambda h, i: (i, h)),
        out_shape=jax.ShapeDtypeStruct((seq, MOBA_HEADS * MOBA_HD), BF16),
        scratch_shapes=_flash_t_scratch(tq),
        compiler_params=_params(("parallel", "arbitrary")),
        name="moba_attn",
    )(q, k, vt, kmean)


def _even_in_weight(w_in):
    d = w_in.shape[0]
    cq, ckv, kr, qs, ks, vs = jnp.split(w_in, [512, 1024, 1088, 2112, 2240], axis=1)
    return jnp.concatenate([cq, ckv, kr, jnp.zeros((d, 64), w_in.dtype), qs, ks, vs], axis=1).astype(BF16)


def _mla_q_weight(w_q_up):
    r = w_q_up.shape[0]
    w = w_q_up.reshape(r, MLA_HEADS, MLA_NOPE + MLA_ROPE)
    nope = w[:, :, :MLA_NOPE].reshape(r, MLA_HEADS * MLA_NOPE)
    rope = jnp.pad(w[:, :, MLA_NOPE:], ((0, 0), (0, 0), (0, LANES - MLA_ROPE))).reshape(r, MLA_HEADS * LANES)
    return jnp.concatenate([nope, rope], axis=1).astype(BF16)


def _mla_kv_weight(w_kv_up):
    r = w_kv_up.shape[0]
    w = w_kv_up.reshape(r, MLA_HEADS, MLA_NOPE + MLA_V)
    nope = w[:, :, :MLA_NOPE].reshape(r, MLA_HEADS * MLA_NOPE)
    val = w[:, :, MLA_NOPE:].reshape(r, MLA_HEADS * MLA_V)
    return nope.astype(BF16), val.T.astype(BF16)


def kernel(x, p, positions, even_pre_g, even_w_in, mla_q_norm_g, mla_w_q_up, mla_kv_norm_g, mla_w_kv_up, swa_sinks, even_w_out, even_post_g, odd_pre_g, moba_w_qkv, odd_w_out, odd_post_g, mlp_pre_g, mlp_w_up, mlp_w_down, mlp_post_g, ple_w_gate, ple_b_gate, ple_w_proj):
    batch, seq, d = x.shape
    assert batch == 1
    h = x.reshape(seq, d)
    tabs_cos, tabs_sin = _rope_tables(positions, seq)

    cq, ckv, kr, qs, ks, vs = _even_proj(h, even_pre_g[0], _even_in_weight(even_w_in[0]),
                                         mla_q_norm_g[0], mla_kv_norm_g[0], tabs_cos, tabs_sin)
    qn, qr, kn, vt = _mla_up(cq, ckv, _mla_q_weight(mla_w_q_up[0]), *_mla_kv_weight(mla_w_kv_up[0]),
                             tabs_cos, tabs_sin)
    o_mla = _mla_attn_t(qn, qr, kn, kr, vt)
    o_swa = _swa_attn(qs, ks, vs, swa_sinks[0])
    h, hn = _out_proj([o_mla, o_swa], even_w_out[0].astype(BF16), h, even_post_g[0], mlp_pre_g[0])
    h = _mlp(hn, mlp_w_up[0].astype(BF16), mlp_w_down[0].astype(BF16), h, mlp_post_g[0])
    h, hn = _ple(h, p[0, 0], ple_w_gate[0].astype(BF16), ple_b_gate[0], ple_w_proj[0].astype(BF16),
                 g_next=odd_pre_g[0])

    w_qkv = moba_w_qkv[0].astype(BF16)
    (q,) = _odd_proj(hn, w_qkv, 0, tabs_cos, tabs_sin, rope=True, scale=MOBA_HD ** -0.5 * LOG2E)
    k, kmean = _odd_proj(hn, w_qkv, 1, tabs_cos, tabs_sin, rope=True, kmean=True)
    vt = _odd_proj_vt(hn, moba_w_qkv[0][:, 2 * d:].T.astype(BF16))
    o = _moba_attn(q, k, vt, kmean.reshape(seq // MOBA_BLOCK, MOBA_HEADS * MOBA_HD))
    h, hn = _out_proj([o], odd_w_out[0].astype(BF16), h, odd_post_g[0], mlp_pre_g[1])
    h = _mlp(hn, mlp_w_up[1].astype(BF16), mlp_w_down[1].astype(BF16), h, mlp_post_g[1])
    h, _ = _ple(h, p[1, 0], ple_w_gate[1].astype(BF16), ple_b_gate[1], ple_w_proj[1].astype(BF16))
    return h.reshape(batch, seq, d)
```

```python
import functools

import jax
import jax.numpy as jnp
import numpy as np
from jax import lax
from jax.experimental import pallas as pl
from jax.experimental.pallas import tpu as pltpu

F32 = jnp.float32
BF16 = jnp.bfloat16

NORM_EPS = 1e-6
ROPE_THETA = 10000.0

MLA_HEADS = 8
MLA_Q_RANK = 512
MLA_KV_RANK = 512
MLA_NOPE = 128
MLA_ROPE = 64
MLA_V = 128
SWA_HEADS = 16
SWA_KV_HEADS = 2
SWA_HD = 64
SWA_BLOCK = 128
MOBA_HEADS = 16
MOBA_HD = 128
MOBA_BLOCK = 256
MOBA_TOPK = 3

LANES = 128
V7X_VMEM_BYTES = 64 * 1024 * 1024
VMEM_LIMIT = V7X_VMEM_BYTES * 7 // 8
MASK_BIAS = -1e30
LOG2E = 1.4426950408889634
FLASH_TQ = 4096
FLASH_TK = 512
T_CHUNK = 256
T_AHEAD = 4
ONES_ROWS = 16


def _params(sem):
    return pltpu.CompilerParams(dimension_semantics=sem, vmem_limit_bytes=VMEM_LIMIT)


def _rms(t, g):
    return t * lax.rsqrt(jnp.mean(t * t, axis=-1, keepdims=True) + NORM_EPS) * g


def _rope(t, cos, sin_signed, half):
    width = t.shape[1]
    reps = width // LANES
    if reps > 1:
        cos = jnp.concatenate([cos] * reps, axis=1)
        sin_signed = jnp.concatenate([sin_signed] * reps, axis=1)
    lane = lax.broadcasted_iota(jnp.int32, t.shape, 1)
    first = (lane % (2 * half)) < half
    partner = jnp.where(first, pltpu.roll(t, width - half, 1), pltpu.roll(t, half, 1))
    return t * cos + partner * sin_signed


def _tables_kernel(pos_ref, invf_ref, sign_ref, cos_ref, sin_ref):
    ang = pos_ref[...].astype(F32) * invf_ref[...]
    cos_ref[...] = jnp.cos(ang)
    sin_ref[...] = jnp.sin(ang) * sign_ref[...]


def _rope_tables(positions, seq):
    def inv_freq(d):
        half = d // 2
        return jnp.power(ROPE_THETA, -jnp.arange(half, dtype=F32) * (2.0 / d))

    f64, f128 = inv_freq(64), inv_freq(128)
    invf = jnp.concatenate([f64, f64, f64, f64, f128, f128])[None, :]
    sign = np.concatenate([-np.ones(32), np.ones(32), -np.ones(32), np.ones(32),
                           -np.ones(64), np.ones(64)]).astype(np.float32)[None, :]
    tm = min(seq, 1024)
    return pl.pallas_call(
        _tables_kernel,
        grid=(seq // tm,),
        in_specs=[pl.BlockSpec((tm, 1), lambda i: (i, 0)),
                  pl.BlockSpec((1, 256), lambda i: (0, 0)),
                  pl.BlockSpec((1, 256), lambda i: (0, 0))],
        out_specs=[pl.BlockSpec((tm, 256), lambda i: (i, 0)),
                   pl.BlockSpec((tm, 256), lambda i: (i, 0))],
        out_shape=[jax.ShapeDtypeStruct((seq, 256), F32)] * 2,
        compiler_params=_params(("parallel",)),
        name="rope_tables",
    )(positions.reshape(seq, 1), invf, jnp.asarray(sign))


_EVEN_COLS = (0, 512, 1024, 1152, 2176, 2304, 2432)


def _even_proj_kernel(x_ref, gpre_ref, w_ref, gq_ref, gkv_ref, cos_ref, sin_ref,
                      cq_ref, ckv_ref, kr_ref, qs_ref, ks_ref, vs_ref):
    x = _rms(x_ref[...], gpre_ref[...]).astype(BF16)
    cos, sin = cos_ref[...], sin_ref[...]
    c = _EVEN_COLS

    def mm(k):
        return jnp.dot(x, w_ref[:, c[k]:c[k + 1]], preferred_element_type=F32)

    cq_ref[...] = _rms(mm(0), gq_ref[...]).astype(BF16)
    ckv_ref[...] = _rms(mm(1), gkv_ref[...]).astype(BF16)
    kr_ref[...] = _rope(mm(2), cos, sin, 32).astype(BF16)
    qs_ref[...] = (_rope(mm(3), cos, sin, 32) * (SWA_HD ** -0.5)).astype(BF16)
    ks_ref[...] = _rope(mm(4), cos, sin, 32).astype(BF16)
    vs_ref[...] = mm(5).astype(BF16)


def _even_proj(x, g_pre, w, gq, gkv, tabs_cos, tabs_sin):
    seq, d = x.shape
    tm = min(seq, 1024)
    widths = [_EVEN_COLS[k + 1] - _EVEN_COLS[k] for k in range(6)]
    row = lambda i: (i, 0)
    fixed = lambda i: (0, 0)
    return pl.pallas_call(
        _even_proj_kernel,
        grid=(seq // tm,),
        in_specs=[pl.BlockSpec((tm, d), row),
                  pl.BlockSpec((1, d), fixed),
                  pl.BlockSpec(w.shape, fixed, pipeline_mode=pl.Buffered(1)),
                  pl.BlockSpec((1, 512), fixed),
                  pl.BlockSpec((1, 512), fixed),
                  pl.BlockSpec((tm, LANES), row),
                  pl.BlockSpec((tm, LANES), row)],
        out_specs=[pl.BlockSpec((tm, n), row) for n in widths],
        out_shape=[jax.ShapeDtypeStruct((seq, n), BF16) for n in widths],
        compiler_params=_params(("parallel",)),
        name="even_proj",
    )(x, g_pre[None, :], w, gq[None, :], gkv[None, :], tabs_cos, tabs_sin)


def _mla_up_kernel(cq_ref, ckv_ref, wq_ref, wk_ref, wvt_ref, cos_ref, sin_ref,
                   qn_ref, qr_ref, kn_ref, vt_ref, *, scale):
    cq, ckv = cq_ref[...], ckv_ref[...]
    n = MLA_HEADS * LANES
    qn = jnp.dot(cq, wq_ref[:, :n], preferred_element_type=F32)
    qr = jnp.dot(cq, wq_ref[:, n:], preferred_element_type=F32)
    qn_ref[...] = (qn * scale).astype(BF16)
    qr_ref[...] = (_rope(qr, cos_ref[...], sin_ref[...], 32) * scale).astype(BF16)
    kn_ref[...] = jnp.dot(ckv, wk_ref[...], preferred_element_type=F32).astype(BF16)
    vt = lax.dot_general(wvt_ref[...], ckv, (((1,), (1,)), ((), ())), preferred_element_type=F32)
    vt_ref[...] = vt.astype(BF16).reshape(vt_ref.shape)


def _mla_up(cq, ckv, wq, wk, wvt, tabs_cos, tabs_sin):
    seq = cq.shape[0]
    tm = min(seq, FLASH_TK)
    n = MLA_HEADS * LANES
    row = lambda i: (i, 0)
    fixed = lambda i: (0, 0)
    scale = (MLA_NOPE + MLA_ROPE) ** -0.5 * LOG2E
    return pl.pallas_call(
        functools.partial(_mla_up_kernel, scale=scale),
        grid=(seq // tm,),
        in_specs=[pl.BlockSpec((tm, MLA_Q_RANK), row),
                  pl.BlockSpec((tm, MLA_KV_RANK), row),
                  pl.BlockSpec(wq.shape, fixed),
                  pl.BlockSpec(wk.shape, fixed),
                  pl.BlockSpec(wvt.shape, fixed),
                  pl.BlockSpec((tm, LANES), row),
                  pl.BlockSpec((tm, LANES), row)],
        out_specs=[pl.BlockSpec((tm, n), row)] * 3
        + [pl.BlockSpec((MLA_HEADS, None, LANES, tm), lambda i: (0, i, 0, 0))],
        out_shape=[jax.ShapeDtypeStruct((seq, n), BF16)] * 3
        + [jax.ShapeDtypeStruct((MLA_HEADS, seq // tm, LANES, tm), BF16)],
        compiler_params=_params(("parallel",)),
        name="mla_up",
    )(cq, ckv, wq, wk, wvt, tabs_cos, tabs_sin)


def _softmax_pv_t(st, vt1, m_ref, acc_ref):
    m_old = m_ref[...]
    m_new = jnp.maximum(m_old, jnp.max(st, axis=0, keepdims=True))
    alpha = jnp.exp2(m_old - m_new)
    pt = jnp.exp2(st - m_new)
    acc_ref[...] = alpha * acc_ref[...] + jnp.dot(vt1, pt.astype(BF16), preferred_element_type=F32)
    m_ref[...] = m_new


def _flash_attend_t(qt, kv, i, tq, tk, stats, o_ref):
    nc = tq // T_CHUNK
    m_refs, acc_refs = stats[:nc], stats[nc:]
    for c in range(nc):
        m_refs[c][...] = jnp.full(m_refs[c].shape, -jnp.inf, F32)
        acc_refs[c][...] = jnp.zeros(acc_refs[c].shape, F32)
    sub = tq // tk
    qts = [qt[:, c * T_CHUNK:(c + 1) * T_CHUNK] for c in range(nc)]
    key = lax.broadcasted_iota(jnp.int32, (tk, T_CHUNK), 0)
    qry = lax.broadcasted_iota(jnp.int32, (tk, T_CHUNK), 1)
    ones = jnp.ones((ONES_ROWS, tk), BF16)

    def run(work, kvs, diagonal):
        vt1 = [jnp.concatenate([vt, ones], axis=0) for _, vt in kvs]

        def score(n):
            d, c = work[n]
            st = jnp.dot(kvs[d][0], qts[c], preferred_element_type=F32)
            off = c * T_CHUNK - d * tk
            if diagonal and off < tk:
                st = jnp.where(key <= qry + off, st, -jnp.inf)
            return st

        ahead = [score(n) for n in range(min(T_AHEAD, len(work)))]
        for n, (d, c) in enumerate(work):
            if n + T_AHEAD < len(work):
                ahead.append(score(n + T_AHEAD))
            _softmax_pv_t(ahead.pop(0), vt1[d], m_refs[c], acc_refs[c])

    def past(g, carry):
        run([(d, c) for d in range(sub) for c in range(nc)], [kv(g * sub + d) for d in range(sub)], False)
        return carry

    lax.fori_loop(0, i, past, 0)

    run([(d, c) for d in range(sub) for c in range(d * tk // T_CHUNK, nc)],
        [kv(i * sub + d) for d in range(sub)], True)

    out_t = jnp.concatenate([acc_refs[c][:LANES, :] / acc_refs[c][LANES:LANES + 1, :] for c in range(nc)], axis=1)
    o_ref[...] = out_t.T.astype(BF16)


def _flash_t_scratch(tq):
    nc = tq // T_CHUNK
    return [pltpu.VMEM((1, T_CHUNK), F32)] * nc + [pltpu.VMEM((LANES + ONES_ROWS, T_CHUNK), F32)] * nc


def _mla_attn_kernel(qn_ref, qr_ref, kn_ref, kr_ref, vt_ref, o_ref, *stats, tq, tk):
    i = pl.program_id(1)
    q = jnp.concatenate([qn_ref[...], qr_ref[...]], axis=1)
    qt = q.astype(F32).T.astype(BF16)

    def kv(j):
        rows = pl.ds(pl.multiple_of(j * tk, tk), tk)
        return jnp.concatenate([kn_ref[rows, :], kr_ref[rows, :]], axis=1), vt_ref[j]

    _flash_attend_t(qt, kv, i, tq, tk, stats, o_ref)


def _mla_attn(qn, qr, kn, kr, vt):
    seq = qn.shape[0]
    tq = min(seq, FLASH_TQ)
    tk = min(seq, FLASH_TK)
    nk = seq // tk
    return pl.pallas_call(
        functools.partial(_mla_attn_kernel, tq=tq, tk=tk),
        grid=(MLA_HEADS, seq // tq),
        in_specs=[pl.BlockSpec((tq, LANES), lambda h, i: (i, h)),
                  pl.BlockSpec((tq, LANES), lambda h, i: (i, h)),
                  pl.BlockSpec((seq, LANES), lambda h, i: (0, h)),
                  pl.BlockSpec((seq, LANES), lambda h, i: (0, 0)),
                  pl.BlockSpec((None, nk, LANES, tk), lambda h, i: (h, 0, 0, 0))],
        out_specs=pl.BlockSpec((tq, LANES), lambda h, i: (i, h)),
        out_shape=jax.ShapeDtypeStruct((seq, MLA_HEADS * MLA_V), BF16),
        scratch_shapes=_flash_t_scratch(tq),
        compiler_params=_params(("parallel", "arbitrary")),
        name="mla_attn",
    )(qn, qr, kn, kr, vt)


def _swa_kernel(sink_ref, q_ref, kc_ref, kp_ref, vc_ref, vp_ref, o_ref, *, tq):
    i = pl.program_id(0)
    nb = tq // SWA_BLOCK
    group = SWA_HEADS // SWA_KV_HEADS
    pairs = group // 2
    lane = lax.broadcasted_iota(jnp.int32, (2 * SWA_BLOCK, LANES), 1)
    shape = (pairs * SWA_BLOCK, 2 * SWA_BLOCK)
    qp = lax.broadcasted_iota(jnp.int32, shape, 0) % SWA_BLOCK + SWA_BLOCK
    kp = lax.broadcasted_iota(jnp.int32, shape, 1)
    band = (kp <= qp) & (qp - kp < SWA_BLOCK)
    out_lane = lax.broadcasted_iota(jnp.int32, (pairs * SWA_BLOCK, LANES), 1)
    key_row = lax.broadcasted_iota(jnp.int32, (4 * SWA_BLOCK, LANES), 0)
    key_lane = lax.broadcasted_iota(jnp.int32, (4 * SWA_BLOCK, LANES), 1)
    ones_by_head = jnp.where((key_row < 2 * SWA_BLOCK) == (key_lane < SWA_HD), 1.0, 0.0).astype(BF16)

    def split(t, c):
        mine = jnp.where((lane >= c * SWA_HD) & (lane < (c + 1) * SWA_HD), t, 0.0)
        other = pltpu.roll(mine, SWA_HD, 1)
        lo, hi = (mine, other) if c == 0 else (other, mine)
        return jnp.concatenate([lo, hi], axis=0).astype(BF16)

    for b in range(nb):
        rows = slice(b * SWA_BLOCK, (b + 1) * SWA_BLOCK)
        if b == 0:
            k_prev, v_prev = kp_ref[...], vp_ref[...]
        else:
            prev = slice((b - 1) * SWA_BLOCK, b * SWA_BLOCK)
            k_prev, v_prev = kc_ref[prev, :], vc_ref[prev, :]
        kw = jnp.concatenate([k_prev, kc_ref[rows, :]], axis=0).astype(F32)
        vw = jnp.concatenate([v_prev, vc_ref[rows, :]], axis=0).astype(F32)
        first_key = jnp.where(i * nb + b == 0, SWA_BLOCK, 0)
        valid = band & (kp >= first_key)
        for c in range(SWA_KV_HEADS):
            kcat = split(kw, c)
            vcat = split(vw, c)
            cols = [slice((c * pairs + a) * LANES, (c * pairs + a + 1) * LANES) for a in range(pairs)]
            q4 = jnp.concatenate([q_ref[rows, cs] for cs in cols], axis=0)
            s = lax.dot_general(q4, kcat, (((1,), (1,)), ((), ())), preferred_element_type=F32)
            ps, sink_terms = [], []
            for e in range(2):
                sink = jnp.concatenate(
                    [jnp.full((SWA_BLOCK, 1), sink_ref[c * group + 2 * a + e], F32) for a in range(pairs)], axis=0)
                se = jnp.where(valid, s[:, e * 2 * SWA_BLOCK:(e + 1) * 2 * SWA_BLOCK], -jnp.inf)
                m = jnp.maximum(jnp.max(se, axis=1, keepdims=True), sink)
                ps.append(jnp.exp(se - m).astype(BF16))
                sink_terms.append(jnp.exp(sink - m))
            pv = jnp.dot(jnp.concatenate(ps, axis=1), jnp.concatenate([vcat, ones_by_head], axis=1),
                         preferred_element_type=F32)
            den = pv[:, LANES:] + jnp.where(out_lane < SWA_HD, sink_terms[0], sink_terms[1])
            o4 = (pv[:, :LANES] / den).astype(BF16)
            for a, cs in enumerate(cols):
                o_ref[rows, cs] = o4[a * SWA_BLOCK:(a + 1) * SWA_BLOCK]


def _swa_attn(qs, ks, vs, sinks):
    seq = qs.shape[0]
    tq = min(seq, 512)
    nb = tq // SWA_BLOCK
    cur = lambda i: (i, 0)
    prev = lambda i: (jnp.maximum(i * nb - 1, 0), 0)
    return pl.pallas_call(
        functools.partial(_swa_kernel, tq=tq),
        grid=(seq // tq,),
        in_specs=[pl.BlockSpec(memory_space=pltpu.SMEM),
                  pl.BlockSpec((tq, SWA_HEADS * SWA_HD), cur),
                  pl.BlockSpec((tq, LANES), cur),
                  pl.BlockSpec((SWA_BLOCK, LANES), prev),
                  pl.BlockSpec((tq, LANES), cur),
                  pl.BlockSpec((SWA_BLOCK, LANES), prev)],
        out_specs=pl.BlockSpec((tq, SWA_HEADS * SWA_HD), cur),
        out_shape=jax.ShapeDtypeStruct((seq, SWA_HEADS * SWA_HD), BF16),
        compiler_params=_params(("parallel",)),
        name="swa_attn",
    )(sinks, qs, ks, ks, vs, vs)


def _out_proj_kernel(*refs, n_a):
    a_refs = refs[:n_a]
    w_ref, h_ref, gpost_ref, gnext_ref, hout_ref, hn_ref = refs[n_a:]
    tm = h_ref.shape[0]
    for rows in [slice(r * tm // 4, (r + 1) * tm // 4) for r in range(4)]:
        m = None
        k0 = 0
        for a_ref in a_refs:
            kw = a_ref.shape[1]
            part = jnp.dot(a_ref[rows, :], w_ref[k0:k0 + kw, :], preferred_element_type=F32)
            m = part if m is None else m + part
            k0 += kw
        h = h_ref[rows, :] + _rms(m, gpost_ref[...])
        hout_ref[rows, :] = h
        hn_ref[rows, :] = _rms(h, gnext_ref[...]).astype(BF16)


def _out_proj(a_list, w, h, g_post, g_next):
    seq, d = h.shape
    tm = min(seq, 512)
    row = lambda i: (i, 0)
    fixed = lambda i: (0, 0)
    return pl.pallas_call(
        functools.partial(_out_proj_kernel, n_a=len(a_list)),
        grid=(seq // tm,),
        in_specs=[pl.BlockSpec((tm, a.shape[1]), row) for a in a_list]
        + [pl.BlockSpec(w.shape, fixed), pl.BlockSpec((tm, d), row),
           pl.BlockSpec((1, d), fixed), pl.BlockSpec((1, d), fixed)],
        out_specs=[pl.BlockSpec((tm, d), row), pl.BlockSpec((tm, d), row)],
        out_shape=[jax.ShapeDtypeStruct((seq, d), F32), jax.ShapeDtypeStruct((seq, d), BF16)],
        compiler_params=_params(("parallel",)),
        name="out_proj",
    )(*a_list, w, h, g_post[None, :], g_next[None, :])


def _mlp_kernel(x_ref, wu_ref, wd_ref, h_ref, g_ref, o_ref, acc_ref):
    f = pl.program_id(1)

    @pl.when(f == 0)
    def _():
        acc_ref[...] = jnp.zeros(acc_ref.shape, F32)

    a = jnp.maximum(jnp.dot(x_ref[...], wu_ref[...], preferred_element_type=F32), 0.0)
    acc_ref[...] += jnp.dot((a * a).astype(BF16), wd_ref[...], preferred_element_type=F32)

    @pl.when(f == pl.num_programs(1) - 1)
    def _():
        o_ref[...] = h_ref[...] + _rms(acc_ref[...], g_ref[...])


def _mlp(xn, w_up, w_down, h, g_post):
    seq, d = h.shape
    d_ff = w_up.shape[1]
    tm = min(seq, 512)
    tf = min(d_ff, 1024)
    return pl.pallas_call(
        _mlp_kernel,
        grid=(seq // tm, d_ff // tf),
        in_specs=[pl.BlockSpec((tm, d), lambda i, f: (i, 0)),
                  pl.BlockSpec((d, tf), lambda i, f: (0, f)),
                  pl.BlockSpec((tf, d), lambda i, f: (f, 0)),
                  pl.BlockSpec((tm, d), lambda i, f: (i, 0)),
                  pl.BlockSpec((1, d), lambda i, f: (0, 0))],
        out_specs=pl.BlockSpec((tm, d), lambda i, f: (i, 0)),
        out_shape=jax.ShapeDtypeStruct((seq, d), F32),
        scratch_shapes=[pltpu.VMEM((tm, d), F32)],
        compiler_params=_params(("parallel", "arbitrary")),
        name="mlp",
    )(xn, w_up, w_down, h, g_post[None, :])


def _ple_kernel(h_ref, p_ref, wg_ref, b_ref, wp_ref, *rest, with_next):
    h = h_ref[...]
    z = jnp.dot(h.astype(BF16), wg_ref[...], preferred_element_type=F32) + b_ref[...]
    gate = 1.0 / (1.0 + jnp.exp(-z))
    e = jnp.dot(p_ref[...].astype(BF16), wp_ref[...], preferred_element_type=F32)
    out = h + gate * e
    if with_next:
        gnext_ref, o_ref, hn_ref = rest
        hn_ref[...] = _rms(out, gnext_ref[...]).astype(BF16)
    else:
        (o_ref,) = rest
    o_ref[...] = out


def _ple(h, p, w_gate, b_gate, w_proj, g_next=None):
    seq, d = h.shape
    tm = min(seq, 512)
    row = lambda i: (i, 0)
    fixed = lambda i: (0, 0)
    with_next = g_next is not None
    in_specs = [pl.BlockSpec((tm, d), row), pl.BlockSpec((tm, p.shape[1]), row),
                pl.BlockSpec(w_gate.shape, fixed), pl.BlockSpec((1, d), fixed),
                pl.BlockSpec(w_proj.shape, fixed)]
    args = [h, p, w_gate, b_gate[None, :], w_proj]
    out_specs = [pl.BlockSpec((tm, d), row)]
    out_shape = [jax.ShapeDtypeStruct((seq, d), F32)]
    if with_next:
        in_specs.append(pl.BlockSpec((1, d), fixed))
        args.append(g_next[None, :])
        out_specs.append(pl.BlockSpec((tm, d), row))
        out_shape.append(jax.ShapeDtypeStruct((seq, d), BF16))
    res = pl.pallas_call(
        functools.partial(_ple_kernel, with_next=with_next),
        grid=(seq // tm,),
        in_specs=in_specs, out_specs=out_specs, out_shape=out_shape,
        compiler_params=_params(("parallel",)),
        name="ple",
    )(*args)
    return res if with_next else (res[0], None)


def _odd_proj_kernel(x_ref, w_ref, cos_ref, sin_ref, o_ref, *rest, rope, scale, kmean, tm):
    t = jnp.dot(x_ref[...], w_ref[...], preferred_element_type=F32)
    if rope:
        t = _rope(t, cos_ref[...], sin_ref[...], MOBA_HD // 2)
    if kmean:
        (km_ref,) = rest
        for b in range(tm // MOBA_BLOCK):
            blk = t[b * MOBA_BLOCK:(b + 1) * MOBA_BLOCK, :]
            km_ref[b] = jnp.sum(blk, axis=0, keepdims=True) * (1.0 / MOBA_BLOCK)
    if scale != 1.0:
        t = t * scale
    o_ref[...] = t.astype(BF16)


def _odd_proj(xn, w, col, tabs_cos, tabs_sin, *, rope, scale=1.0, kmean=False):
    seq, d = xn.shape
    n = d
    tm = min(seq, 1024)
    row = lambda i: (i, 0)
    out_specs = [pl.BlockSpec((tm, n), row)]
    out_shape = [jax.ShapeDtypeStruct((seq, n), BF16)]
    if kmean:
        nb = tm // MOBA_BLOCK
        out_specs.append(pl.BlockSpec((nb, 1, n), lambda i: (i, 0, 0)))
        out_shape.append(jax.ShapeDtypeStruct((seq // MOBA_BLOCK, 1, n), F32))
    res = pl.pallas_call(
        functools.partial(_odd_proj_kernel, rope=rope, scale=scale, kmean=kmean, tm=tm),
        grid=(seq // tm,),
        in_specs=[pl.BlockSpec((tm, d), row),
                  pl.BlockSpec((d, n), lambda i: (0, col), pipeline_mode=pl.Buffered(1)),
                  pl.BlockSpec((tm, LANES), lambda i: (i, 1)),
                  pl.BlockSpec((tm, LANES), lambda i: (i, 1))],
        out_specs=out_specs, out_shape=out_shape,
        compiler_params=_params(("parallel",)),
        name="odd_proj",
    )(xn, w, tabs_cos, tabs_sin)
    return res


def _odd_vt_kernel(x_ref, wt_ref, vt_ref):
    vt = lax.dot_general(wt_ref[...], x_ref[...], (((1,), (1,)), ((), ())), preferred_element_type=F32)
    vt_ref[...] = vt.astype(BF16).reshape(vt_ref.shape)


def _odd_proj_vt(xn, wt):
    seq, d = xn.shape
    tm = min(seq, FLASH_TK)
    heads = wt.shape[0] // LANES
    return pl.pallas_call(
        _odd_vt_kernel,
        grid=(seq // tm,),
        in_specs=[pl.BlockSpec((tm, d), lambda i: (i, 0)), pl.BlockSpec(wt.shape, lambda i: (0, 0))],
        out_specs=pl.BlockSpec((heads, None, LANES, tm), lambda i: (0, i, 0, 0)),
        out_shape=jax.ShapeDtypeStruct((heads, seq // tm, LANES, tm), BF16),
        compiler_params=_params(("parallel",)),
        name="odd_proj_vt",
    )(xn, wt)


def _moba_kernel(q_ref, k_ref, vt_ref, km_ref, o_ref, *stats, tq, tk, nkb):
    i = pl.program_id(1)
    q = q_ref[...]

    km = km_ref[...]
    if nkb < 64:
        km = jnp.concatenate([km, jnp.zeros((64 - nkb, LANES), F32)], axis=0)
    km_hi = km.astype(BF16)
    km_mid = (km - km_hi.astype(F32)).astype(BF16)
    gate = lax.dot_general(jnp.concatenate([km_hi, km_mid], axis=1), jnp.concatenate([q, q], axis=1),
                           (((1,), (1,)), ((), ())), preferred_element_type=F32)

    blk = lax.broadcasted_iota(jnp.int32, (64, tq), 0)
    qblk = (i * tq + lax.broadcasted_iota(jnp.int32, (64, tq), 1)) // MOBA_BLOCK
    gm = jnp.where(blk < qblk, gate, -jnp.inf)
    sel = blk == qblk
    for r in range(MOBA_TOPK):
        mx = jnp.max(gm, axis=0, keepdims=True)
        idx = jnp.min(jnp.where(gm == mx, blk, 64), axis=0, keepdims=True)
        pick = (blk == idx) & (qblk > r)
        sel = sel | pick
        gm = jnp.where(pick, -jnp.inf, gm)
    bias_t = jnp.where(sel, 0.0, MASK_BIAS)
    qat = jnp.concatenate([q.astype(F32).T, bias_t, jnp.zeros((64, tq), F32)], axis=0).astype(BF16)

    kb_per_tile = tk // MOBA_BLOCK
    krow = lax.broadcasted_iota(jnp.int32, (tk, LANES), 0) // MOBA_BLOCK
    klane = lax.broadcasted_iota(jnp.int32, (tk, LANES), 1)

    def kv(j):
        rows = pl.ds(pl.multiple_of(j * tk, tk), tk)
        onehot = jnp.where(klane == krow + j * kb_per_tile, 1.0, 0.0).astype(BF16)
        return jnp.concatenate([k_ref[rows, :], onehot], axis=1), vt_ref[j]

    _flash_attend_t(qat, kv, i, tq, tk, stats, o_ref)


def _moba_attn(q, k, vt, kmean):
    seq = q.shape[0]
    nkb = seq // MOBA_BLOCK
    assert nkb <= 64 and nkb % 8 == 0, "the gate matmul stacks blocks on 64 sublanes"
    tq = min(seq, FLASH_TQ)
    tk = min(seq, FLASH_TK)
    nk = seq // tk
    return pl.pallas_call(
        functools.partial(_moba_kernel, tq=tq, tk=tk, nkb=nkb),
        grid=(MOBA_HEADS, seq // tq),
        in_specs=[pl.BlockSpec((tq, LANES), lambda h, i: (i, h)),
                  pl.BlockSpec((seq, LANES), lambda h, i: (0, h)),
                  pl.BlockSpec((None, nk, LANES, tk), lambda h, i: (h, 0, 0, 0)),
                  pl.BlockSpec((nkb, LANES), lambda h, i: (0, h))],
        out_specs=pl.BlockSpec((tq, LANES), lambda h, i: (i, h)),
        out_shape=jax.ShapeDtypeStruct((seq, MOBA_HEADS * MOBA_HD), BF16),
        scratch_shapes=_flash_t_scratch(tq),
        compiler_params=_params(("parallel", "arbitrary")),
        name="moba_attn",
    )(q, k, vt, kmean)


def _even_in_weight(w_in):
    d = w_in.shape[0]
    cq, ckv, kr, qs, ks, vs = jnp.split(w_in, [512, 1024, 1088, 2112, 2240], axis=1)
    return jnp.concatenate([cq, ckv, kr, jnp.zeros((d, 64), w_in.dtype), qs, ks, vs], axis=1).astype(BF16)


def _mla_q_weight(w_q_up):
    r = w_q_up.shape[0]
    w = w_q_up.reshape(r, MLA_HEADS, MLA_NOPE + MLA_ROPE)
    nope = w[:, :, :MLA_NOPE].reshape(r, MLA_HEADS * MLA_NOPE)
    rope = jnp.pad(w[:, :, MLA_NOPE:], ((0, 0), (0, 0), (0, LANES - MLA_ROPE))).reshape(r, MLA_HEADS * LANES)
    return jnp.concatenate([nope, rope], axis=1).astype(BF16)


def _mla_kv_weight(w_kv_up):
    r = w_kv_up.shape[0]
    w = w_kv_up.reshape(r, MLA_HEADS, MLA_NOPE + MLA_V)
    nope = w[:, :, :MLA_NOPE].reshape(r, MLA_HEADS * MLA_NOPE)
    val = w[:, :, MLA_NOPE:].reshape(r, MLA_HEADS * MLA_V)
    return nope.astype(BF16), val.T.astype(BF16)


def kernel(x, p, positions, even_pre_g, even_w_in, mla_q_norm_g, mla_w_q_up, mla_kv_norm_g, mla_w_kv_up, swa_sinks, even_w_out, even_post_g, odd_pre_g, moba_w_qkv, odd_w_out, odd_post_g, mlp_pre_g, mlp_w_up, mlp_w_down, mlp_post_g, ple_w_gate, ple_b_gate, ple_w_proj):
    batch, seq, d = x.shape
    assert batch == 1
    h = x.reshape(seq, d)
    tabs_cos, tabs_sin = _rope_tables(positions, seq)

    cq, ckv, kr, qs, ks, vs = _even_proj(h, even_pre_g[0], _even_in_weight(even_w_in[0]),
                                         mla_q_norm_g[0], mla_kv_norm_g[0], tabs_cos, tabs_sin)
    qn, qr, kn, vt = _mla_up(cq, ckv, _mla_q_weight(mla_w_q_up[0]), *_mla_kv_weight(mla_w_kv_up[0]),
                             tabs_cos, tabs_sin)
    o_mla = _mla_attn(qn, qr, kn, kr, vt)
    o_swa = _swa_attn(qs, ks, vs, swa_sinks[0])
    h, hn = _out_proj([o_mla, o_swa], even_w_out[0].astype(BF16), h, even_post_g[0], mlp_pre_g[0])
    h = _mlp(hn, mlp_w_up[0].astype(BF16), mlp_w_down[0].astype(BF16), h, mlp_post_g[0])
    h, hn = _ple(h, p[0, 0], ple_w_gate[0].astype(BF16), ple_b_gate[0], ple_w_proj[0].astype(BF16),
                 g_next=odd_pre_g[0])

    w_qkv = moba_w_qkv[0].astype(BF16)
    (q,) = _odd_proj(hn, w_qkv, 0, tabs_cos, tabs_sin, rope=True, scale=MOBA_HD ** -0.5 * LOG2E)
    k, kmean = _odd_proj(hn, w_qkv, 1, tabs_cos, tabs_sin, rope=True, kmean=True)
    vt = _odd_proj_vt(hn, moba_w_qkv[0][:, 2 * d:].T.astype(BF16))
    o = _moba_attn(q, k, vt, kmean.reshape(seq // MOBA_BLOCK, MOBA_HEADS * MOBA_HD))
    h, hn = _out_proj([o], odd_w_out[0].astype(BF16), h, odd_post_g[0], mlp_pre_g[1])
    h = _mlp(hn, mlp_w_up[1].astype(BF16), mlp_w_down[1].astype(BF16), h, mlp_post_g[1])
    h, _ = _ple(h, p[1, 0], ple_w_gate[1].astype(BF16), ple_b_gate[1], ple_w_proj[1].astype(BF16))
    return h.reshape(batch, seq, d)
```

```python
import functools

import jax
import jax.numpy as jnp
import numpy as np
from jax import lax
from jax.experimental import pallas as pl
from jax.experimental.pallas import tpu as pltpu

F32 = jnp.float32
BF16 = jnp.bfloat16

NORM_EPS = 1e-6
ROPE_THETA = 10000.0

MLA_HEADS = 8
MLA_Q_RANK = 512
MLA_KV_RANK = 512
MLA_NOPE = 128
MLA_ROPE = 64
MLA_V = 128
SWA_HEADS = 16
SWA_KV_HEADS = 2
SWA_HD = 64
SWA_BLOCK = 128
MOBA_HEADS = 16
MOBA_HD = 128
MOBA_BLOCK = 256
MOBA_TOPK = 3

LANES = 128
V7X_VMEM_BYTES = 64 * 1024 * 1024
VMEM_LIMIT = V7X_VMEM_BYTES * 7 // 8
MASK_BIAS = -1e30
LOG2E = 1.4426950408889634
FLASH_TQ = 8192
FLASH_TK = 512
T_CHUNK = 256
T_AHEAD = 4
ONES_ROWS = 16


def _params(sem):
    return pltpu.CompilerParams(dimension_semantics=sem, vmem_limit_bytes=VMEM_LIMIT)


def _rms(t, g):
    return t * lax.rsqrt(jnp.mean(t * t, axis=-1, keepdims=True) + NORM_EPS) * g


def _rope(t, cos, sin_signed, half):
    width = t.shape[1]
    reps = width // LANES
    if reps > 1:
        cos = jnp.concatenate([cos] * reps, axis=1)
        sin_signed = jnp.concatenate([sin_signed] * reps, axis=1)
    lane = lax.broadcasted_iota(jnp.int32, t.shape, 1)
    first = (lane % (2 * half)) < half
    partner = jnp.where(first, pltpu.roll(t, width - half, 1), pltpu.roll(t, half, 1))
    return t * cos + partner * sin_signed


def _tables_kernel(pos_ref, invf_ref, sign_ref, cos_ref, sin_ref):
    ang = pos_ref[...].astype(F32) * invf_ref[...]
    cos_ref[...] = jnp.cos(ang)
    sin_ref[...] = jnp.sin(ang) * sign_ref[...]


def _rope_tables(positions, seq):
    def inv_freq(d):
        half = d // 2
        return jnp.power(ROPE_THETA, -jnp.arange(half, dtype=F32) * (2.0 / d))

    f64, f128 = inv_freq(64), inv_freq(128)
    invf = jnp.concatenate([f64, f64, f64, f64, f128, f128])[None, :]
    sign = np.concatenate([-np.ones(32), np.ones(32), -np.ones(32), np.ones(32),
                           -np.ones(64), np.ones(64)]).astype(np.float32)[None, :]
    tm = min(seq, 1024)
    return pl.pallas_call(
        _tables_kernel,
        grid=(seq // tm,),
        in_specs=[pl.BlockSpec((tm, 1), lambda i: (i, 0)),
                  pl.BlockSpec((1, 256), lambda i: (0, 0)),
                  pl.BlockSpec((1, 256), lambda i: (0, 0))],
        out_specs=[pl.BlockSpec((tm, 256), lambda i: (i, 0)),
                   pl.BlockSpec((tm, 256), lambda i: (i, 0))],
        out_shape=[jax.ShapeDtypeStruct((seq, 256), F32)] * 2,
        compiler_params=_params(("parallel",)),
        name="rope_tables",
    )(positions.reshape(seq, 1), invf, jnp.asarray(sign))


_EVEN_COLS = (0, 512, 1024, 1152, 2176, 2304, 2432)


def _even_proj_kernel(x_ref, gpre_ref, w_ref, gq_ref, gkv_ref, cos_ref, sin_ref,
                      cq_ref, ckv_ref, kr_ref, qs_ref, ks_ref, vs_ref):
    x = _rms(x_ref[...], gpre_ref[...]).astype(BF16)
    cos, sin = cos_ref[...], sin_ref[...]
    c = _EVEN_COLS

    def mm(k):
        return jnp.dot(x, w_ref[:, c[k]:c[k + 1]], preferred_element_type=F32)

    cq_ref[...] = _rms(mm(0), gq_ref[...]).astype(BF16)
    ckv_ref[...] = _rms(mm(1), gkv_ref[...]).astype(BF16)
    kr_ref[...] = _rope(mm(2), cos, sin, 32).astype(BF16)
    qs_ref[...] = (_rope(mm(3), cos, sin, 32) * (SWA_HD ** -0.5)).astype(BF16)
    ks_ref[...] = _rope(mm(4), cos, sin, 32).astype(BF16)
    vs_ref[...] = mm(5).astype(BF16)


def _even_proj(x, g_pre, w, gq, gkv, tabs_cos, tabs_sin):
    seq, d = x.shape
    tm = min(seq, 1024)
    widths = [_EVEN_COLS[k + 1] - _EVEN_COLS[k] for k in range(6)]
    row = lambda i: (i, 0)
    fixed = lambda i: (0, 0)
    return pl.pallas_call(
        _even_proj_kernel,
        grid=(seq // tm,),
        in_specs=[pl.BlockSpec((tm, d), row),
                  pl.BlockSpec((1, d), fixed),
                  pl.BlockSpec(w.shape, fixed, pipeline_mode=pl.Buffered(1)),
                  pl.BlockSpec((1, 512), fixed),
                  pl.BlockSpec((1, 512), fixed),
                  pl.BlockSpec((tm, LANES), row),
                  pl.BlockSpec((tm, LANES), row)],
        out_specs=[pl.BlockSpec((tm, n), row) for n in widths],
        out_shape=[jax.ShapeDtypeStruct((seq, n), BF16) for n in widths],
        compiler_params=_params(("parallel",)),
        name="even_proj",
    )(x, g_pre[None, :], w, gq[None, :], gkv[None, :], tabs_cos, tabs_sin)


def _mla_up_kernel(cq_ref, ckv_ref, wq_ref, wk_ref, wvt_ref, cos_ref, sin_ref,
                   qn_ref, qr_ref, kn_ref, vt_ref, *, scale):
    cq, ckv = cq_ref[...], ckv_ref[...]
    n = MLA_HEADS * LANES
    qn = jnp.dot(cq, wq_ref[:, :n], preferred_element_type=F32)
    qr = jnp.dot(cq, wq_ref[:, n:], preferred_element_type=F32)
    qn_ref[...] = (qn * scale).astype(BF16)
    qr_ref[...] = (_rope(qr, cos_ref[...], sin_ref[...], 32) * scale).astype(BF16)
    kn_ref[...] = jnp.dot(ckv, wk_ref[...], preferred_element_type=F32).astype(BF16)
    vt = lax.dot_general(wvt_ref[...], ckv, (((1,), (1,)), ((), ())), preferred_element_type=F32)
    vt_ref[...] = vt.astype(BF16).reshape(vt_ref.shape)


def _mla_up(cq, ckv, wq, wk, wvt, tabs_cos, tabs_sin):
    seq = cq.shape[0]
    tm = min(seq, FLASH_TK)
    n = MLA_HEADS * LANES
    row = lambda i: (i, 0)
    fixed = lambda i: (0, 0)
    scale = (MLA_NOPE + MLA_ROPE) ** -0.5 * LOG2E
    return pl.pallas_call(
        functools.partial(_mla_up_kernel, scale=scale),
        grid=(seq // tm,),
        in_specs=[pl.BlockSpec((tm, MLA_Q_RANK), row),
                  pl.BlockSpec((tm, MLA_KV_RANK), row),
                  pl.BlockSpec(wq.shape, fixed),
                  pl.BlockSpec(wk.shape, fixed),
                  pl.BlockSpec(wvt.shape, fixed),
                  pl.BlockSpec((tm, LANES), row),
                  pl.BlockSpec((tm, LANES), row)],
        out_specs=[pl.BlockSpec((tm, n), row)] * 3
        + [pl.BlockSpec((MLA_HEADS, None, LANES, tm), lambda i: (0, i, 0, 0))],
        out_shape=[jax.ShapeDtypeStruct((seq, n), BF16)] * 3
        + [jax.ShapeDtypeStruct((MLA_HEADS, seq // tm, LANES, tm), BF16)],
        compiler_params=_params(("parallel",)),
        name="mla_up",
    )(cq, ckv, wq, wk, wvt, tabs_cos, tabs_sin)


def _softmax_pv_t(st, vt1, m_ref, acc_ref):
    m_old = m_ref[...]
    m_new = jnp.maximum(m_old, jnp.max(st, axis=0, keepdims=True))
    alpha = jnp.exp2(m_old - m_new)
    pt = jnp.exp2(st - m_new)
    acc_ref[...] = alpha * acc_ref[...] + jnp.dot(vt1, pt.astype(BF16), preferred_element_type=F32)
    m_ref[...] = m_new


def _flash_attend_t(qt, kv, i, tq, tk, stats, o_ref):
    nc = tq // T_CHUNK
    m_refs, acc_refs = stats[:nc], stats[nc:]
    for c in range(nc):
        m_refs[c][...] = jnp.full(m_refs[c].shape, -jnp.inf, F32)
        acc_refs[c][...] = jnp.zeros(acc_refs[c].shape, F32)
    sub = tq // tk
    qts = [qt[:, c * T_CHUNK:(c + 1) * T_CHUNK] for c in range(nc)]
    key = lax.broadcasted_iota(jnp.int32, (tk, T_CHUNK), 0)
    qry = lax.broadcasted_iota(jnp.int32, (tk, T_CHUNK), 1)
    ones = jnp.ones((ONES_ROWS, tk), BF16)

    def run(work, kvs, diagonal):
        vt1 = [jnp.concatenate([vt, ones], axis=0) for _, vt in kvs]

        def score(n):
            d, c = work[n]
            st = jnp.dot(kvs[d][0], qts[c], preferred_element_type=F32)
            off = c * T_CHUNK - d * tk
            if diagonal and off < tk:
                st = jnp.where(key <= qry + off, st, -jnp.inf)
            return st

        ahead = [score(n) for n in range(min(T_AHEAD, len(work)))]
        for n, (d, c) in enumerate(work):
            if n + T_AHEAD < len(work):
                ahead.append(score(n + T_AHEAD))
            _softmax_pv_t(ahead.pop(0), vt1[d], m_refs[c], acc_refs[c])

    def past(g, carry):
        run([(d, c) for d in range(sub) for c in range(nc)], [kv(g * sub + d) for d in range(sub)], False)
        return carry

    lax.fori_loop(0, i, past, 0)

    run([(d, c) for d in range(sub) for c in range(d * tk // T_CHUNK, nc)],
        [kv(i * sub + d) for d in range(sub)], True)

    out_t = jnp.concatenate([acc_refs[c][:LANES, :] / acc_refs[c][LANES:LANES + 1, :] for c in range(nc)], axis=1)
    o_ref[...] = out_t.T.astype(BF16)


def _flash_t_scratch(tq):
    nc = tq // T_CHUNK
    return [pltpu.VMEM((1, T_CHUNK), F32)] * nc + [pltpu.VMEM((LANES + ONES_ROWS, T_CHUNK), F32)] * nc


def _mla_attn_kernel(qn_ref, qr_ref, kn_ref, kr_ref, vt_ref, o_ref, *stats, tq, tk):
    i = pl.program_id(1)
    q = jnp.concatenate([qn_ref[...], qr_ref[...]], axis=1)
    qt = q.astype(F32).T.astype(BF16)

    def kv(j):
        rows = pl.ds(pl.multiple_of(j * tk, tk), tk)
        return jnp.concatenate([kn_ref[rows, :], kr_ref[rows, :]], axis=1), vt_ref[j]

    _flash_attend_t(qt, kv, i, tq, tk, stats, o_ref)


def _mla_attn(qn, qr, kn, kr, vt):
    seq = qn.shape[0]
    tq = min(seq, FLASH_TQ)
    tk = min(seq, FLASH_TK)
    nk = seq // tk
    return pl.pallas_call(
        functools.partial(_mla_attn_kernel, tq=tq, tk=tk),
        grid=(MLA_HEADS, seq // tq),
        in_specs=[pl.BlockSpec((tq, LANES), lambda h, i: (i, h)),
                  pl.BlockSpec((tq, LANES), lambda h, i: (i, h)),
                  pl.BlockSpec((seq, LANES), lambda h, i: (0, h)),
                  pl.BlockSpec((seq, LANES), lambda h, i: (0, 0)),
                  pl.BlockSpec((None, nk, LANES, tk), lambda h, i: (h, 0, 0, 0))],
        out_specs=pl.BlockSpec((tq, LANES), lambda h, i: (i, h)),
        out_shape=jax.ShapeDtypeStruct((seq, MLA_HEADS * MLA_V), BF16),
        scratch_shapes=_flash_t_scratch(tq),
        compiler_params=_params(("parallel", "arbitrary")),
        name="mla_attn",
    )(qn, qr, kn, kr, vt)


def _swa_kernel(sink_ref, q_ref, kc_ref, kp_ref, vc_ref, vp_ref, o_ref, *, tq):
    i = pl.program_id(0)
    nb = tq // SWA_BLOCK
    group = SWA_HEADS // SWA_KV_HEADS
    pairs = group // 2
    lane = lax.broadcasted_iota(jnp.int32, (2 * SWA_BLOCK, LANES), 1)
    shape = (pairs * SWA_BLOCK, 2 * SWA_BLOCK)
    qp = lax.broadcasted_iota(jnp.int32, shape, 0) % SWA_BLOCK + SWA_BLOCK
    kp = lax.broadcasted_iota(jnp.int32, shape, 1)
    band = (kp <= qp) & (qp - kp < SWA_BLOCK)
    out_lane = lax.broadcasted_iota(jnp.int32, (pairs * SWA_BLOCK, LANES), 1)
    key_row = lax.broadcasted_iota(jnp.int32, (4 * SWA_BLOCK, LANES), 0)
    key_lane = lax.broadcasted_iota(jnp.int32, (4 * SWA_BLOCK, LANES), 1)
    ones_by_head = jnp.where((key_row < 2 * SWA_BLOCK) == (key_lane < SWA_HD), 1.0, 0.0).astype(BF16)

    def split(t, c):
        mine = jnp.where((lane >= c * SWA_HD) & (lane < (c + 1) * SWA_HD), t, 0.0)
        other = pltpu.roll(mine, SWA_HD, 1)
        lo, hi = (mine, other) if c == 0 else (other, mine)
        return jnp.concatenate([lo, hi], axis=0).astype(BF16)

    for b in range(nb):
        rows = slice(b * SWA_BLOCK, (b + 1) * SWA_BLOCK)
        if b == 0:
            k_prev, v_prev = kp_ref[...], vp_ref[...]
        else:
            prev = slice((b - 1) * SWA_BLOCK, b * SWA_BLOCK)
            k_prev, v_prev = kc_ref[prev, :], vc_ref[prev, :]
        kw = jnp.concatenate([k_prev, kc_ref[rows, :]], axis=0).astype(F32)
        vw = jnp.concatenate([v_prev, vc_ref[rows, :]], axis=0).astype(F32)
        first_key = jnp.where(i * nb + b == 0, SWA_BLOCK, 0)
        valid = band & (kp >= first_key)
        for c in range(SWA_KV_HEADS):
            kcat = split(kw, c)
            vcat = split(vw, c)
            cols = [slice((c * pairs + a) * LANES, (c * pairs + a + 1) * LANES) for a in range(pairs)]
            q4 = jnp.concatenate([q_ref[rows, cs] for cs in cols], axis=0)
            s = lax.dot_general(q4, kcat, (((1,), (1,)), ((), ())), preferred_element_type=F32)
            ps, sink_terms = [], []
            for e in range(2):
                sink = jnp.concatenate(
                    [jnp.full((SWA_BLOCK, 1), sink_ref[c * group + 2 * a + e], F32) for a in range(pairs)], axis=0)
                se = jnp.where(valid, s[:, e * 2 * SWA_BLOCK:(e + 1) * 2 * SWA_BLOCK], -jnp.inf)
                m = jnp.maximum(jnp.max(se, axis=1, keepdims=True), sink)
                ps.append(jnp.exp(se - m).astype(BF16))
                sink_terms.append(jnp.exp(sink - m))
            pv = jnp.dot(jnp.concatenate(ps, axis=1), jnp.concatenate([vcat, ones_by_head], axis=1),
                         preferred_element_type=F32)
            den = pv[:, LANES:] + jnp.where(out_lane < SWA_HD, sink_terms[0], sink_terms[1])
            o4 = (pv[:, :LANES] / den).astype(BF16)
            for a, cs in enumerate(cols):
                o_ref[rows, cs] = o4[a * SWA_BLOCK:(a + 1) * SWA_BLOCK]


def _swa_attn(qs, ks, vs, sinks):
    seq = qs.shape[0]
    tq = min(seq, 512)
    nb = tq // SWA_BLOCK
    cur = lambda i: (i, 0)
    prev = lambda i: (jnp.maximum(i * nb - 1, 0), 0)
    return pl.pallas_call(
        functools.partial(_swa_kernel, tq=tq),
        grid=(seq // tq,),
        in_specs=[pl.BlockSpec(memory_space=pltpu.SMEM),
                  pl.BlockSpec((tq, SWA_HEADS * SWA_HD), cur),
                  pl.BlockSpec((tq, LANES), cur),
                  pl.BlockSpec((SWA_BLOCK, LANES), prev),
                  pl.BlockSpec((tq, LANES), cur),
                  pl.BlockSpec((SWA_BLOCK, LANES), prev)],
        out_specs=pl.BlockSpec((tq, SWA_HEADS * SWA_HD), cur),
        out_shape=jax.ShapeDtypeStruct((seq, SWA_HEADS * SWA_HD), BF16),
        compiler_params=_params(("parallel",)),
        name="swa_attn",
    )(sinks, qs, ks, ks, vs, vs)


def _out_proj_kernel(*refs, n_a):
    a_refs = refs[:n_a]
    w_ref, h_ref, gpost_ref, gnext_ref, hout_ref, hn_ref = refs[n_a:]
    tm = h_ref.shape[0]
    for rows in [slice(r * tm // 4, (r + 1) * tm // 4) for r in range(4)]:
        m = None
        k0 = 0
        for a_ref in a_refs:
            kw = a_ref.shape[1]
            part = jnp.dot(a_ref[rows, :], w_ref[k0:k0 + kw, :], preferred_element_type=F32)
            m = part if m is None else m + part
            k0 += kw
        h = h_ref[rows, :] + _rms(m, gpost_ref[...])
        hout_ref[rows, :] = h
        hn_ref[rows, :] = _rms(h, gnext_ref[...]).astype(BF16)


def _out_proj(a_list, w, h, g_post, g_next):
    seq, d = h.shape
    tm = min(seq, 512)
    row = lambda i: (i, 0)
    fixed = lambda i: (0, 0)
    return pl.pallas_call(
        functools.partial(_out_proj_kernel, n_a=len(a_list)),
        grid=(seq // tm,),
        in_specs=[pl.BlockSpec((tm, a.shape[1]), row) for a in a_list]
        + [pl.BlockSpec(w.shape, fixed), pl.BlockSpec((tm, d), row),
           pl.BlockSpec((1, d), fixed), pl.BlockSpec((1, d), fixed)],
        out_specs=[pl.BlockSpec((tm, d), row), pl.BlockSpec((tm, d), row)],
        out_shape=[jax.ShapeDtypeStruct((seq, d), F32), jax.ShapeDtypeStruct((seq, d), BF16)],
        compiler_params=_params(("parallel",)),
        name="out_proj",
    )(*a_list, w, h, g_post[None, :], g_next[None, :])


def _mlp_kernel(x_ref, wu_ref, wd_ref, h_ref, g_ref, o_ref, acc_ref):
    f = pl.program_id(1)

    @pl.when(f == 0)
    def _():
        acc_ref[...] = jnp.zeros(acc_ref.shape, F32)

    a = jnp.maximum(jnp.dot(x_ref[...], wu_ref[...], preferred_element_type=F32), 0.0)
    acc_ref[...] += jnp.dot((a * a).astype(BF16), wd_ref[...], preferred_element_type=F32)

    @pl.when(f == pl.num_programs(1) - 1)
    def _():
        o_ref[...] = h_ref[...] + _rms(acc_ref[...], g_ref[...])


def _mlp(xn, w_up, w_down, h, g_post):
    seq, d = h.shape
    d_ff = w_up.shape[1]
    tm = min(seq, 512)
    tf = min(d_ff, 1024)
    return pl.pallas_call(
        _mlp_kernel,
        grid=(seq // tm, d_ff // tf),
        in_specs=[pl.BlockSpec((tm, d), lambda i, f: (i, 0)),
                  pl.BlockSpec((d, tf), lambda i, f: (0, f)),
                  pl.BlockSpec((tf, d), lambda i, f: (f, 0)),
                  pl.BlockSpec((tm, d), lambda i, f: (i, 0)),
                  pl.BlockSpec((1, d), lambda i, f: (0, 0))],
        out_specs=pl.BlockSpec((tm, d), lambda i, f: (i, 0)),
        out_shape=jax.ShapeDtypeStruct((seq, d), F32),
        scratch_shapes=[pltpu.VMEM((tm, d), F32)],
        compiler_params=_params(("parallel", "arbitrary")),
        name="mlp",
    )(xn, w_up, w_down, h, g_post[None, :])


def _ple_kernel(h_ref, p_ref, wg_ref, b_ref, wp_ref, *rest, with_next):
    h = h_ref[...]
    z = jnp.dot(h.astype(BF16), wg_ref[...], preferred_element_type=F32) + b_ref[...]
    gate = 1.0 / (1.0 + jnp.exp(-z))
    e = jnp.dot(p_ref[...].astype(BF16), wp_ref[...], preferred_element_type=F32)
    out = h + gate * e
    if with_next:
        gnext_ref, o_ref, hn_ref = rest
        hn_ref[...] = _rms(out, gnext_ref[...]).astype(BF16)
    else:
        (o_ref,) = rest
    o_ref[...] = out


def _ple(h, p, w_gate, b_gate, w_proj, g_next=None):
    seq, d = h.shape
    tm = min(seq, 512)
    row = lambda i: (i, 0)
    fixed = lambda i: (0, 0)
    with_next = g_next is not None
    in_specs = [pl.BlockSpec((tm, d), row), pl.BlockSpec((tm, p.shape[1]), row),
                pl.BlockSpec(w_gate.shape, fixed), pl.BlockSpec((1, d), fixed),
                pl.BlockSpec(w_proj.shape, fixed)]
    args = [h, p, w_gate, b_gate[None, :], w_proj]
    out_specs = [pl.BlockSpec((tm, d), row)]
    out_shape = [jax.ShapeDtypeStruct((seq, d), F32)]
    if with_next:
        in_specs.append(pl.BlockSpec((1, d), fixed))
        args.append(g_next[None, :])
        out_specs.append(pl.BlockSpec((tm, d), row))
        out_shape.append(jax.ShapeDtypeStruct((seq, d), BF16))
    res = pl.pallas_call(
        functools.partial(_ple_kernel, with_next=with_next),
        grid=(seq // tm,),
        in_specs=in_specs, out_specs=out_specs, out_shape=out_shape,
        compiler_params=_params(("parallel",)),
        name="ple",
    )(*args)
    return res if with_next else (res[0], None)


def _odd_proj_kernel(x_ref, w_ref, cos_ref, sin_ref, o_ref, *rest, rope, scale, kmean, tm):
    t = jnp.dot(x_ref[...], w_ref[...], preferred_element_type=F32)
    if rope:
        t = _rope(t, cos_ref[...], sin_ref[...], MOBA_HD // 2)
    if kmean:
        (km_ref,) = rest
        for b in range(tm // MOBA_BLOCK):
            blk = t[b * MOBA_BLOCK:(b + 1) * MOBA_BLOCK, :]
            km_ref[b] = jnp.sum(blk, axis=0, keepdims=True) * (1.0 / MOBA_BLOCK)
    if scale != 1.0:
        t = t * scale
    o_ref[...] = t.astype(BF16)


def _odd_proj(xn, w, col, tabs_cos, tabs_sin, *, rope, scale=1.0, kmean=False):
    seq, d = xn.shape
    n = d
    tm = min(seq, 1024)
    row = lambda i: (i, 0)
    out_specs = [pl.BlockSpec((tm, n), row)]
    out_shape = [jax.ShapeDtypeStruct((seq, n), BF16)]
    if kmean:
        nb = tm // MOBA_BLOCK
        out_specs.append(pl.BlockSpec((nb, 1, n), lambda i: (i, 0, 0)))
        out_shape.append(jax.ShapeDtypeStruct((seq // MOBA_BLOCK, 1, n), F32))
    res = pl.pallas_call(
        functools.partial(_odd_proj_kernel, rope=rope, scale=scale, kmean=kmean, tm=tm),
        grid=(seq // tm,),
        in_specs=[pl.BlockSpec((tm, d), row),
                  pl.BlockSpec((d, n), lambda i: (0, col), pipeline_mode=pl.Buffered(1)),
                  pl.BlockSpec((tm, LANES), lambda i: (i, 1)),
                  pl.BlockSpec((tm, LANES), lambda i: (i, 1))],
        out_specs=out_specs, out_shape=out_shape,
        compiler_params=_params(("parallel",)),
        name="odd_proj",
    )(xn, w, tabs_cos, tabs_sin)
    return res


def _odd_vt_kernel(x_ref, wt_ref, vt_ref):
    vt = lax.dot_general(wt_ref[...], x_ref[...], (((1,), (1,)), ((), ())), preferred_element_type=F32)
    vt_ref[...] = vt.astype(BF16).reshape(vt_ref.shape)


def _odd_proj_vt(xn, wt):
    seq, d = xn.shape
    tm = min(seq, FLASH_TK)
    heads = wt.shape[0] // LANES
    return pl.pallas_call(
        _odd_vt_kernel,
        grid=(seq // tm,),
        in_specs=[pl.BlockSpec((tm, d), lambda i: (i, 0)), pl.BlockSpec(wt.shape, lambda i: (0, 0))],
        out_specs=pl.BlockSpec((heads, None, LANES, tm), lambda i: (0, i, 0, 0)),
        out_shape=jax.ShapeDtypeStruct((heads, seq // tm, LANES, tm), BF16),
        compiler_params=_params(("parallel",)),
        name="odd_proj_vt",
    )(xn, wt)


def _moba_kernel(q_ref, k_ref, vt_ref, km_ref, o_ref, *stats, tq, tk, nkb):
    i = pl.program_id(1)
    q = q_ref[...]

    km = km_ref[...]
    if nkb < 64:
        km = jnp.concatenate([km, jnp.zeros((64 - nkb, LANES), F32)], axis=0)
    km_hi = km.astype(BF16)
    km_mid = (km - km_hi.astype(F32)).astype(BF16)
    gate = lax.dot_general(jnp.concatenate([km_hi, km_mid], axis=1), jnp.concatenate([q, q], axis=1),
                           (((1,), (1,)), ((), ())), preferred_element_type=F32)

    blk = lax.broadcasted_iota(jnp.int32, (64, tq), 0)
    qblk = (i * tq + lax.broadcasted_iota(jnp.int32, (64, tq), 1)) // MOBA_BLOCK
    gm = jnp.where(blk < qblk, gate, -jnp.inf)
    sel = blk == qblk
    for r in range(MOBA_TOPK):
        mx = jnp.max(gm, axis=0, keepdims=True)
        idx = jnp.min(jnp.where(gm == mx, blk, 64), axis=0, keepdims=True)
        pick = (blk == idx) & (qblk > r)
        sel = sel | pick
        gm = jnp.where(pick, -jnp.inf, gm)
    bias_t = jnp.where(sel, 0.0, MASK_BIAS)
    qat = jnp.concatenate([q.astype(F32).T, bias_t, jnp.zeros((64, tq), F32)], axis=0).astype(BF16)

    kb_per_tile = tk // MOBA_BLOCK
    krow = lax.broadcasted_iota(jnp.int32, (tk, LANES), 0) // MOBA_BLOCK
    klane = lax.broadcasted_iota(jnp.int32, (tk, LANES), 1)

    def kv(j):
        rows = pl.ds(pl.multiple_of(j * tk, tk), tk)
        onehot = jnp.where(klane == krow + j * kb_per_tile, 1.0, 0.0).astype(BF16)
        return jnp.concatenate([k_ref[rows, :], onehot], axis=1), vt_ref[j]

    _flash_attend_t(qat, kv, i, tq, tk, stats, o_ref)


def _moba_attn(q, k, vt, kmean):
    seq = q.shape[0]
    nkb = seq // MOBA_BLOCK
    assert nkb <= 64 and nkb % 8 == 0, "the gate matmul stacks blocks on 64 sublanes"
    tq = min(seq, FLASH_TQ)
    tk = min(seq, FLASH_TK)
    nk = seq // tk
    return pl.pallas_call(
        functools.partial(_moba_kernel, tq=tq, tk=tk, nkb=nkb),
        grid=(MOBA_HEADS, seq // tq),
        in_specs=[pl.BlockSpec((tq, LANES), lambda h, i: (i, h)),
                  pl.BlockSpec((seq, LANES), lambda h, i: (0, h)),
                  pl.BlockSpec((None, nk, LANES, tk), lambda h, i: (h, 0, 0, 0)),
                  pl.BlockSpec((nkb, LANES), lambda h, i: (0, h))],
        out_specs=pl.BlockSpec((tq, LANES), lambda h, i: (i, h)),
        out_shape=jax.ShapeDtypeStruct((seq, MOBA_HEADS * MOBA_HD), BF16),
        scratch_shapes=_flash_t_scratch(tq),
        compiler_params=_params(("parallel", "arbitrary")),
        name="moba_attn",
    )(q, k, vt, kmean)


def _even_in_weight(w_in):
    d = w_in.shape[0]
    cq, ckv, kr, qs, ks, vs = jnp.split(w_in, [512, 1024, 1088, 2112, 2240], axis=1)
    return jnp.concatenate([cq, ckv, kr, jnp.zeros((d, 64), w_in.dtype), qs, ks, vs], axis=1).astype(BF16)


def _mla_q_weight(w_q_up):
    r = w_q_up.shape[0]
    w = w_q_up.reshape(r, MLA_HEADS, MLA_NOPE + MLA_ROPE)
    nope = w[:, :, :MLA_NOPE].reshape(r, MLA_HEADS * MLA_NOPE)
    rope = jnp.pad(w[:, :, MLA_NOPE:], ((0, 0), (0, 0), (0, LANES - MLA_ROPE))).reshape(r, MLA_HEADS * LANES)
    return jnp.concatenate([nope, rope], axis=1).astype(BF16)


def _mla_kv_weight(w_kv_up):
    r = w_kv_up.shape[0]
    w = w_kv_up.reshape(r, MLA_HEADS, MLA_NOPE + MLA_V)
    nope = w[:, :, :MLA_NOPE].reshape(r, MLA_HEADS * MLA_NOPE)
    val = w[:, :, MLA_NOPE:].reshape(r, MLA_HEADS * MLA_V)
    return nope.astype(BF16), val.T.astype(BF16)


def kernel(x, p, positions, even_pre_g, even_w_in, mla_q_norm_g, mla_w_q_up, mla_kv_norm_g, mla_w_kv_up, swa_sinks, even_w_out, even_post_g, odd_pre_g, moba_w_qkv, odd_w_out, odd_post_g, mlp_pre_g, mlp_w_up, mlp_w_down, mlp_post_g, ple_w_gate, ple_b_gate, ple_w_proj):
    batch, seq, d = x.shape
    assert batch == 1
    h = x.reshape(seq, d)
    tabs_cos, tabs_sin = _rope_tables(positions, seq)

    cq, ckv, kr, qs, ks, vs = _even_proj(h, even_pre_g[0], _even_in_weight(even_w_in[0]),
                                         mla_q_norm_g[0], mla_kv_norm_g[0], tabs_cos, tabs_sin)
    qn, qr, kn, vt = _mla_up(cq, ckv, _mla_q_weight(mla_w_q_up[0]), *_mla_kv_weight(mla_w_kv_up[0]),
                             tabs_cos, tabs_sin)
    o_mla = _mla_attn(qn, qr, kn, kr, vt)
    o_swa = _swa_attn(qs, ks, vs, swa_sinks[0])
    h, hn = _out_proj([o_mla, o_swa], even_w_out[0].astype(BF16), h, even_post_g[0], mlp_pre_g[0])
    h = _mlp(hn, mlp_w_up[0].astype(BF16), mlp_w_down[0].astype(BF16), h, mlp_post_g[0])
    h, hn = _ple(h, p[0, 0], ple_w_gate[0].astype(BF16), ple_b_gate[0], ple_w_proj[0].astype(BF16),
                 g_next=odd_pre_g[0])

    w_qkv = moba_w_qkv[0].astype(BF16)
    (q,) = _odd_proj(hn, w_qkv, 0, tabs_cos, tabs_sin, rope=True, scale=MOBA_HD ** -0.5 * LOG2E)
    k, kmean = _odd_proj(hn, w_qkv, 1, tabs_cos, tabs_sin, rope=True, kmean=True)
    vt = _odd_proj_vt(hn, moba_w_qkv[0][:, 2 * d:].T.astype(BF16))
    o = _moba_attn(q, k, vt, kmean.reshape(seq // MOBA_BLOCK, MOBA_HEADS * MOBA_HD))
    h, hn = _out_proj([o], odd_w_out[0].astype(BF16), h, odd_post_g[0], mlp_pre_g[1])
    h = _mlp(hn, mlp_w_up[1].astype(BF16), mlp_w_down[1].astype(BF16), h, mlp_post_g[1])
    h, _ = _ple(h, p[1, 0], ple_w_gate[1].astype(BF16), ple_b_gate[1], ple_w_proj[1].astype(BF16))
    return h.reshape(batch, seq, d)
```

```python
import functools

import jax
import jax.numpy as jnp
import numpy as np
from jax import lax
from jax.experimental import pallas as pl
from jax.experimental.pallas import tpu as pltpu

F32 = jnp.float32
BF16 = jnp.bfloat16

NORM_EPS = 1e-6
ROPE_THETA = 10000.0

MLA_HEADS = 8
MLA_Q_RANK = 512
MLA_KV_RANK = 512
MLA_NOPE = 128
MLA_ROPE = 64
MLA_V = 128
SWA_HEADS = 16
SWA_KV_HEADS = 2
SWA_HD = 64
SWA_BLOCK = 128
MOBA_HEADS = 16
MOBA_HD = 128
MOBA_BLOCK = 256
MOBA_TOPK = 3

LANES = 128
V7X_VMEM_BYTES = 64 * 1024 * 1024
VMEM_LIMIT = V7X_VMEM_BYTES * 7 // 8
MASK_BIAS = -1e30
LOG2E = 1.4426950408889634
FLASH_TQ = 4096
FLASH_TK = 512
T_CHUNK = 256
T_AHEAD = 4
ONES_ROWS = 16


def _params(sem):
    return pltpu.CompilerParams(dimension_semantics=sem, vmem_limit_bytes=VMEM_LIMIT)


def _rms(t, g):
    return t * lax.rsqrt(jnp.mean(t * t, axis=-1, keepdims=True) + NORM_EPS) * g


def _rope(t, cos, sin_signed, half):
    width = t.shape[1]
    reps = width // LANES
    if reps > 1:
        cos = jnp.concatenate([cos] * reps, axis=1)
        sin_signed = jnp.concatenate([sin_signed] * reps, axis=1)
    lane = lax.broadcasted_iota(jnp.int32, t.shape, 1)
    first = (lane % (2 * half)) < half
    partner = jnp.where(first, pltpu.roll(t, width - half, 1), pltpu.roll(t, half, 1))
    return t * cos + partner * sin_signed


def _tables_kernel(pos_ref, invf_ref, sign_ref, cos_ref, sin_ref):
    ang = pos_ref[...].astype(F32) * invf_ref[...]
    cos_ref[...] = jnp.cos(ang)
    sin_ref[...] = jnp.sin(ang) * sign_ref[...]


def _rope_tables(positions, seq):
    def inv_freq(d):
        half = d // 2
        return jnp.power(ROPE_THETA, -jnp.arange(half, dtype=F32) * (2.0 / d))

    f64, f128 = inv_freq(64), inv_freq(128)
    invf = jnp.concatenate([f64, f64, f64, f64, f128, f128])[None, :]
    sign = np.concatenate([-np.ones(32), np.ones(32), -np.ones(32), np.ones(32),
                           -np.ones(64), np.ones(64)]).astype(np.float32)[None, :]
    tm = min(seq, 1024)
    return pl.pallas_call(
        _tables_kernel,
        grid=(seq // tm,),
        in_specs=[pl.BlockSpec((tm, 1), lambda i: (i, 0)),
                  pl.BlockSpec((1, 256), lambda i: (0, 0)),
                  pl.BlockSpec((1, 256), lambda i: (0, 0))],
        out_specs=[pl.BlockSpec((tm, 256), lambda i: (i, 0)),
                   pl.BlockSpec((tm, 256), lambda i: (i, 0))],
        out_shape=[jax.ShapeDtypeStruct((seq, 256), F32)] * 2,
        compiler_params=_params(("parallel",)),
        name="rope_tables",
    )(positions.reshape(seq, 1), invf, jnp.asarray(sign))


_EVEN_COLS = (0, 512, 1024, 1152, 2176, 2304, 2432)


def _even_proj_kernel(x_ref, gpre_ref, w_ref, gq_ref, gkv_ref, cos_ref, sin_ref,
                      cq_ref, ckv_ref, kr_ref, qs_ref, ks_ref, vs_ref):
    x = _rms(x_ref[...], gpre_ref[...]).astype(BF16)
    cos, sin = cos_ref[...], sin_ref[...]
    c = _EVEN_COLS

    def mm(k):
        return jnp.dot(x, w_ref[:, c[k]:c[k + 1]], preferred_element_type=F32)

    cq_ref[...] = _rms(mm(0), gq_ref[...]).astype(BF16)
    ckv_ref[...] = _rms(mm(1), gkv_ref[...]).astype(BF16)
    kr_ref[...] = _rope(mm(2), cos, sin, 32).astype(BF16)
    qs_ref[...] = (_rope(mm(3), cos, sin, 32) * (SWA_HD ** -0.5)).astype(BF16)
    ks_ref[...] = _rope(mm(4), cos, sin, 32).astype(BF16)
    vs_ref[...] = mm(5).astype(BF16)


def _even_proj(x, g_pre, w, gq, gkv, tabs_cos, tabs_sin):
    seq, d = x.shape
    tm = min(seq, 1024)
    widths = [_EVEN_COLS[k + 1] - _EVEN_COLS[k] for k in range(6)]
    row = lambda i: (i, 0)
    fixed = lambda i: (0, 0)
    return pl.pallas_call(
        _even_proj_kernel,
        grid=(seq // tm,),
        in_specs=[pl.BlockSpec((tm, d), row),
                  pl.BlockSpec((1, d), fixed),
                  pl.BlockSpec(w.shape, fixed, pipeline_mode=pl.Buffered(1)),
                  pl.BlockSpec((1, 512), fixed),
                  pl.BlockSpec((1, 512), fixed),
                  pl.BlockSpec((tm, LANES), row),
                  pl.BlockSpec((tm, LANES), row)],
        out_specs=[pl.BlockSpec((tm, n), row) for n in widths],
        out_shape=[jax.ShapeDtypeStruct((seq, n), BF16) for n in widths],
        compiler_params=_params(("parallel",)),
        name="even_proj",
    )(x, g_pre[None, :], w, gq[None, :], gkv[None, :], tabs_cos, tabs_sin)


def _mla_up_kernel(cq_ref, ckv_ref, wq_ref, wk_ref, wvt_ref, cos_ref, sin_ref,
                   qn_ref, qr_ref, kn_ref, vt_ref, *, scale):
    cq, ckv = cq_ref[...], ckv_ref[...]
    n = MLA_HEADS * LANES
    qn = jnp.dot(cq, wq_ref[:, :n], preferred_element_type=F32)
    qr = jnp.dot(cq, wq_ref[:, n:], preferred_element_type=F32)
    qn_ref[...] = (qn * scale).astype(BF16)
    qr_ref[...] = (_rope(qr, cos_ref[...], sin_ref[...], 32) * scale).astype(BF16)
    kn_ref[...] = jnp.dot(ckv, wk_ref[...], preferred_element_type=F32).astype(BF16)
    vt = lax.dot_general(wvt_ref[...], ckv, (((1,), (1,)), ((), ())), preferred_element_type=F32)
    vt_ref[...] = vt.astype(BF16).reshape(vt_ref.shape)


def _mla_up(cq, ckv, wq, wk, wvt, tabs_cos, tabs_sin):
    seq = cq.shape[0]
    tm = min(seq, FLASH_TK)
    n = MLA_HEADS * LANES
    row = lambda i: (i, 0)
    fixed = lambda i: (0, 0)
    scale = (MLA_NOPE + MLA_ROPE) ** -0.5 * LOG2E
    return pl.pallas_call(
        functools.partial(_mla_up_kernel, scale=scale),
        grid=(seq // tm,),
        in_specs=[pl.BlockSpec((tm, MLA_Q_RANK), row),
                  pl.BlockSpec((tm, MLA_KV_RANK), row),
                  pl.BlockSpec(wq.shape, fixed),
                  pl.BlockSpec(wk.shape, fixed),
                  pl.BlockSpec(wvt.shape, fixed),
                  pl.BlockSpec((tm, LANES), row),
                  pl.BlockSpec((tm, LANES), row)],
        out_specs=[pl.BlockSpec((tm, n), row)] * 3
        + [pl.BlockSpec((MLA_HEADS, None, LANES, tm), lambda i: (0, i, 0, 0))],
        out_shape=[jax.ShapeDtypeStruct((seq, n), BF16)] * 3
        + [jax.ShapeDtypeStruct((MLA_HEADS, seq // tm, LANES, tm), BF16)],
        compiler_params=_params(("parallel",)),
        name="mla_up",
    )(cq, ckv, wq, wk, wvt, tabs_cos, tabs_sin)


def _softmax_pv_t(st, vt1, m_ref, acc_ref):
    m_old = m_ref[...]
    m_new = jnp.maximum(m_old, jnp.max(st, axis=0, keepdims=True))
    alpha = jnp.exp2(m_old - m_new)
    pt = jnp.exp2(st - m_new)
    acc_ref[...] = alpha * acc_ref[...] + jnp.dot(vt1, pt.astype(BF16), preferred_element_type=F32)
    m_ref[...] = m_new


def _flash_attend_t(qt, kv, i, tq, tk, stats, o_ref):
    nc = tq // T_CHUNK
    m_refs, acc_refs = stats[:nc], stats[nc:]
    for c in range(nc):
        m_refs[c][...] = jnp.full(m_refs[c].shape, -jnp.inf, F32)
        acc_refs[c][...] = jnp.zeros(acc_refs[c].shape, F32)
    sub = tq // tk
    qts = [qt[:, c * T_CHUNK:(c + 1) * T_CHUNK] for c in range(nc)]
    key = lax.broadcasted_iota(jnp.int32, (tk, T_CHUNK), 0)
    qry = lax.broadcasted_iota(jnp.int32, (tk, T_CHUNK), 1)
    ones = jnp.ones((ONES_ROWS, tk), BF16)

    def run(work, kvs, diagonal):
        vt1 = [jnp.concatenate([vt, ones], axis=0) for _, vt in kvs]

        def score(n):
            d, c = work[n]
            st = jnp.dot(kvs[d][0], qts[c], preferred_element_type=F32)
            off = c * T_CHUNK - d * tk
            if diagonal and off < tk:
                st = jnp.where(key <= qry + off, st, -jnp.inf)
            return st

        ahead = [score(n) for n in range(min(T_AHEAD, len(work)))]
        for n, (d, c) in enumerate(work):
            if n + T_AHEAD < len(work):
                ahead.append(score(n + T_AHEAD))
            _softmax_pv_t(ahead.pop(0), vt1[d], m_refs[c], acc_refs[c])

    def past(g, carry):
        run([(d, c) for d in range(sub) for c in range(nc)], [kv(g * sub + d) for d in range(sub)], False)
        return carry

    lax.fori_loop(0, i, past, 0)

    run([(d, c) for d in range(sub) for c in range(d * tk // T_CHUNK, nc)],
        [kv(i * sub + d) for d in range(sub)], True)

    out_t = jnp.concatenate([acc_refs[c][:LANES, :] / acc_refs[c][LANES:LANES + 1, :] for c in range(nc)], axis=1)
    o_ref[...] = out_t.T.astype(BF16)


def _flash_t_scratch(tq):
    nc = tq // T_CHUNK
    return [pltpu.VMEM((1, T_CHUNK), F32)] * nc + [pltpu.VMEM((LANES + ONES_ROWS, T_CHUNK), F32)] * nc


def _mla_attn_kernel(qn_ref, qr_ref, kn_ref, kr_ref, vt_ref, o_ref, *stats, tq, tk):
    i = pl.program_id(1)
    q = jnp.concatenate([qn_ref[...], qr_ref[...]], axis=1)
    qt = q.astype(F32).T.astype(BF16)

    def kv(j):
        rows = pl.ds(pl.multiple_of(j * tk, tk), tk)
        return jnp.concatenate([kn_ref[rows, :], kr_ref[rows, :]], axis=1), vt_ref[j]

    _flash_attend_t(qt, kv, i, tq, tk, stats, o_ref)


def _mla_attn(qn, qr, kn, kr, vt):
    seq = qn.shape[0]
    tq = min(seq, FLASH_TQ)
    tk = min(seq, FLASH_TK)
    nk = seq // tk
    return pl.pallas_call(
        functools.partial(_mla_attn_kernel, tq=tq, tk=tk),
        grid=(MLA_HEADS, seq // tq),
        in_specs=[pl.BlockSpec((tq, LANES), lambda h, i: (i, h)),
                  pl.BlockSpec((tq, LANES), lambda h, i: (i, h)),
                  pl.BlockSpec((seq, LANES), lambda h, i: (0, h)),
                  pl.BlockSpec((seq, LANES), lambda h, i: (0, 0)),
                  pl.BlockSpec((None, nk, LANES, tk), lambda h, i: (h, 0, 0, 0))],
        out_specs=pl.BlockSpec((tq, LANES), lambda h, i: (i, h)),
        out_shape=jax.ShapeDtypeStruct((seq, MLA_HEADS * MLA_V), BF16),
        scratch_shapes=_flash_t_scratch(tq),
        compiler_params=_params(("parallel", "arbitrary")),
        name="mla_attn",
    )(qn, qr, kn, kr, vt)


def _swa_kernel(sink_ref, q_ref, kc_ref, kp_ref, vc_ref, vp_ref, o_ref, *, tq):
    i = pl.program_id(0)
    nb = tq // SWA_BLOCK
    group = SWA_HEADS // SWA_KV_HEADS
    pairs = group // 2
    lane = lax.broadcasted_iota(jnp.int32, (2 * SWA_BLOCK, LANES), 1)
    shape = (pairs * SWA_BLOCK, 2 * SWA_BLOCK)
    qp = lax.broadcasted_iota(jnp.int32, shape, 0) % SWA_BLOCK + SWA_BLOCK
    kp = lax.broadcasted_iota(jnp.int32, shape, 1)
    band = (kp <= qp) & (qp - kp < SWA_BLOCK)
    out_lane = lax.broadcasted_iota(jnp.int32, (pairs * SWA_BLOCK, LANES), 1)
    key_row = lax.broadcasted_iota(jnp.int32, (4 * SWA_BLOCK, LANES), 0)
    key_lane = lax.broadcasted_iota(jnp.int32, (4 * SWA_BLOCK, LANES), 1)
    ones_by_head = jnp.where((key_row < 2 * SWA_BLOCK) == (key_lane < SWA_HD), 1.0, 0.0).astype(BF16)

    def split(t, c):
        mine = jnp.where((lane >= c * SWA_HD) & (lane < (c + 1) * SWA_HD), t, 0.0)
        other = pltpu.roll(mine, SWA_HD, 1)
        lo, hi = (mine, other) if c == 0 else (other, mine)
        return jnp.concatenate([lo, hi], axis=0).astype(BF16)

    for b in range(nb):
        rows = slice(b * SWA_BLOCK, (b + 1) * SWA_BLOCK)
        if b == 0:
            k_prev, v_prev = kp_ref[...], vp_ref[...]
        else:
            prev = slice((b - 1) * SWA_BLOCK, b * SWA_BLOCK)
            k_prev, v_prev = kc_ref[prev, :], vc_ref[prev, :]
        kw = jnp.concatenate([k_prev, kc_ref[rows, :]], axis=0).astype(F32)
        vw = jnp.concatenate([v_prev, vc_ref[rows, :]], axis=0).astype(F32)
        first_key = jnp.where(i * nb + b == 0, SWA_BLOCK, 0)
        valid = band & (kp >= first_key)
        for c in range(SWA_KV_HEADS):
            kcat = split(kw, c)
            vcat = split(vw, c)
            cols = [slice((c * pairs + a) * LANES, (c * pairs + a + 1) * LANES) for a in range(pairs)]
            q4 = jnp.concatenate([q_ref[rows, cs] for cs in cols], axis=0)
            s = lax.dot_general(q4, kcat, (((1,), (1,)), ((), ())), preferred_element_type=F32)
            ps, sink_terms = [], []
            for e in range(2):
                sink = jnp.concatenate(
                    [jnp.full((SWA_BLOCK, 1), sink_ref[c * group + 2 * a + e], F32) for a in range(pairs)], axis=0)
                se = jnp.where(valid, s[:, e * 2 * SWA_BLOCK:(e + 1) * 2 * SWA_BLOCK], -jnp.inf)
                m = jnp.maximum(jnp.max(se, axis=1, keepdims=True), sink)
                ps.append(jnp.exp(se - m).astype(BF16))
                sink_terms.append(jnp.exp(sink - m))
            pv = jnp.dot(jnp.concatenate(ps, axis=1), jnp.concatenate([vcat, ones_by_head], axis=1),
                         preferred_element_type=F32)
            den = pv[:, LANES:] + jnp.where(out_lane < SWA_HD, sink_terms[0], sink_terms[1])
            o4 = (pv[:, :LANES] / den).astype(BF16)
            for a, cs in enumerate(cols):
                o_ref[rows, cs] = o4[a * SWA_BLOCK:(a + 1) * SWA_BLOCK]


def _swa_attn(qs, ks, vs, sinks):
    seq = qs.shape[0]
    tq = min(seq, 512)
    nb = tq // SWA_BLOCK
    cur = lambda i: (i, 0)
    prev = lambda i: (jnp.maximum(i * nb - 1, 0), 0)
    return pl.pallas_call(
        functools.partial(_swa_kernel, tq=tq),
        grid=(seq // tq,),
        in_specs=[pl.BlockSpec(memory_space=pltpu.SMEM),
                  pl.BlockSpec((tq, SWA_HEADS * SWA_HD), cur),
                  pl.BlockSpec((tq, LANES), cur),
                  pl.BlockSpec((SWA_BLOCK, LANES), prev),
                  pl.BlockSpec((tq, LANES), cur),
                  pl.BlockSpec((SWA_BLOCK, LANES), prev)],
        out_specs=pl.BlockSpec((tq, SWA_HEADS * SWA_HD), cur),
        out_shape=jax.ShapeDtypeStruct((seq, SWA_HEADS * SWA_HD), BF16),
        compiler_params=_params(("parallel",)),
        name="swa_attn",
    )(sinks, qs, ks, ks, vs, vs)


def _out_proj_kernel(*refs, n_a):
    a_refs = refs[:n_a]
    w_ref, h_ref, gpost_ref, gnext_ref, hout_ref, hn_ref = refs[n_a:]
    tm = h_ref.shape[0]
    for rows in [slice(r * tm // 4, (r + 1) * tm // 4) for r in range(4)]:
        m = None
        k0 = 0
        for a_ref in a_refs:
            kw = a_ref.shape[1]
            part = jnp.dot(a_ref[rows, :], w_ref[k0:k0 + kw, :], preferred_element_type=F32)
            m = part if m is None else m + part
            k0 += kw
        h = h_ref[rows, :] + _rms(m, gpost_ref[...])
        hout_ref[rows, :] = h
        hn_ref[rows, :] = _rms(h, gnext_ref[...]).astype(BF16)


def _out_proj(a_list, w, h, g_post, g_next):
    seq, d = h.shape
    tm = min(seq, 512)
    row = lambda i: (i, 0)
    fixed = lambda i: (0, 0)
    return pl.pallas_call(
        functools.partial(_out_proj_kernel, n_a=len(a_list)),
        grid=(seq // tm,),
        in_specs=[pl.BlockSpec((tm, a.shape[1]), row) for a in a_list]
        + [pl.BlockSpec(w.shape, fixed), pl.BlockSpec((tm, d), row),
           pl.BlockSpec((1, d), fixed), pl.BlockSpec((1, d), fixed)],
        out_specs=[pl.BlockSpec((tm, d), row), pl.BlockSpec((tm, d), row)],
        out_shape=[jax.ShapeDtypeStruct((seq, d), F32), jax.ShapeDtypeStruct((seq, d), BF16)],
        compiler_params=_params(("parallel",)),
        name="out_proj",
    )(*a_list, w, h, g_post[None, :], g_next[None, :])


def _mlp_kernel(x_ref, wu_ref, wd_ref, h_ref, g_ref, o_ref, acc_ref):
    f = pl.program_id(1)

    @pl.when(f == 0)
    def _():
        acc_ref[...] = jnp.zeros(acc_ref.shape, F32)

    a = jnp.maximum(jnp.dot(x_ref[...], wu_ref[...], preferred_element_type=F32), 0.0)
    acc_ref[...] += jnp.dot((a * a).astype(BF16), wd_ref[...], preferred_element_type=F32)

    @pl.when(f == pl.num_programs(1) - 1)
    def _():
        o_ref[...] = h_ref[...] + _rms(acc_ref[...], g_ref[...])


def _mlp(xn, w_up, w_down, h, g_post):
    seq, d = h.shape
    d_ff = w_up.shape[1]
    tm = min(seq, 512)
    tf = min(d_ff, 1024)
    return pl.pallas_call(
        _mlp_kernel,
        grid=(seq // tm, d_ff // tf),
        in_specs=[pl.BlockSpec((tm, d), lambda i, f: (i, 0)),
                  pl.BlockSpec((d, tf), lambda i, f: (0, f)),
                  pl.BlockSpec((tf, d), lambda i, f: (f, 0)),
                  pl.BlockSpec((tm, d), lambda i, f: (i, 0)),
                  pl.BlockSpec((1, d), lambda i, f: (0, 0))],
        out_specs=pl.BlockSpec((tm, d), lambda i, f: (i, 0)),
        out_shape=jax.ShapeDtypeStruct((seq, d), F32),
        scratch_shapes=[pltpu.VMEM((tm, d), F32)],
        compiler_params=_params(("parallel", "arbitrary")),
        name="mlp",
    )(xn, w_up, w_down, h, g_post[None, :])


def _ple_kernel(h_ref, p_ref, wg_ref, b_ref, wp_ref, *rest, with_next):
    h = h_ref[...]
    z = jnp.dot(h.astype(BF16), wg_ref[...], preferred_element_type=F32) + b_ref[...]
    gate = 1.0 / (1.0 + jnp.exp(-z))
    e = jnp.dot(p_ref[...].astype(BF16), wp_ref[...], preferred_element_type=F32)
    out = h + gate * e
    if with_next:
        gnext_ref, o_ref, hn_ref = rest
        hn_ref[...] = _rms(out, gnext_ref[...]).astype(BF16)
    else:
        (o_ref,) = rest
    o_ref[...] = out


def _ple(h, p, w_gate, b_gate, w_proj, g_next=None):
    seq, d = h.shape
    tm = min(seq, 512)
    row = lambda i: (i, 0)
    fixed = lambda i: (0, 0)
    with_next = g_next is not None
    in_specs = [pl.BlockSpec((tm, d), row), pl.BlockSpec((tm, p.shape[1]), row),
                pl.BlockSpec(w_gate.shape, fixed), pl.BlockSpec((1, d), fixed),
                pl.BlockSpec(w_proj.shape, fixed)]
    args = [h, p, w_gate, b_gate[None, :], w_proj]
    out_specs = [pl.BlockSpec((tm, d), row)]
    out_shape = [jax.ShapeDtypeStruct((seq, d), F32)]
    if with_next:
        in_specs.append(pl.BlockSpec((1, d), fixed))
        args.append(g_next[None, :])
        out_specs.append(pl.BlockSpec((tm, d), row))
        out_shape.append(jax.ShapeDtypeStruct((seq, d), BF16))
    res = pl.pallas_call(
        functools.partial(_ple_kernel, with_next=with_next),
        grid=(seq // tm,),
        in_specs=in_specs, out_specs=out_specs, out_shape=out_shape,
        compiler_params=_params(("parallel",)),
        name="ple",
    )(*args)
    return res if with_next else (res[0], None)


def _odd_proj_kernel(x_ref, w_ref, cos_ref, sin_ref, o_ref, *rest, rope, scale, kmean, tm):
    t = jnp.dot(x_ref[...], w_ref[...], preferred_element_type=F32)
    if rope:
        t = _rope(t, cos_ref[...], sin_ref[...], MOBA_HD // 2)
    if kmean:
        (km_ref,) = rest
        for b in range(tm // MOBA_BLOCK):
            blk = t[b * MOBA_BLOCK:(b + 1) * MOBA_BLOCK, :]
            km_ref[b] = jnp.sum(blk, axis=0, keepdims=True) * (1.0 / MOBA_BLOCK)
    if scale != 1.0:
        t = t * scale
    if not kmean:
        o_ref[...] = t.astype(BF16)
        return
    tb = t.astype(BF16)
    row_blk = (pl.program_id(0) * tm + lax.broadcasted_iota(jnp.int32, (tm, LANES), 0)) // MOBA_BLOCK
    onehot = jnp.where(lax.broadcasted_iota(jnp.int32, (tm, LANES), 1) == row_blk, 1.0, 0.0).astype(BF16)
    for h in range(tb.shape[1] // LANES):
        o_ref[:, 2 * h * LANES:(2 * h + 1) * LANES] = tb[:, h * LANES:(h + 1) * LANES]
        o_ref[:, (2 * h + 1) * LANES:(2 * h + 2) * LANES] = onehot


def _odd_proj(xn, w, col, tabs_cos, tabs_sin, *, rope, scale=1.0, kmean=False):
    seq, d = xn.shape
    n = d
    tm = min(seq, 1024)
    row = lambda i: (i, 0)
    n_out = 2 * n if kmean else n
    out_specs = [pl.BlockSpec((tm, n_out), row)]
    out_shape = [jax.ShapeDtypeStruct((seq, n_out), BF16)]
    if kmean:
        nb = tm // MOBA_BLOCK
        out_specs.append(pl.BlockSpec((nb, 1, n), lambda i: (i, 0, 0)))
        out_shape.append(jax.ShapeDtypeStruct((seq // MOBA_BLOCK, 1, n), F32))
    res = pl.pallas_call(
        functools.partial(_odd_proj_kernel, rope=rope, scale=scale, kmean=kmean, tm=tm),
        grid=(seq // tm,),
        in_specs=[pl.BlockSpec((tm, d), row),
                  pl.BlockSpec((d, n), lambda i: (0, col), pipeline_mode=pl.Buffered(1)),
                  pl.BlockSpec((tm, LANES), lambda i: (i, 1)),
                  pl.BlockSpec((tm, LANES), lambda i: (i, 1))],
        out_specs=out_specs, out_shape=out_shape,
        compiler_params=_params(("parallel",)),
        name="odd_proj",
    )(xn, w, tabs_cos, tabs_sin)
    return res


def _odd_vt_kernel(x_ref, wt_ref, vt_ref):
    vt = lax.dot_general(wt_ref[...], x_ref[...], (((1,), (1,)), ((), ())), preferred_element_type=F32)
    vt_ref[...] = vt.astype(BF16).reshape(vt_ref.shape)


def _odd_proj_vt(xn, wt):
    seq, d = xn.shape
    tm = min(seq, FLASH_TK)
    heads = wt.shape[0] // LANES
    return pl.pallas_call(
        _odd_vt_kernel,
        grid=(seq // tm,),
        in_specs=[pl.BlockSpec((tm, d), lambda i: (i, 0)), pl.BlockSpec(wt.shape, lambda i: (0, 0))],
        out_specs=pl.BlockSpec((heads, None, LANES, tm), lambda i: (0, i, 0, 0)),
        out_shape=jax.ShapeDtypeStruct((heads, seq // tm, LANES, tm), BF16),
        compiler_params=_params(("parallel",)),
        name="odd_proj_vt",
    )(xn, wt)


def _moba_kernel(q_ref, k_ref, vt_ref, km_ref, o_ref, *stats, tq, tk, nkb):
    i = pl.program_id(1)
    q = q_ref[...]

    km = km_ref[...]
    if nkb < 64:
        km = jnp.concatenate([km, jnp.zeros((64 - nkb, LANES), F32)], axis=0)
    km_hi = km.astype(BF16)
    km_mid = (km - km_hi.astype(F32)).astype(BF16)
    gate = lax.dot_general(jnp.concatenate([km_hi, km_mid], axis=1), jnp.concatenate([q, q], axis=1),
                           (((1,), (1,)), ((), ())), preferred_element_type=F32)

    blk = lax.broadcasted_iota(jnp.int32, (64, tq), 0)
    qblk = (i * tq + lax.broadcasted_iota(jnp.int32, (64, tq), 1)) // MOBA_BLOCK
    gm = jnp.where(blk < qblk, gate, -jnp.inf)
    sel = blk == qblk
    for r in range(MOBA_TOPK):
        mx = jnp.max(gm, axis=0, keepdims=True)
        idx = jnp.min(jnp.where(gm == mx, blk, 64), axis=0, keepdims=True)
        pick = (blk == idx) & (qblk > r)
        sel = sel | pick
        gm = jnp.where(pick, -jnp.inf, gm)
    bias_t = jnp.where(sel, 0.0, MASK_BIAS)
    qat = jnp.concatenate([q.astype(F32).T, bias_t, jnp.zeros((64, tq), F32)], axis=0).astype(BF16)

    def kv(j):
        rows = pl.ds(pl.multiple_of(j * tk, tk), tk)
        return k_ref[rows, :], vt_ref[j]

    _flash_attend_t(qat, kv, i, tq, tk, stats, o_ref)


def _moba_attn(q, k, vt, kmean):
    seq = q.shape[0]
    nkb = seq // MOBA_BLOCK
    assert nkb <= 64 and nkb % 8 == 0, "the gate matmul stacks blocks on 64 sublanes"
    tq = min(seq, FLASH_TQ)
    tk = min(seq, FLASH_TK)
    nk = seq // tk
    return pl.pallas_call(
        functools.partial(_moba_kernel, tq=tq, tk=tk, nkb=nkb),
        grid=(MOBA_HEADS, seq // tq),
        in_specs=[pl.BlockSpec((tq, LANES), lambda h, i: (i, h)),
                  pl.BlockSpec((seq, 2 * LANES), lambda h, i: (0, h)),
                  pl.BlockSpec((None, nk, LANES, tk), lambda h, i: (h, 0, 0, 0)),
                  pl.BlockSpec((nkb, LANES), lambda h, i: (0, h))],
        out_specs=pl.BlockSpec((tq, LANES), lambda h, i: (i, h)),
        out_shape=jax.ShapeDtypeStruct((seq, MOBA_HEADS * MOBA_HD), BF16),
        scratch_shapes=_flash_t_scratch(tq),
        compiler_params=_params(("parallel", "arbitrary")),
        name="moba_attn",
    )(q, k, vt, kmean)


def _even_in_weight(w_in):
    d = w_in.shape[0]
    cq, ckv, kr, qs, ks, vs = jnp.split(w_in, [512, 1024, 1088, 2112, 2240], axis=1)
    return jnp.concatenate([cq, ckv, kr, jnp.zeros((d, 64), w_in.dtype), qs, ks, vs], axis=1).astype(BF16)


def _mla_q_weight(w_q_up):
    r = w_q_up.shape[0]
    w = w_q_up.reshape(r, MLA_HEADS, MLA_NOPE + MLA_ROPE)
    nope = w[:, :, :MLA_NOPE].reshape(r, MLA_HEADS * MLA_NOPE)
    rope = jnp.pad(w[:, :, MLA_NOPE:], ((0, 0), (0, 0), (0, LANES - MLA_ROPE))).reshape(r, MLA_HEADS * LANES)
    return jnp.concatenate([nope, rope], axis=1).astype(BF16)


def _mla_kv_weight(w_kv_up):
    r = w_kv_up.shape[0]
    w = w_kv_up.reshape(r, MLA_HEADS, MLA_NOPE + MLA_V)
    nope = w[:, :, :MLA_NOPE].reshape(r, MLA_HEADS * MLA_NOPE)
    val = w[:, :, MLA_NOPE:].reshape(r, MLA_HEADS * MLA_V)
    return nope.astype(BF16), val.T.astype(BF16)


def kernel(x, p, positions, even_pre_g, even_w_in, mla_q_norm_g, mla_w_q_up, mla_kv_norm_g, mla_w_kv_up, swa_sinks, even_w_out, even_post_g, odd_pre_g, moba_w_qkv, odd_w_out, odd_post_g, mlp_pre_g, mlp_w_up, mlp_w_down, mlp_post_g, ple_w_gate, ple_b_gate, ple_w_proj):
    batch, seq, d = x.shape
    assert batch == 1
    h = x.reshape(seq, d)
    tabs_cos, tabs_sin = _rope_tables(positions, seq)

    cq, ckv, kr, qs, ks, vs = _even_proj(h, even_pre_g[0], _even_in_weight(even_w_in[0]),
                                         mla_q_norm_g[0], mla_kv_norm_g[0], tabs_cos, tabs_sin)
    qn, qr, kn, vt = _mla_up(cq, ckv, _mla_q_weight(mla_w_q_up[0]), *_mla_kv_weight(mla_w_kv_up[0]),
                             tabs_cos, tabs_sin)
    o_mla = _mla_attn(qn, qr, kn, kr, vt)
    o_swa = _swa_attn(qs, ks, vs, swa_sinks[0])
    h, hn = _out_proj([o_mla, o_swa], even_w_out[0].astype(BF16), h, even_post_g[0], mlp_pre_g[0])
    h = _mlp(hn, mlp_w_up[0].astype(BF16), mlp_w_down[0].astype(BF16), h, mlp_post_g[0])
    h, hn = _ple(h, p[0, 0], ple_w_gate[0].astype(BF16), ple_b_gate[0], ple_w_proj[0].astype(BF16),
                 g_next=odd_pre_g[0])

    w_qkv = moba_w_qkv[0].astype(BF16)
    (q,) = _odd_proj(hn, w_qkv, 0, tabs_cos, tabs_sin, rope=True, scale=MOBA_HD ** -0.5 * LOG2E)
    k, kmean = _odd_proj(hn, w_qkv, 1, tabs_cos, tabs_sin, rope=True, kmean=True)
    vt = _odd_proj_vt(hn, moba_w_qkv[0][:, 2 * d:].T.astype(BF16))
    o = _moba_attn(q, k, vt, kmean.reshape(seq // MOBA_BLOCK, MOBA_HEADS * MOBA_HD))
    h, hn = _out_proj([o], odd_w_out[0].astype(BF16), h, odd_post_g[0], mlp_pre_g[1])
    h = _mlp(hn, mlp_w_up[1].astype(BF16), mlp_w_down[1].astype(BF16), h, mlp_post_g[1])
    h, _ = _ple(h, p[1, 0], ple_w_gate[1].astype(BF16), ple_b_gate[1], ple_w_proj[1].astype(BF16))
    return h.reshape(batch, seq, d)
```

```python
import functools

import jax
import jax.numpy as jnp
import numpy as np
from jax import lax
from jax.experimental import pallas as pl
from jax.experimental.pallas import tpu as pltpu

F32 = jnp.float32
BF16 = jnp.bfloat16

NORM_EPS = 1e-6
ROPE_THETA = 10000.0

MLA_HEADS = 8
MLA_Q_RANK = 512
MLA_KV_RANK = 512
MLA_NOPE = 128
MLA_ROPE = 64
MLA_V = 128
SWA_HEADS = 16
SWA_KV_HEADS = 2
SWA_HD = 64
SWA_BLOCK = 128
MOBA_HEADS = 16
MOBA_HD = 128
MOBA_BLOCK = 256
MOBA_TOPK = 3

LANES = 128
V7X_VMEM_BYTES = 64 * 1024 * 1024
VMEM_LIMIT = V7X_VMEM_BYTES * 7 // 8
MASK_BIAS = -1e30
LOG2E = 1.4426950408889634
FLASH_TQ = 4096
FLASH_TK = 512
T_CHUNK = 256
T_AHEAD = 4
ONES_ROWS = 16


def _params(sem):
    return pltpu.CompilerParams(dimension_semantics=sem, vmem_limit_bytes=VMEM_LIMIT)


def _rms(t, g):
    return t * lax.rsqrt(jnp.mean(t * t, axis=-1, keepdims=True) + NORM_EPS) * g


def _rope(t, cos, sin_signed, half):
    width = t.shape[1]
    reps = width // LANES
    if reps > 1:
        cos = jnp.concatenate([cos] * reps, axis=1)
        sin_signed = jnp.concatenate([sin_signed] * reps, axis=1)
    lane = lax.broadcasted_iota(jnp.int32, t.shape, 1)
    first = (lane % (2 * half)) < half
    partner = jnp.where(first, pltpu.roll(t, width - half, 1), pltpu.roll(t, half, 1))
    return t * cos + partner * sin_signed


def _tables_kernel(pos_ref, invf_ref, sign_ref, cos_ref, sin_ref):
    ang = pos_ref[...].astype(F32) * invf_ref[...]
    cos_ref[...] = jnp.cos(ang)
    sin_ref[...] = jnp.sin(ang) * sign_ref[...]


def _rope_tables(positions, seq):
    def inv_freq(d):
        half = d // 2
        return jnp.power(ROPE_THETA, -jnp.arange(half, dtype=F32) * (2.0 / d))

    f64, f128 = inv_freq(64), inv_freq(128)
    invf = jnp.concatenate([f64, f64, f64, f64, f128, f128])[None, :]
    sign = np.concatenate([-np.ones(32), np.ones(32), -np.ones(32), np.ones(32),
                           -np.ones(64), np.ones(64)]).astype(np.float32)[None, :]
    tm = min(seq, 1024)
    return pl.pallas_call(
        _tables_kernel,
        grid=(seq // tm,),
        in_specs=[pl.BlockSpec((tm, 1), lambda i: (i, 0)),
                  pl.BlockSpec((1, 256), lambda i: (0, 0)),
                  pl.BlockSpec((1, 256), lambda i: (0, 0))],
        out_specs=[pl.BlockSpec((tm, 256), lambda i: (i, 0)),
                   pl.BlockSpec((tm, 256), lambda i: (i, 0))],
        out_shape=[jax.ShapeDtypeStruct((seq, 256), F32)] * 2,
        compiler_params=_params(("parallel",)),
        name="rope_tables",
    )(positions.reshape(seq, 1), invf, jnp.asarray(sign))


_EVEN_COLS = (0, 512, 1024, 1152, 2176, 2304, 2432)


def _even_proj_kernel(x_ref, gpre_ref, w_ref, gq_ref, gkv_ref, cos_ref, sin_ref,
                      cq_ref, ckv_ref, kr_ref, qs_ref, ks_ref, vs_ref):
    x = _rms(x_ref[...], gpre_ref[...]).astype(BF16)
    cos, sin = cos_ref[...], sin_ref[...]
    c = _EVEN_COLS

    def mm(k):
        return jnp.dot(x, w_ref[:, c[k]:c[k + 1]], preferred_element_type=F32)

    cq_ref[...] = _rms(mm(0), gq_ref[...]).astype(BF16)
    ckv_ref[...] = _rms(mm(1), gkv_ref[...]).astype(BF16)
    kr_ref[...] = _rope(mm(2), cos, sin, 32).astype(BF16)
    qs_ref[...] = (_rope(mm(3), cos, sin, 32) * (SWA_HD ** -0.5)).astype(BF16)
    ks_ref[...] = _rope(mm(4), cos, sin, 32).astype(BF16)
    vs_ref[...] = mm(5).astype(BF16)


def _even_proj(x, g_pre, w, gq, gkv, tabs_cos, tabs_sin):
    seq, d = x.shape
    tm = min(seq, 1024)
    widths = [_EVEN_COLS[k + 1] - _EVEN_COLS[k] for k in range(6)]
    row = lambda i: (i, 0)
    fixed = lambda i: (0, 0)
    return pl.pallas_call(
        _even_proj_kernel,
        grid=(seq // tm,),
        in_specs=[pl.BlockSpec((tm, d), row),
                  pl.BlockSpec((1, d), fixed),
                  pl.BlockSpec(w.shape, fixed, pipeline_mode=pl.Buffered(1)),
                  pl.BlockSpec((1, 512), fixed),
                  pl.BlockSpec((1, 512), fixed),
                  pl.BlockSpec((tm, LANES), row),
                  pl.BlockSpec((tm, LANES), row)],
        out_specs=[pl.BlockSpec((tm, n), row) for n in widths],
        out_shape=[jax.ShapeDtypeStruct((seq, n), BF16) for n in widths],
        compiler_params=_params(("parallel",)),
        name="even_proj",
    )(x, g_pre[None, :], w, gq[None, :], gkv[None, :], tabs_cos, tabs_sin)


def _mla_up_kernel(cq_ref, ckv_ref, wq_ref, wk_ref, wvt_ref, cos_ref, sin_ref,
                   qn_ref, qr_ref, kn_ref, vt_ref, *, scale):
    cq, ckv = cq_ref[...], ckv_ref[...]
    n = MLA_HEADS * LANES
    qn = jnp.dot(cq, wq_ref[:, :n], preferred_element_type=F32)
    qr = jnp.dot(cq, wq_ref[:, n:], preferred_element_type=F32)
    qn_ref[...] = (qn * scale).astype(BF16)
    qr_ref[...] = (_rope(qr, cos_ref[...], sin_ref[...], 32) * scale).astype(BF16)
    kn_ref[...] = jnp.dot(ckv, wk_ref[...], preferred_element_type=F32).astype(BF16)
    vt = lax.dot_general(wvt_ref[...], ckv, (((1,), (1,)), ((), ())), preferred_element_type=F32)
    vt_ref[...] = vt.astype(BF16).reshape(vt_ref.shape)


def _mla_up(cq, ckv, wq, wk, wvt, tabs_cos, tabs_sin):
    seq = cq.shape[0]
    tm = min(seq, FLASH_TK)
    n = MLA_HEADS * LANES
    row = lambda i: (i, 0)
    fixed = lambda i: (0, 0)
    scale = (MLA_NOPE + MLA_ROPE) ** -0.5 * LOG2E
    return pl.pallas_call(
        functools.partial(_mla_up_kernel, scale=scale),
        grid=(seq // tm,),
        in_specs=[pl.BlockSpec((tm, MLA_Q_RANK), row),
                  pl.BlockSpec((tm, MLA_KV_RANK), row),
                  pl.BlockSpec(wq.shape, fixed),
                  pl.BlockSpec(wk.shape, fixed),
                  pl.BlockSpec(wvt.shape, fixed),
                  pl.BlockSpec((tm, LANES), row),
                  pl.BlockSpec((tm, LANES), row)],
        out_specs=[pl.BlockSpec((tm, n), row)] * 3
        + [pl.BlockSpec((MLA_HEADS, None, LANES, tm), lambda i: (0, i, 0, 0))],
        out_shape=[jax.ShapeDtypeStruct((seq, n), BF16)] * 3
        + [jax.ShapeDtypeStruct((MLA_HEADS, seq // tm, LANES, tm), BF16)],
        compiler_params=_params(("parallel",)),
        name="mla_up",
    )(cq, ckv, wq, wk, wvt, tabs_cos, tabs_sin)


def _softmax_pv_t(st, vt1, m_ref, acc_ref):
    m_old = m_ref[...]
    m_new = jnp.maximum(m_old, jnp.max(st, axis=0, keepdims=True))
    alpha = jnp.exp2(m_old - m_new)
    pt = jnp.exp2(st - m_new)
    acc_ref[...] = alpha * acc_ref[...] + jnp.dot(vt1, pt.astype(BF16), preferred_element_type=F32)
    m_ref[...] = m_new


def _flash_attend_t(qt, kv, i, tq, tk, stats, o_ref):
    nc = tq // T_CHUNK
    m_refs, acc_refs = stats[:nc], stats[nc:]
    for c in range(nc):
        m_refs[c][...] = jnp.full(m_refs[c].shape, -jnp.inf, F32)
        acc_refs[c][...] = jnp.zeros(acc_refs[c].shape, F32)
    sub = tq // tk
    qts = [qt[:, c * T_CHUNK:(c + 1) * T_CHUNK] for c in range(nc)]
    key = lax.broadcasted_iota(jnp.int32, (tk, T_CHUNK), 0)
    qry = lax.broadcasted_iota(jnp.int32, (tk, T_CHUNK), 1)
    ones = jnp.ones((ONES_ROWS, tk), BF16)

    def run(work, kvs, diagonal):
        vt1 = [jnp.concatenate([vt, ones], axis=0) for _, vt in kvs]

        def score(n):
            d, c = work[n]
            st = jnp.dot(kvs[d][0], qts[c], preferred_element_type=F32)
            off = c * T_CHUNK - d * tk
            if diagonal and off < tk:
                st = jnp.where(key <= qry + off, st, -jnp.inf)
            return st

        ahead = [score(n) for n in range(min(T_AHEAD, len(work)))]
        for n, (d, c) in enumerate(work):
            if n + T_AHEAD < len(work):
                ahead.append(score(n + T_AHEAD))
            _softmax_pv_t(ahead.pop(0), vt1[d], m_refs[c], acc_refs[c])

    def past(g, carry):
        run([(d, c) for d in range(sub) for c in range(nc)], [kv(g * sub + d) for d in range(sub)], False)
        return carry

    lax.fori_loop(0, i, past, 0)

    run([(d, c) for d in range(sub) for c in range(d * tk // T_CHUNK, nc)],
        [kv(i * sub + d) for d in range(sub)], True)

    out_t = jnp.concatenate([acc_refs[c][:LANES, :] / acc_refs[c][LANES:LANES + 1, :] for c in range(nc)], axis=1)
    o_ref[...] = out_t.T.astype(BF16)


def _flash_t_scratch(tq):
    nc = tq // T_CHUNK
    return [pltpu.VMEM((1, T_CHUNK), F32)] * nc + [pltpu.VMEM((LANES + ONES_ROWS, T_CHUNK), F32)] * nc


def _mla_attn_kernel(qn_ref, qr_ref, kn_ref, kr_ref, vt_ref, o_ref, *stats, tq, tk):
    i = pl.program_id(1)
    q = jnp.concatenate([qn_ref[...], qr_ref[...]], axis=1)
    qt = q.astype(F32).T.astype(BF16)

    def kv(j):
        rows = pl.ds(pl.multiple_of(j * tk, tk), tk)
        return jnp.concatenate([kn_ref[rows, :], kr_ref[rows, :]], axis=1), vt_ref[j]

    _flash_attend_t(qt, kv, i, tq, tk, stats, o_ref)


def _mla_attn(qn, qr, kn, kr, vt):
    seq = qn.shape[0]
    tq = min(seq, FLASH_TQ)
    tk = min(seq, FLASH_TK)
    nk = seq // tk
    return pl.pallas_call(
        functools.partial(_mla_attn_kernel, tq=tq, tk=tk),
        grid=(MLA_HEADS, seq // tq),
        in_specs=[pl.BlockSpec((tq, LANES), lambda h, i: (i, h)),
                  pl.BlockSpec((tq, LANES), lambda h, i: (i, h)),
                  pl.BlockSpec((seq, LANES), lambda h, i: (0, h)),
                  pl.BlockSpec((seq, LANES), lambda h, i: (0, 0)),
                  pl.BlockSpec((None, nk, LANES, tk), lambda h, i: (h, 0, 0, 0))],
        out_specs=pl.BlockSpec((tq, LANES), lambda h, i: (i, h)),
        out_shape=jax.ShapeDtypeStruct((seq, MLA_HEADS * MLA_V), BF16),
        scratch_shapes=_flash_t_scratch(tq),
        compiler_params=_params(("parallel", "arbitrary")),
        name="mla_attn",
    )(qn, qr, kn, kr, vt)


def _swa_kernel(sink_ref, q_ref, kc_ref, kp_ref, vc_ref, vp_ref, o_ref, *, tq):
    i = pl.program_id(0)
    nb = tq // SWA_BLOCK
    group = SWA_HEADS // SWA_KV_HEADS
    pairs = group // 2
    lane = lax.broadcasted_iota(jnp.int32, (2 * SWA_BLOCK, LANES), 1)
    shape = (pairs * SWA_BLOCK, 2 * SWA_BLOCK)
    qp = lax.broadcasted_iota(jnp.int32, shape, 0) % SWA_BLOCK + SWA_BLOCK
    kp = lax.broadcasted_iota(jnp.int32, shape, 1)
    band = (kp <= qp) & (qp - kp < SWA_BLOCK)
    out_lane = lax.broadcasted_iota(jnp.int32, (pairs * SWA_BLOCK, LANES), 1)
    key_row = lax.broadcasted_iota(jnp.int32, (4 * SWA_BLOCK, LANES), 0)
    key_lane = lax.broadcasted_iota(jnp.int32, (4 * SWA_BLOCK, LANES), 1)
    ones_by_head = jnp.where((key_row < 2 * SWA_BLOCK) == (key_lane < SWA_HD), 1.0, 0.0).astype(BF16)

    def split(t, c):
        mine = jnp.where((lane >= c * SWA_HD) & (lane < (c + 1) * SWA_HD), t, 0.0)
        other = pltpu.roll(mine, SWA_HD, 1)
        lo, hi = (mine, other) if c == 0 else (other, mine)
        return jnp.concatenate([lo, hi], axis=0).astype(BF16)

    for b in range(nb):
        rows = slice(b * SWA_BLOCK, (b + 1) * SWA_BLOCK)
        if b == 0:
            k_prev, v_prev = kp_ref[...], vp_ref[...]
        else:
            prev = slice((b - 1) * SWA_BLOCK, b * SWA_BLOCK)
            k_prev, v_prev = kc_ref[prev, :], vc_ref[prev, :]
        kw = jnp.concatenate([k_prev, kc_ref[rows, :]], axis=0).astype(F32)
        vw = jnp.concatenate([v_prev, vc_ref[rows, :]], axis=0).astype(F32)
        first_key = jnp.where(i * nb + b == 0, SWA_BLOCK, 0)
        valid = band & (kp >= first_key)
        for c in range(SWA_KV_HEADS):
            kcat = split(kw, c)
            vcat = split(vw, c)
            cols = [slice((c * pairs + a) * LANES, (c * pairs + a + 1) * LANES) for a in range(pairs)]
            q4 = jnp.concatenate([q_ref[rows, cs] for cs in cols], axis=0)
            s = lax.dot_general(q4, kcat, (((1,), (1,)), ((), ())), preferred_element_type=F32)
            ps, sink_terms = [], []
            for e in range(2):
                sink = jnp.concatenate(
                    [jnp.full((SWA_BLOCK, 1), sink_ref[c * group + 2 * a + e], F32) for a in range(pairs)], axis=0)
                se = jnp.where(valid, s[:, e * 2 * SWA_BLOCK:(e + 1) * 2 * SWA_BLOCK], -jnp.inf)
                m = jnp.maximum(jnp.max(se, axis=1, keepdims=True), sink)
                ps.append(jnp.exp(se - m).astype(BF16))
                sink_terms.append(jnp.exp(sink - m))
            pv = jnp.dot(jnp.concatenate(ps, axis=1), jnp.concatenate([vcat, ones_by_head], axis=1),
                         preferred_element_type=F32)
            den = pv[:, LANES:] + jnp.where(out_lane < SWA_HD, sink_terms[0], sink_terms[1])
            o4 = (pv[:, :LANES] / den).astype(BF16)
            for a, cs in enumerate(cols):
                o_ref[rows, cs] = o4[a * SWA_BLOCK:(a + 1) * SWA_BLOCK]


def _swa_attn(qs, ks, vs, sinks):
    seq = qs.shape[0]
    tq = min(seq, 512)
    nb = tq // SWA_BLOCK
    cur = lambda i: (i, 0)
    prev = lambda i: (jnp.maximum(i * nb - 1, 0), 0)
    return pl.pallas_call(
        functools.partial(_swa_kernel, tq=tq),
        grid=(seq // tq,),
        in_specs=[pl.BlockSpec(memory_space=pltpu.SMEM),
                  pl.BlockSpec((tq, SWA_HEADS * SWA_HD), cur),
                  pl.BlockSpec((tq, LANES), cur),
                  pl.BlockSpec((SWA_BLOCK, LANES), prev),
                  pl.BlockSpec((tq, LANES), cur),
                  pl.BlockSpec((SWA_BLOCK, LANES), prev)],
        out_specs=pl.BlockSpec((tq, SWA_HEADS * SWA_HD), cur),
        out_shape=jax.ShapeDtypeStruct((seq, SWA_HEADS * SWA_HD), BF16),
        compiler_params=_params(("parallel",)),
        name="swa_attn",
    )(sinks, qs, ks, ks, vs, vs)


def _out_proj_kernel(*refs, n_a):
    a_refs = refs[:n_a]
    w_ref, h_ref, gpost_ref, gnext_ref, hout_ref, hn_ref = refs[n_a:]
    tm = h_ref.shape[0]
    for rows in [slice(r * tm // 4, (r + 1) * tm // 4) for r in range(4)]:
        m = None
        k0 = 0
        for a_ref in a_refs:
            kw = a_ref.shape[1]
            part = jnp.dot(a_ref[rows, :], w_ref[k0:k0 + kw, :], preferred_element_type=F32)
            m = part if m is None else m + part
            k0 += kw
        h = h_ref[rows, :] + _rms(m, gpost_ref[...])
        hout_ref[rows, :] = h
        hn_ref[rows, :] = _rms(h, gnext_ref[...]).astype(BF16)


def _out_proj(a_list, w, h, g_post, g_next):
    seq, d = h.shape
    tm = min(seq, 512)
    row = lambda i: (i, 0)
    fixed = lambda i: (0, 0)
    return pl.pallas_call(
        functools.partial(_out_proj_kernel, n_a=len(a_list)),
        grid=(seq // tm,),
        in_specs=[pl.BlockSpec((tm, a.shape[1]), row) for a in a_list]
        + [pl.BlockSpec(w.shape, fixed), pl.BlockSpec((tm, d), row),
           pl.BlockSpec((1, d), fixed), pl.BlockSpec((1, d), fixed)],
        out_specs=[pl.BlockSpec((tm, d), row), pl.BlockSpec((tm, d), row)],
        out_shape=[jax.ShapeDtypeStruct((seq, d), F32), jax.ShapeDtypeStruct((seq, d), BF16)],
        compiler_params=_params(("parallel",)),
        name="out_proj",
    )(*a_list, w, h, g_post[None, :], g_next[None, :])


def _mlp_kernel(x_ref, wu_ref, wd_ref, h_ref, g_ref, o_ref, acc_ref):
    f = pl.program_id(1)

    @pl.when(f == 0)
    def _():
        acc_ref[...] = jnp.zeros(acc_ref.shape, F32)

    a = jnp.maximum(jnp.dot(x_ref[...], wu_ref[...], preferred_element_type=F32), 0.0)
    acc_ref[...] += jnp.dot((a * a).astype(BF16), wd_ref[...], preferred_element_type=F32)

    @pl.when(f == pl.num_programs(1) - 1)
    def _():
        o_ref[...] = h_ref[...] + _rms(acc_ref[...], g_ref[...])


def _mlp(xn, w_up, w_down, h, g_post):
    seq, d = h.shape
    d_ff = w_up.shape[1]
    tm = min(seq, 512)
    tf = min(d_ff, 1024)
    return pl.pallas_call(
        _mlp_kernel,
        grid=(seq // tm, d_ff // tf),
        in_specs=[pl.BlockSpec((tm, d), lambda i, f: (i, 0)),
                  pl.BlockSpec((d, tf), lambda i, f: (0, f)),
                  pl.BlockSpec((tf, d), lambda i, f: (f, 0)),
                  pl.BlockSpec((tm, d), lambda i, f: (i, 0)),
                  pl.BlockSpec((1, d), lambda i, f: (0, 0))],
        out_specs=pl.BlockSpec((tm, d), lambda i, f: (i, 0)),
        out_shape=jax.ShapeDtypeStruct((seq, d), F32),
        scratch_shapes=[pltpu.VMEM((tm, d), F32)],
        compiler_params=_params(("parallel", "arbitrary")),
        name="mlp",
    )(xn, w_up, w_down, h, g_post[None, :])


def _ple_kernel(h_ref, p_ref, wg_ref, b_ref, wp_ref, *rest, with_next):
    h = h_ref[...]
    hb, pb = h.astype(BF16), p_ref[...].astype(BF16)
    half = h.shape[1] // 2
    outs = []
    for cols in (slice(0, half), slice(half, 2 * half)):
        z = jnp.dot(hb, wg_ref[:, cols], preferred_element_type=F32) + b_ref[:, cols]
        gate = 1.0 / (1.0 + jnp.exp(-z))
        e = jnp.dot(pb, wp_ref[:, cols], preferred_element_type=F32)
        outs.append(h[:, cols] + gate * e)
    out = jnp.concatenate(outs, axis=1)
    if with_next:
        gnext_ref, o_ref, hn_ref = rest
        hn_ref[...] = _rms(out, gnext_ref[...]).astype(BF16)
    else:
        (o_ref,) = rest
    o_ref[...] = out


def _ple(h, p, w_gate, b_gate, w_proj, g_next=None):
    seq, d = h.shape
    tm = min(seq, 512)
    row = lambda i: (i, 0)
    fixed = lambda i: (0, 0)
    with_next = g_next is not None
    in_specs = [pl.BlockSpec((tm, d), row), pl.BlockSpec((tm, p.shape[1]), row),
                pl.BlockSpec(w_gate.shape, fixed), pl.BlockSpec((1, d), fixed),
                pl.BlockSpec(w_proj.shape, fixed)]
    args = [h, p, w_gate, b_gate[None, :], w_proj]
    out_specs = [pl.BlockSpec((tm, d), row)]
    out_shape = [jax.ShapeDtypeStruct((seq, d), F32)]
    if with_next:
        in_specs.append(pl.BlockSpec((1, d), fixed))
        args.append(g_next[None, :])
        out_specs.append(pl.BlockSpec((tm, d), row))
        out_shape.append(jax.ShapeDtypeStruct((seq, d), BF16))
    res = pl.pallas_call(
        functools.partial(_ple_kernel, with_next=with_next),
        grid=(seq // tm,),
        in_specs=in_specs, out_specs=out_specs, out_shape=out_shape,
        compiler_params=_params(("parallel",)),
        name="ple",
    )(*args)
    return res if with_next else (res[0], None)


def _odd_proj_kernel(x_ref, w_ref, cos_ref, sin_ref, o_ref, *rest, rope, scale, kmean, tm):
    t = jnp.dot(x_ref[...], w_ref[...], preferred_element_type=F32)
    if rope:
        t = _rope(t, cos_ref[...], sin_ref[...], MOBA_HD // 2)
    if kmean:
        (km_ref,) = rest
        for b in range(tm // MOBA_BLOCK):
            blk = t[b * MOBA_BLOCK:(b + 1) * MOBA_BLOCK, :]
            km_ref[b] = jnp.sum(blk, axis=0, keepdims=True) * (1.0 / MOBA_BLOCK)
    if scale != 1.0:
        t = t * scale
    o_ref[...] = t.astype(BF16)


def _odd_proj(xn, w, col, tabs_cos, tabs_sin, *, rope, scale=1.0, kmean=False):
    seq, d = xn.shape
    n = d
    tm = min(seq, 1024)
    row = lambda i: (i, 0)
    out_specs = [pl.BlockSpec((tm, n), row)]
    out_shape = [jax.ShapeDtypeStruct((seq, n), BF16)]
    if kmean:
        nb = tm // MOBA_BLOCK
        out_specs.append(pl.BlockSpec((nb, 1, n), lambda i: (i, 0, 0)))
        out_shape.append(jax.ShapeDtypeStruct((seq // MOBA_BLOCK, 1, n), F32))
    res = pl.pallas_call(
        functools.partial(_odd_proj_kernel, rope=rope, scale=scale, kmean=kmean, tm=tm),
        grid=(seq // tm,),
        in_specs=[pl.BlockSpec((tm, d), row),
                  pl.BlockSpec((d, n), lambda i: (0, col), pipeline_mode=pl.Buffered(1)),
                  pl.BlockSpec((tm, LANES), lambda i: (i, 1)),
                  pl.BlockSpec((tm, LANES), lambda i: (i, 1))],
        out_specs=out_specs, out_shape=out_shape,
        compiler_params=_params(("parallel",)),
        name="odd_proj",
    )(xn, w, tabs_cos, tabs_sin)
    return res


def _odd_vt_kernel(x_ref, wt_ref, vt_ref):
    vt = lax.dot_general(wt_ref[...], x_ref[...], (((1,), (1,)), ((), ())), preferred_element_type=F32)
    vt_ref[...] = vt.astype(BF16).reshape(vt_ref.shape)


def _odd_proj_vt(xn, wt):
    seq, d = xn.shape
    tm = min(seq, FLASH_TK)
    heads = wt.shape[0] // LANES
    return pl.pallas_call(
        _odd_vt_kernel,
        grid=(seq // tm,),
        in_specs=[pl.BlockSpec((tm, d), lambda i: (i, 0)), pl.BlockSpec(wt.shape, lambda i: (0, 0))],
        out_specs=pl.BlockSpec((heads, None, LANES, tm), lambda i: (0, i, 0, 0)),
        out_shape=jax.ShapeDtypeStruct((heads, seq // tm, LANES, tm), BF16),
        compiler_params=_params(("parallel",)),
        name="odd_proj_vt",
    )(xn, wt)


def _moba_kernel(q_ref, k_ref, vt_ref, km_ref, o_ref, *stats, tq, tk, nkb):
    i = pl.program_id(1)
    q = q_ref[...]

    km = km_ref[...]
    if nkb < 64:
        km = jnp.concatenate([km, jnp.zeros((64 - nkb, LANES), F32)], axis=0)
    km_hi = km.astype(BF16)
    km_mid = (km - km_hi.astype(F32)).astype(BF16)
    gate = lax.dot_general(jnp.concatenate([km_hi, km_mid], axis=1), jnp.concatenate([q, q], axis=1),
                           (((1,), (1,)), ((), ())), preferred_element_type=F32)

    blk = lax.broadcasted_iota(jnp.int32, (64, tq), 0)
    qblk = (i * tq + lax.broadcasted_iota(jnp.int32, (64, tq), 1)) // MOBA_BLOCK
    gm = jnp.where(blk < qblk, gate, -jnp.inf)
    sel = blk == qblk
    for r in range(MOBA_TOPK):
        mx = jnp.max(gm, axis=0, keepdims=True)
        idx = jnp.min(jnp.where(gm == mx, blk, 64), axis=0, keepdims=True)
        pick = (blk == idx) & (qblk > r)
        sel = sel | pick
        gm = jnp.where(pick, -jnp.inf, gm)
    bias_t = jnp.where(sel, 0.0, MASK_BIAS)
    qat = jnp.concatenate([q.astype(F32).T, bias_t, jnp.zeros((64, tq), F32)], axis=0).astype(BF16)

    kb_per_tile = tk // MOBA_BLOCK
    krow = lax.broadcasted_iota(jnp.int32, (tk, LANES), 0) // MOBA_BLOCK
    klane = lax.broadcasted_iota(jnp.int32, (tk, LANES), 1)

    def kv(j):
        rows = pl.ds(pl.multiple_of(j * tk, tk), tk)
        onehot = jnp.where(klane == krow + j * kb_per_tile, 1.0, 0.0).astype(BF16)
        return jnp.concatenate([k_ref[rows, :], onehot], axis=1), vt_ref[j]

    _flash_attend_t(qat, kv, i, tq, tk, stats, o_ref)


def _moba_attn(q, k, vt, kmean):
    seq = q.shape[0]
    nkb = seq // MOBA_BLOCK
    assert nkb <= 64 and nkb % 8 == 0, "the gate matmul stacks blocks on 64 sublanes"
    tq = min(seq, FLASH_TQ)
    tk = min(seq, FLASH_TK)
    nk = seq // tk
    return pl.pallas_call(
        functools.partial(_moba_kernel, tq=tq, tk=tk, nkb=nkb),
        grid=(MOBA_HEADS, seq // tq),
        in_specs=[pl.BlockSpec((tq, LANES), lambda h, i: (i, h)),
                  pl.BlockSpec((seq, LANES), lambda h, i: (0, h)),
                  pl.BlockSpec((None, nk, LANES, tk), lambda h, i: (h, 0, 0, 0)),
                  pl.BlockSpec((nkb, LANES), lambda h, i: (0, h))],
        out_specs=pl.BlockSpec((tq, LANES), lambda h, i: (i, h)),
        out_shape=jax.ShapeDtypeStruct((seq, MOBA_HEADS * MOBA_HD), BF16),
        scratch_shapes=_flash_t_scratch(tq),
        compiler_params=_params(("parallel", "arbitrary")),
        name="moba_attn",
    )(q, k, vt, kmean)


def _even_in_weight(w_in):
    d = w_in.shape[0]
    cq, ckv, kr, qs, ks, vs = jnp.split(w_in, [512, 1024, 1088, 2112, 2240], axis=1)
    return jnp.concatenate([cq, ckv, kr, jnp.zeros((d, 64), w_in.dtype), qs, ks, vs], axis=1).astype(BF16)


def _mla_q_weight(w_q_up):
    r = w_q_up.shape[0]
    w = w_q_up.reshape(r, MLA_HEADS, MLA_NOPE + MLA_ROPE)
    nope = w[:, :, :MLA_NOPE].reshape(r, MLA_HEADS * MLA_NOPE)
    rope = jnp.pad(w[:, :, MLA_NOPE:], ((0, 0), (0, 0), (0, LANES - MLA_ROPE))).reshape(r, MLA_HEADS * LANES)
    return jnp.concatenate([nope, rope], axis=1).astype(BF16)


def _mla_kv_weight(w_kv_up):
    r = w_kv_up.shape[0]
    w = w_kv_up.reshape(r, MLA_HEADS, MLA_NOPE + MLA_V)
    nope = w[:, :, :MLA_NOPE].reshape(r, MLA_HEADS * MLA_NOPE)
    val = w[:, :, MLA_NOPE:].reshape(r, MLA_HEADS * MLA_V)
    return nope.astype(BF16), val.T.astype(BF16)


def kernel(x, p, positions, even_pre_g, even_w_in, mla_q_norm_g, mla_w_q_up, mla_kv_norm_g, mla_w_kv_up, swa_sinks, even_w_out, even_post_g, odd_pre_g, moba_w_qkv, odd_w_out, odd_post_g, mlp_pre_g, mlp_w_up, mlp_w_down, mlp_post_g, ple_w_gate, ple_b_gate, ple_w_proj):
    batch, seq, d = x.shape
    assert batch == 1
    h = x.reshape(seq, d)
    tabs_cos, tabs_sin = _rope_tables(positions, seq)

    cq, ckv, kr, qs, ks, vs = _even_proj(h, even_pre_g[0], _even_in_weight(even_w_in[0]),
                                         mla_q_norm_g[0], mla_kv_norm_g[0], tabs_cos, tabs_sin)
    qn, qr, kn, vt = _mla_up(cq, ckv, _mla_q_weight(mla_w_q_up[0]), *_mla_kv_weight(mla_w_kv_up[0]),
                             tabs_cos, tabs_sin)
    o_mla = _mla_attn(qn, qr, kn, kr, vt)
    o_swa = _swa_attn(qs, ks, vs, swa_sinks[0])
    h, hn = _out_proj([o_mla, o_swa], even_w_out[0].astype(BF16), h, even_post_g[0], mlp_pre_g[0])
    h = _mlp(hn, mlp_w_up[0].astype(BF16), mlp_w_down[0].astype(BF16), h, mlp_post_g[0])
    h, hn = _ple(h, p[0, 0], ple_w_gate[0].astype(BF16), ple_b_gate[0], ple_w_proj[0].astype(BF16),
                 g_next=odd_pre_g[0])

    w_qkv = moba_w_qkv[0].astype(BF16)
    (q,) = _odd_proj(hn, w_qkv, 0, tabs_cos, tabs_sin, rope=True, scale=MOBA_HD ** -0.5 * LOG2E)
    k, kmean = _odd_proj(hn, w_qkv, 1, tabs_cos, tabs_sin, rope=True, kmean=True)
    vt = _odd_proj_vt(hn, moba_w_qkv[0][:, 2 * d:].T.astype(BF16))
    o = _moba_attn(q, k, vt, kmean.reshape(seq // MOBA_BLOCK, MOBA_HEADS * MOBA_HD))
    h, hn = _out_proj([o], odd_w_out[0].astype(BF16), h, odd_post_g[0], mlp_pre_g[1])
    h = _mlp(hn, mlp_w_up[1].astype(BF16), mlp_w_down[1].astype(BF16), h, mlp_post_g[1])
    h, _ = _ple(h, p[1, 0], ple_w_gate[1].astype(BF16), ple_b_gate[1], ple_w_proj[1].astype(BF16))
    return h.reshape(batch, seq, d)
```
